```python
import jax
import jax.numpy as jnp
from jax import lax
import numpy as np

D_MODEL = 1024
BATCH = 8
SEQ = 8192
DEPTH = 1

CHUNK = 64
Q_BLOCK = 128
HEAD_DIM = 64
FOX_HEADS = 8
DSA_HEADS = 8
IDX_HEADS = 8
IDX_DIM = 32
TOPK_MAX = 256
ROPE_THETA = 500000.0
ROPE_FRACTION = 4
D_FF = 2816
N_SUBLAYERS = 3
NORM_EPS = 1e-6
FOX_WIDTH = FOX_HEADS * HEAD_DIM
DSA_WIDTH = DSA_HEADS * HEAD_DIM
IN_SIZES = (FOX_WIDTH, FOX_WIDTH, FOX_WIDTH, FOX_HEADS, DSA_WIDTH, HEAD_DIM, HEAD_DIM, IDX_HEADS * IDX_DIM, IDX_DIM, IDX_HEADS, D_MODEL, D_MODEL)
D_IN = sum(IN_SIZES)

kernel_name = 'hybrid_fox_dsa_macaron_block'


def rms_norm(x, g):
    xf = x.astype(jnp.float32)
    y = xf * lax.rsqrt(jnp.mean(xf * xf, axis=-1, keepdims=True) + NORM_EPS)
    return (y * g.astype(jnp.float32)).astype(x.dtype)


def modulate(x, g, shift, scale):
    return rms_norm(x, g) * (1 + scale[:, None, :]) + shift[:, None, :]


def swiglu(h, w1, w3, w2):
    return (jax.nn.silu(h @ w1) * (h @ w3)) @ w2


def rope_tables(positions, rot_dim):
    inv_freq = ROPE_THETA ** (-jnp.arange(0, rot_dim, 2, dtype=jnp.float32) / rot_dim)
    ang = positions.astype(jnp.float32)[..., None] * inv_freq
    return jnp.cos(ang), jnp.sin(ang)


def partial_rope(x, cos, sin):
    half = cos.shape[-1]
    x1 = x[..., :half].astype(jnp.float32)
    x2 = x[..., half:2 * half].astype(jnp.float32)
    r1 = (x1 * cos - x2 * sin).astype(x.dtype)
    r2 = (x2 * cos + x1 * sin).astype(x.dtype)
    return jnp.concatenate([r1, r2, x[..., 2 * half:]], axis=-1)


def fox_attention(q, k, v, log_f):
    B, S, H, dh = q.shape
    F = jnp.cumsum(log_f.astype(jnp.float32), axis=1).transpose(0, 2, 1)
    kpos = jnp.arange(S)
    scale = dh ** -0.5

    def block(i):
        qs = i * Q_BLOCK
        qb = lax.dynamic_slice_in_dim(q, qs, Q_BLOCK, axis=1)
        Fq = lax.dynamic_slice_in_dim(F, qs, Q_BLOCK, axis=2)
        s = jnp.einsum('bqhd,bkhd->bhqk', qb, k, preferred_element_type=jnp.float32) * scale
        s = s + Fq[..., None] - F[:, :, None, :]
        qpos = qs + jnp.arange(Q_BLOCK)
        mask = kpos[None, :] <= qpos[:, None]
        s = jnp.where(mask[None, None], s, -jnp.inf)
        p = jax.nn.softmax(s, axis=-1)
        return jnp.einsum('bhqk,bkhd->bqhd', p.astype(v.dtype), v)

    out = lax.map(block, jnp.arange(S // Q_BLOCK))
    return out.transpose(1, 0, 2, 3, 4).reshape(B, S, H, dh)


def dsa_attention(q, k, v, iq, ik, iw, top_k):
    B, S, H, dh = q.shape
    kchunk = jnp.arange(S) // CHUNK
    scale = dh ** -0.5
    idx_scale = IDX_DIM ** -0.5

    def block(i):
        qs = i * Q_BLOCK
        qb = lax.dynamic_slice_in_dim(q, qs, Q_BLOCK, axis=1)
        iqb = lax.dynamic_slice_in_dim(iq, qs, Q_BLOCK, axis=1)
        iwb = lax.dynamic_slice_in_dim(iw, qs, Q_BLOCK, axis=1)
        qchunk = (qs + jnp.arange(Q_BLOCK)) // CHUNK
        admissible = kchunk[None, :] <= qchunk[:, None]
        dots = jnp.einsum('bqhd,bkd->bqhk', iqb, ik, preferred_element_type=jnp.float32) * idx_scale
        score = jnp.einsum('bqh,bqhk->bqk', iwb.astype(jnp.float32), jax.nn.relu(dots))
        score = jnp.where(admissible[None], score, -jnp.inf)
        _, sel = lax.top_k(score, top_k)
        kg = jax.vmap(lambda kb, ib: kb[ib])(k, sel)
        vg = jax.vmap(lambda vb, ib: vb[ib])(v, sel)
        valid = (sel // CHUNK) <= qchunk[None, :, None]
        s = jnp.einsum('bqhd,bqkd->bhqk', qb, kg, preferred_element_type=jnp.float32) * scale
        s = jnp.where(valid[:, None], s, -jnp.inf)
        p = jax.nn.softmax(s, axis=-1)
        return jnp.einsum('bhqk,bqkd->bqhd', p.astype(vg.dtype), vg)

    out = lax.map(block, jnp.arange(S // Q_BLOCK))
    return out.transpose(1, 0, 2, 3, 4).reshape(B, S, H, dh)


def hybrid_mixer(h, w_in, fox_f_bias, fox_qk_g, dsa_qk_g, w_br_fox, w_br_dsa, w_out, rope_a, rope_i, top_k):
    B, S, _ = h.shape
    cos_a, sin_a = rope_a
    cos_i, sin_i = rope_i
    z = h @ w_in
    split_points = [int(p) for p in np.cumsum(IN_SIZES)[:-1]]
    (fq, fk, fv, ff, dq, dk, dv, iq, ik, iw, ga, gb) = jnp.split(z, split_points, axis=-1)
    fq = rms_norm(fq.reshape(B, S, FOX_HEADS, HEAD_DIM), fox_qk_g[0])
    fk = rms_norm(fk.reshape(B, S, FOX_HEADS, HEAD_DIM), fox_qk_g[1])
    fv = fv.reshape(B, S, FOX_HEADS, HEAD_DIM)
    log_f = jax.nn.log_sigmoid(ff.astype(jnp.float32) + fox_f_bias.astype(jnp.float32))
    ya = fox_attention(fq, fk, fv, log_f).reshape(B, S, FOX_WIDTH)
    dq = partial_rope(rms_norm(dq.reshape(B, S, DSA_HEADS, HEAD_DIM), dsa_qk_g[0]), cos_a[:, :, None, :], sin_a[:, :, None, :])
    dk = partial_rope(rms_norm(dk, dsa_qk_g[1]), cos_a, sin_a)
    iq = partial_rope(iq.reshape(B, S, IDX_HEADS, IDX_DIM), cos_i[:, :, None, :], sin_i[:, :, None, :])
    ik = partial_rope(ik, cos_i, sin_i)
    iw = iw * (IDX_HEADS ** -0.5)
    yb = dsa_attention(dq, dk, dv, iq, ik, iw, top_k).reshape(B, S, DSA_WIDTH)
    merged = jax.nn.sigmoid(ga) * (ya @ w_br_fox) + jax.nn.sigmoid(gb) * (yb @ w_br_dsa)
    return merged @ w_out


def setup_inputs(seed: int = 0) -> dict:
    key = jax.random.key(seed)
    ks = jax.random.split(key, 24)
    L = DEPTH

    def nrm(k, shape, fan_in, gain=1.0):
        return jax.random.normal(k, shape, jnp.float32) * (gain * fan_in ** -0.5)

    x = jax.random.normal(ks[0], (BATCH, SEQ, D_MODEL), jnp.float32)
    c = jax.random.normal(ks[1], (BATCH, D_MODEL), jnp.float32)
    offsets = jax.random.randint(ks[2], (BATCH, 1), 0, 1024) * CHUNK
    positions = (offsets + jnp.arange(SEQ, dtype=jnp.int32)[None, :]).astype(jnp.int32)
    ada_w = nrm(ks[3], (L, D_MODEL, 3 * N_SUBLAYERS * D_MODEL), D_MODEL, 0.5)
    ada_b = 0.02 * jax.random.normal(ks[4], (L, 3 * N_SUBLAYERS * D_MODEL), jnp.float32)
    norm_g = 1.0 + 0.05 * jax.random.normal(ks[5], (L, N_SUBLAYERS, D_MODEL), jnp.float32)
    ffn1_w1 = nrm(ks[6], (L, D_MODEL, D_FF), D_MODEL)
    ffn1_w3 = nrm(ks[7], (L, D_MODEL, D_FF), D_MODEL)
    ffn1_w2 = nrm(ks[8], (L, D_FF, D_MODEL), D_FF)
    w_in = nrm(ks[9], (L, D_MODEL, D_IN), D_MODEL)
    fox_f_bias = 2.0 + 0.1 * jax.random.normal(ks[10], (L, FOX_HEADS), jnp.float32)
    fox_qk_g = 1.0 + 0.05 * jax.random.normal(ks[11], (L, 2, HEAD_DIM), jnp.float32)
    dsa_qk_g = 1.0 + 0.05 * jax.random.normal(ks[12], (L, 2, HEAD_DIM), jnp.float32)
    w_br_fox = nrm(ks[13], (L, FOX_WIDTH, D_MODEL), FOX_WIDTH)
    w_br_dsa = nrm(ks[14], (L, DSA_WIDTH, D_MODEL), DSA_WIDTH)
    w_out = nrm(ks[15], (L, D_MODEL, D_MODEL), D_MODEL)
    ffn2_w1 = nrm(ks[16], (L, D_MODEL, D_FF), D_MODEL)
    ffn2_w3 = nrm(ks[17], (L, D_MODEL, D_FF), D_MODEL)
    ffn2_w2 = nrm(ks[18], (L, D_FF, D_MODEL), D_FF)
    return {'x': x, 'c': c, 'positions': positions, 'ada_w': ada_w, 'ada_b': ada_b,
            'norm_g': norm_g, 'ffn1_w1': ffn1_w1, 'ffn1_w3': ffn1_w3, 'ffn1_w2': ffn1_w2,
            'w_in': w_in, 'fox_f_bias': fox_f_bias, 'fox_qk_g': fox_qk_g, 'dsa_qk_g': dsa_qk_g,
            'w_br_fox': w_br_fox, 'w_br_dsa': w_br_dsa, 'w_out': w_out,
            'ffn2_w1': ffn2_w1, 'ffn2_w3': ffn2_w3, 'ffn2_w2': ffn2_w2}


def reference(x, c, positions, ada_w, ada_b, norm_g, ffn1_w1, ffn1_w3, ffn1_w2, w_in, fox_f_bias, fox_qk_g, dsa_qk_g, w_br_fox, w_br_dsa, w_out, ffn2_w1, ffn2_w3, ffn2_w2):
    B, S, _ = x.shape
    top_k = min(TOPK_MAX, S // 4)
    rope_a = rope_tables(positions, HEAD_DIM // ROPE_FRACTION)
    rope_i = rope_tables(positions, IDX_DIM // ROPE_FRACTION)
    cond = jax.nn.silu(c)
    for l in range(DEPTH):
        mod = (cond @ ada_w[l] + ada_b[l]).reshape(B, N_SUBLAYERS, 3, D_MODEL)
        h = modulate(x, norm_g[l, 0], mod[:, 0, 0], mod[:, 0, 1])
        x = x + 0.5 * mod[:, 0, 2][:, None, :] * swiglu(h, ffn1_w1[l], ffn1_w3[l], ffn1_w2[l])
        h = modulate(x, norm_g[l, 1], mod[:, 1, 0], mod[:, 1, 1])
        y = hybrid_mixer(h, w_in[l], fox_f_bias[l], fox_qk_g[l], dsa_qk_g[l], w_br_fox[l], w_br_dsa[l], w_out[l], rope_a, rope_i, top_k)
        x = x + mod[:, 1, 2][:, None, :] * y
        h = modulate(x, norm_g[l, 2], mod[:, 2, 0], mod[:, 2, 1])
        x = x + 0.5 * mod[:, 2, 2][:, None, :] * swiglu(h, ffn2_w1[l], ffn2_w3[l], ffn2_w2[l])
    return x
```

```python
import functools

import numpy as np
import jax
import jax.numpy as jnp
from jax import lax
from jax.experimental import pallas as pl
from jax.experimental.pallas import tpu as pltpu

D_MODEL = 1024
CHUNK = 64
HEAD_DIM = 64
FOX_HEADS = 8
DSA_HEADS = 8
IDX_HEADS = 8
IDX_DIM = 32
TOPK_MAX = 256
ROPE_THETA = 500000.0
ROPE_FRACTION = 4
D_FF = 2816
N_SUBLAYERS = 3
NORM_EPS = 1e-6
FOX_WIDTH = FOX_HEADS * HEAD_DIM
DSA_WIDTH = DSA_HEADS * HEAD_DIM
IDX_WIDTH = IDX_HEADS * IDX_DIM

LANES = 128
VMEM_LIMIT_BYTES = 56 * 1024 * 1024

TOKEN_TILE = 512
FOX_TILE = 512
DSA_Q_TILE = 128
DSA_K_TILE = 512

INT_MIN = -2 ** 31
F32 = jnp.float32
BF16 = jnp.bfloat16


def _cparams(sem):
    return pltpu.CompilerParams(dimension_semantics=sem, vmem_limit_bytes=VMEM_LIMIT_BYTES)


def _const_spec(shape):
    nd = len(shape)
    return pl.BlockSpec(shape, lambda *_: (0,) * nd, pipeline_mode=pl.Buffered(1))


def _modulated(x, mod_ref, g_ref, sub):
    ms = jnp.mean(x * x, axis=-1, keepdims=True)
    y = x * lax.rsqrt(ms + NORM_EPS) * g_ref[sub:sub + 1, :]
    shift = mod_ref[0, 3 * sub:3 * sub + 1, :]
    scale = mod_ref[0, 3 * sub + 1:3 * sub + 2, :]
    return y * (1.0 + scale) + shift


def _ada_kernel(c_ref, w_ref, b_ref, o_ref):
    c = c_ref[...]
    cond = c * jax.nn.sigmoid(c)
    o_ref[...] = jnp.dot(cond, w_ref[...], preferred_element_type=F32,
                         precision=lax.Precision.HIGHEST) + b_ref[...]


def _ada_call(c, w, b):
    bsz, d = c.shape
    n = w.shape[1]
    tn = 1152
    return pl.pallas_call(
        _ada_kernel,
        grid=(n // tn,),
        in_specs=[pl.BlockSpec((bsz, d), lambda j: (0, 0)),
                  pl.BlockSpec((d, tn), lambda j: (0, j)),
                  pl.BlockSpec((1, tn), lambda j: (0, j))],
        out_specs=pl.BlockSpec((bsz, tn), lambda j: (0, j)),
        out_shape=jax.ShapeDtypeStruct((bsz, n), F32),
        compiler_params=_cparams(("arbitrary",)),
        name="ada_ln",
    )(c, w, b.reshape(1, n))


def _ffn_kernel(x_ref, mod_ref, g_ref, w1_ref, w3_ref, w2_ref, o_ref, *, sub):
    x = x_ref[0]
    h = _modulated(x, mod_ref, g_ref, sub).astype(BF16)
    a = jnp.dot(h, w1_ref[...], preferred_element_type=F32)
    b = jnp.dot(h, w3_ref[...], preferred_element_type=F32)
    act = (a * jax.nn.sigmoid(a) * b).astype(BF16)
    y = jnp.dot(act, w2_ref[...], preferred_element_type=F32)
    gate = mod_ref[0, 3 * sub + 2:3 * sub + 3, :]
    o_ref[0] = x + (0.5 * gate) * y


def _ffn_call(x, mod, g, w1, w3, w2, sub):
    bsz, s, d = x.shape
    tm = min(TOKEN_TILE, s)
    f = w1.shape[1]
    tok = pl.BlockSpec((1, tm, d), lambda b, i: (b, i, 0))
    return pl.pallas_call(
        functools.partial(_ffn_kernel, sub=sub),
        grid=(bsz, s // tm),
        in_specs=[tok,
                  pl.BlockSpec((1, 3 * N_SUBLAYERS, d), lambda b, i: (b, 0, 0)),
                  _const_spec((N_SUBLAYERS, d)),
                  _const_spec((d, f)), _const_spec((d, f)), _const_spec((f, d))],
        out_specs=tok,
        out_shape=jax.ShapeDtypeStruct(x.shape, F32),
        compiler_params=_cparams(("parallel", "parallel")),
        name=f"ffn{sub}",
    )(x, mod, g, w1, w3, w2)


def _split3(v):
    hi = v.astype(BF16)
    r = v - hi.astype(F32)
    mid = r.astype(BF16)
    lo = (r - mid.astype(F32)).astype(BF16)
    return hi, mid, lo


def _group_norm(z, gmat, gain, group):
    sq = z * z
    hi = sq.astype(BF16)
    lo = (sq - hi.astype(F32)).astype(BF16)
    ssq = jnp.dot(hi, gmat, preferred_element_type=F32) + jnp.dot(lo, gmat, preferred_element_type=F32)
    return z * lax.rsqrt(ssq * (1.0 / group) + NORM_EPS) * gain


def _rope(z, cos, sin, half, period):
    width = z.shape[1]
    lane = lax.broadcasted_iota(jnp.int32, z.shape, 1) % period
    up = pltpu.roll(z, width - half, axis=1)
    dn = pltpu.roll(z, half, axis=1)
    return z * cos + jnp.where(lane < half, up, dn) * sin


def _mix_in_kernel(x_ref, mod_ref, g_ref, pos_ref, wa_ref, wb_ref, gmat_ref, gains_ref, freq_ref, fbias_ref,
                   fq_ref, fk_ref, fv_ref, dq_ref, dk4_ref, dv2_ref, ik8_ref, iq_ref, small_ref, carry_ref):
    @pl.when(pl.program_id(1) == 0)
    def _():
        carry_ref[...] = jnp.zeros_like(carry_ref)

    x = x_ref[0]
    tm = x.shape[0]
    h = _modulated(x, mod_ref, g_ref, 1).astype(BF16)
    za = jnp.dot(h, wa_ref[...], preferred_element_type=F32)
    zb = jnp.dot(h, wb_ref[...], preferred_element_type=F32)

    gmat = gmat_ref[...]
    scale = HEAD_DIM ** -0.5
    fq = _group_norm(za[:, 0:512], gmat, gains_ref[0:1, :], HEAD_DIM) * scale
    fk = _group_norm(za[:, 512:1024], gmat, gains_ref[1:2, :], HEAD_DIM)
    fq_ref[0] = fq.astype(BF16)
    fk_ref[0] = fk.astype(BF16)
    fv_ref[0] = za[:, 1024:1536].astype(BF16)

    pos = pos_ref[0]
    ang_a = pos * freq_ref[0:1, :]
    ang_i = pos * freq_ref[1:2, :]
    cos_a, sin_a = jnp.cos(ang_a), jnp.sin(ang_a)
    cos_i, sin_i = jnp.cos(ang_i), jnp.sin(ang_i)
    rot_a = HEAD_DIM // ROPE_FRACTION // 2
    rot_i = IDX_DIM // ROPE_FRACTION // 2

    dq = _group_norm(za[:, 1536:2048], gmat, gains_ref[2:3, :], HEAD_DIM)
    dq = _rope(dq, jnp.concatenate([cos_a] * 4, axis=1), jnp.concatenate([sin_a] * 4, axis=1), rot_a, HEAD_DIM)
    dq_ref[0] = (dq * scale).astype(BF16)

    dk4 = _group_norm(zb[:, 0:256], gmat[0:256, 0:256], gains_ref[3:4, 0:256], HEAD_DIM)
    dk4 = _rope(dk4, jnp.concatenate([cos_a] * 2, axis=1), jnp.concatenate([sin_a] * 2, axis=1), rot_a, HEAD_DIM)
    dk4_ref[0] = dk4.astype(BF16)

    cos_i2 = jnp.concatenate([cos_i] * 2, axis=1)
    sin_i2 = jnp.concatenate([sin_i] * 2, axis=1)
    ik8_ref[0] = _rope(zb[:, 256:512], cos_i2, sin_i2, rot_i, IDX_DIM).astype(BF16)
    dv2_ref[0] = zb[:, 512:640].astype(BF16)
    iq_ref[0] = _rope(zb[:, 640:896], cos_i2, sin_i2, rot_i, IDX_DIM).astype(BF16)

    sm = zb[:, 896:1024]
    v = sm + fbias_ref[...]
    logf = jnp.minimum(v, 0.0) - jnp.log1p(jnp.exp(-jnp.abs(v)))
    row = lax.broadcasted_iota(jnp.int32, (tm, tm), 0)
    col = lax.broadcasted_iota(jnp.int32, (tm, tm), 1)
    tri = jnp.where(row >= col, 1.0, 0.0).astype(BF16)
    hi, mid, lo = _split3(logf)
    csum = (jnp.dot(tri, hi, preferred_element_type=F32) + jnp.dot(tri, mid, preferred_element_type=F32)
            + jnp.dot(tri, lo, preferred_element_type=F32)) + carry_ref[...]
    carry_ref[...] = csum[tm - 1:tm, :]
    lane = lax.broadcasted_iota(jnp.int32, sm.shape, 1)
    small_ref[0] = jnp.where(lane < FOX_HEADS, csum, sm * ((IDX_HEADS * IDX_DIM) ** -0.5))


def _mix_in_call(x, mod, g, pos, wa, wb, gmat, gains, freq, fbias):
    bsz, s, d = x.shape
    tm = min(TOKEN_TILE, s)

    def tok(w):
        return pl.BlockSpec((1, tm, w), lambda b, i: (b, i, 0))

    widths = (512, 512, 512, 512, 256, 128, 256, 256)
    out_shapes = [jax.ShapeDtypeStruct((bsz, s, w), BF16) for w in widths]
    out_shapes.append(jax.ShapeDtypeStruct((bsz, s, LANES), F32))
    return pl.pallas_call(
        _mix_in_kernel,
        grid=(bsz, s // tm),
        in_specs=[tok(d),
                  pl.BlockSpec((1, 3 * N_SUBLAYERS, d), lambda b, i: (b, 0, 0)),
                  _const_spec((N_SUBLAYERS, d)),
                  tok(LANES),
                  _const_spec(wa.shape), _const_spec(wb.shape), _const_spec(gmat.shape),
                  _const_spec(gains.shape), _const_spec(freq.shape), _const_spec(fbias.shape)],
        out_specs=[tok(w) for w in widths] + [tok(LANES)],
        out_shape=out_shapes,
        scratch_shapes=[pltpu.VMEM((1, LANES), F32)],
        compiler_params=_cparams(("arbitrary", "arbitrary")),
        name="mix_in",
    )(x, mod, g, pos, wa, wb, gmat, gains, freq, fbias)


def _fox_kernel(q_ref, k_ref, v_ref, f_ref, hm_ref, o_ref, m_ref, l_ref, acc_ref):
    i = pl.program_id(2)
    hp = pl.program_id(1)
    tq = q_ref.shape[1]
    tk = tq
    q = q_ref[0]
    qs = [q * hm_ref[a:a + 1, :] for a in range(2)]
    m_ref[...] = jnp.full_like(m_ref, -jnp.inf)
    l_ref[...] = jnp.zeros_like(l_ref)
    acc_ref[...] = jnp.zeros_like(acc_ref)

    def block(j, masked):
        start = pl.multiple_of(j * tk, tk)
        kb = k_ref[0, pl.ds(start, tk), :]
        vb = v_ref[0, pl.ds(start, tk), :]
        for a in range(2):
            s = lax.dot_general(qs[a], kb, (((1,), (1,)), ((), ())), preferred_element_type=F32)
            s = s - f_ref[0, pl.ds(2 * hp + a, 1), pl.ds(start, tk)]
            if masked:
                row = lax.broadcasted_iota(jnp.int32, s.shape, 0)
                col = lax.broadcasted_iota(jnp.int32, s.shape, 1)
                s = jnp.where(col <= row, s, -jnp.inf)
            m_old = m_ref[a]
            m_new = jnp.maximum(m_old, jnp.max(s, axis=1, keepdims=True))
            p = jnp.exp(s - m_new)
            alpha = jnp.exp(m_old - m_new)
            l_ref[a] = alpha * l_ref[a] + jnp.sum(p, axis=1, keepdims=True)
            acc_ref[a] = alpha * acc_ref[a] + jnp.dot(p.astype(BF16), vb, preferred_element_type=F32)
            m_ref[a] = m_new

    def body(j, c):
        block(j, False)
        return c

    lax.fori_loop(0, i, body, 0)
    block(i, True)
    oa = acc_ref[0] / l_ref[0]
    ob = acc_ref[1] / l_ref[1]
    lane = lax.broadcasted_iota(jnp.int32, oa.shape, 1)
    o_ref[0] = jnp.where(lane < HEAD_DIM, oa, ob).astype(BF16)


def _fox_call(fq, fk, fv, frow, hm2):
    bsz, s, _ = fq.shape
    tq = min(FOX_TILE, s)
    pairs = FOX_HEADS // 2
    return pl.pallas_call(
        _fox_kernel,
        grid=(bsz, pairs, s // tq),
        in_specs=[pl.BlockSpec((1, tq, LANES), lambda b, h, i: (b, i, h)),
                  pl.BlockSpec((1, s, LANES), lambda b, h, i: (b, 0, h)),
                  pl.BlockSpec((1, s, LANES), lambda b, h, i: (b, 0, h)),
                  pl.BlockSpec((1, FOX_HEADS, s), lambda b, h, i: (b, 0, 0)),
                  pl.BlockSpec((2, LANES), lambda b, h, i: (0, 0))],
        out_specs=pl.BlockSpec((1, tq, LANES), lambda b, h, i: (b, i, h)),
        out_shape=jax.ShapeDtypeStruct((bsz, s, FOX_WIDTH), BF16),
        scratch_shapes=[pltpu.VMEM((2, tq, 1), F32), pltpu.VMEM((2, tq, 1), F32),
                        pltpu.VMEM((2, tq, LANES), F32)],
        compiler_params=_cparams(("parallel", "parallel", "arbitrary")),
        name="fox_attn",
    )(fq, fk, fv, frow, hm2)


def _dsa_kernel(dq_ref, iq_ref, small_ref, dk4_ref, dv2_ref, ik8_ref, hm8_ref, hm4_ref, o_ref,
                keys_ref, m_ref, l_ref, acc_ref, *, top_k):
    i = pl.program_id(1)
    tq = dq_ref.shape[1]
    tk = DSA_K_TILE if dk4_ref.shape[1] >= DSA_K_TILE else dk4_ref.shape[1]
    q_start = i * tq
    nkb = (q_start + tq + tk - 1) // tk

    row = lax.broadcasted_iota(jnp.int32, (tq, tk), 0)
    col = lax.broadcasted_iota(jnp.int32, (tq, tk), 1)
    adm_end = q_start + (row // CHUNK + 1) * CHUNK

    iq = iq_ref[0]
    iqm = [iq * hm8_ref[h:h + 1, :] for h in range(IDX_HEADS)]
    sm = small_ref[0]
    wcol = [sm[:, IDX_HEADS + h:IDX_HEADS + h + 1] for h in range(IDX_HEADS)]

    def score_block(j, c):
        start = pl.multiple_of(j * tk, tk)
        kb = ik8_ref[0, pl.ds(start, tk), :]
        sc = jnp.zeros((tq, tk), F32)
        for h in range(IDX_HEADS):
            d = lax.dot_general(iqm[h], kb, (((1,), (1,)), ((), ())), preferred_element_type=F32)
            sc = sc + jnp.maximum(d, 0.0) * wcol[h]
        sc = jnp.where(sc == 0.0, 0.0, sc)
        bits = lax.bitcast_convert_type(sc, jnp.int32)
        key = bits ^ ((bits >> 31) & 0x7FFFFFFF)
        key = jnp.where(col + start < adm_end, key, INT_MIN)
        keys_ref[:, pl.ds(start, tk)] = key
        return c

    lax.fori_loop(0, nkb, score_block, 0)

    def count(pred):
        def body(j, acc):
            start = pl.multiple_of(j * tk, tk)
            hit = jnp.where(pred(keys_ref[:, pl.ds(start, tk)]), 1, 0)
            for t in range(tk // LANES):
                acc = acc + hit[:, t * LANES:(t + 1) * LANES]
            return acc
        acc = lax.fori_loop(0, nkb, body, jnp.zeros((tq, LANES), jnp.int32))
        return jnp.sum(acc, axis=1, keepdims=True)

    def bisect(it, thr):
        cand = thr + jnp.left_shift(jnp.int32(1), 31 - it)
        cnt = count(lambda kblk: kblk >= cand)
        return jnp.where(cnt >= top_k, cand, thr)

    thr = lax.fori_loop(0, 32, bisect, jnp.full((tq, 1), INT_MIN, jnp.int32))
    thr = jnp.maximum(thr, INT_MIN + 1)
    n_ge = count(lambda kblk: kblk >= thr)

    @pl.when(jnp.max(n_ge) > top_k)
    def _():
        n_gt = count(lambda kblk: kblk > thr)
        need = top_k - n_gt
        r2 = lax.broadcasted_iota(jnp.int32, (tk, tk), 0)
        c2 = lax.broadcasted_iota(jnp.int32, (tk, tk), 1)
        upper = jnp.where(r2 < c2, 1.0, 0.0).astype(BF16)

        def demote(j, seen):
            start = pl.multiple_of(j * tk, tk)
            kblk = keys_ref[:, pl.ds(start, tk)]
            eq = kblk == thr
            eqf = jnp.where(eq, 1.0, 0.0)
            before = jnp.dot(eqf.astype(BF16), upper, preferred_element_type=F32) + seen
            keys_ref[:, pl.ds(start, tk)] = jnp.where(eq & (before >= need.astype(F32)), thr - 1, kblk)
            return seen + jnp.sum(eqf, axis=1, keepdims=True)

        lax.fori_loop(0, nkb, demote, jnp.zeros((tq, 1), F32))

    dq = dq_ref[0]
    half = dq.shape[1] // 2
    dqm = [dq[:, (h // 4) * half:(h // 4 + 1) * half] * hm4_ref[h % 4:h % 4 + 1, :] for h in range(DSA_HEADS)]
    m_ref[...] = jnp.full_like(m_ref, -jnp.inf)
    l_ref[...] = jnp.zeros_like(l_ref)
    acc_ref[...] = jnp.zeros_like(acc_ref)

    def attend(j, c):
        start = pl.multiple_of(j * tk, tk)
        sel = keys_ref[:, pl.ds(start, tk)] >= thr
        kb = dk4_ref[0, pl.ds(start, tk), :]
        vb = dv2_ref[0, pl.ds(start, tk), :]
        for h in range(DSA_HEADS):
            s = lax.dot_general(dqm[h], kb, (((1,), (1,)), ((), ())), preferred_element_type=F32)
            s = jnp.where(sel, s, -jnp.inf)
            m_old = m_ref[h]
            m_new = jnp.maximum(m_old, jnp.max(s, axis=1, keepdims=True))
            m_safe = jnp.where(m_new == -jnp.inf, 0.0, m_new)
            p = jnp.exp(s - m_safe)
            alpha = jnp.exp(m_old - m_safe)
            l_ref[h] = alpha * l_ref[h] + jnp.sum(p, axis=1, keepdims=True)
            acc_ref[h] = alpha * acc_ref[h] + jnp.dot(p.astype(BF16), vb, preferred_element_type=F32)
            m_ref[h] = m_new
        return c

    lax.fori_loop(0, nkb, attend, 0)
    lane = lax.broadcasted_iota(jnp.int32, (tq, LANES), 1)
    outs = []
    for c in range(DSA_HEADS // 2):
        oa = acc_ref[2 * c] / l_ref[2 * c]
        ob = acc_ref[2 * c + 1] / l_ref[2 * c + 1]
        outs.append(jnp.where(lane < HEAD_DIM, oa, ob))
    o_ref[0] = jnp.concatenate(outs, axis=1).astype(BF16)


def _dsa_call(dq, iq, small, dk4, dv2, ik8, hm8, hm4, top_k):
    bsz, s, _ = dq.shape
    tq = min(DSA_Q_TILE, s)

    def tok(w):
        return pl.BlockSpec((1, tq, w), lambda b, i: (b, i, 0))

    def seq(w):
        return pl.BlockSpec((1, s, w), lambda b, i: (b, 0, 0))

    return pl.pallas_call(
        functools.partial(_dsa_kernel, top_k=top_k),
        grid=(bsz, s // tq),
        in_specs=[tok(DSA_WIDTH), tok(IDX_WIDTH), tok(LANES), seq(256), seq(LANES), seq(256),
                  pl.BlockSpec(hm8.shape, lambda b, i: (0, 0)), pl.BlockSpec(hm4.shape, lambda b, i: (0, 0))],
        out_specs=tok(DSA_WIDTH),
        out_shape=jax.ShapeDtypeStruct((bsz, s, DSA_WIDTH), BF16),
        scratch_shapes=[pltpu.VMEM((tq, s), jnp.int32),
                        pltpu.VMEM((DSA_HEADS, tq, 1), F32), pltpu.VMEM((DSA_HEADS, tq, 1), F32),
                        pltpu.VMEM((DSA_HEADS, tq, LANES), F32)],
        compiler_params=_cparams(("parallel", "arbitrary")),
        name="dsa_attn",
    )(dq, iq, small, dk4, dv2, ik8, hm8, hm4)


def _merge_kernel(x_ref, mod_ref, g_ref, ya_ref, yb_ref, wg_ref, wfa_ref, wfb_ref, wo_ref, o_ref):
    x = x_ref[0]
    d = x.shape[1]
    h = _modulated(x, mod_ref, g_ref, 1).astype(BF16)
    zg = jnp.dot(h, wg_ref[...], preferred_element_type=F32)
    pa = jnp.dot(ya_ref[0], wfa_ref[...], preferred_element_type=F32)
    pb = jnp.dot(yb_ref[0], wfb_ref[...], preferred_element_type=F32)
    merged = jax.nn.sigmoid(zg[:, :d]) * pa + jax.nn.sigmoid(zg[:, d:]) * pb
    y = jnp.dot(merged.astype(BF16), wo_ref[...], preferred_element_type=F32)
    o_ref[0] = x + mod_ref[0, 5:6, :] * y


def _merge_call(x, mod, g, ya, yb, wg, wfa, wfb, wo):
    bsz, s, d = x.shape
    tm = min(TOKEN_TILE, s)

    def tok(w):
        return pl.BlockSpec((1, tm, w), lambda b, i: (b, i, 0))

    return pl.pallas_call(
        _merge_kernel,
        grid=(bsz, s // tm),
        in_specs=[tok(d),
                  pl.BlockSpec((1, 3 * N_SUBLAYERS, d), lambda b, i: (b, 0, 0)),
                  _const_spec((N_SUBLAYERS, d)),
                  tok(FOX_WIDTH), tok(DSA_WIDTH),
                  _const_spec(wg.shape), _const_spec(wfa.shape), _const_spec(wfb.shape), _const_spec(wo.shape)],
        out_specs=tok(d),
        out_shape=jax.ShapeDtypeStruct(x.shape, F32),
        compiler_params=_cparams(("parallel", "parallel")),
        name="merge_out",
    )(x, mod, g, ya, yb, wg, wfa, wfb, wo)


def _head_mask(n_heads, width):
    return jnp.asarray(np.kron(np.eye(n_heads), np.ones((1, width))), BF16)


def _rope_freq_row(rot_dim, period):
    inv_freq = ROPE_THETA ** (-jnp.arange(0, rot_dim, 2, dtype=F32) / rot_dim)
    half = rot_dim // 2
    head = jnp.concatenate([-inv_freq, inv_freq, jnp.zeros((period - 2 * half,), F32)])
    return jnp.tile(head, LANES // period)


def kernel(x, c, positions, ada_w, ada_b, norm_g, ffn1_w1, ffn1_w3, ffn1_w2, w_in, fox_f_bias, fox_qk_g, dsa_qk_g,
           w_br_fox, w_br_dsa, w_out, ffn2_w1, ffn2_w3, ffn2_w2):
    bsz, s, d = x.shape
    top_k = min(TOPK_MAX, s // 4)
    depth = ada_w.shape[0]
    pos = jnp.broadcast_to(positions.astype(F32)[:, :, None], (bsz, s, LANES))
    freq = jnp.stack([_rope_freq_row(HEAD_DIM // ROPE_FRACTION, HEAD_DIM),
                      _rope_freq_row(IDX_DIM // ROPE_FRACTION, IDX_DIM)])
    gmat = jnp.asarray(np.kron(np.eye(FOX_HEADS), np.ones((HEAD_DIM, HEAD_DIM))), BF16)
    hm2 = _head_mask(2, HEAD_DIM)
    hm4 = _head_mask(4, HEAD_DIM)
    hm8 = _head_mask(IDX_HEADS, IDX_DIM)
    o_fq, o_fk, o_fv = 0, FOX_WIDTH, 2 * FOX_WIDTH
    o_ff = 3 * FOX_WIDTH
    o_dq = o_ff + FOX_HEADS
    o_dk = o_dq + DSA_WIDTH
    o_dv = o_dk + HEAD_DIM
    o_iq = o_dv + HEAD_DIM
    o_ik = o_iq + IDX_WIDTH
    o_iw = o_ik + IDX_DIM
    o_ga = o_iw + IDX_HEADS

    for l in range(depth):
        mod = _ada_call(c, ada_w[l], ada_b[l]).reshape(bsz, 3 * N_SUBLAYERS, d)
        g = norm_g[l]
        x = _ffn_call(x, mod, g, ffn1_w1[l].astype(BF16), ffn1_w3[l].astype(BF16), ffn1_w2[l].astype(BF16), 0)

        w = w_in[l]
        wa = jnp.concatenate([w[:, o_fq:o_fq + 3 * FOX_WIDTH], w[:, o_dq:o_dq + DSA_WIDTH]], axis=1).astype(BF16)
        pad = jnp.zeros((d, LANES - FOX_HEADS - IDX_HEADS), F32)
        wb = jnp.concatenate([jnp.tile(w[:, o_dk:o_dk + HEAD_DIM], (1, 4)),
                              jnp.tile(w[:, o_ik:o_ik + IDX_DIM], (1, IDX_HEADS)),
                              jnp.tile(w[:, o_dv:o_dv + HEAD_DIM], (1, 2)),
                              w[:, o_iq:o_iq + IDX_WIDTH],
                              w[:, o_ff:o_ff + FOX_HEADS], w[:, o_iw:o_iw + IDX_HEADS], pad], axis=1).astype(BF16)
        wg = w[:, o_ga:o_ga + 2 * d].astype(BF16)
        gains = jnp.stack([jnp.tile(fox_qk_g[l, 0], FOX_HEADS), jnp.tile(fox_qk_g[l, 1], FOX_HEADS),
                           jnp.tile(dsa_qk_g[l, 0], DSA_HEADS), jnp.tile(dsa_qk_g[l, 1], DSA_HEADS)]).astype(F32)
        fbias = jnp.concatenate([fox_f_bias[l].astype(F32), jnp.zeros((LANES - FOX_HEADS,), F32)]).reshape(1, LANES)

        fq, fk, fv, dq, dk4, dv2, ik8, iq, small = _mix_in_call(x, mod, g, pos, wa, wb, gmat, gains, freq, fbias)
        frow = jnp.transpose(small[:, :, :FOX_HEADS], (0, 2, 1))
        ya = _fox_call(fq, fk, fv, frow, hm2)
        yb = _dsa_call(dq, iq, small, dk4, dv2, ik8, hm8, hm4, top_k)
        x = _merge_call(x, mod, g, ya, yb, wg, w_br_fox[l].astype(BF16), w_br_dsa[l].astype(BF16),
                        w_out[l].astype(BF16))
        x = _ffn_call(x, mod, g, ffn2_w1[l].astype(BF16), ffn2_w3[l].astype(BF16), ffn2_w2[l].astype(BF16), 2)
    return x
```

```python
import functools

import numpy as np
import jax
import jax.numpy as jnp
from jax import lax
from jax.experimental import pallas as pl
from jax.experimental.pallas import tpu as pltpu

D_MODEL = 1024
CHUNK = 64
HEAD_DIM = 64
FOX_HEADS = 8
DSA_HEADS = 8
IDX_HEADS = 8
IDX_DIM = 32
TOPK_MAX = 256
ROPE_THETA = 500000.0
ROPE_FRACTION = 4
D_FF = 2816
N_SUBLAYERS = 3
NORM_EPS = 1e-6
FOX_WIDTH = FOX_HEADS * HEAD_DIM
DSA_WIDTH = DSA_HEADS * HEAD_DIM
IDX_WIDTH = IDX_HEADS * IDX_DIM

LANES = 128
VMEM_LIMIT_BYTES = 56 * 1024 * 1024

TOKEN_TILE = 512
FOX_TILE = 512
DSA_Q_TILE = 128
DSA_K_TILE = 512

INT_MIN = -2 ** 31
F32 = jnp.float32
BF16 = jnp.bfloat16
_NT = (((1,), (1,)), ((), ()))


def _cparams(sem):
    return pltpu.CompilerParams(dimension_semantics=sem, vmem_limit_bytes=VMEM_LIMIT_BYTES)


def _const_spec(shape):
    nd = len(shape)
    return pl.BlockSpec(shape, lambda *_: (0,) * nd, pipeline_mode=pl.Buffered(1))


def _modulated(x, mod_ref, g_ref, sub):
    ms = jnp.mean(x * x, axis=-1, keepdims=True)
    y = x * lax.rsqrt(ms + NORM_EPS) * g_ref[sub:sub + 1, :]
    shift = mod_ref[0, 3 * sub:3 * sub + 1, :]
    scale = mod_ref[0, 3 * sub + 1:3 * sub + 2, :]
    return y * (1.0 + scale) + shift


def _ada_kernel(c_ref, w_ref, b_ref, o_ref):
    c = c_ref[...]
    cond = c * jax.nn.sigmoid(c)
    o_ref[...] = jnp.dot(cond, w_ref[...], preferred_element_type=F32,
                         precision=lax.Precision.HIGHEST) + b_ref[...]


def _ada_call(c, w, b):
    bsz, d = c.shape
    n = w.shape[1]
    tn = 1152
    return pl.pallas_call(
        _ada_kernel,
        grid=(n // tn,),
        in_specs=[pl.BlockSpec((bsz, d), lambda j: (0, 0)),
                  pl.BlockSpec((d, tn), lambda j: (0, j)),
                  pl.BlockSpec((1, tn), lambda j: (0, j))],
        out_specs=pl.BlockSpec((bsz, tn), lambda j: (0, j)),
        out_shape=jax.ShapeDtypeStruct((bsz, n), F32),
        compiler_params=_cparams(("arbitrary",)),
        name="ada_ln",
    )(c, w, b.reshape(1, n))


def _ffn_kernel(x_ref, mod_ref, g_ref, w1_ref, w3_ref, w2_ref, o_ref, *, sub):
    x = x_ref[0]
    h = _modulated(x, mod_ref, g_ref, sub).astype(BF16)
    a = jnp.dot(h, w1_ref[...], preferred_element_type=F32)
    b = jnp.dot(h, w3_ref[...], preferred_element_type=F32)
    act = (a * jax.nn.sigmoid(a) * b).astype(BF16)
    y = jnp.dot(act, w2_ref[...], preferred_element_type=F32)
    gate = mod_ref[0, 3 * sub + 2:3 * sub + 3, :]
    o_ref[0] = x + (0.5 * gate) * y


def _ffn_call(x, mod, g, w1, w3, w2, sub):
    bsz, s, d = x.shape
    tm = min(TOKEN_TILE, s)
    f = w1.shape[1]
    tok = pl.BlockSpec((1, tm, d), lambda b, i: (b, i, 0))
    return pl.pallas_call(
        functools.partial(_ffn_kernel, sub=sub),
        grid=(bsz, s // tm),
        in_specs=[tok,
                  pl.BlockSpec((1, 3 * N_SUBLAYERS, d), lambda b, i: (b, 0, 0)),
                  _const_spec((N_SUBLAYERS, d)),
                  _const_spec((d, f)), _const_spec((d, f)), _const_spec((f, d))],
        out_specs=tok,
        out_shape=jax.ShapeDtypeStruct(x.shape, F32),
        compiler_params=_cparams(("parallel", "parallel")),
        name=f"ffn{sub}",
    )(x, mod, g, w1, w3, w2)


def _split3(v):
    hi = v.astype(BF16)
    r = v - hi.astype(F32)
    mid = r.astype(BF16)
    lo = (r - mid.astype(F32)).astype(BF16)
    return hi, mid, lo


def _group_norm(z, gmat, gain, group):
    sq = z * z
    hi = sq.astype(BF16)
    lo = (sq - hi.astype(F32)).astype(BF16)
    ssq = jnp.dot(hi, gmat, preferred_element_type=F32) + jnp.dot(lo, gmat, preferred_element_type=F32)
    return z * lax.rsqrt(ssq * (1.0 / group) + NORM_EPS) * gain


def _rope(z, cos, sin, half, period):
    width = z.shape[1]
    lane = lax.broadcasted_iota(jnp.int32, z.shape, 1) % period
    up = pltpu.roll(z, width - half, axis=1)
    dn = pltpu.roll(z, half, axis=1)
    return z * cos + jnp.where(lane < half, up, dn) * sin


def _mix_in_kernel(x_ref, mod_ref, g_ref, pos_ref, wa_ref, wb_ref, gmat_ref, gains_ref, freq_ref, fbias_ref,
                   fq_ref, fk_ref, fv_ref, dq_ref, dk4_ref, dv2_ref, ik8_ref, iq_ref, faug_ref, small_ref,
                   carry_ref):
    @pl.when(pl.program_id(1) == 0)
    def _():
        carry_ref[...] = jnp.zeros_like(carry_ref)

    x = x_ref[0]
    tm = x.shape[0]
    h = _modulated(x, mod_ref, g_ref, 1).astype(BF16)
    za = jnp.dot(h, wa_ref[...], preferred_element_type=F32)
    zb = jnp.dot(h, wb_ref[...], preferred_element_type=F32)

    gmat = gmat_ref[...]
    scale = HEAD_DIM ** -0.5
    fq = _group_norm(za[:, 0:512], gmat, gains_ref[0:1, :], HEAD_DIM) * scale
    fk = _group_norm(za[:, 512:1024], gmat, gains_ref[1:2, :], HEAD_DIM)
    fq_ref[0] = fq.astype(BF16)
    fk_ref[0] = fk.astype(BF16)
    fv_ref[0] = za[:, 1024:1536].astype(BF16)

    pos = pos_ref[0]
    ang_a = pos * freq_ref[0:1, :]
    ang_i = pos * freq_ref[1:2, :]
    cos_a, sin_a = jnp.cos(ang_a), jnp.sin(ang_a)
    cos_i, sin_i = jnp.cos(ang_i), jnp.sin(ang_i)
    rot_a = HEAD_DIM // ROPE_FRACTION // 2
    rot_i = IDX_DIM // ROPE_FRACTION // 2

    dq = _group_norm(za[:, 1536:2048], gmat, gains_ref[2:3, :], HEAD_DIM)
    dq = _rope(dq, jnp.concatenate([cos_a] * 4, axis=1), jnp.concatenate([sin_a] * 4, axis=1), rot_a, HEAD_DIM)
    dq_ref[0] = (dq * scale).astype(BF16)

    dk4 = _group_norm(zb[:, 0:256], gmat[0:256, 0:256], gains_ref[3:4, 0:256], HEAD_DIM)
    dk4 = _rope(dk4, jnp.concatenate([cos_a] * 2, axis=1), jnp.concatenate([sin_a] * 2, axis=1), rot_a, HEAD_DIM)
    dk4_ref[0] = dk4.astype(BF16)

    cos_i2 = jnp.concatenate([cos_i] * 2, axis=1)
    sin_i2 = jnp.concatenate([sin_i] * 2, axis=1)
    ik8_ref[0] = _rope(zb[:, 256:512], cos_i2, sin_i2, rot_i, IDX_DIM).astype(BF16)
    dv2_ref[0] = zb[:, 512:640].astype(BF16)
    iq_ref[0] = _rope(zb[:, 640:896], cos_i2, sin_i2, rot_i, IDX_DIM).astype(BF16)

    sm = zb[:, 896:1024]
    v = sm + fbias_ref[...]
    logf = jnp.minimum(v, 0.0) - jnp.log1p(jnp.exp(-jnp.abs(v)))
    row = lax.broadcasted_iota(jnp.int32, (tm, tm), 0)
    col = lax.broadcasted_iota(jnp.int32, (tm, tm), 1)
    tri = jnp.where(row >= col, 1.0, 0.0).astype(BF16)
    hi, mid, lo = _split3(logf)
    csum = (jnp.dot(tri, hi, preferred_element_type=F32) + jnp.dot(tri, mid, preferred_element_type=F32)
            + jnp.dot(tri, lo, preferred_element_type=F32)) + carry_ref[...]
    carry_ref[...] = csum[tm - 1:tm, :]
    lane = lax.broadcasted_iota(jnp.int32, sm.shape, 1)
    small_ref[0] = jnp.where(lane < FOX_HEADS, csum, sm * ((IDX_HEADS * IDX_DIM) ** -0.5))

    nf = -csum
    t0 = nf.astype(BF16).astype(F32)
    r1 = nf - t0
    t1 = r1.astype(BF16).astype(F32)
    t2 = r1 - t1
    aug = jnp.where(lane < FOX_HEADS, t0,
                    jnp.where(lane < 2 * FOX_HEADS, pltpu.roll(t1, FOX_HEADS, axis=1),
                              jnp.where(lane < 3 * FOX_HEADS, pltpu.roll(t2, 2 * FOX_HEADS, axis=1), 0.0)))
    faug_ref[0] = aug.astype(BF16)


def _mix_in_call(x, mod, g, pos, wa, wb, gmat, gains, freq, fbias):
    bsz, s, d = x.shape
    tm = min(TOKEN_TILE, s)

    def tok(w):
        return pl.BlockSpec((1, tm, w), lambda b, i: (b, i, 0))

    widths = (512, 512, 512, 512, 256, 128, 256, 256, 128)
    out_shapes = [jax.ShapeDtypeStruct((bsz, s, w), BF16) for w in widths]
    out_shapes.append(jax.ShapeDtypeStruct((bsz, s, LANES), F32))
    return pl.pallas_call(
        _mix_in_kernel,
        grid=(bsz, s // tm),
        in_specs=[tok(d),
                  pl.BlockSpec((1, 3 * N_SUBLAYERS, d), lambda b, i: (b, 0, 0)),
                  _const_spec((N_SUBLAYERS, d)),
                  tok(LANES),
                  _const_spec(wa.shape), _const_spec(wb.shape), _const_spec(gmat.shape),
                  _const_spec(gains.shape), _const_spec(freq.shape), _const_spec(fbias.shape)],
        out_specs=[tok(w) for w in widths] + [tok(LANES)],
        out_shape=out_shapes,
        scratch_shapes=[pltpu.VMEM((1, LANES), F32)],
        compiler_params=_cparams(("arbitrary", "arbitrary")),
        name="mix_in",
    )(x, mod, g, pos, wa, wb, gmat, gains, freq, fbias)


def _fox_kernel(q_ref, k_ref, faug_ref, vt_ref, hm_ref, fsel_ref, o_ref, qf_ref, m_ref, l_ref, acc_ref):
    i = pl.program_id(2)
    tq = q_ref.shape[1]
    tk = tq
    q = q_ref[0]
    for a in range(2):
        sel_row = fsel_ref[0, a:a + 1, :]
        qf_ref[a * tq:(a + 1) * tq, 0:LANES] = q * hm_ref[a:a + 1, :]
        qf_ref[a * tq:(a + 1) * tq, LANES:2 * LANES] = jnp.broadcast_to(sel_row, (tq, LANES))
    m_ref[...] = jnp.full_like(m_ref, -jnp.inf)
    l_ref[...] = jnp.zeros_like(l_ref)
    acc_ref[...] = jnp.zeros_like(acc_ref)

    def block(j, masked):
        start = pl.multiple_of(j * tk, tk)
        kf = jnp.concatenate([k_ref[0, pl.ds(start, tk), :], faug_ref[0, pl.ds(start, tk), :]], axis=1)
        s = lax.dot_general(kf, qf_ref[...], _NT, preferred_element_type=F32)
        if masked:
            krow = lax.broadcasted_iota(jnp.int32, s.shape, 0)
            qcol = lax.broadcasted_iota(jnp.int32, s.shape, 1) % tq
            s = jnp.where(krow <= qcol, s, -jnp.inf)
        m_old = m_ref[...]
        m_new = jnp.maximum(m_old, jnp.max(s, axis=0, keepdims=True))
        p = jnp.exp(s - m_new)
        alpha = jnp.exp(m_old - m_new)
        l_ref[...] = alpha * l_ref[...] + jnp.sum(p, axis=0, keepdims=True)
        m_ref[...] = m_new
        pb = p.astype(BF16)
        for a in range(2):
            vt = vt_ref[0, a * HEAD_DIM:(a + 1) * HEAD_DIM, pl.ds(start, tk)]
            pv = jnp.dot(vt, pb[:, a * tq:(a + 1) * tq], preferred_element_type=F32)
            acc_ref[a] = alpha[:, a * tq:(a + 1) * tq] * acc_ref[a] + pv

    def body(j, c):
        block(j, False)
        return c

    lax.fori_loop(0, i, body, 0)
    block(i, True)
    inv = 1.0 / l_ref[...]
    for a in range(2):
        o_ref[0, a * HEAD_DIM:(a + 1) * HEAD_DIM, :] = (acc_ref[a] * inv[:, a * tq:(a + 1) * tq]).astype(BF16)


def _fox_call(fq, fk, faug, fvt, hm2, fsel):
    bsz, s, _ = fq.shape
    tq = min(FOX_TILE, s)
    pairs = FOX_HEADS // 2
    return pl.pallas_call(
        _fox_kernel,
        grid=(bsz, pairs, s // tq),
        in_specs=[pl.BlockSpec((1, tq, LANES), lambda b, h, i: (b, i, h)),
                  pl.BlockSpec((1, s, LANES), lambda b, h, i: (b, 0, h)),
                  pl.BlockSpec((1, s, LANES), lambda b, h, i: (b, 0, 0)),
                  pl.BlockSpec((1, LANES, s), lambda b, h, i: (b, h, 0)),
                  pl.BlockSpec((2, LANES), lambda b, h, i: (0, 0)),
                  pl.BlockSpec((1, 2, LANES), lambda b, h, i: (h, 0, 0))],
        out_specs=pl.BlockSpec((1, LANES, tq), lambda b, h, i: (b, h, i)),
        out_shape=jax.ShapeDtypeStruct((bsz, FOX_WIDTH, s), BF16),
        scratch_shapes=[pltpu.VMEM((2 * tq, 2 * LANES), BF16),
                        pltpu.VMEM((1, 2 * tq), F32), pltpu.VMEM((1, 2 * tq), F32),
                        pltpu.VMEM((2, HEAD_DIM, tq), F32)],
        compiler_params=_cparams(("parallel", "parallel", "arbitrary")),
        name="fox_attn",
    )(fq, fk, faug, fvt, hm2, fsel)


def _dsa_kernel(dq_ref, iq_ref, wt_ref, dk4_ref, dvt_ref, ik8_ref, hm8_ref, hm4_ref, o_ref,
                iq8_ref, q8_ref, keys_ref, m_ref, l_ref, acc_ref, *, top_k):
    i = pl.program_id(1)
    tq = dq_ref.shape[1]
    s_len = dk4_ref.shape[1]
    tk = min(DSA_K_TILE, s_len)
    q_start = i * tq
    nkb = (q_start + tq + tk - 1) // tk

    iq = iq_ref[0]
    dq = dq_ref[0]
    half = dq.shape[1] // 2
    for h in range(IDX_HEADS):
        iq8_ref[h * tq:(h + 1) * tq, :] = iq * hm8_ref[h:h + 1, :]
        q8_ref[h * tq:(h + 1) * tq, :] = dq[:, (h // 4) * half:(h // 4 + 1) * half] * hm4_ref[h % 4:h % 4 + 1, :]

    krow = lax.broadcasted_iota(jnp.int32, (tk, tq), 0)
    qcol = lax.broadcasted_iota(jnp.int32, (tk, tq), 1)
    adm_end = q_start + (qcol // CHUNK + 1) * CHUNK
    wt = wt_ref[0, :, pl.ds(pl.multiple_of(q_start, tq), tq)]

    def score_block(j, c):
        start = pl.multiple_of(j * tk, tk)
        d = lax.dot_general(ik8_ref[0, pl.ds(start, tk), :], iq8_ref[...], _NT, preferred_element_type=F32)
        sc = jnp.zeros((tk, tq), F32)
        for h in range(IDX_HEADS):
            sc = sc + jnp.maximum(d[:, h * tq:(h + 1) * tq], 0.0) * wt[IDX_HEADS + h:IDX_HEADS + h + 1, :]
        sc = jnp.where(sc == 0.0, 0.0, sc)
        bits = lax.bitcast_convert_type(sc, jnp.int32)
        key = bits ^ ((bits >> 31) & 0x7FFFFFFF)
        keys_ref[pl.ds(start, tk), :] = jnp.where(krow + start < adm_end, key, INT_MIN)
        return c

    lax.fori_loop(0, nkb, score_block, 0)

    def count(pred):
        def body(j, acc):
            start = pl.multiple_of(j * tk, tk)
            hit = jnp.where(pred(keys_ref[pl.ds(start, tk), :]), 1, 0)
            return acc + jnp.sum(hit, axis=0, keepdims=True)
        return lax.fori_loop(0, nkb, body, jnp.zeros((1, tq), jnp.int32))

    def bisect(it, thr):
        cand = thr + jnp.left_shift(jnp.int32(1), 31 - it)
        cnt = count(lambda kblk: kblk >= cand)
        return jnp.where(cnt >= top_k, cand, thr)

    thr = lax.fori_loop(0, 32, bisect, jnp.full((1, tq), INT_MIN, jnp.int32))
    thr = jnp.maximum(thr, INT_MIN + 1)
    n_ge = count(lambda kblk: kblk >= thr)

    @pl.when(jnp.max(n_ge) > top_k)
    def _():
        n_gt = count(lambda kblk: kblk > thr)
        need = (top_k - n_gt).astype(F32)
        r2 = lax.broadcasted_iota(jnp.int32, (tk, tk), 0)
        c2 = lax.broadcasted_iota(jnp.int32, (tk, tk), 1)
        lower = jnp.where(c2 < r2, 1.0, 0.0).astype(BF16)

        def demote(j, seen):
            start = pl.multiple_of(j * tk, tk)
            kblk = keys_ref[pl.ds(start, tk), :]
            eq = kblk == thr
            eqf = jnp.where(eq, 1.0, 0.0)
            before = jnp.dot(lower, eqf.astype(BF16), preferred_element_type=F32) + seen
            keys_ref[pl.ds(start, tk), :] = jnp.where(eq & (before >= need), thr - 1, kblk)
            return seen + jnp.sum(eqf, axis=0, keepdims=True)

        lax.fori_loop(0, nkb, demote, jnp.zeros((1, tq), F32))

    m_ref[...] = jnp.full_like(m_ref, -jnp.inf)
    l_ref[...] = jnp.zeros_like(l_ref)
    acc_ref[...] = jnp.zeros_like(acc_ref)

    def attend(j, c):
        start = pl.multiple_of(j * tk, tk)
        bias = jnp.where(keys_ref[pl.ds(start, tk), :] >= thr, 0.0, -jnp.inf)
        s = lax.dot_general(dk4_ref[0, pl.ds(start, tk), :], q8_ref[...], _NT, preferred_element_type=F32)
        s = s + jnp.concatenate([bias] * DSA_HEADS, axis=1)
        m_old = m_ref[...]
        m_new = jnp.maximum(m_old, jnp.max(s, axis=0, keepdims=True))
        m_safe = jnp.where(m_new == -jnp.inf, 0.0, m_new)
        p = jnp.exp(s - m_safe)
        alpha = jnp.exp(m_old - m_safe)
        l_ref[...] = alpha * l_ref[...] + jnp.sum(p, axis=0, keepdims=True)
        m_ref[...] = m_new
        pv = jnp.dot(dvt_ref[0, :, pl.ds(start, tk)], p.astype(BF16), preferred_element_type=F32)
        acc_ref[...] = alpha * acc_ref[...] + pv
        return c

    lax.fori_loop(0, nkb, attend, 0)
    out = acc_ref[...] / l_ref[...]
    for h in range(DSA_HEADS):
        o_ref[0, h * HEAD_DIM:(h + 1) * HEAD_DIM, :] = out[:, h * tq:(h + 1) * tq].astype(BF16)


def _dsa_call(dq, iq, wt, dk4, dvt, ik8, hm8, hm4, top_k):
    bsz, s, _ = dq.shape
    tq = min(DSA_Q_TILE, s)

    def tok(w):
        return pl.BlockSpec((1, tq, w), lambda b, i: (b, i, 0))

    def seq(w):
        return pl.BlockSpec((1, s, w), lambda b, i: (b, 0, 0))

    def seqt(r):
        return pl.BlockSpec((1, r, s), lambda b, i: (b, 0, 0))

    return pl.pallas_call(
        functools.partial(_dsa_kernel, top_k=top_k),
        grid=(bsz, s // tq),
        in_specs=[tok(DSA_WIDTH), tok(IDX_WIDTH), seqt(2 * IDX_HEADS), seq(256), seqt(HEAD_DIM), seq(256),
                  pl.BlockSpec(hm8.shape, lambda b, i: (0, 0)), pl.BlockSpec(hm4.shape, lambda b, i: (0, 0))],
        out_specs=pl.BlockSpec((1, DSA_WIDTH, tq), lambda b, i: (b, 0, i)),
        out_shape=jax.ShapeDtypeStruct((bsz, DSA_WIDTH, s), BF16),
        scratch_shapes=[pltpu.VMEM((IDX_HEADS * tq, 256), BF16), pltpu.VMEM((DSA_HEADS * tq, 256), BF16),
                        pltpu.VMEM((s, tq), jnp.int32),
                        pltpu.VMEM((1, DSA_HEADS * tq), F32), pltpu.VMEM((1, DSA_HEADS * tq), F32),
                        pltpu.VMEM((HEAD_DIM, DSA_HEADS * tq), F32)],
        compiler_params=_cparams(("parallel", "arbitrary")),
        name="dsa_attn",
    )(dq, iq, wt, dk4, dvt, ik8, hm8, hm4)


def _merge_kernel(x_ref, mod_ref, g_ref, ya_ref, yb_ref, wg_ref, wfa_ref, wfb_ref, wo_ref, o_ref):
    x = x_ref[0]
    d = x.shape[1]
    h = _modulated(x, mod_ref, g_ref, 1).astype(BF16)
    zg = jnp.dot(h, wg_ref[...], preferred_element_type=F32)
    pa = jnp.dot(ya_ref[0], wfa_ref[...], preferred_element_type=F32)
    pb = jnp.dot(yb_ref[0], wfb_ref[...], preferred_element_type=F32)
    merged = jax.nn.sigmoid(zg[:, :d]) * pa + jax.nn.sigmoid(zg[:, d:]) * pb
    y = jnp.dot(merged.astype(BF16), wo_ref[...], preferred_element_type=F32)
    o_ref[0] = x + mod_ref[0, 5:6, :] * y


def _merge_call(x, mod, g, ya, yb, wg, wfa, wfb, wo):
    bsz, s, d = x.shape
    tm = min(TOKEN_TILE, s)

    def tok(w):
        return pl.BlockSpec((1, tm, w), lambda b, i: (b, i, 0))

    return pl.pallas_call(
        _merge_kernel,
        grid=(bsz, s // tm),
        in_specs=[tok(d),
                  pl.BlockSpec((1, 3 * N_SUBLAYERS, d), lambda b, i: (b, 0, 0)),
                  _const_spec((N_SUBLAYERS, d)),
                  tok(FOX_WIDTH), tok(DSA_WIDTH),
                  _const_spec(wg.shape), _const_spec(wfa.shape), _const_spec(wfb.shape), _const_spec(wo.shape)],
        out_specs=tok(d),
        out_shape=jax.ShapeDtypeStruct(x.shape, F32),
        compiler_params=_cparams(("parallel", "parallel")),
        name="merge_out",
    )(x, mod, g, ya, yb, wg, wfa, wfb, wo)


def _head_mask(n_heads, width):
    return jnp.asarray(np.kron(np.eye(n_heads), np.ones((1, width))), BF16)


def _rope_freq_row(rot_dim, period):
    inv_freq = ROPE_THETA ** (-jnp.arange(0, rot_dim, 2, dtype=F32) / rot_dim)
    half = rot_dim // 2
    head = jnp.concatenate([-inv_freq, inv_freq, jnp.zeros((period - 2 * half,), F32)])
    return jnp.tile(head, LANES // period)


def kernel(x, c, positions, ada_w, ada_b, norm_g, ffn1_w1, ffn1_w3, ffn1_w2, w_in, fox_f_bias, fox_qk_g, dsa_qk_g,
           w_br_fox, w_br_dsa, w_out, ffn2_w1, ffn2_w3, ffn2_w2):
    bsz, s, d = x.shape
    top_k = min(TOPK_MAX, s // 4)
    depth = ada_w.shape[0]
    pos = jnp.broadcast_to(positions.astype(F32)[:, :, None], (bsz, s, LANES))
    freq = jnp.stack([_rope_freq_row(HEAD_DIM // ROPE_FRACTION, HEAD_DIM),
                      _rope_freq_row(IDX_DIM // ROPE_FRACTION, IDX_DIM)])
    gmat = jnp.asarray(np.kron(np.eye(FOX_HEADS), np.ones((HEAD_DIM, HEAD_DIM))), BF16)
    hm2 = _head_mask(2, HEAD_DIM)
    hm4 = _head_mask(4, HEAD_DIM)
    hm8 = _head_mask(IDX_HEADS, IDX_DIM)
    fsel = jnp.asarray(np.tile(np.eye(FOX_HEADS), (1, LANES // FOX_HEADS))
                       * (np.arange(LANES) < 3 * FOX_HEADS), BF16).reshape(FOX_HEADS // 2, 2, LANES)
    o_fq, o_fk, o_fv = 0, FOX_WIDTH, 2 * FOX_WIDTH
    o_ff = 3 * FOX_WIDTH
    o_dq = o_ff + FOX_HEADS
    o_dk = o_dq + DSA_WIDTH
    o_dv = o_dk + HEAD_DIM
    o_iq = o_dv + HEAD_DIM
    o_ik = o_iq + IDX_WIDTH
    o_iw = o_ik + IDX_DIM
    o_ga = o_iw + IDX_HEADS

    for l in range(depth):
        mod = _ada_call(c, ada_w[l], ada_b[l]).reshape(bsz, 3 * N_SUBLAYERS, d)
        g = norm_g[l]
        x = _ffn_call(x, mod, g, ffn1_w1[l].astype(BF16), ffn1_w3[l].astype(BF16), ffn1_w2[l].astype(BF16), 0)

        w = w_in[l]
        wa = jnp.concatenate([w[:, o_fq:o_fq + 3 * FOX_WIDTH], w[:, o_dq:o_dq + DSA_WIDTH]], axis=1).astype(BF16)
        pad = jnp.zeros((d, LANES - FOX_HEADS - IDX_HEADS), F32)
        wb = jnp.concatenate([jnp.tile(w[:, o_dk:o_dk + HEAD_DIM], (1, 4)),
                              jnp.tile(w[:, o_ik:o_ik + IDX_DIM], (1, IDX_HEADS)),
                              jnp.tile(w[:, o_dv:o_dv + HEAD_DIM], (1, 2)),
                              w[:, o_iq:o_iq + IDX_WIDTH],
                              w[:, o_ff:o_ff + FOX_HEADS], w[:, o_iw:o_iw + IDX_HEADS], pad], axis=1).astype(BF16)
        wg = w[:, o_ga:o_ga + 2 * d].astype(BF16)
        gains = jnp.stack([jnp.tile(fox_qk_g[l, 0], FOX_HEADS), jnp.tile(fox_qk_g[l, 1], FOX_HEADS),
                           jnp.tile(dsa_qk_g[l, 0], DSA_HEADS), jnp.tile(dsa_qk_g[l, 1], DSA_HEADS)]).astype(F32)
        fbias = jnp.concatenate([fox_f_bias[l].astype(F32), jnp.zeros((LANES - FOX_HEADS,), F32)]).reshape(1, LANES)

        fq, fk, fv, dq, dk4, dv2, ik8, iq, faug, small = _mix_in_call(x, mod, g, pos, wa, wb, gmat, gains, freq, fbias)
        fvt = jnp.transpose(fv, (0, 2, 1))
        dvt = jnp.transpose(dv2[:, :, :HEAD_DIM], (0, 2, 1))
        wt = jnp.transpose(small[:, :, :2 * IDX_HEADS], (0, 2, 1))
        yat = _fox_call(fq, fk, faug, fvt, hm2, fsel)
        ybt = _dsa_call(dq, iq, wt, dk4, dvt, ik8, hm8, hm4, top_k)
        ya = jnp.transpose(yat, (0, 2, 1))
        yb = jnp.transpose(ybt, (0, 2, 1))
        x = _merge_call(x, mod, g, ya, yb, wg, w_br_fox[l].astype(BF16), w_br_dsa[l].astype(BF16),
                        w_out[l].astype(BF16))
        x = _ffn_call(x, mod, g, ffn2_w1[l].astype(BF16), ffn2_w3[l].astype(BF16), ffn2_w2[l].astype(BF16), 2)
    return x
```

```python
import functools

import numpy as np
import jax
import jax.numpy as jnp
from jax import lax
from jax.experimental import pallas as pl
from jax.experimental.pallas import tpu as pltpu

D_MODEL = 1024
CHUNK = 64
HEAD_DIM = 64
FOX_HEADS = 8
DSA_HEADS = 8
IDX_HEADS = 8
IDX_DIM = 32
TOPK_MAX = 256
ROPE_THETA = 500000.0
ROPE_FRACTION = 4
D_FF = 2816
N_SUBLAYERS = 3
NORM_EPS = 1e-6
FOX_WIDTH = FOX_HEADS * HEAD_DIM
DSA_WIDTH = DSA_HEADS * HEAD_DIM
IDX_WIDTH = IDX_HEADS * IDX_DIM

LANES = 128
VMEM_LIMIT_BYTES = 56 * 1024 * 1024

TOKEN_TILE = 512
FOX_TILE = 1024
DSA_Q_TILE = 256
DSA_K_TILE = 1024
LANE_GROUP = 256
COUNT_ROWS = 32

INT_MIN = -2 ** 31
F32 = jnp.float32
BF16 = jnp.bfloat16
_NT = (((1,), (1,)), ((), ()))


def _cparams(sem):
    return pltpu.CompilerParams(dimension_semantics=sem, vmem_limit_bytes=VMEM_LIMIT_BYTES)


def _const_spec(shape):
    nd = len(shape)
    return pl.BlockSpec(shape, lambda *_: (0,) * nd, pipeline_mode=pl.Buffered(1))


def _modulated(x, mod_ref, g_ref, sub):
    ms = jnp.mean(x * x, axis=-1, keepdims=True)
    y = x * lax.rsqrt(ms + NORM_EPS) * g_ref[sub:sub + 1, :]
    shift = mod_ref[0, 3 * sub:3 * sub + 1, :]
    scale = mod_ref[0, 3 * sub + 1:3 * sub + 2, :]
    return y * (1.0 + scale) + shift


def _ada_kernel(c_ref, w_ref, b_ref, o_ref):
    c = c_ref[...]
    cond = c * jax.nn.sigmoid(c)
    o_ref[...] = jnp.dot(cond, w_ref[...], preferred_element_type=F32,
                         precision=lax.Precision.HIGHEST) + b_ref[...]


def _ada_call(c, w, b):
    bsz, d = c.shape
    n = w.shape[1]
    tn = 1152
    return pl.pallas_call(
        _ada_kernel,
        grid=(n // tn,),
        in_specs=[pl.BlockSpec((bsz, d), lambda j: (0, 0)),
                  pl.BlockSpec((d, tn), lambda j: (0, j)),
                  pl.BlockSpec((1, tn), lambda j: (0, j))],
        out_specs=pl.BlockSpec((bsz, tn), lambda j: (0, j)),
        out_shape=jax.ShapeDtypeStruct((bsz, n), F32),
        compiler_params=_cparams(("arbitrary",)),
        name="ada_ln",
    )(c, w, b.reshape(1, n))


def _ffn_kernel(x_ref, mod_ref, g_ref, w1_ref, w3_ref, w2_ref, o_ref, *, sub):
    x = x_ref[0]
    h = _modulated(x, mod_ref, g_ref, sub).astype(BF16)
    a = jnp.dot(h, w1_ref[...], preferred_element_type=F32)
    b = jnp.dot(h, w3_ref[...], preferred_element_type=F32)
    act = (a * jax.nn.sigmoid(a) * b).astype(BF16)
    y = jnp.dot(act, w2_ref[...], preferred_element_type=F32)
    gate = mod_ref[0, 3 * sub + 2:3 * sub + 3, :]
    o_ref[0] = x + (0.5 * gate) * y


def _ffn_call(x, mod, g, w1, w3, w2, sub):
    bsz, s, d = x.shape
    tm = min(TOKEN_TILE, s)
    f = w1.shape[1]
    tok = pl.BlockSpec((1, tm, d), lambda b, i: (b, i, 0))
    return pl.pallas_call(
        functools.partial(_ffn_kernel, sub=sub),
        grid=(bsz, s // tm),
        in_specs=[tok,
                  pl.BlockSpec((1, 3 * N_SUBLAYERS, d), lambda b, i: (b, 0, 0)),
                  _const_spec((N_SUBLAYERS, d)),
                  _const_spec((d, f)), _const_spec((d, f)), _const_spec((f, d))],
        out_specs=tok,
        out_shape=jax.ShapeDtypeStruct(x.shape, F32),
        compiler_params=_cparams(("parallel", "parallel")),
        name=f"ffn{sub}",
    )(x, mod, g, w1, w3, w2)


def _split3(v):
    hi = v.astype(BF16)
    r = v - hi.astype(F32)
    mid = r.astype(BF16)
    lo = (r - mid.astype(F32)).astype(BF16)
    return hi, mid, lo


def _group_norm(z, gmat, gain, group):
    sq = z * z
    hi = sq.astype(BF16)
    lo = (sq - hi.astype(F32)).astype(BF16)
    ssq = jnp.dot(hi, gmat, preferred_element_type=F32) + jnp.dot(lo, gmat, preferred_element_type=F32)
    return z * lax.rsqrt(ssq * (1.0 / group) + NORM_EPS) * gain


def _rope(z, cos, sin, half, period):
    width = z.shape[1]
    lane = lax.broadcasted_iota(jnp.int32, z.shape, 1) % period
    up = pltpu.roll(z, width - half, axis=1)
    dn = pltpu.roll(z, half, axis=1)
    return z * cos + jnp.where(lane < half, up, dn) * sin


def _mix_in_kernel(x_ref, mod_ref, g_ref, pos_ref, wa_ref, wb_ref, gmat_ref, gains_ref, freq_ref, fbias_ref,
                   fq_ref, fk_ref, fv_ref, dq_ref, dk4_ref, dv2_ref, ik8_ref, iq_ref, faug_ref, small_ref,
                   carry_ref):
    @pl.when(pl.program_id(1) == 0)
    def _():
        carry_ref[...] = jnp.zeros_like(carry_ref)

    x = x_ref[0]
    tm = x.shape[0]
    h = _modulated(x, mod_ref, g_ref, 1).astype(BF16)
    za = jnp.dot(h, wa_ref[...], preferred_element_type=F32)
    zb = jnp.dot(h, wb_ref[...], preferred_element_type=F32)

    gmat = gmat_ref[...]
    scale = HEAD_DIM ** -0.5
    fq = _group_norm(za[:, 0:512], gmat, gains_ref[0:1, :], HEAD_DIM) * scale
    fk = _group_norm(za[:, 512:1024], gmat, gains_ref[1:2, :], HEAD_DIM)
    fq_ref[0] = fq.astype(BF16)
    fk_ref[0] = fk.astype(BF16)
    fv_ref[0] = za[:, 1024:1536].astype(BF16)

    pos = pos_ref[0]
    ang_a = pos * freq_ref[0:1, :]
    ang_i = pos * freq_ref[1:2, :]
    cos_a, sin_a = jnp.cos(ang_a), jnp.sin(ang_a)
    cos_i, sin_i = jnp.cos(ang_i), jnp.sin(ang_i)
    rot_a = HEAD_DIM // ROPE_FRACTION // 2
    rot_i = IDX_DIM // ROPE_FRACTION // 2

    dq = _group_norm(za[:, 1536:2048], gmat, gains_ref[2:3, :], HEAD_DIM)
    dq = _rope(dq, jnp.concatenate([cos_a] * 4, axis=1), jnp.concatenate([sin_a] * 4, axis=1), rot_a, HEAD_DIM)
    dq_ref[0] = (dq * scale).astype(BF16)

    dk4 = _group_norm(zb[:, 0:256], gmat[0:256, 0:256], gains_ref[3:4, 0:256], HEAD_DIM)
    dk4 = _rope(dk4, jnp.concatenate([cos_a] * 2, axis=1), jnp.concatenate([sin_a] * 2, axis=1), rot_a, HEAD_DIM)
    dk4_ref[0] = dk4.astype(BF16)

    cos_i2 = jnp.concatenate([cos_i] * 2, axis=1)
    sin_i2 = jnp.concatenate([sin_i] * 2, axis=1)
    ik8_ref[0] = _rope(zb[:, 256:512], cos_i2, sin_i2, rot_i, IDX_DIM).astype(BF16)
    dv2_ref[0] = zb[:, 512:640].astype(BF16)
    iq_ref[0] = _rope(zb[:, 640:896], cos_i2, sin_i2, rot_i, IDX_DIM).astype(BF16)

    sm = zb[:, 896:1024]
    v = sm + fbias_ref[...]
    logf = jnp.minimum(v, 0.0) - jnp.log1p(jnp.exp(-jnp.abs(v)))
    row = lax.broadcasted_iota(jnp.int32, (tm, tm), 0)
    col = lax.broadcasted_iota(jnp.int32, (tm, tm), 1)
    tri = jnp.where(row >= col, 1.0, 0.0).astype(BF16)
    hi, mid, lo = _split3(logf)
    csum = (jnp.dot(tri, hi, preferred_element_type=F32) + jnp.dot(tri, mid, preferred_element_type=F32)
            + jnp.dot(tri, lo, preferred_element_type=F32)) + carry_ref[...]
    carry_ref[...] = csum[tm - 1:tm, :]
    lane = lax.broadcasted_iota(jnp.int32, sm.shape, 1)
    small_ref[0] = jnp.where(lane < FOX_HEADS, csum, sm * ((IDX_HEADS * IDX_DIM) ** -0.5))

    nf = -csum
    t0 = nf.astype(BF16).astype(F32)
    r1 = nf - t0
    t1 = r1.astype(BF16).astype(F32)
    t2 = r1 - t1
    aug = jnp.where(lane < FOX_HEADS, t0,
                    jnp.where(lane < 2 * FOX_HEADS, pltpu.roll(t1, FOX_HEADS, axis=1),
                              jnp.where(lane < 3 * FOX_HEADS, pltpu.roll(t2, 2 * FOX_HEADS, axis=1), 0.0)))
    faug_ref[0] = aug.astype(BF16)


def _mix_in_call(x, mod, g, pos, wa, wb, gmat, gains, freq, fbias):
    bsz, s, d = x.shape
    tm = min(TOKEN_TILE, s)

    def tok(w):
        return pl.BlockSpec((1, tm, w), lambda b, i: (b, i, 0))

    widths = (512, 512, 512, 512, 256, 128, 256, 256, 128)
    out_shapes = [jax.ShapeDtypeStruct((bsz, s, w), BF16) for w in widths]
    out_shapes.append(jax.ShapeDtypeStruct((bsz, s, LANES), F32))
    return pl.pallas_call(
        _mix_in_kernel,
        grid=(bsz, s // tm),
        in_specs=[tok(d),
                  pl.BlockSpec((1, 3 * N_SUBLAYERS, d), lambda b, i: (b, 0, 0)),
                  _const_spec((N_SUBLAYERS, d)),
                  tok(LANES),
                  _const_spec(wa.shape), _const_spec(wb.shape), _const_spec(gmat.shape),
                  _const_spec(gains.shape), _const_spec(freq.shape), _const_spec(fbias.shape)],
        out_specs=[tok(w) for w in widths] + [tok(LANES)],
        out_shape=out_shapes,
        scratch_shapes=[pltpu.VMEM((1, LANES), F32)],
        compiler_params=_cparams(("arbitrary", "arbitrary")),
        name="mix_in",
    )(x, mod, g, pos, wa, wb, gmat, gains, freq, fbias)


def _fox_kernel(q_ref, k_ref, faug_ref, vt_ref, hm_ref, fsel_ref, o_ref, qf_ref, m_ref, l_ref, acc_ref):
    i = pl.program_id(2)
    tq = q_ref.shape[1]
    tk = tq
    q = q_ref[0]
    for a in range(2):
        sel_row = fsel_ref[0, a:a + 1, :]
        qf_ref[a * tq:(a + 1) * tq, 0:LANES] = q * hm_ref[a:a + 1, :]
        qf_ref[a * tq:(a + 1) * tq, LANES:2 * LANES] = jnp.broadcast_to(sel_row, (tq, LANES))
    m_ref[...] = jnp.full_like(m_ref, -jnp.inf)
    l_ref[...] = jnp.zeros_like(l_ref)
    acc_ref[...] = jnp.zeros_like(acc_ref)

    def block(j, masked):
        start = pl.multiple_of(j * tk, tk)
        kf = jnp.concatenate([k_ref[0, pl.ds(start, tk), :], faug_ref[0, pl.ds(start, tk), :]], axis=1)
        n_groups = 2 * tq // LANE_GROUP

        def scores(g):
            s = lax.dot_general(kf, qf_ref[g * LANE_GROUP:(g + 1) * LANE_GROUP, :], _NT,
                                preferred_element_type=F32)
            if masked:
                krow = lax.broadcasted_iota(jnp.int32, s.shape, 0)
                qcol = lax.broadcasted_iota(jnp.int32, s.shape, 1) + (g * LANE_GROUP) % tq
                s = jnp.where(krow <= qcol, s, -jnp.inf)
            return s

        s_next = scores(0)
        for g in range(n_groups):
            s = s_next
            if g + 1 < n_groups:
                s_next = scores(g + 1)
            sl = slice(g * LANE_GROUP, (g + 1) * LANE_GROUP)
            m_old = m_ref[:, sl]
            m_new = jnp.maximum(m_old, jnp.max(s, axis=0, keepdims=True))
            p = jnp.exp(s - m_new)
            alpha = jnp.exp(m_old - m_new)
            l_ref[:, sl] = alpha * l_ref[:, sl] + jnp.sum(p, axis=0, keepdims=True)
            m_ref[:, sl] = m_new
            a = (g * LANE_GROUP) // tq
            vt = vt_ref[0, a * HEAD_DIM:(a + 1) * HEAD_DIM, pl.ds(start, tk)]
            acc_ref[:, sl] = alpha * acc_ref[:, sl] + jnp.dot(vt, p.astype(BF16), preferred_element_type=F32)

    def body(j, c):
        block(j, False)
        return c

    lax.fori_loop(0, i, body, 0)
    block(i, True)
    out = acc_ref[...] / l_ref[...]
    for a in range(2):
        o_ref[0, a * HEAD_DIM:(a + 1) * HEAD_DIM, :] = out[:, a * tq:(a + 1) * tq].astype(BF16)


def _fox_call(fq, fk, faug, fvt, hm2, fsel):
    bsz, s, _ = fq.shape
    tq = min(FOX_TILE, s)
    pairs = FOX_HEADS // 2
    return pl.pallas_call(
        _fox_kernel,
        grid=(bsz, pairs, s // tq),
        in_specs=[pl.BlockSpec((1, tq, LANES), lambda b, h, i: (b, i, h)),
                  pl.BlockSpec((1, s, LANES), lambda b, h, i: (b, 0, h)),
                  pl.BlockSpec((1, s, LANES), lambda b, h, i: (b, 0, 0)),
                  pl.BlockSpec((1, LANES, s), lambda b, h, i: (b, h, 0)),
                  pl.BlockSpec((2, LANES), lambda b, h, i: (0, 0)),
                  pl.BlockSpec((1, 2, LANES), lambda b, h, i: (h, 0, 0))],
        out_specs=pl.BlockSpec((1, LANES, tq), lambda b, h, i: (b, h, i)),
        out_shape=jax.ShapeDtypeStruct((bsz, FOX_WIDTH, s), BF16),
        scratch_shapes=[pltpu.VMEM((2 * tq, 2 * LANES), BF16),
                        pltpu.VMEM((1, 2 * tq), F32), pltpu.VMEM((1, 2 * tq), F32),
                        pltpu.VMEM((HEAD_DIM, 2 * tq), F32)],
        compiler_params=_cparams(("parallel", "parallel", "arbitrary")),
        name="fox_attn",
    )(fq, fk, faug, fvt, hm2, fsel)


def _dsa_kernel(dq_ref, iq_ref, wt_ref, dk4_ref, dvt_ref, ik8_ref, hm8_ref, hm4_ref, o_ref,
                iq8_ref, q8_ref, keys_ref, m_ref, l_ref, acc_ref, *, top_k):
    i = pl.program_id(1)
    tq = dq_ref.shape[1]
    s_len = dk4_ref.shape[1]
    tk = min(DSA_K_TILE, s_len)
    q_start = i * tq
    nkb = (q_start + tq + tk - 1) // tk

    iq = iq_ref[0]
    dq = dq_ref[0]
    half = dq.shape[1] // 2
    for h in range(IDX_HEADS):
        iq8_ref[h * tq:(h + 1) * tq, :] = iq * hm8_ref[h:h + 1, :]
        q8_ref[h * tq:(h + 1) * tq, :] = dq[:, (h // 4) * half:(h // 4 + 1) * half] * hm4_ref[h % 4:h % 4 + 1, :]

    krow = lax.broadcasted_iota(jnp.int32, (tk, tq), 0)
    qcol = lax.broadcasted_iota(jnp.int32, (tk, tq), 1)
    adm_end = q_start + (qcol // CHUNK + 1) * CHUNK
    wt = wt_ref[0, :, pl.ds(pl.multiple_of(q_start, tq), tq)]

    def score_block(j, c):
        start = pl.multiple_of(j * tk, tk)
        d = lax.dot_general(ik8_ref[0, pl.ds(start, tk), :], iq8_ref[...], _NT, preferred_element_type=F32)
        sc = jnp.zeros((tk, tq), F32)
        for h in range(IDX_HEADS):
            sc = sc + jnp.maximum(d[:, h * tq:(h + 1) * tq], 0.0) * wt[IDX_HEADS + h:IDX_HEADS + h + 1, :]
        sc = jnp.where(sc == 0.0, 0.0, sc)
        bits = lax.bitcast_convert_type(sc, jnp.int32)
        key = bits ^ ((bits >> 31) & 0x7FFFFFFF)
        keys_ref[pl.ds(start, tk), :] = jnp.where(krow + start < adm_end, key, INT_MIN)
        return c

    lax.fori_loop(0, nkb, score_block, 0)

    def count(pred):
        def body(j, acc):
            start = pl.multiple_of(j * tk, tk)
            hit = jnp.where(pred(keys_ref[pl.ds(start, tk), :]), 1, 0)
            return acc + jnp.sum(hit.reshape(tk // COUNT_ROWS, COUNT_ROWS, tq), axis=0)
        acc = lax.fori_loop(0, nkb, body, jnp.zeros((COUNT_ROWS, tq), jnp.int32))
        return jnp.sum(acc, axis=0, keepdims=True)

    def bisect(it, thr):
        cand = thr + jnp.left_shift(jnp.int32(1), 31 - it)
        cnt = count(lambda kblk: kblk >= cand)
        return jnp.where(cnt >= top_k, cand, thr)

    thr = lax.fori_loop(0, 32, bisect, jnp.full((1, tq), INT_MIN, jnp.int32))
    thr = jnp.maximum(thr, INT_MIN + 1)
    n_ge = count(lambda kblk: kblk >= thr)

    @pl.when(jnp.max(n_ge) > top_k)
    def _():
        n_gt = count(lambda kblk: kblk > thr)
        need = (top_k - n_gt).astype(F32)
        r2 = lax.broadcasted_iota(jnp.int32, (tk, tk), 0)
        c2 = lax.broadcasted_iota(jnp.int32, (tk, tk), 1)
        lower = jnp.where(c2 < r2, 1.0, 0.0).astype(BF16)

        def demote(j, seen):
            start = pl.multiple_of(j * tk, tk)
            kblk = keys_ref[pl.ds(start, tk), :]
            eq = kblk == thr
            eqf = jnp.where(eq, 1.0, 0.0)
            before = jnp.dot(lower, eqf.astype(BF16), preferred_element_type=F32) + seen
            keys_ref[pl.ds(start, tk), :] = jnp.where(eq & (before >= need), thr - 1, kblk)
            return seen + jnp.sum(eqf, axis=0, keepdims=True)

        lax.fori_loop(0, nkb, demote, jnp.zeros((1, tq), F32))

    m_ref[...] = jnp.full_like(m_ref, -jnp.inf)
    l_ref[...] = jnp.zeros_like(l_ref)
    acc_ref[...] = jnp.zeros_like(acc_ref)

    def attend(j, c):
        start = pl.multiple_of(j * tk, tk)
        bias = jnp.where(keys_ref[pl.ds(start, tk), :] >= thr, 0.0, -jnp.inf)
        bias = jnp.concatenate([bias] * (LANE_GROUP // tq), axis=1)
        kb = dk4_ref[0, pl.ds(start, tk), :]
        vt = dvt_ref[0, :, pl.ds(start, tk)]
        n_groups = DSA_HEADS * tq // LANE_GROUP

        def scores(g):
            return lax.dot_general(kb, q8_ref[g * LANE_GROUP:(g + 1) * LANE_GROUP, :], _NT,
                                   preferred_element_type=F32) + bias

        s_next = scores(0)
        for g in range(n_groups):
            s = s_next
            if g + 1 < n_groups:
                s_next = scores(g + 1)
            sl = slice(g * LANE_GROUP, (g + 1) * LANE_GROUP)
            m_old = m_ref[:, sl]
            m_new = jnp.maximum(m_old, jnp.max(s, axis=0, keepdims=True))
            m_safe = jnp.where(m_new == -jnp.inf, 0.0, m_new)
            p = jnp.exp(s - m_safe)
            alpha = jnp.exp(m_old - m_safe)
            l_ref[:, sl] = alpha * l_ref[:, sl] + jnp.sum(p, axis=0, keepdims=True)
            m_ref[:, sl] = m_new
            acc_ref[:, sl] = alpha * acc_ref[:, sl] + jnp.dot(vt, p.astype(BF16), preferred_element_type=F32)
        return c

    lax.fori_loop(0, nkb, attend, 0)
    out = acc_ref[...] / l_ref[...]
    for h in range(DSA_HEADS):
        o_ref[0, h * HEAD_DIM:(h + 1) * HEAD_DIM, :] = out[:, h * tq:(h + 1) * tq].astype(BF16)


def _dsa_call(dq, iq, wt, dk4, dvt, ik8, hm8, hm4, top_k):
    bsz, s, _ = dq.shape
    tq = min(DSA_Q_TILE, s)

    def tok(w):
        return pl.BlockSpec((1, tq, w), lambda b, i: (b, i, 0))

    def seq(w):
        return pl.BlockSpec((1, s, w), lambda b, i: (b, 0, 0))

    def seqt(r):
        return pl.BlockSpec((1, r, s), lambda b, i: (b, 0, 0))

    return pl.pallas_call(
        functools.partial(_dsa_kernel, top_k=top_k),
        grid=(bsz, s // tq),
        in_specs=[tok(DSA_WIDTH), tok(IDX_WIDTH), seqt(2 * IDX_HEADS), seq(256), seqt(HEAD_DIM), seq(256),
                  pl.BlockSpec(hm8.shape, lambda b, i: (0, 0)), pl.BlockSpec(hm4.shape, lambda b, i: (0, 0))],
        out_specs=pl.BlockSpec((1, DSA_WIDTH, tq), lambda b, i: (b, 0, i)),
        out_shape=jax.ShapeDtypeStruct((bsz, DSA_WIDTH, s), BF16),
        scratch_shapes=[pltpu.VMEM((IDX_HEADS * tq, 256), BF16), pltpu.VMEM((DSA_HEADS * tq, 256), BF16),
                        pltpu.VMEM((s, tq), jnp.int32),
                        pltpu.VMEM((1, DSA_HEADS * tq), F32), pltpu.VMEM((1, DSA_HEADS * tq), F32),
                        pltpu.VMEM((HEAD_DIM, DSA_HEADS * tq), F32)],
        compiler_params=_cparams(("parallel", "arbitrary")),
        name="dsa_attn",
    )(dq, iq, wt, dk4, dvt, ik8, hm8, hm4)


def _merge_kernel(x_ref, mod_ref, g_ref, ya_ref, yb_ref, wg_ref, wfa_ref, wfb_ref, wo_ref, o_ref):
    x = x_ref[0]
    d = x.shape[1]
    h = _modulated(x, mod_ref, g_ref, 1).astype(BF16)
    zg = jnp.dot(h, wg_ref[...], preferred_element_type=F32)
    pa = jnp.dot(ya_ref[0], wfa_ref[...], preferred_element_type=F32)
    pb = jnp.dot(yb_ref[0], wfb_ref[...], preferred_element_type=F32)
    merged = jax.nn.sigmoid(zg[:, :d]) * pa + jax.nn.sigmoid(zg[:, d:]) * pb
    y = jnp.dot(merged.astype(BF16), wo_ref[...], preferred_element_type=F32)
    o_ref[0] = x + mod_ref[0, 5:6, :] * y


def _merge_call(x, mod, g, ya, yb, wg, wfa, wfb, wo):
    bsz, s, d = x.shape
    tm = min(TOKEN_TILE, s)

    def tok(w):
        return pl.BlockSpec((1, tm, w), lambda b, i: (b, i, 0))

    return pl.pallas_call(
        _merge_kernel,
        grid=(bsz, s // tm),
        in_specs=[tok(d),
                  pl.BlockSpec((1, 3 * N_SUBLAYERS, d), lambda b, i: (b, 0, 0)),
                  _const_spec((N_SUBLAYERS, d)),
                  tok(FOX_WIDTH), tok(DSA_WIDTH),
                  _const_spec(wg.shape), _const_spec(wfa.shape), _const_spec(wfb.shape), _const_spec(wo.shape)],
        out_specs=tok(d),
        out_shape=jax.ShapeDtypeStruct(x.shape, F32),
        compiler_params=_cparams(("parallel", "parallel")),
        name="merge_out",
    )(x, mod, g, ya, yb, wg, wfa, wfb, wo)


def _head_mask(n_heads, width):
    return jnp.asarray(np.kron(np.eye(n_heads), np.ones((1, width))), BF16)


def _rope_freq_row(rot_dim, period):
    inv_freq = ROPE_THETA ** (-jnp.arange(0, rot_dim, 2, dtype=F32) / rot_dim)
    half = rot_dim // 2
    head = jnp.concatenate([-inv_freq, inv_freq, jnp.zeros((period - 2 * half,), F32)])
    return jnp.tile(head, LANES // period)


def kernel(x, c, positions, ada_w, ada_b, norm_g, ffn1_w1, ffn1_w3, ffn1_w2, w_in, fox_f_bias, fox_qk_g, dsa_qk_g,
           w_br_fox, w_br_dsa, w_out, ffn2_w1, ffn2_w3, ffn2_w2):
    bsz, s, d = x.shape
    top_k = min(TOPK_MAX, s // 4)
    depth = ada_w.shape[0]
    pos = jnp.broadcast_to(positions.astype(F32)[:, :, None], (bsz, s, LANES))
    freq = jnp.stack([_rope_freq_row(HEAD_DIM // ROPE_FRACTION, HEAD_DIM),
                      _rope_freq_row(IDX_DIM // ROPE_FRACTION, IDX_DIM)])
    gmat = jnp.asarray(np.kron(np.eye(FOX_HEADS), np.ones((HEAD_DIM, HEAD_DIM))), BF16)
    hm2 = _head_mask(2, HEAD_DIM)
    hm4 = _head_mask(4, HEAD_DIM)
    hm8 = _head_mask(IDX_HEADS, IDX_DIM)
    fsel = jnp.asarray(np.tile(np.eye(FOX_HEADS), (1, LANES // FOX_HEADS))
                       * (np.arange(LANES) < 3 * FOX_HEADS), BF16).reshape(FOX_HEADS // 2, 2, LANES)
    o_fq, o_fk, o_fv = 0, FOX_WIDTH, 2 * FOX_WIDTH
    o_ff = 3 * FOX_WIDTH
    o_dq = o_ff + FOX_HEADS
    o_dk = o_dq + DSA_WIDTH
    o_dv = o_dk + HEAD_DIM
    o_iq = o_dv + HEAD_DIM
    o_ik = o_iq + IDX_WIDTH
    o_iw = o_ik + IDX_DIM
    o_ga = o_iw + IDX_HEADS

    for l in range(depth):
        mod = _ada_call(c, ada_w[l], ada_b[l]).reshape(bsz, 3 * N_SUBLAYERS, d)
        g = norm_g[l]
        x = _ffn_call(x, mod, g, ffn1_w1[l].astype(BF16), ffn1_w3[l].astype(BF16), ffn1_w2[l].astype(BF16), 0)

        w = w_in[l]
        wa = jnp.concatenate([w[:, o_fq:o_fq + 3 * FOX_WIDTH], w[:, o_dq:o_dq + DSA_WIDTH]], axis=1).astype(BF16)
        pad = jnp.zeros((d, LANES - FOX_HEADS - IDX_HEADS), F32)
        wb = jnp.concatenate([jnp.tile(w[:, o_dk:o_dk + HEAD_DIM], (1, 4)),
                              jnp.tile(w[:, o_ik:o_ik + IDX_DIM], (1, IDX_HEADS)),
                              jnp.tile(w[:, o_dv:o_dv + HEAD_DIM], (1, 2)),
                              w[:, o_iq:o_iq + IDX_WIDTH],
                              w[:, o_ff:o_ff + FOX_HEADS], w[:, o_iw:o_iw + IDX_HEADS], pad], axis=1).astype(BF16)
        wg = w[:, o_ga:o_ga + 2 * d].astype(BF16)
        gains = jnp.stack([jnp.tile(fox_qk_g[l, 0], FOX_HEADS), jnp.tile(fox_qk_g[l, 1], FOX_HEADS),
                           jnp.tile(dsa_qk_g[l, 0], DSA_HEADS), jnp.tile(dsa_qk_g[l, 1], DSA_HEADS)]).astype(F32)
        fbias = jnp.concatenate([fox_f_bias[l].astype(F32), jnp.zeros((LANES - FOX_HEADS,), F32)]).reshape(1, LANES)

        fq, fk, fv, dq, dk4, dv2, ik8, iq, faug, small = _mix_in_call(x, mod, g, pos, wa, wb, gmat, gains, freq, fbias)
        fvt = jnp.transpose(fv, (0, 2, 1))
        dvt = jnp.transpose(dv2[:, :, :HEAD_DIM], (0, 2, 1))
        wt = jnp.transpose(small[:, :, :2 * IDX_HEADS], (0, 2, 1))
        yat = _fox_call(fq, fk, faug, fvt, hm2, fsel)
        ybt = _dsa_call(dq, iq, wt, dk4, dvt, ik8, hm8, hm4, top_k)
        ya = jnp.transpose(yat, (0, 2, 1))
        yb = jnp.transpose(ybt, (0, 2, 1))
        x = _merge_call(x, mod, g, ya, yb, wg, w_br_fox[l].astype(BF16), w_br_dsa[l].astype(BF16),
                        w_out[l].astype(BF16))
        x = _ffn_call(x, mod, g, ffn2_w1[l].astype(BF16), ffn2_w3[l].astype(BF16), ffn2_w2[l].astype(BF16), 2)
    return x
```

```python
import functools

import numpy as np
import jax
import jax.numpy as jnp
from jax import lax
from jax.experimental import pallas as pl
from jax.experimental.pallas import tpu as pltpu

D_MODEL = 1024
CHUNK = 64
HEAD_DIM = 64
FOX_HEADS = 8
DSA_HEADS = 8
IDX_HEADS = 8
IDX_DIM = 32
TOPK_MAX = 256
ROPE_THETA = 500000.0
ROPE_FRACTION = 4
D_FF = 2816
N_SUBLAYERS = 3
NORM_EPS = 1e-6
FOX_WIDTH = FOX_HEADS * HEAD_DIM
DSA_WIDTH = DSA_HEADS * HEAD_DIM
IDX_WIDTH = IDX_HEADS * IDX_DIM

LANES = 128
VMEM_LIMIT_BYTES = 56 * 1024 * 1024

TOKEN_TILE = 512
FOX_TILE = 1024
DSA_Q_TILE = 256
DSA_K_TILE = 1024
LANE_GROUP = 256
QK_AHEAD = 2
COUNT_ROWS = 32

V_ROWS = HEAD_DIM + 16
LOG2E = 1.4426950408889634
INT_MIN = -2 ** 31
F32 = jnp.float32
BF16 = jnp.bfloat16
_NT = (((1,), (1,)), ((), ()))


def _cparams(sem):
    return pltpu.CompilerParams(dimension_semantics=sem, vmem_limit_bytes=VMEM_LIMIT_BYTES)


def _const_spec(shape):
    nd = len(shape)
    return pl.BlockSpec(shape, lambda *_: (0,) * nd, pipeline_mode=pl.Buffered(1))


def _modulated(x, mod_ref, g_ref, sub):
    ms = jnp.mean(x * x, axis=-1, keepdims=True)
    y = x * lax.rsqrt(ms + NORM_EPS) * g_ref[sub:sub + 1, :]
    shift = mod_ref[0, 3 * sub:3 * sub + 1, :]
    scale = mod_ref[0, 3 * sub + 1:3 * sub + 2, :]
    return y * (1.0 + scale) + shift


def _ada_kernel(c_ref, w_ref, b_ref, o_ref):
    c = c_ref[...]
    cond = c * jax.nn.sigmoid(c)
    o_ref[...] = jnp.dot(cond, w_ref[...], preferred_element_type=F32,
                         precision=lax.Precision.HIGHEST) + b_ref[...]


def _ada_call(c, w, b):
    bsz, d = c.shape
    n = w.shape[1]
    tn = 1152
    return pl.pallas_call(
        _ada_kernel,
        grid=(n // tn,),
        in_specs=[pl.BlockSpec((bsz, d), lambda j: (0, 0)),
                  pl.BlockSpec((d, tn), lambda j: (0, j)),
                  pl.BlockSpec((1, tn), lambda j: (0, j))],
        out_specs=pl.BlockSpec((bsz, tn), lambda j: (0, j)),
        out_shape=jax.ShapeDtypeStruct((bsz, n), F32),
        compiler_params=_cparams(("arbitrary",)),
        name="ada_ln",
    )(c, w, b.reshape(1, n))


def _ffn_kernel(x_ref, mod_ref, g_ref, w1_ref, w3_ref, w2_ref, o_ref, *, sub):
    x = x_ref[0]
    h = _modulated(x, mod_ref, g_ref, sub).astype(BF16)
    a = jnp.dot(h, w1_ref[...], preferred_element_type=F32)
    b = jnp.dot(h, w3_ref[...], preferred_element_type=F32)
    act = (a * jax.nn.sigmoid(a) * b).astype(BF16)
    y = jnp.dot(act, w2_ref[...], preferred_element_type=F32)
    gate = mod_ref[0, 3 * sub + 2:3 * sub + 3, :]
    o_ref[0] = x + (0.5 * gate) * y


def _ffn_call(x, mod, g, w1, w3, w2, sub):
    bsz, s, d = x.shape
    tm = min(TOKEN_TILE, s)
    f = w1.shape[1]
    tok = pl.BlockSpec((1, tm, d), lambda b, i: (b, i, 0))
    return pl.pallas_call(
        functools.partial(_ffn_kernel, sub=sub),
        grid=(bsz, s // tm),
        in_specs=[tok,
                  pl.BlockSpec((1, 3 * N_SUBLAYERS, d), lambda b, i: (b, 0, 0)),
                  _const_spec((N_SUBLAYERS, d)),
                  _const_spec((d, f)), _const_spec((d, f)), _const_spec((f, d))],
        out_specs=tok,
        out_shape=jax.ShapeDtypeStruct(x.shape, F32),
        compiler_params=_cparams(("parallel", "parallel")),
        name=f"ffn{sub}",
    )(x, mod, g, w1, w3, w2)


def _split3(v):
    hi = v.astype(BF16)
    r = v - hi.astype(F32)
    mid = r.astype(BF16)
    lo = (r - mid.astype(F32)).astype(BF16)
    return hi, mid, lo


def _group_norm(z, gmat, gain, group):
    sq = z * z
    hi = sq.astype(BF16)
    lo = (sq - hi.astype(F32)).astype(BF16)
    ssq = jnp.dot(hi, gmat, preferred_element_type=F32) + jnp.dot(lo, gmat, preferred_element_type=F32)
    return z * lax.rsqrt(ssq * (1.0 / group) + NORM_EPS) * gain


def _rope(z, cos, sin, half, period):
    width = z.shape[1]
    lane = lax.broadcasted_iota(jnp.int32, z.shape, 1) % period
    up = pltpu.roll(z, width - half, axis=1)
    dn = pltpu.roll(z, half, axis=1)
    return z * cos + jnp.where(lane < half, up, dn) * sin


def _mix_in_kernel(x_ref, mod_ref, g_ref, pos_ref, wa_ref, wb_ref, gmat_ref, gains_ref, freq_ref, fbias_ref,
                   fq_ref, fk_ref, fv_ref, dq_ref, dk4_ref, dv2_ref, ik8_ref, iq_ref, faug_ref, small_ref,
                   carry_ref):
    @pl.when(pl.program_id(1) == 0)
    def _():
        carry_ref[...] = jnp.zeros_like(carry_ref)

    x = x_ref[0]
    tm = x.shape[0]
    h = _modulated(x, mod_ref, g_ref, 1).astype(BF16)
    za = jnp.dot(h, wa_ref[...], preferred_element_type=F32)
    zb = jnp.dot(h, wb_ref[...], preferred_element_type=F32)

    gmat = gmat_ref[...]
    scale = HEAD_DIM ** -0.5 * LOG2E
    fq = _group_norm(za[:, 0:512], gmat, gains_ref[0:1, :], HEAD_DIM) * scale
    fk = _group_norm(za[:, 512:1024], gmat, gains_ref[1:2, :], HEAD_DIM)
    fq_ref[0] = fq.astype(BF16)
    fk_ref[0] = fk.astype(BF16)
    fv_ref[0] = za[:, 1024:1536].astype(BF16)

    pos = pos_ref[0]
    ang_a = pos * freq_ref[0:1, :]
    ang_i = pos * freq_ref[1:2, :]
    cos_a, sin_a = jnp.cos(ang_a), jnp.sin(ang_a)
    cos_i, sin_i = jnp.cos(ang_i), jnp.sin(ang_i)
    rot_a = HEAD_DIM // ROPE_FRACTION // 2
    rot_i = IDX_DIM // ROPE_FRACTION // 2

    dq = _group_norm(za[:, 1536:2048], gmat, gains_ref[2:3, :], HEAD_DIM)
    dq = _rope(dq, jnp.concatenate([cos_a] * 4, axis=1), jnp.concatenate([sin_a] * 4, axis=1), rot_a, HEAD_DIM)
    dq_ref[0] = (dq * scale).astype(BF16)

    dk4 = _group_norm(zb[:, 0:256], gmat[0:256, 0:256], gains_ref[3:4, 0:256], HEAD_DIM)
    dk4 = _rope(dk4, jnp.concatenate([cos_a] * 2, axis=1), jnp.concatenate([sin_a] * 2, axis=1), rot_a, HEAD_DIM)
    dk4_ref[0] = dk4.astype(BF16)

    cos_i2 = jnp.concatenate([cos_i] * 2, axis=1)
    sin_i2 = jnp.concatenate([sin_i] * 2, axis=1)
    ik8_ref[0] = _rope(zb[:, 256:512], cos_i2, sin_i2, rot_i, IDX_DIM).astype(BF16)
    dv2_ref[0] = zb[:, 512:640].astype(BF16)
    iq_ref[0] = _rope(zb[:, 640:896], cos_i2, sin_i2, rot_i, IDX_DIM).astype(BF16)

    sm = zb[:, 896:1024]
    v = sm + fbias_ref[...]
    logf = jnp.minimum(v, 0.0) - jnp.log1p(jnp.exp(-jnp.abs(v)))
    row = lax.broadcasted_iota(jnp.int32, (tm, tm), 0)
    col = lax.broadcasted_iota(jnp.int32, (tm, tm), 1)
    tri = jnp.where(row >= col, 1.0, 0.0).astype(BF16)
    hi, mid, lo = _split3(logf)
    csum = (jnp.dot(tri, hi, preferred_element_type=F32) + jnp.dot(tri, mid, preferred_element_type=F32)
            + jnp.dot(tri, lo, preferred_element_type=F32)) + carry_ref[...]
    carry_ref[...] = csum[tm - 1:tm, :]
    lane = lax.broadcasted_iota(jnp.int32, sm.shape, 1)
    small_ref[0] = jnp.where(lane < FOX_HEADS, csum, sm * ((IDX_HEADS * IDX_DIM) ** -0.5))

    nf = -csum * LOG2E
    t0 = nf.astype(BF16).astype(F32)
    r1 = nf - t0
    t1 = r1.astype(BF16).astype(F32)
    t2 = r1 - t1
    aug = jnp.where(lane < FOX_HEADS, t0,
                    jnp.where(lane < 2 * FOX_HEADS, pltpu.roll(t1, FOX_HEADS, axis=1),
                              jnp.where(lane < 3 * FOX_HEADS, pltpu.roll(t2, 2 * FOX_HEADS, axis=1), 0.0)))
    faug_ref[0] = aug.astype(BF16)


def _mix_in_call(x, mod, g, pos, wa, wb, gmat, gains, freq, fbias):
    bsz, s, d = x.shape
    tm = min(TOKEN_TILE, s)

    def tok(w):
        return pl.BlockSpec((1, tm, w), lambda b, i: (b, i, 0))

    widths = (512, 512, 512, 512, 256, 128, 256, 256, 128)
    out_shapes = [jax.ShapeDtypeStruct((bsz, s, w), BF16) for w in widths]
    out_shapes.append(jax.ShapeDtypeStruct((bsz, s, LANES), F32))
    return pl.pallas_call(
        _mix_in_kernel,
        grid=(bsz, s // tm),
        in_specs=[tok(d),
                  pl.BlockSpec((1, 3 * N_SUBLAYERS, d), lambda b, i: (b, 0, 0)),
                  _const_spec((N_SUBLAYERS, d)),
                  tok(LANES),
                  _const_spec(wa.shape), _const_spec(wb.shape), _const_spec(gmat.shape),
                  _const_spec(gains.shape), _const_spec(freq.shape), _const_spec(fbias.shape)],
        out_specs=[tok(w) for w in widths] + [tok(LANES)],
        out_shape=out_shapes,
        scratch_shapes=[pltpu.VMEM((1, LANES), F32)],
        compiler_params=_cparams(("arbitrary", "arbitrary")),
        name="mix_in",
    )(x, mod, g, pos, wa, wb, gmat, gains, freq, fbias)


def _fox_kernel(q_ref, k_ref, faug_ref, vt_ref, hm_ref, fsel_ref, o_ref, qf_ref, m_ref, acc_ref):
    i = pl.program_id(2)
    tq = q_ref.shape[1]
    tk = tq
    q = q_ref[0]
    for a in range(2):
        sel_row = fsel_ref[0, a:a + 1, :]
        qf_ref[a * tq:(a + 1) * tq, 0:LANES] = q * hm_ref[a:a + 1, :]
        qf_ref[a * tq:(a + 1) * tq, LANES:2 * LANES] = jnp.broadcast_to(sel_row, (tq, LANES))
    m_ref[...] = jnp.full_like(m_ref, -jnp.inf)
    acc_ref[...] = jnp.zeros_like(acc_ref)

    def block(j, masked):
        start = pl.multiple_of(j * tk, tk)
        kf = jnp.concatenate([k_ref[0, pl.ds(start, tk), :], faug_ref[0, pl.ds(start, tk), :]], axis=1)
        n_groups = 2 * tq // LANE_GROUP

        def scores(g):
            s = lax.dot_general(kf, qf_ref[g * LANE_GROUP:(g + 1) * LANE_GROUP, :], _NT,
                                preferred_element_type=F32)
            if masked:
                krow = lax.broadcasted_iota(jnp.int32, s.shape, 0)
                qcol = lax.broadcasted_iota(jnp.int32, s.shape, 1) + (g * LANE_GROUP) % tq
                s = jnp.where(krow <= qcol, s, -jnp.inf)
            return s

        ahead = [scores(g) for g in range(min(QK_AHEAD, n_groups))]
        for g in range(n_groups):
            s = ahead.pop(0)
            if g + QK_AHEAD < n_groups:
                ahead.append(scores(g + QK_AHEAD))
            sl = slice(g * LANE_GROUP, (g + 1) * LANE_GROUP)
            m_old = m_ref[:, sl]
            m_new = jnp.maximum(m_old, jnp.max(s, axis=0, keepdims=True))
            p = jnp.exp2((s - m_new).astype(BF16))
            alpha = jnp.exp2(m_old - m_new)
            m_ref[:, sl] = m_new
            a = (g * LANE_GROUP) // tq
            vt = vt_ref[0, a * V_ROWS:(a + 1) * V_ROWS, pl.ds(start, tk)]
            acc_ref[:, sl] = alpha * acc_ref[:, sl] + jnp.dot(vt, p, preferred_element_type=F32)

    def body(j, c):
        block(j, False)
        return c

    lax.fori_loop(0, i, body, 0)
    block(i, True)
    out = acc_ref[0:HEAD_DIM, :] / acc_ref[HEAD_DIM:HEAD_DIM + 1, :]
    for a in range(2):
        o_ref[0, a * HEAD_DIM:(a + 1) * HEAD_DIM, :] = out[:, a * tq:(a + 1) * tq].astype(BF16)


def _fox_call(fq, fk, faug, fvt, hm2, fsel):
    bsz, s, _ = fq.shape
    tq = min(FOX_TILE, s)
    pairs = FOX_HEADS // 2
    return pl.pallas_call(
        _fox_kernel,
        grid=(bsz, pairs, s // tq),
        in_specs=[pl.BlockSpec((1, tq, LANES), lambda b, h, i: (b, i, h)),
                  pl.BlockSpec((1, s, LANES), lambda b, h, i: (b, 0, h)),
                  pl.BlockSpec((1, s, LANES), lambda b, h, i: (b, 0, 0)),
                  pl.BlockSpec((1, 2 * V_ROWS, s), lambda b, h, i: (b, h, 0)),
                  pl.BlockSpec((2, LANES), lambda b, h, i: (0, 0)),
                  pl.BlockSpec((1, 2, LANES), lambda b, h, i: (h, 0, 0))],
        out_specs=pl.BlockSpec((1, LANES, tq), lambda b, h, i: (b, h, i)),
        out_shape=jax.ShapeDtypeStruct((bsz, FOX_WIDTH, s), BF16),
        scratch_shapes=[pltpu.VMEM((2 * tq, 2 * LANES), BF16),
                        pltpu.VMEM((1, 2 * tq), F32),
                        pltpu.VMEM((V_ROWS, 2 * tq), F32)],
        compiler_params=_cparams(("parallel", "parallel", "arbitrary")),
        name="fox_attn",
    )(fq, fk, faug, fvt, hm2, fsel)


def _dsa_kernel(dq_ref, iq_ref, wt_ref, dk4_ref, dvt_ref, ik8_ref, hm8_ref, hm4_ref, o_ref,
                iq8_ref, q8_ref, keys_ref, m_ref, acc_ref, *, top_k):
    i = pl.program_id(1)
    tq = dq_ref.shape[1]
    s_len = dk4_ref.shape[1]
    tk = min(DSA_K_TILE, s_len)
    q_start = i * tq
    nkb = (q_start + tq + tk - 1) // tk

    iq = iq_ref[0]
    dq = dq_ref[0]
    half = dq.shape[1] // 2
    for h in range(IDX_HEADS):
        iq8_ref[h * tq:(h + 1) * tq, :] = iq * hm8_ref[h:h + 1, :]
        q8_ref[h * tq:(h + 1) * tq, :] = dq[:, (h // 4) * half:(h // 4 + 1) * half] * hm4_ref[h % 4:h % 4 + 1, :]

    krow = lax.broadcasted_iota(jnp.int32, (tk, tq), 0)
    qcol = lax.broadcasted_iota(jnp.int32, (tk, tq), 1)
    adm_end = q_start + (qcol // CHUNK + 1) * CHUNK
    wt = wt_ref[0, :, pl.ds(pl.multiple_of(q_start, tq), tq)]

    def score_block(j, c):
        start = pl.multiple_of(j * tk, tk)
        d = lax.dot_general(ik8_ref[0, pl.ds(start, tk), :], iq8_ref[...], _NT, preferred_element_type=F32)
        sc = jnp.zeros((tk, tq), F32)
        for h in range(IDX_HEADS):
            sc = sc + jnp.maximum(d[:, h * tq:(h + 1) * tq], 0.0) * wt[IDX_HEADS + h:IDX_HEADS + h + 1, :]
        sc = jnp.where(sc == 0.0, 0.0, sc)
        bits = lax.bitcast_convert_type(sc, jnp.int32)
        key = bits ^ ((bits >> 31) & 0x7FFFFFFF)
        keys_ref[pl.ds(start, tk), :] = jnp.where(krow + start < adm_end, key, INT_MIN)
        return c

    lax.fori_loop(0, nkb, score_block, 0)

    def count(pred):
        def body(j, acc):
            start = pl.multiple_of(j * tk, tk)
            hit = jnp.where(pred(keys_ref[pl.ds(start, tk), :]), 1, 0)
            return acc + jnp.sum(hit.reshape(tk // COUNT_ROWS, COUNT_ROWS, tq), axis=0)
        acc = lax.fori_loop(0, nkb, body, jnp.zeros((COUNT_ROWS, tq), jnp.int32))
        return jnp.sum(acc, axis=0, keepdims=True)

    def bisect(it, thr):
        cand = thr + jnp.left_shift(jnp.int32(1), 31 - it)
        cnt = count(lambda kblk: kblk >= cand)
        return jnp.where(cnt >= top_k, cand, thr)

    thr = lax.fori_loop(0, 32, bisect, jnp.full((1, tq), INT_MIN, jnp.int32))
    thr = jnp.maximum(thr, INT_MIN + 1)
    n_ge = count(lambda kblk: kblk >= thr)

    @pl.when(jnp.max(n_ge) > top_k)
    def _():
        n_gt = count(lambda kblk: kblk > thr)
        need = (top_k - n_gt).astype(F32)
        r2 = lax.broadcasted_iota(jnp.int32, (tk, tk), 0)
        c2 = lax.broadcasted_iota(jnp.int32, (tk, tk), 1)
        lower = jnp.where(c2 < r2, 1.0, 0.0).astype(BF16)

        def demote(j, seen):
            start = pl.multiple_of(j * tk, tk)
            kblk = keys_ref[pl.ds(start, tk), :]
            eq = kblk == thr
            eqf = jnp.where(eq, 1.0, 0.0)
            before = jnp.dot(lower, eqf.astype(BF16), preferred_element_type=F32) + seen
            keys_ref[pl.ds(start, tk), :] = jnp.where(eq & (before >= need), thr - 1, kblk)
            return seen + jnp.sum(eqf, axis=0, keepdims=True)

        lax.fori_loop(0, nkb, demote, jnp.zeros((1, tq), F32))

    m_ref[...] = jnp.full_like(m_ref, -jnp.inf)
    acc_ref[...] = jnp.zeros_like(acc_ref)

    def attend(j, c):
        start = pl.multiple_of(j * tk, tk)
        bias = jnp.where(keys_ref[pl.ds(start, tk), :] >= thr, 0.0, -jnp.inf)
        bias = jnp.concatenate([bias] * (LANE_GROUP // tq), axis=1)
        kb = dk4_ref[0, pl.ds(start, tk), :]
        vt = dvt_ref[0, :, pl.ds(start, tk)]
        n_groups = DSA_HEADS * tq // LANE_GROUP

        def scores(g):
            return lax.dot_general(kb, q8_ref[g * LANE_GROUP:(g + 1) * LANE_GROUP, :], _NT,
                                   preferred_element_type=F32) + bias

        ahead = [scores(g) for g in range(min(QK_AHEAD, n_groups))]
        for g in range(n_groups):
            s = ahead.pop(0)
            if g + QK_AHEAD < n_groups:
                ahead.append(scores(g + QK_AHEAD))
            sl = slice(g * LANE_GROUP, (g + 1) * LANE_GROUP)
            m_old = m_ref[:, sl]
            m_new = jnp.maximum(m_old, jnp.max(s, axis=0, keepdims=True))
            m_safe = jnp.where(m_new == -jnp.inf, 0.0, m_new)
            p = jnp.exp2((s - m_safe).astype(BF16))
            alpha = jnp.exp2(m_old - m_safe)
            m_ref[:, sl] = m_new
            acc_ref[:, sl] = alpha * acc_ref[:, sl] + jnp.dot(vt, p, preferred_element_type=F32)
        return c

    lax.fori_loop(0, nkb, attend, 0)
    out = acc_ref[0:HEAD_DIM, :] / acc_ref[HEAD_DIM:HEAD_DIM + 1, :]
    for h in range(DSA_HEADS):
        o_ref[0, h * HEAD_DIM:(h + 1) * HEAD_DIM, :] = out[:, h * tq:(h + 1) * tq].astype(BF16)


def _dsa_call(dq, iq, wt, dk4, dvt, ik8, hm8, hm4, top_k):
    bsz, s, _ = dq.shape
    tq = min(DSA_Q_TILE, s)

    def tok(w):
        return pl.BlockSpec((1, tq, w), lambda b, i: (b, i, 0))

    def seq(w):
        return pl.BlockSpec((1, s, w), lambda b, i: (b, 0, 0))

    def seqt(r):
        return pl.BlockSpec((1, r, s), lambda b, i: (b, 0, 0))

    return pl.pallas_call(
        functools.partial(_dsa_kernel, top_k=top_k),
        grid=(bsz, s // tq),
        in_specs=[tok(DSA_WIDTH), tok(IDX_WIDTH), seqt(2 * IDX_HEADS), seq(256), seqt(V_ROWS), seq(256),
                  pl.BlockSpec(hm8.shape, lambda b, i: (0, 0)), pl.BlockSpec(hm4.shape, lambda b, i: (0, 0))],
        out_specs=pl.BlockSpec((1, DSA_WIDTH, tq), lambda b, i: (b, 0, i)),
        out_shape=jax.ShapeDtypeStruct((bsz, DSA_WIDTH, s), BF16),
        scratch_shapes=[pltpu.VMEM((IDX_HEADS * tq, 256), BF16), pltpu.VMEM((DSA_HEADS * tq, 256), BF16),
                        pltpu.VMEM((s, tq), jnp.int32),
                        pltpu.VMEM((1, DSA_HEADS * tq), F32),
                        pltpu.VMEM((V_ROWS, DSA_HEADS * tq), F32)],
        compiler_params=_cparams(("parallel", "arbitrary")),
        name="dsa_attn",
    )(dq, iq, wt, dk4, dvt, ik8, hm8, hm4)


def _merge_kernel(x_ref, mod_ref, g_ref, ya_ref, yb_ref, wg_ref, wfa_ref, wfb_ref, wo_ref, o_ref):
    x = x_ref[0]
    d = x.shape[1]
    h = _modulated(x, mod_ref, g_ref, 1).astype(BF16)
    zg = jnp.dot(h, wg_ref[...], preferred_element_type=F32)
    pa = jnp.dot(ya_ref[0], wfa_ref[...], preferred_element_type=F32)
    pb = jnp.dot(yb_ref[0], wfb_ref[...], preferred_element_type=F32)
    merged = jax.nn.sigmoid(zg[:, :d]) * pa + jax.nn.sigmoid(zg[:, d:]) * pb
    y = jnp.dot(merged.astype(BF16), wo_ref[...], preferred_element_type=F32)
    o_ref[0] = x + mod_ref[0, 5:6, :] * y


def _merge_call(x, mod, g, ya, yb, wg, wfa, wfb, wo):
    bsz, s, d = x.shape
    tm = min(TOKEN_TILE, s)

    def tok(w):
        return pl.BlockSpec((1, tm, w), lambda b, i: (b, i, 0))

    return pl.pallas_call(
        _merge_kernel,
        grid=(bsz, s // tm),
        in_specs=[tok(d),
                  pl.BlockSpec((1, 3 * N_SUBLAYERS, d), lambda b, i: (b, 0, 0)),
                  _const_spec((N_SUBLAYERS, d)),
                  tok(FOX_WIDTH), tok(DSA_WIDTH),
                  _const_spec(wg.shape), _const_spec(wfa.shape), _const_spec(wfb.shape), _const_spec(wo.shape)],
        out_specs=tok(d),
        out_shape=jax.ShapeDtypeStruct(x.shape, F32),
        compiler_params=_cparams(("parallel", "parallel")),
        name="merge_out",
    )(x, mod, g, ya, yb, wg, wfa, wfb, wo)


def _head_mask(n_heads, width):
    return jnp.asarray(np.kron(np.eye(n_heads), np.ones((1, width))), BF16)


def _rope_freq_row(rot_dim, period):
    inv_freq = ROPE_THETA ** (-jnp.arange(0, rot_dim, 2, dtype=F32) / rot_dim)
    half = rot_dim // 2
    head = jnp.concatenate([-inv_freq, inv_freq, jnp.zeros((period - 2 * half,), F32)])
    return jnp.tile(head, LANES // period)


def kernel(x, c, positions, ada_w, ada_b, norm_g, ffn1_w1, ffn1_w3, ffn1_w2, w_in, fox_f_bias, fox_qk_g, dsa_qk_g,
           w_br_fox, w_br_dsa, w_out, ffn2_w1, ffn2_w3, ffn2_w2):
    bsz, s, d = x.shape
    top_k = min(TOPK_MAX, s // 4)
    depth = ada_w.shape[0]
    pos = jnp.broadcast_to(positions.astype(F32)[:, :, None], (bsz, s, LANES))
    freq = jnp.stack([_rope_freq_row(HEAD_DIM // ROPE_FRACTION, HEAD_DIM),
                      _rope_freq_row(IDX_DIM // ROPE_FRACTION, IDX_DIM)])
    gmat = jnp.asarray(np.kron(np.eye(FOX_HEADS), np.ones((HEAD_DIM, HEAD_DIM))), BF16)
    hm2 = _head_mask(2, HEAD_DIM)
    hm4 = _head_mask(4, HEAD_DIM)
    hm8 = _head_mask(IDX_HEADS, IDX_DIM)
    fsel = jnp.asarray(np.tile(np.eye(FOX_HEADS), (1, LANES // FOX_HEADS))
                       * (np.arange(LANES) < 3 * FOX_HEADS), BF16).reshape(FOX_HEADS // 2, 2, LANES)
    o_fq, o_fk, o_fv = 0, FOX_WIDTH, 2 * FOX_WIDTH
    o_ff = 3 * FOX_WIDTH
    o_dq = o_ff + FOX_HEADS
    o_dk = o_dq + DSA_WIDTH
    o_dv = o_dk + HEAD_DIM
    o_iq = o_dv + HEAD_DIM
    o_ik = o_iq + IDX_WIDTH
    o_iw = o_ik + IDX_DIM
    o_ga = o_iw + IDX_HEADS

    for l in range(depth):
        mod = _ada_call(c, ada_w[l], ada_b[l]).reshape(bsz, 3 * N_SUBLAYERS, d)
        g = norm_g[l]
        x = _ffn_call(x, mod, g, ffn1_w1[l].astype(BF16), ffn1_w3[l].astype(BF16), ffn1_w2[l].astype(BF16), 0)

        w = w_in[l]
        wa = jnp.concatenate([w[:, o_fq:o_fq + 3 * FOX_WIDTH], w[:, o_dq:o_dq + DSA_WIDTH]], axis=1).astype(BF16)
        pad = jnp.zeros((d, LANES - FOX_HEADS - IDX_HEADS), F32)
        wb = jnp.concatenate([jnp.tile(w[:, o_dk:o_dk + HEAD_DIM], (1, 4)),
                              jnp.tile(w[:, o_ik:o_ik + IDX_DIM], (1, IDX_HEADS)),
                              jnp.tile(w[:, o_dv:o_dv + HEAD_DIM], (1, 2)),
                              w[:, o_iq:o_iq + IDX_WIDTH],
                              w[:, o_ff:o_ff + FOX_HEADS], w[:, o_iw:o_iw + IDX_HEADS], pad], axis=1).astype(BF16)
        wg = w[:, o_ga:o_ga + 2 * d].astype(BF16)
        gains = jnp.stack([jnp.tile(fox_qk_g[l, 0], FOX_HEADS), jnp.tile(fox_qk_g[l, 1], FOX_HEADS),
                           jnp.tile(dsa_qk_g[l, 0], DSA_HEADS), jnp.tile(dsa_qk_g[l, 1], DSA_HEADS)]).astype(F32)
        fbias = jnp.concatenate([fox_f_bias[l].astype(F32), jnp.zeros((LANES - FOX_HEADS,), F32)]).reshape(1, LANES)

        fq, fk, fv, dq, dk4, dv2, ik8, iq, faug, small = _mix_in_call(x, mod, g, pos, wa, wb, gmat, gains, freq, fbias)
        ones = jnp.ones((bsz, FOX_HEADS, V_ROWS - HEAD_DIM, s), BF16)
        fvt = jnp.transpose(fv.reshape(bsz, s, FOX_HEADS, HEAD_DIM), (0, 2, 3, 1))
        fvt = jnp.concatenate([fvt, ones], axis=2).reshape(bsz, FOX_HEADS * V_ROWS, s)
        dvt = jnp.concatenate([jnp.transpose(dv2[:, :, :HEAD_DIM], (0, 2, 1)), ones[:, 0]], axis=1)
        wt = jnp.transpose(small[:, :, :2 * IDX_HEADS], (0, 2, 1))
        yat = _fox_call(fq, fk, faug, fvt, hm2, fsel)
        ybt = _dsa_call(dq, iq, wt, dk4, dvt, ik8, hm8, hm4, top_k)
        ya = jnp.transpose(yat, (0, 2, 1))
        yb = jnp.transpose(ybt, (0, 2, 1))
        x = _merge_call(x, mod, g, ya, yb, wg, w_br_fox[l].astype(BF16), w_br_dsa[l].astype(BF16),
                        w_out[l].astype(BF16))
        x = _ffn_call(x, mod, g, ffn2_w1[l].astype(BF16), ffn2_w3[l].astype(BF16), ffn2_w2[l].astype(BF16), 2)
    return x
```

```python
import functools

import numpy as np
import jax
import jax.numpy as jnp
from jax import lax
from jax.experimental import pallas as pl
from jax.experimental.pallas import tpu as pltpu

D_MODEL = 1024
CHUNK = 64
HEAD_DIM = 64
FOX_HEADS = 8
DSA_HEADS = 8
IDX_HEADS = 8
IDX_DIM = 32
TOPK_MAX = 256
ROPE_THETA = 500000.0
ROPE_FRACTION = 4
D_FF = 2816
N_SUBLAYERS = 3
NORM_EPS = 1e-6
FOX_WIDTH = FOX_HEADS * HEAD_DIM
DSA_WIDTH = DSA_HEADS * HEAD_DIM
IDX_WIDTH = IDX_HEADS * IDX_DIM

LANES = 128
VMEM_LIMIT_BYTES = 56 * 1024 * 1024

TOKEN_TILE = 512
FOX_TILE = 1024
DSA_Q_TILE = 256
DSA_K_TILE = 1024
FOX_GROUP = 512
DSA_GROUP = 1024
QK_AHEAD = 2
COUNT_ROWS = 32

V_ROWS = HEAD_DIM + 16
LOG2E = 1.4426950408889634
INT_MIN = -2 ** 31
F32_MIN_NORMAL = 1.1754943508222875e-38
F32 = jnp.float32
BF16 = jnp.bfloat16
_NT = (((1,), (1,)), ((), ()))


def _cparams(sem):
    return pltpu.CompilerParams(dimension_semantics=sem, vmem_limit_bytes=VMEM_LIMIT_BYTES)


def _const_spec(shape):
    nd = len(shape)
    return pl.BlockSpec(shape, lambda *_: (0,) * nd, pipeline_mode=pl.Buffered(1))


def _modulated(x, mod_ref, g_ref, sub):
    ms = jnp.mean(x * x, axis=-1, keepdims=True)
    y = x * lax.rsqrt(ms + NORM_EPS) * g_ref[sub:sub + 1, :]
    shift = mod_ref[0, 3 * sub:3 * sub + 1, :]
    scale = mod_ref[0, 3 * sub + 1:3 * sub + 2, :]
    return y * (1.0 + scale) + shift


def _ada_kernel(c_ref, w_ref, b_ref, o_ref):
    c = c_ref[...]
    cond = c * jax.nn.sigmoid(c)
    o_ref[...] = jnp.dot(cond, w_ref[...], preferred_element_type=F32,
                         precision=lax.Precision.HIGHEST) + b_ref[...]


def _ada_call(c, w, b):
    bsz, d = c.shape
    n = w.shape[1]
    tn = 1152
    return pl.pallas_call(
        _ada_kernel,
        grid=(n // tn,),
        in_specs=[pl.BlockSpec((bsz, d), lambda j: (0, 0)),
                  pl.BlockSpec((d, tn), lambda j: (0, j)),
                  pl.BlockSpec((1, tn), lambda j: (0, j))],
        out_specs=pl.BlockSpec((bsz, tn), lambda j: (0, j)),
        out_shape=jax.ShapeDtypeStruct((bsz, n), F32),
        compiler_params=_cparams(("arbitrary",)),
        name="ada_ln",
    )(c, w, b.reshape(1, n))


def _ffn_kernel(x_ref, mod_ref, g_ref, w1_ref, w3_ref, w2_ref, o_ref, *, sub):
    x = x_ref[0]
    h = _modulated(x, mod_ref, g_ref, sub).astype(BF16)
    a = jnp.dot(h, w1_ref[...], preferred_element_type=F32)
    b = jnp.dot(h, w3_ref[...], preferred_element_type=F32)
    act = (a * jax.nn.sigmoid(a) * b).astype(BF16)
    y = jnp.dot(act, w2_ref[...], preferred_element_type=F32)
    gate = mod_ref[0, 3 * sub + 2:3 * sub + 3, :]
    o_ref[0] = x + (0.5 * gate) * y


def _ffn_call(x, mod, g, w1, w3, w2, sub):
    bsz, s, d = x.shape
    tm = min(TOKEN_TILE, s)
    f = w1.shape[1]
    tok = pl.BlockSpec((1, tm, d), lambda b, i: (b, i, 0))
    return pl.pallas_call(
        functools.partial(_ffn_kernel, sub=sub),
        grid=(bsz, s // tm),
        in_specs=[tok,
                  pl.BlockSpec((1, 3 * N_SUBLAYERS, d), lambda b, i: (b, 0, 0)),
                  _const_spec((N_SUBLAYERS, d)),
                  _const_spec((d, f)), _const_spec((d, f)), _const_spec((f, d))],
        out_specs=tok,
        out_shape=jax.ShapeDtypeStruct(x.shape, F32),
        compiler_params=_cparams(("parallel", "parallel")),
        name=f"ffn{sub}",
    )(x, mod, g, w1, w3, w2)


def _split3(v):
    hi = v.astype(BF16)
    r = v - hi.astype(F32)
    mid = r.astype(BF16)
    lo = (r - mid.astype(F32)).astype(BF16)
    return hi, mid, lo


def _group_norm(z, gmat, gain, group):
    sq = z * z
    hi = sq.astype(BF16)
    lo = (sq - hi.astype(F32)).astype(BF16)
    ssq = jnp.dot(hi, gmat, preferred_element_type=F32) + jnp.dot(lo, gmat, preferred_element_type=F32)
    return z * lax.rsqrt(ssq * (1.0 / group) + NORM_EPS) * gain


def _rope(z, cos, sin, half, period):
    width = z.shape[1]
    lane = lax.broadcasted_iota(jnp.int32, z.shape, 1) % period
    up = pltpu.roll(z, width - half, axis=1)
    dn = pltpu.roll(z, half, axis=1)
    return z * cos + jnp.where(lane < half, up, dn) * sin


def _mix_in_kernel(x_ref, mod_ref, g_ref, pos_ref, wa_ref, wb_ref, gmat_ref, gains_ref, freq_ref, fbias_ref,
                   fq_ref, fk_ref, fv_ref, dq_ref, dk4_ref, dv2_ref, ik8_ref, iq_ref, faug_ref, small_ref,
                   carry_ref):
    @pl.when(pl.program_id(1) == 0)
    def _():
        carry_ref[...] = jnp.zeros_like(carry_ref)

    x = x_ref[0]
    tm = x.shape[0]
    h = _modulated(x, mod_ref, g_ref, 1).astype(BF16)
    za = jnp.dot(h, wa_ref[...], preferred_element_type=F32)
    zb = jnp.dot(h, wb_ref[...], preferred_element_type=F32)

    gmat = gmat_ref[...]
    scale = HEAD_DIM ** -0.5 * LOG2E
    fq = _group_norm(za[:, 0:512], gmat, gains_ref[0:1, :], HEAD_DIM) * scale
    fk = _group_norm(za[:, 512:1024], gmat, gains_ref[1:2, :], HEAD_DIM)
    fq_ref[0] = fq.astype(BF16)
    fk_ref[0] = fk.astype(BF16)
    fv_ref[0] = za[:, 1024:1536].astype(BF16)

    pos = pos_ref[0]
    ang_a = pos * freq_ref[0:1, :]
    ang_i = pos * freq_ref[1:2, :]
    cos_a, sin_a = jnp.cos(ang_a), jnp.sin(ang_a)
    cos_i, sin_i = jnp.cos(ang_i), jnp.sin(ang_i)
    rot_a = HEAD_DIM // ROPE_FRACTION // 2
    rot_i = IDX_DIM // ROPE_FRACTION // 2

    dq = _group_norm(za[:, 1536:2048], gmat, gains_ref[2:3, :], HEAD_DIM)
    dq = _rope(dq, jnp.concatenate([cos_a] * 4, axis=1), jnp.concatenate([sin_a] * 4, axis=1), rot_a, HEAD_DIM)
    dq_ref[0] = (dq * scale).astype(BF16)

    dk4 = _group_norm(zb[:, 0:256], gmat[0:256, 0:256], gains_ref[3:4, 0:256], HEAD_DIM)
    dk4 = _rope(dk4, jnp.concatenate([cos_a] * 2, axis=1), jnp.concatenate([sin_a] * 2, axis=1), rot_a, HEAD_DIM)
    dk4_ref[0] = dk4.astype(BF16)

    cos_i2 = jnp.concatenate([cos_i] * 2, axis=1)
    sin_i2 = jnp.concatenate([sin_i] * 2, axis=1)
    ik8_ref[0] = _rope(zb[:, 256:512], cos_i2, sin_i2, rot_i, IDX_DIM).astype(BF16)
    dv2_ref[0] = zb[:, 512:640].astype(BF16)
    iq_ref[0] = _rope(zb[:, 640:896], cos_i2, sin_i2, rot_i, IDX_DIM).astype(BF16)

    sm = zb[:, 896:1024]
    v = sm + fbias_ref[...]
    logf = jnp.minimum(v, 0.0) - jnp.log1p(jnp.exp(-jnp.abs(v)))
    row = lax.broadcasted_iota(jnp.int32, (tm, tm), 0)
    col = lax.broadcasted_iota(jnp.int32, (tm, tm), 1)
    tri = jnp.where(row >= col, 1.0, 0.0).astype(BF16)
    hi, mid, lo = _split3(logf)
    csum = (jnp.dot(tri, hi, preferred_element_type=F32) + jnp.dot(tri, mid, preferred_element_type=F32)
            + jnp.dot(tri, lo, preferred_element_type=F32)) + carry_ref[...]
    carry_ref[...] = csum[tm - 1:tm, :]
    lane = lax.broadcasted_iota(jnp.int32, sm.shape, 1)
    small_ref[0] = jnp.where(lane < FOX_HEADS, csum, sm * ((IDX_HEADS * IDX_DIM) ** -0.5))

    nf = -csum * LOG2E
    t0 = nf.astype(BF16).astype(F32)
    r1 = nf - t0
    t1 = r1.astype(BF16).astype(F32)
    t2 = r1 - t1
    aug = jnp.where(lane < FOX_HEADS, t0,
                    jnp.where(lane < 2 * FOX_HEADS, pltpu.roll(t1, FOX_HEADS, axis=1),
                              jnp.where(lane < 3 * FOX_HEADS, pltpu.roll(t2, 2 * FOX_HEADS, axis=1), 0.0)))
    faug_ref[0] = aug.astype(BF16)


def _mix_in_call(x, mod, g, pos, wa, wb, gmat, gains, freq, fbias):
    bsz, s, d = x.shape
    tm = min(TOKEN_TILE, s)

    def tok(w):
        return pl.BlockSpec((1, tm, w), lambda b, i: (b, i, 0))

    widths = (512, 512, 512, 512, 256, 128, 256, 256, 128)
    out_shapes = [jax.ShapeDtypeStruct((bsz, s, w), BF16) for w in widths]
    out_shapes.append(jax.ShapeDtypeStruct((bsz, s, LANES), F32))
    return pl.pallas_call(
        _mix_in_kernel,
        grid=(bsz, s // tm),
        in_specs=[tok(d),
                  pl.BlockSpec((1, 3 * N_SUBLAYERS, d), lambda b, i: (b, 0, 0)),
                  _const_spec((N_SUBLAYERS, d)),
                  tok(LANES),
                  _const_spec(wa.shape), _const_spec(wb.shape), _const_spec(gmat.shape),
                  _const_spec(gains.shape), _const_spec(freq.shape), _const_spec(fbias.shape)],
        out_specs=[tok(w) for w in widths] + [tok(LANES)],
        out_shape=out_shapes,
        scratch_shapes=[pltpu.VMEM((1, LANES), F32)],
        compiler_params=_cparams(("arbitrary", "arbitrary")),
        name="mix_in",
    )(x, mod, g, pos, wa, wb, gmat, gains, freq, fbias)


def _fox_kernel(q_ref, k_ref, faug_ref, vt_ref, hm_ref, fsel_ref, o_ref, qf_ref, m_ref, acc_ref):
    i = pl.program_id(2)
    tq = q_ref.shape[1]
    tk = tq
    q = q_ref[0]
    for a in range(2):
        sel_row = fsel_ref[0, a:a + 1, :]
        qf_ref[a * tq:(a + 1) * tq, 0:LANES] = q * hm_ref[a:a + 1, :]
        qf_ref[a * tq:(a + 1) * tq, LANES:2 * LANES] = jnp.broadcast_to(sel_row, (tq, LANES))
    m_ref[...] = jnp.full_like(m_ref, -jnp.inf)
    acc_ref[...] = jnp.zeros_like(acc_ref)

    def block(j, masked):
        start = pl.multiple_of(j * tk, tk)
        kf = jnp.concatenate([k_ref[0, pl.ds(start, tk), :], faug_ref[0, pl.ds(start, tk), :]], axis=1)
        grp = min(FOX_GROUP, tq)
        n_groups = 2 * tq // grp

        def scores(g):
            s = lax.dot_general(kf, qf_ref[g * grp:(g + 1) * grp, :], _NT,
                                preferred_element_type=F32)
            if masked:
                krow = lax.broadcasted_iota(jnp.int32, s.shape, 0)
                qcol = lax.broadcasted_iota(jnp.int32, s.shape, 1) + (g * grp) % tq
                s = jnp.where(krow <= qcol, s, -jnp.inf)
            return s

        ahead = [scores(g) for g in range(min(QK_AHEAD, n_groups))]
        for g in range(n_groups):
            s = ahead.pop(0)
            if g + QK_AHEAD < n_groups:
                ahead.append(scores(g + QK_AHEAD))
            sl = slice(g * grp, (g + 1) * grp)
            m_old = m_ref[:, sl]
            m_new = jnp.maximum(m_old, jnp.max(s, axis=0, keepdims=True))
            p = jnp.exp2((s - m_new).astype(BF16))
            alpha = jnp.exp2(m_old - m_new)
            m_ref[:, sl] = m_new
            a = (g * grp) // tq
            vt = vt_ref[0, a * V_ROWS:(a + 1) * V_ROWS, pl.ds(start, tk)]
            acc_ref[:, sl] = alpha * acc_ref[:, sl] + jnp.dot(vt, p, preferred_element_type=F32)

    def body(j, c):
        block(j, False)
        return c

    lax.fori_loop(0, i, body, 0)
    block(i, True)
    out = acc_ref[0:HEAD_DIM, :] / acc_ref[HEAD_DIM:HEAD_DIM + 1, :]
    for a in range(2):
        o_ref[0, a * HEAD_DIM:(a + 1) * HEAD_DIM, :] = out[:, a * tq:(a + 1) * tq].astype(BF16)


def _fox_call(fq, fk, faug, fvt, hm2, fsel):
    bsz, s, _ = fq.shape
    tq = min(FOX_TILE, s)
    pairs = FOX_HEADS // 2
    return pl.pallas_call(
        _fox_kernel,
        grid=(bsz, pairs, s // tq),
        in_specs=[pl.BlockSpec((1, tq, LANES), lambda b, h, i: (b, i, h)),
                  pl.BlockSpec((1, s, LANES), lambda b, h, i: (b, 0, h)),
                  pl.BlockSpec((1, s, LANES), lambda b, h, i: (b, 0, 0)),
                  pl.BlockSpec((1, 2 * V_ROWS, s), lambda b, h, i: (b, h, 0)),
                  pl.BlockSpec((2, LANES), lambda b, h, i: (0, 0)),
                  pl.BlockSpec((1, 2, LANES), lambda b, h, i: (h, 0, 0))],
        out_specs=pl.BlockSpec((1, LANES, tq), lambda b, h, i: (b, h, i)),
        out_shape=jax.ShapeDtypeStruct((bsz, FOX_WIDTH, s), BF16),
        scratch_shapes=[pltpu.VMEM((2 * tq, 2 * LANES), BF16),
                        pltpu.VMEM((1, 2 * tq), F32),
                        pltpu.VMEM((V_ROWS, 2 * tq), F32)],
        compiler_params=_cparams(("parallel", "parallel", "arbitrary")),
        name="fox_attn",
    )(fq, fk, faug, fvt, hm2, fsel)


def _dsa_kernel(dq_ref, iq_ref, wt_ref, dk4_ref, dvt_ref, ik8_ref, hm8_ref, hm4_ref, o_ref,
                iq8_ref, q8_ref, keys_ref, hi_ref, m_ref, acc_ref, *, top_k):
    i = pl.program_id(1)
    tq = dq_ref.shape[1]
    s_len = dk4_ref.shape[1]
    tk = min(DSA_K_TILE, s_len)
    q_start = i * tq
    nkb = (q_start + tq + tk - 1) // tk

    iq = iq_ref[0]
    dq = dq_ref[0]
    half = dq.shape[1] // 2
    for h in range(IDX_HEADS):
        iq8_ref[h * tq:(h + 1) * tq, :] = iq * hm8_ref[h:h + 1, :]
        q8_ref[h * tq:(h + 1) * tq, :] = dq[:, (h // 4) * half:(h // 4 + 1) * half] * hm4_ref[h % 4:h % 4 + 1, :]

    krow = lax.broadcasted_iota(jnp.int32, (tk, tq), 0)
    qcol = lax.broadcasted_iota(jnp.int32, (tk, tq), 1)
    adm_end = q_start + (qcol // CHUNK + 1) * CHUNK
    wt = wt_ref[0, :, pl.ds(pl.multiple_of(q_start, tq), tq)]

    def score_block(j, c):
        start = pl.multiple_of(j * tk, tk)
        d = lax.dot_general(ik8_ref[0, pl.ds(start, tk), :], iq8_ref[...], _NT, preferred_element_type=F32)
        sc = jnp.zeros((tk, tq), F32)
        for h in range(IDX_HEADS):
            sc = sc + jnp.maximum(d[:, h * tq:(h + 1) * tq], 0.0) * wt[IDX_HEADS + h:IDX_HEADS + h + 1, :]
        sc = jnp.where(jnp.abs(sc) < F32_MIN_NORMAL, 0.0, sc)
        bits = lax.bitcast_convert_type(sc, jnp.int32)
        key = bits ^ ((bits >> 31) & 0x7FFFFFFF)
        adm = krow + start < adm_end
        keys_ref[pl.ds(start, tk), :] = jnp.where(adm, key, INT_MIN)
        top = lax.bitcast_convert_type(bits & -65536, F32)
        hi_ref[pl.ds(start, tk), :] = jnp.where(adm, top, jnp.nan).astype(BF16)
        return c

    lax.fori_loop(0, nkb, score_block, 0)

    def count(pred):
        def body(j, acc):
            start = pl.multiple_of(j * tk, tk)
            hit = jnp.where(pred(keys_ref[pl.ds(start, tk), :]), 1, 0)
            return acc + jnp.sum(hit.reshape(tk // COUNT_ROWS, COUNT_ROWS, tq), axis=0)
        acc = lax.fori_loop(0, nkb, body, jnp.zeros((COUNT_ROWS, tq), jnp.int32))
        return jnp.sum(acc, axis=0, keepdims=True)

    def count_hi(cand):
        def body(j, acc):
            start = pl.multiple_of(j * tk, tk)
            hit = jnp.where(hi_ref[pl.ds(start, tk), :] >= cand, jnp.ones((), BF16), jnp.zeros((), BF16))
            part = hit[0:COUNT_ROWS]
            for r in range(1, tk // COUNT_ROWS):
                part = part + hit[r * COUNT_ROWS:(r + 1) * COUNT_ROWS]
            return acc + part.astype(F32)
        acc = lax.fori_loop(0, nkb, body, jnp.zeros((COUNT_ROWS, tq), F32))
        return jnp.sum(acc, axis=0, keepdims=True)

    def bisect_hi(it, t16):
        cand = t16 + jnp.left_shift(jnp.int32(1), 15 - it)
        pattern = jnp.where(cand >= 0, cand, cand ^ 0x7FFF)
        pattern = jnp.where((pattern & 0x7F80) == 0, jnp.where(cand > 0, 0x0080, 0), pattern)
        cand_f = lax.bitcast_convert_type(jnp.left_shift(pattern, 16), F32).astype(BF16)
        return jnp.where(count_hi(cand_f) >= top_k, cand, t16)

    def bisect(it, thr):
        cand = thr + jnp.left_shift(jnp.int32(1), 31 - it)
        cnt = count(lambda kblk: kblk >= cand)
        return jnp.where(cnt >= top_k, cand, thr)

    t16 = lax.fori_loop(0, 16, bisect_hi, jnp.full((1, tq), -2 ** 15, jnp.int32))
    thr = lax.fori_loop(16, 32, bisect, jnp.left_shift(t16, 16))
    thr = jnp.maximum(thr, INT_MIN + 1)
    n_ge = count(lambda kblk: kblk >= thr)

    @pl.when(jnp.max(n_ge) > top_k)
    def _():
        n_gt = count(lambda kblk: kblk > thr)
        need = (top_k - n_gt).astype(F32)
        r2 = lax.broadcasted_iota(jnp.int32, (tk, tk), 0)
        c2 = lax.broadcasted_iota(jnp.int32, (tk, tk), 1)
        lower = jnp.where(c2 < r2, 1.0, 0.0).astype(BF16)

        def demote(j, seen):
            start = pl.multiple_of(j * tk, tk)
            kblk = keys_ref[pl.ds(start, tk), :]
            eq = kblk == thr
            eqf = jnp.where(eq, 1.0, 0.0)
            before = jnp.dot(lower, eqf.astype(BF16), preferred_element_type=F32) + seen
            keys_ref[pl.ds(start, tk), :] = jnp.where(eq & (before >= need), thr - 1, kblk)
            return seen + jnp.sum(eqf, axis=0, keepdims=True)

        lax.fori_loop(0, nkb, demote, jnp.zeros((1, tq), F32))

    m_ref[...] = jnp.full_like(m_ref, -jnp.inf)
    acc_ref[...] = jnp.zeros_like(acc_ref)

    def attend(j, c):
        start = pl.multiple_of(j * tk, tk)
        bias = jnp.where(keys_ref[pl.ds(start, tk), :] >= thr, 0.0, -jnp.inf)
        grp = min(DSA_GROUP, DSA_HEADS * tq)
        bias = jnp.concatenate([bias] * (grp // tq), axis=1)
        kb = dk4_ref[0, pl.ds(start, tk), :]
        vt = dvt_ref[0, :, pl.ds(start, tk)]
        n_groups = DSA_HEADS * tq // grp

        def scores(g):
            return lax.dot_general(kb, q8_ref[g * grp:(g + 1) * grp, :], _NT,
                                   preferred_element_type=F32) + bias

        ahead = [scores(g) for g in range(min(QK_AHEAD, n_groups))]
        for g in range(n_groups):
            s = ahead.pop(0)
            if g + QK_AHEAD < n_groups:
                ahead.append(scores(g + QK_AHEAD))
            sl = slice(g * grp, (g + 1) * grp)
            m_old = m_ref[:, sl]
            m_new = jnp.maximum(m_old, jnp.max(s, axis=0, keepdims=True))
            m_safe = jnp.where(m_new == -jnp.inf, 0.0, m_new)
            p = jnp.exp2((s - m_safe).astype(BF16))
            alpha = jnp.exp2(m_old - m_safe)
            m_ref[:, sl] = m_new
            acc_ref[:, sl] = alpha * acc_ref[:, sl] + jnp.dot(vt, p, preferred_element_type=F32)
        return c

    lax.fori_loop(0, nkb, attend, 0)
    out = acc_ref[0:HEAD_DIM, :] / acc_ref[HEAD_DIM:HEAD_DIM + 1, :]
    for h in range(DSA_HEADS):
        o_ref[0, h * HEAD_DIM:(h + 1) * HEAD_DIM, :] = out[:, h * tq:(h + 1) * tq].astype(BF16)


def _dsa_call(dq, iq, wt, dk4, dvt, ik8, hm8, hm4, top_k):
    bsz, s, _ = dq.shape
    tq = min(DSA_Q_TILE, s)

    def tok(w):
        return pl.BlockSpec((1, tq, w), lambda b, i: (b, i, 0))

    def seq(w):
        return pl.BlockSpec((1, s, w), lambda b, i: (b, 0, 0))

    def seqt(r):
        return pl.BlockSpec((1, r, s), lambda b, i: (b, 0, 0))

    return pl.pallas_call(
        functools.partial(_dsa_kernel, top_k=top_k),
        grid=(bsz, s // tq),
        in_specs=[tok(DSA_WIDTH), tok(IDX_WIDTH), seqt(2 * IDX_HEADS), seq(256), seqt(V_ROWS), seq(256),
                  pl.BlockSpec(hm8.shape, lambda b, i: (0, 0)), pl.BlockSpec(hm4.shape, lambda b, i: (0, 0))],
        out_specs=pl.BlockSpec((1, DSA_WIDTH, tq), lambda b, i: (b, 0, i)),
        out_shape=jax.ShapeDtypeStruct((bsz, DSA_WIDTH, s), BF16),
        scratch_shapes=[pltpu.VMEM((IDX_HEADS * tq, 256), BF16), pltpu.VMEM((DSA_HEADS * tq, 256), BF16),
                        pltpu.VMEM((s, tq), jnp.int32), pltpu.VMEM((s, tq), BF16),
                        pltpu.VMEM((1, DSA_HEADS * tq), F32),
                        pltpu.VMEM((V_ROWS, DSA_HEADS * tq), F32)],
        compiler_params=_cparams(("parallel", "arbitrary")),
        name="dsa_attn",
    )(dq, iq, wt, dk4, dvt, ik8, hm8, hm4)


def _merge_kernel(x_ref, mod_ref, g_ref, ya_ref, yb_ref, wg_ref, wfa_ref, wfb_ref, wo_ref, o_ref):
    x = x_ref[0]
    d = x.shape[1]
    h = _modulated(x, mod_ref, g_ref, 1).astype(BF16)
    zg = jnp.dot(h, wg_ref[...], preferred_element_type=F32)
    pa = jnp.dot(ya_ref[0], wfa_ref[...], preferred_element_type=F32)
    pb = jnp.dot(yb_ref[0], wfb_ref[...], preferred_element_type=F32)
    merged = jax.nn.sigmoid(zg[:, :d]) * pa + jax.nn.sigmoid(zg[:, d:]) * pb
    y = jnp.dot(merged.astype(BF16), wo_ref[...], preferred_element_type=F32)
    o_ref[0] = x + mod_ref[0, 5:6, :] * y


def _merge_call(x, mod, g, ya, yb, wg, wfa, wfb, wo):
    bsz, s, d = x.shape
    tm = min(TOKEN_TILE, s)

    def tok(w):
        return pl.BlockSpec((1, tm, w), lambda b, i: (b, i, 0))

    return pl.pallas_call(
        _merge_kernel,
        grid=(bsz, s // tm),
        in_specs=[tok(d),
                  pl.BlockSpec((1, 3 * N_SUBLAYERS, d), lambda b, i: (b, 0, 0)),
                  _const_spec((N_SUBLAYERS, d)),
                  tok(FOX_WIDTH), tok(DSA_WIDTH),
                  _const_spec(wg.shape), _const_spec(wfa.shape), _const_spec(wfb.shape), _const_spec(wo.shape)],
        out_specs=tok(d),
        out_shape=jax.ShapeDtypeStruct(x.shape, F32),
        compiler_params=_cparams(("parallel", "parallel")),
        name="merge_out",
    )(x, mod, g, ya, yb, wg, wfa, wfb, wo)


def _head_mask(n_heads, width):
    return jnp.asarray(np.kron(np.eye(n_heads), np.ones((1, width))), BF16)


def _rope_freq_row(rot_dim, period):
    inv_freq = ROPE_THETA ** (-jnp.arange(0, rot_dim, 2, dtype=F32) / rot_dim)
    half = rot_dim // 2
    head = jnp.concatenate([-inv_freq, inv_freq, jnp.zeros((period - 2 * half,), F32)])
    return jnp.tile(head, LANES // period)


def kernel(x, c, positions, ada_w, ada_b, norm_g, ffn1_w1, ffn1_w3, ffn1_w2, w_in, fox_f_bias, fox_qk_g, dsa_qk_g,
           w_br_fox, w_br_dsa, w_out, ffn2_w1, ffn2_w3, ffn2_w2):
    bsz, s, d = x.shape
    top_k = min(TOPK_MAX, s // 4)
    depth = ada_w.shape[0]
    pos = jnp.broadcast_to(positions.astype(F32)[:, :, None], (bsz, s, LANES))
    freq = jnp.stack([_rope_freq_row(HEAD_DIM // ROPE_FRACTION, HEAD_DIM),
                      _rope_freq_row(IDX_DIM // ROPE_FRACTION, IDX_DIM)])
    gmat = jnp.asarray(np.kron(np.eye(FOX_HEADS), np.ones((HEAD_DIM, HEAD_DIM))), BF16)
    hm2 = _head_mask(2, HEAD_DIM)
    hm4 = _head_mask(4, HEAD_DIM)
    hm8 = _head_mask(IDX_HEADS, IDX_DIM)
    fsel = jnp.asarray(np.tile(np.eye(FOX_HEADS), (1, LANES // FOX_HEADS))
                       * (np.arange(LANES) < 3 * FOX_HEADS), BF16).reshape(FOX_HEADS // 2, 2, LANES)
    o_fq, o_fk, o_fv = 0, FOX_WIDTH, 2 * FOX_WIDTH
    o_ff = 3 * FOX_WIDTH
    o_dq = o_ff + FOX_HEADS
    o_dk = o_dq + DSA_WIDTH
    o_dv = o_dk + HEAD_DIM
    o_iq = o_dv + HEAD_DIM
    o_ik = o_iq + IDX_WIDTH
    o_iw = o_ik + IDX_DIM
    o_ga = o_iw + IDX_HEADS

    for l in range(depth):
        mod = _ada_call(c, ada_w[l], ada_b[l]).reshape(bsz, 3 * N_SUBLAYERS, d)
        g = norm_g[l]
        x = _ffn_call(x, mod, g, ffn1_w1[l].astype(BF16), ffn1_w3[l].astype(BF16), ffn1_w2[l].astype(BF16), 0)

        w = w_in[l]
        wa = jnp.concatenate([w[:, o_fq:o_fq + 3 * FOX_WIDTH], w[:, o_dq:o_dq + DSA_WIDTH]], axis=1).astype(BF16)
        pad = jnp.zeros((d, LANES - FOX_HEADS - IDX_HEADS), F32)
        wb = jnp.concatenate([jnp.tile(w[:, o_dk:o_dk + HEAD_DIM], (1, 4)),
                              jnp.tile(w[:, o_ik:o_ik + IDX_DIM], (1, IDX_HEADS)),
                              jnp.tile(w[:, o_dv:o_dv + HEAD_DIM], (1, 2)),
                              w[:, o_iq:o_iq + IDX_WIDTH],
                              w[:, o_ff:o_ff + FOX_HEADS], w[:, o_iw:o_iw + IDX_HEADS], pad], axis=1).astype(BF16)
        wg = w[:, o_ga:o_ga + 2 * d].astype(BF16)
        gains = jnp.stack([jnp.tile(fox_qk_g[l, 0], FOX_HEADS), jnp.tile(fox_qk_g[l, 1], FOX_HEADS),
                           jnp.tile(dsa_qk_g[l, 0], DSA_HEADS), jnp.tile(dsa_qk_g[l, 1], DSA_HEADS)]).astype(F32)
        fbias = jnp.concatenate([fox_f_bias[l].astype(F32), jnp.zeros((LANES - FOX_HEADS,), F32)]).reshape(1, LANES)

        fq, fk, fv, dq, dk4, dv2, ik8, iq, faug, small = _mix_in_call(x, mod, g, pos, wa, wb, gmat, gains, freq, fbias)
        ones = jnp.ones((bsz, FOX_HEADS, V_ROWS - HEAD_DIM, s), BF16)
        fvt = jnp.transpose(fv.reshape(bsz, s, FOX_HEADS, HEAD_DIM), (0, 2, 3, 1))
        fvt = jnp.concatenate([fvt, ones], axis=2).reshape(bsz, FOX_HEADS * V_ROWS, s)
        dvt = jnp.concatenate([jnp.transpose(dv2[:, :, :HEAD_DIM], (0, 2, 1)), ones[:, 0]], axis=1)
        wt = jnp.transpose(small[:, :, :2 * IDX_HEADS], (0, 2, 1))
        yat = _fox_call(fq, fk, faug, fvt, hm2, fsel)
        ybt = _dsa_call(dq, iq, wt, dk4, dvt, ik8, hm8, hm4, top_k)
        ya = jnp.transpose(yat, (0, 2, 1))
        yb = jnp.transpose(ybt, (0, 2, 1))
        x = _merge_call(x, mod, g, ya, yb, wg, w_br_fox[l].astype(BF16), w_br_dsa[l].astype(BF16),
                        w_out[l].astype(BF16))
        x = _ffn_call(x, mod, g, ffn2_w1[l].astype(BF16), ffn2_w3[l].astype(BF16), ffn2_w2[l].astype(BF16), 2)
    return x
```

```python
import functools

import numpy as np
import jax
import jax.numpy as jnp
from jax import lax
from jax.experimental import pallas as pl
from jax.experimental.pallas import tpu as pltpu

D_MODEL = 1024
CHUNK = 64
HEAD_DIM = 64
FOX_HEADS = 8
DSA_HEADS = 8
IDX_HEADS = 8
IDX_DIM = 32
TOPK_MAX = 256
ROPE_THETA = 500000.0
ROPE_FRACTION = 4
D_FF = 2816
N_SUBLAYERS = 3
NORM_EPS = 1e-6
FOX_WIDTH = FOX_HEADS * HEAD_DIM
DSA_WIDTH = DSA_HEADS * HEAD_DIM
IDX_WIDTH = IDX_HEADS * IDX_DIM

LANES = 128
VMEM_LIMIT_BYTES = 56 * 1024 * 1024

TOKEN_TILE = 512
FOX_TILE = 1024
DSA_Q_TILE = 256
DSA_K_TILE = 1024
FOX_GROUP = 512
DSA_GROUP = 1024
QK_AHEAD = 2
LOW_PASSES_PER_CHECK = 4
COUNT_ROWS = 32

V_ROWS = HEAD_DIM + 16
LOG2E = 1.4426950408889634
INT_MIN = -2 ** 31
F32_MIN_NORMAL = 1.1754943508222875e-38
F32 = jnp.float32
BF16 = jnp.bfloat16
_NT = (((1,), (1,)), ((), ()))


def _cparams(sem):
    return pltpu.CompilerParams(dimension_semantics=sem, vmem_limit_bytes=VMEM_LIMIT_BYTES)


def _const_spec(shape):
    nd = len(shape)
    return pl.BlockSpec(shape, lambda *_: (0,) * nd, pipeline_mode=pl.Buffered(1))


def _modulated(x, mod_ref, g_ref, sub):
    ms = jnp.mean(x * x, axis=-1, keepdims=True)
    y = x * lax.rsqrt(ms + NORM_EPS) * g_ref[sub:sub + 1, :]
    shift = mod_ref[0, 3 * sub:3 * sub + 1, :]
    scale = mod_ref[0, 3 * sub + 1:3 * sub + 2, :]
    return y * (1.0 + scale) + shift


def _ada_kernel(c_ref, w_ref, b_ref, o_ref):
    c = c_ref[...]
    cond = c * jax.nn.sigmoid(c)
    o_ref[...] = jnp.dot(cond, w_ref[...], preferred_element_type=F32,
                         precision=lax.Precision.HIGHEST) + b_ref[...]


def _ada_call(c, w, b):
    bsz, d = c.shape
    n = w.shape[1]
    tn = 1152
    return pl.pallas_call(
        _ada_kernel,
        grid=(n // tn,),
        in_specs=[pl.BlockSpec((bsz, d), lambda j: (0, 0)),
                  pl.BlockSpec((d, tn), lambda j: (0, j)),
                  pl.BlockSpec((1, tn), lambda j: (0, j))],
        out_specs=pl.BlockSpec((bsz, tn), lambda j: (0, j)),
        out_shape=jax.ShapeDtypeStruct((bsz, n), F32),
        compiler_params=_cparams(("arbitrary",)),
        name="ada_ln",
    )(c, w, b.reshape(1, n))


def _ffn_kernel(x_ref, mod_ref, g_ref, w1_ref, w3_ref, w2_ref, o_ref, *, sub):
    x = x_ref[0]
    h = _modulated(x, mod_ref, g_ref, sub).astype(BF16)
    a = jnp.dot(h, w1_ref[...], preferred_element_type=F32)
    b = jnp.dot(h, w3_ref[...], preferred_element_type=F32)
    act = (a * jax.nn.sigmoid(a) * b).astype(BF16)
    y = jnp.dot(act, w2_ref[...], preferred_element_type=F32)
    gate = mod_ref[0, 3 * sub + 2:3 * sub + 3, :]
    o_ref[0] = x + (0.5 * gate) * y


def _ffn_call(x, mod, g, w1, w3, w2, sub):
    bsz, s, d = x.shape
    tm = min(TOKEN_TILE, s)
    f = w1.shape[1]
    tok = pl.BlockSpec((1, tm, d), lambda b, i: (b, i, 0))
    return pl.pallas_call(
        functools.partial(_ffn_kernel, sub=sub),
        grid=(bsz, s // tm),
        in_specs=[tok,
                  pl.BlockSpec((1, 3 * N_SUBLAYERS, d), lambda b, i: (b, 0, 0)),
                  _const_spec((N_SUBLAYERS, d)),
                  _const_spec((d, f)), _const_spec((d, f)), _const_spec((f, d))],
        out_specs=tok,
        out_shape=jax.ShapeDtypeStruct(x.shape, F32),
        compiler_params=_cparams(("parallel", "parallel")),
        name=f"ffn{sub}",
    )(x, mod, g, w1, w3, w2)


def _split3(v):
    hi = v.astype(BF16)
    r = v - hi.astype(F32)
    mid = r.astype(BF16)
    lo = (r - mid.astype(F32)).astype(BF16)
    return hi, mid, lo


def _group_norm(z, gmat, gain, group):
    sq = z * z
    hi = sq.astype(BF16)
    lo = (sq - hi.astype(F32)).astype(BF16)
    ssq = jnp.dot(hi, gmat, preferred_element_type=F32) + jnp.dot(lo, gmat, preferred_element_type=F32)
    return z * lax.rsqrt(ssq * (1.0 / group) + NORM_EPS) * gain


def _rope(z, cos, sin, half, period):
    width = z.shape[1]
    lane = lax.broadcasted_iota(jnp.int32, z.shape, 1) % period
    up = pltpu.roll(z, width - half, axis=1)
    dn = pltpu.roll(z, half, axis=1)
    return z * cos + jnp.where(lane < half, up, dn) * sin


def _mix_in_kernel(x_ref, mod_ref, g_ref, pos_ref, wa_ref, wb_ref, gmat_ref, gains_ref, freq_ref, fbias_ref,
                   fq_ref, fk_ref, fv_ref, dq_ref, dk4_ref, dv2_ref, ik8_ref, iq_ref, faug_ref, small_ref,
                   carry_ref):
    @pl.when(pl.program_id(1) == 0)
    def _():
        carry_ref[...] = jnp.zeros_like(carry_ref)

    x = x_ref[0]
    tm = x.shape[0]
    h = _modulated(x, mod_ref, g_ref, 1).astype(BF16)
    za = jnp.dot(h, wa_ref[...], preferred_element_type=F32)
    zb = jnp.dot(h, wb_ref[...], preferred_element_type=F32)

    gmat = gmat_ref[...]
    scale = HEAD_DIM ** -0.5 * LOG2E
    fq = _group_norm(za[:, 0:512], gmat, gains_ref[0:1, :], HEAD_DIM) * scale
    fk = _group_norm(za[:, 512:1024], gmat, gains_ref[1:2, :], HEAD_DIM)
    fq_ref[0] = fq.astype(BF16)
    fk_ref[0] = fk.astype(BF16)
    fv_ref[0] = za[:, 1024:1536].astype(BF16)

    pos = pos_ref[0]
    ang_a = pos * freq_ref[0:1, :]
    ang_i = pos * freq_ref[1:2, :]
    cos_a, sin_a = jnp.cos(ang_a), jnp.sin(ang_a)
    cos_i, sin_i = jnp.cos(ang_i), jnp.sin(ang_i)
    rot_a = HEAD_DIM // ROPE_FRACTION // 2
    rot_i = IDX_DIM // ROPE_FRACTION // 2

    dq = _group_norm(za[:, 1536:2048], gmat, gains_ref[2:3, :], HEAD_DIM)
    dq = _rope(dq, jnp.concatenate([cos_a] * 4, axis=1), jnp.concatenate([sin_a] * 4, axis=1), rot_a, HEAD_DIM)
    dq_ref[0] = (dq * scale).astype(BF16)

    dk4 = _group_norm(zb[:, 0:256], gmat[0:256, 0:256], gains_ref[3:4, 0:256], HEAD_DIM)
    dk4 = _rope(dk4, jnp.concatenate([cos_a] * 2, axis=1), jnp.concatenate([sin_a] * 2, axis=1), rot_a, HEAD_DIM)
    dk4_ref[0] = dk4.astype(BF16)

    cos_i2 = jnp.concatenate([cos_i] * 2, axis=1)
    sin_i2 = jnp.concatenate([sin_i] * 2, axis=1)
    ik8_ref[0] = _rope(zb[:, 256:512], cos_i2, sin_i2, rot_i, IDX_DIM).astype(BF16)
    dv2_ref[0] = zb[:, 512:640].astype(BF16)
    iq_ref[0] = _rope(zb[:, 640:896], cos_i2, sin_i2, rot_i, IDX_DIM).astype(BF16)

    sm = zb[:, 896:1024]
    v = sm + fbias_ref[...]
    logf = jnp.minimum(v, 0.0) - jnp.log1p(jnp.exp(-jnp.abs(v)))
    row = lax.broadcasted_iota(jnp.int32, (tm, tm), 0)
    col = lax.broadcasted_iota(jnp.int32, (tm, tm), 1)
    tri = jnp.where(row >= col, 1.0, 0.0).astype(BF16)
    hi, mid, lo = _split3(logf)
    csum = (jnp.dot(tri, hi, preferred_element_type=F32) + jnp.dot(tri, mid, preferred_element_type=F32)
            + jnp.dot(tri, lo, preferred_element_type=F32)) + carry_ref[...]
    carry_ref[...] = csum[tm - 1:tm, :]
    lane = lax.broadcasted_iota(jnp.int32, sm.shape, 1)
    small_ref[0] = jnp.where(lane < FOX_HEADS, csum, sm * ((IDX_HEADS * IDX_DIM) ** -0.5))

    nf = -csum * LOG2E
    t0 = nf.astype(BF16).astype(F32)
    r1 = nf - t0
    t1 = r1.astype(BF16).astype(F32)
    t2 = r1 - t1
    aug = jnp.where(lane < FOX_HEADS, t0,
                    jnp.where(lane < 2 * FOX_HEADS, pltpu.roll(t1, FOX_HEADS, axis=1),
                              jnp.where(lane < 3 * FOX_HEADS, pltpu.roll(t2, 2 * FOX_HEADS, axis=1), 0.0)))
    faug_ref[0] = aug.astype(BF16)


def _mix_in_call(x, mod, g, pos, wa, wb, gmat, gains, freq, fbias):
    bsz, s, d = x.shape
    tm = min(TOKEN_TILE, s)

    def tok(w):
        return pl.BlockSpec((1, tm, w), lambda b, i: (b, i, 0))

    widths = (512, 512, 512, 512, 256, 128, 256, 256, 128)
    out_shapes = [jax.ShapeDtypeStruct((bsz, s, w), BF16) for w in widths]
    out_shapes.append(jax.ShapeDtypeStruct((bsz, s, LANES), F32))
    return pl.pallas_call(
        _mix_in_kernel,
        grid=(bsz, s // tm),
        in_specs=[tok(d),
                  pl.BlockSpec((1, 3 * N_SUBLAYERS, d), lambda b, i: (b, 0, 0)),
                  _const_spec((N_SUBLAYERS, d)),
                  tok(LANES),
                  _const_spec(wa.shape), _const_spec(wb.shape), _const_spec(gmat.shape),
                  _const_spec(gains.shape), _const_spec(freq.shape), _const_spec(fbias.shape)],
        out_specs=[tok(w) for w in widths] + [tok(LANES)],
        out_shape=out_shapes,
        scratch_shapes=[pltpu.VMEM((1, LANES), F32)],
        compiler_params=_cparams(("arbitrary", "arbitrary")),
        name="mix_in",
    )(x, mod, g, pos, wa, wb, gmat, gains, freq, fbias)


def _fox_kernel(q_ref, k_ref, faug_ref, vt_ref, hm_ref, fsel_ref, o_ref, qf_ref, m_ref, acc_ref):
    i = pl.program_id(2)
    tq = q_ref.shape[1]
    tk = tq
    q = q_ref[0]
    for a in range(2):
        sel_row = fsel_ref[0, a:a + 1, :]
        qf_ref[a * tq:(a + 1) * tq, 0:LANES] = q * hm_ref[a:a + 1, :]
        qf_ref[a * tq:(a + 1) * tq, LANES:2 * LANES] = jnp.broadcast_to(sel_row, (tq, LANES))
    m_ref[...] = jnp.full_like(m_ref, -jnp.inf)
    acc_ref[...] = jnp.zeros_like(acc_ref)

    def block(j, masked):
        start = pl.multiple_of(j * tk, tk)
        kf = jnp.concatenate([k_ref[0, pl.ds(start, tk), :], faug_ref[0, pl.ds(start, tk), :]], axis=1)
        grp = min(FOX_GROUP, tq)
        n_groups = 2 * tq // grp

        def scores(g):
            s = lax.dot_general(kf, qf_ref[g * grp:(g + 1) * grp, :], _NT,
                                preferred_element_type=F32)
            if masked:
                krow = lax.broadcasted_iota(jnp.int32, s.shape, 0)
                qcol = lax.broadcasted_iota(jnp.int32, s.shape, 1) + (g * grp) % tq
                s = jnp.where(krow <= qcol, s, -jnp.inf)
            return s

        ahead = [scores(g) for g in range(min(QK_AHEAD, n_groups))]
        for g in range(n_groups):
            s = ahead.pop(0)
            if g + QK_AHEAD < n_groups:
                ahead.append(scores(g + QK_AHEAD))
            sl = slice(g * grp, (g + 1) * grp)
            m_old = m_ref[:, sl]
            m_new = jnp.maximum(m_old, jnp.max(s, axis=0, keepdims=True))
            p = jnp.exp2((s - m_new).astype(BF16))
            alpha = jnp.exp2(m_old - m_new)
            m_ref[:, sl] = m_new
            a = (g * grp) // tq
            vt = vt_ref[0, a * V_ROWS:(a + 1) * V_ROWS, pl.ds(start, tk)]
            acc_ref[:, sl] = alpha * acc_ref[:, sl] + jnp.dot(vt, p, preferred_element_type=F32)

    def body(j, c):
        block(j, False)
        return c

    lax.fori_loop(0, i, body, 0)
    block(i, True)
    out = acc_ref[0:HEAD_DIM, :] / acc_ref[HEAD_DIM:HEAD_DIM + 1, :]
    for a in range(2):
        o_ref[0, a * HEAD_DIM:(a + 1) * HEAD_DIM, :] = out[:, a * tq:(a + 1) * tq].astype(BF16)


def _fox_call(fq, fk, faug, fvt, hm2, fsel):
    bsz, s, _ = fq.shape
    tq = min(FOX_TILE, s)
    pairs = FOX_HEADS // 2
    return pl.pallas_call(
        _fox_kernel,
        grid=(bsz, pairs, s // tq),
        in_specs=[pl.BlockSpec((1, tq, LANES), lambda b, h, i: (b, i, h)),
                  pl.BlockSpec((1, s, LANES), lambda b, h, i: (b, 0, h)),
                  pl.BlockSpec((1, s, LANES), lambda b, h, i: (b, 0, 0)),
                  pl.BlockSpec((1, 2 * V_ROWS, s), lambda b, h, i: (b, h, 0)),
                  pl.BlockSpec((2, LANES), lambda b, h, i: (0, 0)),
                  pl.BlockSpec((1, 2, LANES), lambda b, h, i: (h, 0, 0))],
        out_specs=pl.BlockSpec((1, LANES, tq), lambda b, h, i: (b, h, i)),
        out_shape=jax.ShapeDtypeStruct((bsz, FOX_WIDTH, s), BF16),
        scratch_shapes=[pltpu.VMEM((2 * tq, 2 * LANES), BF16),
                        pltpu.VMEM((1, 2 * tq), F32),
                        pltpu.VMEM((V_ROWS, 2 * tq), F32)],
        compiler_params=_cparams(("parallel", "parallel", "arbitrary")),
        name="fox_attn",
    )(fq, fk, faug, fvt, hm2, fsel)


def _dsa_kernel(dq_ref, iq_ref, wt_ref, dk4_ref, dvt_ref, ik8_ref, hm8_ref, hm4_ref, o_ref,
                iq8_ref, q8_ref, keys_ref, hi_ref, m_ref, acc_ref, *, top_k):
    i = pl.program_id(1)
    tq = dq_ref.shape[1]
    s_len = dk4_ref.shape[1]
    tk = min(DSA_K_TILE, s_len)
    half_tk = tk // 2
    q_start = i * tq
    rem = (q_start + tq) % tk
    n_full = (q_start + tq) // tk + jnp.where(rem > half_tk, 1, 0)
    has_tail = (rem > 0) & (rem <= half_tk)

    def sweep(fn, init):
        carry = lax.fori_loop(0, n_full, lambda j, c: fn(pl.multiple_of(j * tk, tk), tk, c), init)
        return lax.cond(has_tail, lambda c: fn(pl.multiple_of(n_full * tk, tk), half_tk, c), lambda c: c, carry)

    iq = iq_ref[0]
    dq = dq_ref[0]
    half = dq.shape[1] // 2
    for h in range(IDX_HEADS):
        iq8_ref[h * tq:(h + 1) * tq, :] = iq * hm8_ref[h:h + 1, :]
        q8_ref[h * tq:(h + 1) * tq, :] = dq[:, (h // 4) * half:(h // 4 + 1) * half] * hm4_ref[h % 4:h % 4 + 1, :]

    adm_end = q_start + (lax.broadcasted_iota(jnp.int32, (1, tq), 1) // CHUNK + 1) * CHUNK
    wt = wt_ref[0, :, pl.ds(pl.multiple_of(q_start, tq), tq)]

    def score_block(start, size, c):
        d = lax.dot_general(ik8_ref[0, pl.ds(start, size), :], iq8_ref[...], _NT, preferred_element_type=F32)
        sc = jnp.zeros((size, tq), F32)
        for h in range(IDX_HEADS):
            sc = sc + jnp.maximum(d[:, h * tq:(h + 1) * tq], 0.0) * wt[IDX_HEADS + h:IDX_HEADS + h + 1, :]
        sc = jnp.where(jnp.abs(sc) < F32_MIN_NORMAL, 0.0, sc)
        bits = lax.bitcast_convert_type(sc, jnp.int32)
        key = bits ^ ((bits >> 31) & 0x7FFFFFFF)
        adm = lax.broadcasted_iota(jnp.int32, (size, tq), 0) + start < adm_end
        keys_ref[pl.ds(start, size), :] = jnp.where(adm, key, INT_MIN)
        top = lax.bitcast_convert_type(bits & -65536, F32)
        hi_ref[pl.ds(start, size), :] = jnp.where(adm, top, jnp.nan).astype(BF16)
        return c

    sweep(score_block, 0)

    def count(pred):
        def body(start, size, acc):
            hit = jnp.where(pred(keys_ref[pl.ds(start, size), :]), 1, 0)
            return acc + jnp.sum(hit.reshape(size // COUNT_ROWS, COUNT_ROWS, tq), axis=0)
        return jnp.sum(sweep(body, jnp.zeros((COUNT_ROWS, tq), jnp.int32)), axis=0, keepdims=True)

    def count_hi(cand):
        def body(start, size, acc):
            hit = jnp.where(hi_ref[pl.ds(start, size), :] >= cand, jnp.ones((), BF16), jnp.zeros((), BF16))
            part = hit[0:COUNT_ROWS]
            for r in range(1, size // COUNT_ROWS):
                part = part + hit[r * COUNT_ROWS:(r + 1) * COUNT_ROWS]
            return acc + part.astype(F32)
        return jnp.sum(sweep(body, jnp.zeros((COUNT_ROWS, tq), F32)), axis=0, keepdims=True)

    def step(cand, cnt, thr, done):
        ok = (cnt >= top_k) & (done == 0)
        return jnp.where(ok, cand, thr), jnp.where(ok & (cnt == top_k), 1, done)

    def bisect_hi(it, state):
        t16, done = state
        cand = t16 + jnp.left_shift(jnp.int32(1), 15 - it)
        pattern = jnp.where(cand >= 0, cand, cand ^ 0x7FFF)
        pattern = jnp.where((pattern & 0x7F80) == 0, jnp.where(cand > 0, 0x0080, 0), pattern)
        cand_f = lax.bitcast_convert_type(jnp.left_shift(pattern, 16), F32).astype(BF16)
        return step(cand, count_hi(cand_f), t16, done)

    def bisect_lo(state):
        g, thr, done = state
        for u in range(LOW_PASSES_PER_CHECK):
            cand = thr + jnp.left_shift(jnp.int32(1), 15 - (g * LOW_PASSES_PER_CHECK + u))
            thr, done = step(cand, count(lambda kblk: kblk >= cand), thr, done)
        return g + 1, thr, done

    done = jnp.where(adm_end <= top_k, 1, 0)
    t16, done = lax.fori_loop(0, 16, bisect_hi, (jnp.full((1, tq), -2 ** 15, jnp.int32), done))
    _, thr, _ = lax.while_loop(lambda st: (st[0] < 16 // LOW_PASSES_PER_CHECK) & (jnp.min(st[2]) == 0),
                               bisect_lo, (jnp.int32(0), jnp.left_shift(t16, 16), done))
    thr = jnp.maximum(thr, INT_MIN + 1)
    n_ge = count(lambda kblk: kblk >= thr)

    @pl.when(jnp.max(n_ge) > top_k)
    def _():
        n_gt = count(lambda kblk: kblk > thr)
        need = (top_k - n_gt).astype(F32)

        def demote(start, size, seen):
            r2 = lax.broadcasted_iota(jnp.int32, (size, size), 0)
            c2 = lax.broadcasted_iota(jnp.int32, (size, size), 1)
            lower = jnp.where(c2 < r2, 1.0, 0.0).astype(BF16)
            kblk = keys_ref[pl.ds(start, size), :]
            eq = kblk == thr
            eqf = jnp.where(eq, 1.0, 0.0)
            before = jnp.dot(lower, eqf.astype(BF16), preferred_element_type=F32) + seen
            keys_ref[pl.ds(start, size), :] = jnp.where(eq & (before >= need), thr - 1, kblk)
            return seen + jnp.sum(eqf, axis=0, keepdims=True)

        sweep(demote, jnp.zeros((1, tq), F32))

    m_ref[...] = jnp.full_like(m_ref, -jnp.inf)
    acc_ref[...] = jnp.zeros_like(acc_ref)

    def attend(start, size, c):
        bias = jnp.where(keys_ref[pl.ds(start, size), :] >= thr, 0.0, -jnp.inf)
        grp = min(DSA_GROUP, DSA_HEADS * tq)
        bias = jnp.concatenate([bias] * (grp // tq), axis=1)
        kb = dk4_ref[0, pl.ds(start, size), :]
        vt = dvt_ref[0, :, pl.ds(start, size)]
        n_groups = DSA_HEADS * tq // grp

        def scores(g):
            return lax.dot_general(kb, q8_ref[g * grp:(g + 1) * grp, :], _NT,
                                   preferred_element_type=F32) + bias

        ahead = [scores(g) for g in range(min(QK_AHEAD, n_groups))]
        for g in range(n_groups):
            s = ahead.pop(0)
            if g + QK_AHEAD < n_groups:
                ahead.append(scores(g + QK_AHEAD))
            sl = slice(g * grp, (g + 1) * grp)
            m_old = m_ref[:, sl]
            m_new = jnp.maximum(m_old, jnp.max(s, axis=0, keepdims=True))
            m_safe = jnp.where(m_new == -jnp.inf, 0.0, m_new)
            p = jnp.exp2((s - m_safe).astype(BF16))
            alpha = jnp.exp2(m_old - m_safe)
            m_ref[:, sl] = m_new
            acc_ref[:, sl] = alpha * acc_ref[:, sl] + jnp.dot(vt, p, preferred_element_type=F32)
        return c

    sweep(attend, 0)
    out = acc_ref[0:HEAD_DIM, :] / acc_ref[HEAD_DIM:HEAD_DIM + 1, :]
    for h in range(DSA_HEADS):
        o_ref[0, h * HEAD_DIM:(h + 1) * HEAD_DIM, :] = out[:, h * tq:(h + 1) * tq].astype(BF16)


def _dsa_call(dq, iq, wt, dk4, dvt, ik8, hm8, hm4, top_k):
    bsz, s, _ = dq.shape
    tq = min(DSA_Q_TILE, s)

    def tok(w):
        return pl.BlockSpec((1, tq, w), lambda b, i: (b, i, 0))

    def seq(w):
        return pl.BlockSpec((1, s, w), lambda b, i: (b, 0, 0))

    def seqt(r):
        return pl.BlockSpec((1, r, s), lambda b, i: (b, 0, 0))

    return pl.pallas_call(
        functools.partial(_dsa_kernel, top_k=top_k),
        grid=(bsz, s // tq),
        in_specs=[tok(DSA_WIDTH), tok(IDX_WIDTH), seqt(2 * IDX_HEADS), seq(256), seqt(V_ROWS), seq(256),
                  pl.BlockSpec(hm8.shape, lambda b, i: (0, 0)), pl.BlockSpec(hm4.shape, lambda b, i: (0, 0))],
        out_specs=pl.BlockSpec((1, DSA_WIDTH, tq), lambda b, i: (b, 0, i)),
        out_shape=jax.ShapeDtypeStruct((bsz, DSA_WIDTH, s), BF16),
        scratch_shapes=[pltpu.VMEM((IDX_HEADS * tq, 256), BF16), pltpu.VMEM((DSA_HEADS * tq, 256), BF16),
                        pltpu.VMEM((s, tq), jnp.int32), pltpu.VMEM((s, tq), BF16),
                        pltpu.VMEM((1, DSA_HEADS * tq), F32),
                        pltpu.VMEM((V_ROWS, DSA_HEADS * tq), F32)],
        compiler_params=_cparams(("parallel", "arbitrary")),
        name="dsa_attn",
    )(dq, iq, wt, dk4, dvt, ik8, hm8, hm4)


def _merge_kernel(x_ref, mod_ref, g_ref, ya_ref, yb_ref, wg_ref, wfa_ref, wfb_ref, wo_ref, o_ref):
    x = x_ref[0]
    d = x.shape[1]
    h = _modulated(x, mod_ref, g_ref, 1).astype(BF16)
    zg = jnp.dot(h, wg_ref[...], preferred_element_type=F32)
    pa = jnp.dot(ya_ref[0], wfa_ref[...], preferred_element_type=F32)
    pb = jnp.dot(yb_ref[0], wfb_ref[...], preferred_element_type=F32)
    merged = jax.nn.sigmoid(zg[:, :d]) * pa + jax.nn.sigmoid(zg[:, d:]) * pb
    y = jnp.dot(merged.astype(BF16), wo_ref[...], preferred_element_type=F32)
    o_ref[0] = x + mod_ref[0, 5:6, :] * y


def _merge_call(x, mod, g, ya, yb, wg, wfa, wfb, wo):
    bsz, s, d = x.shape
    tm = min(TOKEN_TILE, s)

    def tok(w):
        return pl.BlockSpec((1, tm, w), lambda b, i: (b, i, 0))

    return pl.pallas_call(
        _merge_kernel,
        grid=(bsz, s // tm),
        in_specs=[tok(d),
                  pl.BlockSpec((1, 3 * N_SUBLAYERS, d), lambda b, i: (b, 0, 0)),
                  _const_spec((N_SUBLAYERS, d)),
                  tok(FOX_WIDTH), tok(DSA_WIDTH),
                  _const_spec(wg.shape), _const_spec(wfa.shape), _const_spec(wfb.shape), _const_spec(wo.shape)],
        out_specs=tok(d),
        out_shape=jax.ShapeDtypeStruct(x.shape, F32),
        compiler_params=_cparams(("parallel", "parallel")),
        name="merge_out",
    )(x, mod, g, ya, yb, wg, wfa, wfb, wo)


def _head_mask(n_heads, width):
    return jnp.asarray(np.kron(np.eye(n_heads), np.ones((1, width))), BF16)


def _rope_freq_row(rot_dim, period):
    inv_freq = ROPE_THETA ** (-jnp.arange(0, rot_dim, 2, dtype=F32) / rot_dim)
    half = rot_dim // 2
    head = jnp.concatenate([-inv_freq, inv_freq, jnp.zeros((period - 2 * half,), F32)])
    return jnp.tile(head, LANES // period)


def kernel(x, c, positions, ada_w, ada_b, norm_g, ffn1_w1, ffn1_w3, ffn1_w2, w_in, fox_f_bias, fox_qk_g, dsa_qk_g,
           w_br_fox, w_br_dsa, w_out, ffn2_w1, ffn2_w3, ffn2_w2):
    bsz, s, d = x.shape
    top_k = min(TOPK_MAX, s // 4)
    depth = ada_w.shape[0]
    pos = jnp.broadcast_to(positions.astype(F32)[:, :, None], (bsz, s, LANES))
    freq = jnp.stack([_rope_freq_row(HEAD_DIM // ROPE_FRACTION, HEAD_DIM),
                      _rope_freq_row(IDX_DIM // ROPE_FRACTION, IDX_DIM)])
    gmat = jnp.asarray(np.kron(np.eye(FOX_HEADS), np.ones((HEAD_DIM, HEAD_DIM))), BF16)
    hm2 = _head_mask(2, HEAD_DIM)
    hm4 = _head_mask(4, HEAD_DIM)
    hm8 = _head_mask(IDX_HEADS, IDX_DIM)
    fsel = jnp.asarray(np.tile(np.eye(FOX_HEADS), (1, LANES // FOX_HEADS))
                       * (np.arange(LANES) < 3 * FOX_HEADS), BF16).reshape(FOX_HEADS // 2, 2, LANES)
    o_fq, o_fk, o_fv = 0, FOX_WIDTH, 2 * FOX_WIDTH
    o_ff = 3 * FOX_WIDTH
    o_dq = o_ff + FOX_HEADS
    o_dk = o_dq + DSA_WIDTH
    o_dv = o_dk + HEAD_DIM
    o_iq = o_dv + HEAD_DIM
    o_ik = o_iq + IDX_WIDTH
    o_iw = o_ik + IDX_DIM
    o_ga = o_iw + IDX_HEADS

    for l in range(depth):
        mod = _ada_call(c, ada_w[l], ada_b[l]).reshape(bsz, 3 * N_SUBLAYERS, d)
        g = norm_g[l]
        x = _ffn_call(x, mod, g, ffn1_w1[l].astype(BF16), ffn1_w3[l].astype(BF16), ffn1_w2[l].astype(BF16), 0)

        w = w_in[l]
        wa = jnp.concatenate([w[:, o_fq:o_fq + 3 * FOX_WIDTH], w[:, o_dq:o_dq + DSA_WIDTH]], axis=1).astype(BF16)
        pad = jnp.zeros((d, LANES - FOX_HEADS - IDX_HEADS), F32)
        wb = jnp.concatenate([jnp.tile(w[:, o_dk:o_dk + HEAD_DIM], (1, 4)),
                              jnp.tile(w[:, o_ik:o_ik + IDX_DIM], (1, IDX_HEADS)),
                              jnp.tile(w[:, o_dv:o_dv + HEAD_DIM], (1, 2)),
                              w[:, o_iq:o_iq + IDX_WIDTH],
                              w[:, o_ff:o_ff + FOX_HEADS], w[:, o_iw:o_iw + IDX_HEADS], pad], axis=1).astype(BF16)
        wg = w[:, o_ga:o_ga + 2 * d].astype(BF16)
        gains = jnp.stack([jnp.tile(fox_qk_g[l, 0], FOX_HEADS), jnp.tile(fox_qk_g[l, 1], FOX_HEADS),
                           jnp.tile(dsa_qk_g[l, 0], DSA_HEADS), jnp.tile(dsa_qk_g[l, 1], DSA_HEADS)]).astype(F32)
        fbias = jnp.concatenate([fox_f_bias[l].astype(F32), jnp.zeros((LANES - FOX_HEADS,), F32)]).reshape(1, LANES)

        fq, fk, fv, dq, dk4, dv2, ik8, iq, faug, small = _mix_in_call(x, mod, g, pos, wa, wb, gmat, gains, freq, fbias)
        ones = jnp.ones((bsz, FOX_HEADS, V_ROWS - HEAD_DIM, s), BF16)
        fvt = jnp.transpose(fv.reshape(bsz, s, FOX_HEADS, HEAD_DIM), (0, 2, 3, 1))
        fvt = jnp.concatenate([fvt, ones], axis=2).reshape(bsz, FOX_HEADS * V_ROWS, s)
        dvt = jnp.concatenate([jnp.transpose(dv2[:, :, :HEAD_DIM], (0, 2, 1)), ones[:, 0]], axis=1)
        wt = jnp.transpose(small[:, :, :2 * IDX_HEADS], (0, 2, 1))
        yat = _fox_call(fq, fk, faug, fvt, hm2, fsel)
        ybt = _dsa_call(dq, iq, wt, dk4, dvt, ik8, hm8, hm4, top_k)
        ya = jnp.transpose(yat, (0, 2, 1))
        yb = jnp.transpose(ybt, (0, 2, 1))
        x = _merge_call(x, mod, g, ya, yb, wg, w_br_fox[l].astype(BF16), w_br_dsa[l].astype(BF16),
                        w_out[l].astype(BF16))
        x = _ffn_call(x, mod, g, ffn2_w1[l].astype(BF16), ffn2_w3[l].astype(BF16), ffn2_w2[l].astype(BF16), 2)
    return x
```

```python
import functools

import numpy as np
import jax
import jax.numpy as jnp
from jax import lax
from jax.experimental import pallas as pl
from jax.experimental.pallas import tpu as pltpu

D_MODEL = 1024
CHUNK = 64
HEAD_DIM = 64
FOX_HEADS = 8
DSA_HEADS = 8
IDX_HEADS = 8
IDX_DIM = 32
TOPK_MAX = 256
ROPE_THETA = 500000.0
ROPE_FRACTION = 4
D_FF = 2816
N_SUBLAYERS = 3
NORM_EPS = 1e-6
FOX_WIDTH = FOX_HEADS * HEAD_DIM
DSA_WIDTH = DSA_HEADS * HEAD_DIM
IDX_WIDTH = IDX_HEADS * IDX_DIM

LANES = 128
VMEM_LIMIT_BYTES = 56 * 1024 * 1024

TOKEN_TILE = 512
FOX_TILE = 1024
DSA_Q_TILE = 256
DSA_K_TILE = 1024
FOX_GROUP = 512
DSA_GROUP = 1024
QK_AHEAD = 3
LOW_PASSES_PER_CHECK = 4
COUNT_ROWS = 32

V_ROWS = HEAD_DIM + 16
LOG2E = 1.4426950408889634
INT_MIN = -2 ** 31
F32_MIN_NORMAL = 1.1754943508222875e-38
F32 = jnp.float32
BF16 = jnp.bfloat16
_NT = (((1,), (1,)), ((), ()))
_TN = (((0,), (0,)), ((), ()))


def _cparams(sem):
    return pltpu.CompilerParams(dimension_semantics=sem, vmem_limit_bytes=VMEM_LIMIT_BYTES)


def _const_spec(shape):
    nd = len(shape)
    return pl.BlockSpec(shape, lambda *_: (0,) * nd, pipeline_mode=pl.Buffered(1))


def _modulated(x, mod_ref, g_ref, sub):
    ms = jnp.mean(x * x, axis=-1, keepdims=True)
    y = x * lax.rsqrt(ms + NORM_EPS) * g_ref[sub:sub + 1, :]
    shift = mod_ref[0, 3 * sub:3 * sub + 1, :]
    scale = mod_ref[0, 3 * sub + 1:3 * sub + 2, :]
    return y * (1.0 + scale) + shift


def _ada_kernel(c_ref, w_ref, b_ref, o_ref):
    c = c_ref[...]
    cond = c * jax.nn.sigmoid(c)
    o_ref[...] = jnp.dot(cond, w_ref[...], preferred_element_type=F32,
                         precision=lax.Precision.HIGHEST) + b_ref[...]


def _ada_call(c, w, b):
    bsz, d = c.shape
    n = w.shape[1]
    tn = 1152
    return pl.pallas_call(
        _ada_kernel,
        grid=(n // tn,),
        in_specs=[pl.BlockSpec((bsz, d), lambda j: (0, 0)),
                  pl.BlockSpec((d, tn), lambda j: (0, j)),
                  pl.BlockSpec((1, tn), lambda j: (0, j))],
        out_specs=pl.BlockSpec((bsz, tn), lambda j: (0, j)),
        out_shape=jax.ShapeDtypeStruct((bsz, n), F32),
        compiler_params=_cparams(("arbitrary",)),
        name="ada_ln",
    )(c, w, b.reshape(1, n))


def _ffn_kernel(x_ref, mod_ref, g_ref, w1_ref, w3_ref, w2_ref, o_ref, *, sub):
    x = x_ref[0]
    h = _modulated(x, mod_ref, g_ref, sub).astype(BF16)
    a = jnp.dot(h, w1_ref[...], preferred_element_type=F32)
    b = jnp.dot(h, w3_ref[...], preferred_element_type=F32)
    act = (a * jax.nn.sigmoid(a) * b).astype(BF16)
    y = jnp.dot(act, w2_ref[...], preferred_element_type=F32)
    gate = mod_ref[0, 3 * sub + 2:3 * sub + 3, :]
    o_ref[0] = x + (0.5 * gate) * y


def _ffn_call(x, mod, g, w1, w3, w2, sub):
    bsz, s, d = x.shape
    tm = min(TOKEN_TILE, s)
    f = w1.shape[1]
    tok = pl.BlockSpec((1, tm, d), lambda b, i: (b, i, 0))
    return pl.pallas_call(
        functools.partial(_ffn_kernel, sub=sub),
        grid=(bsz, s // tm),
        in_specs=[tok,
                  pl.BlockSpec((1, 3 * N_SUBLAYERS, d), lambda b, i: (b, 0, 0)),
                  _const_spec((N_SUBLAYERS, d)),
                  _const_spec((d, f)), _const_spec((d, f)), _const_spec((f, d))],
        out_specs=tok,
        out_shape=jax.ShapeDtypeStruct(x.shape, F32),
        compiler_params=_cparams(("parallel", "parallel")),
        name=f"ffn{sub}",
    )(x, mod, g, w1, w3, w2)


def _split3(v):
    hi = v.astype(BF16)
    r = v - hi.astype(F32)
    mid = r.astype(BF16)
    lo = (r - mid.astype(F32)).astype(BF16)
    return hi, mid, lo


def _group_norm(z, gmat, gain, group):
    sq = z * z
    hi = sq.astype(BF16)
    lo = (sq - hi.astype(F32)).astype(BF16)
    ssq = jnp.dot(hi, gmat, preferred_element_type=F32) + jnp.dot(lo, gmat, preferred_element_type=F32)
    return z * lax.rsqrt(ssq * (1.0 / group) + NORM_EPS) * gain


def _rope(z, cos, sin, half, period):
    width = z.shape[1]
    lane = lax.broadcasted_iota(jnp.int32, z.shape, 1) % period
    up = pltpu.roll(z, width - half, axis=1)
    dn = pltpu.roll(z, half, axis=1)
    return z * cos + jnp.where(lane < half, up, dn) * sin


def _mix_in_kernel(x_ref, mod_ref, g_ref, pos_ref, wa_ref, wb_ref, wt_ref, ones_ref, gmat_ref, gains_ref, freq_ref,
                   fbias_ref, fq_ref, fk_ref, dq_ref, dk4_ref, ik8_ref, iq_ref, faug_ref, fvt_ref, dvt_ref, iwt_ref,
                   carry_ref):
    @pl.when(pl.program_id(1) == 0)
    def _():
        carry_ref[...] = jnp.zeros_like(carry_ref)

    x = x_ref[0]
    tm = x.shape[0]
    h = _modulated(x, mod_ref, g_ref, 1).astype(BF16)
    za = jnp.dot(h, wa_ref[...], preferred_element_type=F32)
    zb = jnp.dot(h, wb_ref[...], preferred_element_type=F32)
    zt = lax.dot_general(wt_ref[...], h, _NT, preferred_element_type=F32)
    n_v = FOX_HEADS * V_ROWS
    fvt_ref[0] = (zt[0:n_v] + jnp.concatenate([ones_ref[...]] * (tm // LANES), axis=1)).astype(BF16)
    dvt_ref[0] = (zt[n_v:n_v + V_ROWS]
                  + jnp.concatenate([ones_ref[0:V_ROWS, :]] * (tm // LANES), axis=1)).astype(BF16)
    iwt_ref[0] = zt[n_v + V_ROWS:, :] * ((IDX_HEADS * IDX_DIM) ** -0.5)

    gmat = gmat_ref[...]
    scale = HEAD_DIM ** -0.5 * LOG2E
    fq = _group_norm(za[:, 0:512], gmat, gains_ref[0:1, :], HEAD_DIM) * scale
    fk = _group_norm(za[:, 512:1024], gmat, gains_ref[1:2, :], HEAD_DIM)
    fq_ref[0] = fq.astype(BF16)
    fk_ref[0] = fk.astype(BF16)

    pos = pos_ref[0]
    ang_a = pos * freq_ref[0:1, :]
    ang_i = pos * freq_ref[1:2, :]
    cos_a, sin_a = jnp.cos(ang_a), jnp.sin(ang_a)
    cos_i, sin_i = jnp.cos(ang_i), jnp.sin(ang_i)
    rot_a = HEAD_DIM // ROPE_FRACTION // 2
    rot_i = IDX_DIM // ROPE_FRACTION // 2

    dq = _group_norm(za[:, 1024:1536], gmat, gains_ref[2:3, :], HEAD_DIM)
    dq = _rope(dq, jnp.concatenate([cos_a] * 4, axis=1), jnp.concatenate([sin_a] * 4, axis=1), rot_a, HEAD_DIM)
    dq_ref[0] = (dq * scale).astype(BF16)

    dk4 = _group_norm(zb[:, 0:256], gmat[0:256, 0:256], gains_ref[3:4, 0:256], HEAD_DIM)
    dk4 = _rope(dk4, jnp.concatenate([cos_a] * 2, axis=1), jnp.concatenate([sin_a] * 2, axis=1), rot_a, HEAD_DIM)
    dk4_ref[0] = dk4.astype(BF16)

    cos_i2 = jnp.concatenate([cos_i] * 2, axis=1)
    sin_i2 = jnp.concatenate([sin_i] * 2, axis=1)
    ik8_ref[0] = _rope(zb[:, 256:512], cos_i2, sin_i2, rot_i, IDX_DIM).astype(BF16)
    iq_ref[0] = _rope(zb[:, 512:768], cos_i2, sin_i2, rot_i, IDX_DIM).astype(BF16)

    sm = zb[:, 768:896]
    v = sm + fbias_ref[...]
    logf = jnp.minimum(v, 0.0) - jnp.log1p(jnp.exp(-jnp.abs(v)))
    row = lax.broadcasted_iota(jnp.int32, (tm, tm), 0)
    col = lax.broadcasted_iota(jnp.int32, (tm, tm), 1)
    tri = jnp.where(row >= col, 1.0, 0.0).astype(BF16)
    hi, mid, lo = _split3(logf)
    csum = (jnp.dot(tri, hi, preferred_element_type=F32) + jnp.dot(tri, mid, preferred_element_type=F32)
            + jnp.dot(tri, lo, preferred_element_type=F32)) + carry_ref[...]
    carry_ref[...] = csum[tm - 1:tm, :]
    lane = lax.broadcasted_iota(jnp.int32, sm.shape, 1)

    nf = -csum * LOG2E
    t0 = nf.astype(BF16).astype(F32)
    r1 = nf - t0
    t1 = r1.astype(BF16).astype(F32)
    t2 = r1 - t1
    aug = jnp.where(lane < FOX_HEADS, t0,
                    jnp.where(lane < 2 * FOX_HEADS, pltpu.roll(t1, FOX_HEADS, axis=1),
                              jnp.where(lane < 3 * FOX_HEADS, pltpu.roll(t2, 2 * FOX_HEADS, axis=1), 0.0)))
    faug_ref[0] = aug.astype(BF16)


def _mix_in_call(x, mod, g, pos, wa, wb, wt, ones, gmat, gains, freq, fbias):
    bsz, s, d = x.shape
    tm = min(TOKEN_TILE, s)

    def tok(w):
        return pl.BlockSpec((1, tm, w), lambda b, i: (b, i, 0))

    def tokt(r):
        return pl.BlockSpec((1, r, tm), lambda b, i: (b, 0, i))

    widths = (512, 512, 512, 256, 256, 256, 128)
    rows = ((FOX_HEADS * V_ROWS, BF16), (V_ROWS, BF16), (2 * IDX_HEADS, F32))
    out_shapes = [jax.ShapeDtypeStruct((bsz, s, w), BF16) for w in widths]
    out_shapes += [jax.ShapeDtypeStruct((bsz, r, s), dt) for r, dt in rows]
    return pl.pallas_call(
        _mix_in_kernel,
        grid=(bsz, s // tm),
        in_specs=[tok(d),
                  pl.BlockSpec((1, 3 * N_SUBLAYERS, d), lambda b, i: (b, 0, 0)),
                  _const_spec((N_SUBLAYERS, d)),
                  tok(LANES),
                  _const_spec(wa.shape), _const_spec(wb.shape), _const_spec(wt.shape), _const_spec(ones.shape),
                  _const_spec(gmat.shape), _const_spec(gains.shape), _const_spec(freq.shape), _const_spec(fbias.shape)],
        out_specs=[tok(w) for w in widths] + [tokt(r) for r, _ in rows],
        out_shape=out_shapes,
        scratch_shapes=[pltpu.VMEM((1, LANES), F32)],
        compiler_params=_cparams(("arbitrary", "arbitrary")),
        name="mix_in",
    )(x, mod, g, pos, wa, wb, wt, ones, gmat, gains, freq, fbias)


def _fox_kernel(q_ref, k_ref, faug_ref, vt_ref, hm_ref, fsel_ref, o_ref, qf_ref, m_ref, acc_ref):
    i = pl.program_id(2)
    tq = q_ref.shape[1]
    tk = tq
    q = q_ref[0]
    for a in range(2):
        sel_row = fsel_ref[0, a:a + 1, :]
        qf_ref[a * tq:(a + 1) * tq, 0:LANES] = q * hm_ref[a:a + 1, :]
        qf_ref[a * tq:(a + 1) * tq, LANES:2 * LANES] = jnp.broadcast_to(sel_row, (tq, LANES))
    m_ref[...] = jnp.full_like(m_ref, -jnp.inf)
    acc_ref[...] = jnp.zeros_like(acc_ref)

    def block(j, masked):
        start = pl.multiple_of(j * tk, tk)
        kf = jnp.concatenate([k_ref[0, pl.ds(start, tk), :], faug_ref[0, pl.ds(start, tk), :]], axis=1)
        grp = min(FOX_GROUP, tq)
        n_groups = 2 * tq // grp

        def rows(g):
            return min(tk, (g * grp) % tq + grp) if masked else tk

        def scores(g):
            s = lax.dot_general(kf[:rows(g)], qf_ref[g * grp:(g + 1) * grp, :], _NT,
                                preferred_element_type=F32)
            if masked:
                krow = lax.broadcasted_iota(jnp.int32, s.shape, 0)
                qcol = lax.broadcasted_iota(jnp.int32, s.shape, 1) + (g * grp) % tq
                s = jnp.where(krow <= qcol, s, -jnp.inf)
            return s

        ahead = [scores(g) for g in range(min(QK_AHEAD, n_groups))]
        for g in range(n_groups):
            s = ahead.pop(0)
            if g + QK_AHEAD < n_groups:
                ahead.append(scores(g + QK_AHEAD))
            sl = slice(g * grp, (g + 1) * grp)
            m_old = m_ref[:, sl]
            m_new = jnp.maximum(m_old, jnp.max(s, axis=0, keepdims=True))
            p = jnp.exp2((s - m_new).astype(BF16))
            alpha = jnp.exp2(m_old - m_new)
            m_ref[:, sl] = m_new
            a = (g * grp) // tq
            vt = vt_ref[0, a * V_ROWS:(a + 1) * V_ROWS, pl.ds(start, rows(g))]
            acc_ref[:, sl] = alpha * acc_ref[:, sl] + jnp.dot(vt, p, preferred_element_type=F32)

    def body(j, c):
        block(j, False)
        return c

    lax.fori_loop(0, i, body, 0)
    block(i, True)
    out = acc_ref[0:HEAD_DIM, :] / acc_ref[HEAD_DIM:HEAD_DIM + 1, :]
    for a in range(2):
        o_ref[0, a * HEAD_DIM:(a + 1) * HEAD_DIM, :] = out[:, a * tq:(a + 1) * tq].astype(BF16)


def _fox_call(fq, fk, faug, fvt, hm2, fsel):
    bsz, s, _ = fq.shape
    tq = min(FOX_TILE, s)
    pairs = FOX_HEADS // 2
    return pl.pallas_call(
        _fox_kernel,
        grid=(bsz, pairs, s // tq),
        in_specs=[pl.BlockSpec((1, tq, LANES), lambda b, h, i: (b, i, h)),
                  pl.BlockSpec((1, s, LANES), lambda b, h, i: (b, 0, h)),
                  pl.BlockSpec((1, s, LANES), lambda b, h, i: (b, 0, 0)),
                  pl.BlockSpec((1, 2 * V_ROWS, s), lambda b, h, i: (b, h, 0)),
                  pl.BlockSpec((2, LANES), lambda b, h, i: (0, 0)),
                  pl.BlockSpec((1, 2, LANES), lambda b, h, i: (h, 0, 0))],
        out_specs=pl.BlockSpec((1, LANES, tq), lambda b, h, i: (b, h, i)),
        out_shape=jax.ShapeDtypeStruct((bsz, FOX_WIDTH, s), BF16),
        scratch_shapes=[pltpu.VMEM((2 * tq, 2 * LANES), BF16),
                        pltpu.VMEM((1, 2 * tq), F32),
                        pltpu.VMEM((V_ROWS, 2 * tq), F32)],
        compiler_params=_cparams(("parallel", "parallel", "arbitrary")),
        name="fox_attn",
    )(fq, fk, faug, fvt, hm2, fsel)


def _dsa_kernel(dq_ref, iq_ref, wt_ref, dk4_ref, dvt_ref, ik8_ref, hm8_ref, hm4_ref, o_ref,
                iq8_ref, q8_ref, keys_ref, hi_ref, m_ref, acc_ref, *, top_k):
    i = pl.program_id(1)
    tq = dq_ref.shape[1]
    s_len = dk4_ref.shape[1]
    tk = min(DSA_K_TILE, s_len)
    half_tk = tk // 2
    q_start = i * tq
    rem = (q_start + tq) % tk
    n_full = (q_start + tq) // tk + jnp.where(rem > half_tk, 1, 0)
    has_tail = (rem > 0) & (rem <= half_tk)

    def sweep(fn, init):
        carry = lax.fori_loop(0, n_full, lambda j, c: fn(pl.multiple_of(j * tk, tk), tk, c), init)
        return lax.cond(has_tail, lambda c: fn(pl.multiple_of(n_full * tk, tk), half_tk, c), lambda c: c, carry)

    iq = iq_ref[0]
    dq = dq_ref[0]
    half = dq.shape[1] // 2
    for h in range(IDX_HEADS):
        iq8_ref[h * tq:(h + 1) * tq, :] = iq * hm8_ref[h:h + 1, :]
        q8_ref[h * tq:(h + 1) * tq, :] = dq[:, (h // 4) * half:(h // 4 + 1) * half] * hm4_ref[h % 4:h % 4 + 1, :]

    adm_end = q_start + (lax.broadcasted_iota(jnp.int32, (1, tq), 1) // CHUNK + 1) * CHUNK
    wt = wt_ref[0, :, pl.ds(pl.multiple_of(q_start, tq), tq)]

    def score_block(start, size, c):
        d = lax.dot_general(ik8_ref[0, pl.ds(start, size), :], iq8_ref[...], _NT, preferred_element_type=F32)
        sc = jnp.zeros((size, tq), F32)
        for h in range(IDX_HEADS):
            sc = sc + jnp.maximum(d[:, h * tq:(h + 1) * tq], 0.0) * wt[IDX_HEADS + h:IDX_HEADS + h + 1, :]
        sc = jnp.where(jnp.abs(sc) < F32_MIN_NORMAL, 0.0, sc)
        bits = lax.bitcast_convert_type(sc, jnp.int32)
        key = bits ^ ((bits >> 31) & 0x7FFFFFFF)
        adm = lax.broadcasted_iota(jnp.int32, (size, tq), 0) + start < adm_end
        keys_ref[pl.ds(start, size), :] = jnp.where(adm, key, INT_MIN)
        top = lax.bitcast_convert_type(bits & -65536, F32)
        hi_ref[pl.ds(start, size), :] = jnp.where(adm, top, jnp.nan).astype(BF16)
        return c

    sweep(score_block, 0)

    def count(pred):
        def body(start, size, acc):
            hit = jnp.where(pred(keys_ref[pl.ds(start, size), :]), 1, 0)
            return acc + jnp.sum(hit.reshape(size // COUNT_ROWS, COUNT_ROWS, tq), axis=0)
        return jnp.sum(sweep(body, jnp.zeros((COUNT_ROWS, tq), jnp.int32)), axis=0, keepdims=True)

    def count_hi(cand):
        def body(start, size, acc):
            hit = jnp.where(hi_ref[pl.ds(start, size), :] >= cand, jnp.ones((), BF16), jnp.zeros((), BF16))
            part = hit[0:COUNT_ROWS]
            for r in range(1, size // COUNT_ROWS):
                part = part + hit[r * COUNT_ROWS:(r + 1) * COUNT_ROWS]
            return acc + part.astype(F32)
        return jnp.sum(sweep(body, jnp.zeros((COUNT_ROWS, tq), F32)), axis=0, keepdims=True)

    def step(cand, cnt, thr, done):
        ok = (cnt >= top_k) & (done == 0)
        return jnp.where(ok, cand, thr), jnp.where(ok & (cnt == top_k), 1, done)

    def bisect_hi(it, state):
        t16, done = state
        cand = t16 + jnp.left_shift(jnp.int32(1), 15 - it)
        pattern = jnp.where(cand >= 0, cand, cand ^ 0x7FFF)
        pattern = jnp.where((pattern & 0x7F80) == 0, jnp.where(cand > 0, 0x0080, 0), pattern)
        cand_f = lax.bitcast_convert_type(jnp.left_shift(pattern, 16), F32).astype(BF16)
        return step(cand, count_hi(cand_f), t16, done)

    def bisect_lo(state):
        g, thr, done = state
        for u in range(LOW_PASSES_PER_CHECK):
            cand = thr + jnp.left_shift(jnp.int32(1), 15 - (g * LOW_PASSES_PER_CHECK + u))
            thr, done = step(cand, count(lambda kblk: kblk >= cand), thr, done)
        return g + 1, thr, done

    done = jnp.where(adm_end <= top_k, 1, 0)
    t16, done = lax.fori_loop(0, 16, bisect_hi, (jnp.full((1, tq), -2 ** 15, jnp.int32), done))
    _, thr, _ = lax.while_loop(lambda st: (st[0] < 16 // LOW_PASSES_PER_CHECK) & (jnp.min(st[2]) == 0),
                               bisect_lo, (jnp.int32(0), jnp.left_shift(t16, 16), done))
    thr = jnp.maximum(thr, INT_MIN + 1)
    n_ge = count(lambda kblk: kblk >= thr)

    @pl.when(jnp.max(n_ge) > top_k)
    def _():
        n_gt = count(lambda kblk: kblk > thr)
        need = (top_k - n_gt).astype(F32)

        def demote(start, size, seen):
            r2 = lax.broadcasted_iota(jnp.int32, (size, size), 0)
            c2 = lax.broadcasted_iota(jnp.int32, (size, size), 1)
            lower = jnp.where(c2 < r2, 1.0, 0.0).astype(BF16)
            kblk = keys_ref[pl.ds(start, size), :]
            eq = kblk == thr
            eqf = jnp.where(eq, 1.0, 0.0)
            before = jnp.dot(lower, eqf.astype(BF16), preferred_element_type=F32) + seen
            keys_ref[pl.ds(start, size), :] = jnp.where(eq & (before >= need), thr - 1, kblk)
            return seen + jnp.sum(eqf, axis=0, keepdims=True)

        sweep(demote, jnp.zeros((1, tq), F32))

    m_ref[...] = jnp.full_like(m_ref, -jnp.inf)
    acc_ref[...] = jnp.zeros_like(acc_ref)

    def attend(start, size, c):
        bias = jnp.where(keys_ref[pl.ds(start, size), :] >= thr, 0.0, -jnp.inf)
        grp = min(DSA_GROUP, DSA_HEADS * tq)
        bias = jnp.concatenate([bias] * (grp // tq), axis=1)
        kb = dk4_ref[0, pl.ds(start, size), :]
        vt = dvt_ref[0, :, pl.ds(start, size)]
        n_groups = DSA_HEADS * tq // grp

        def scores(g):
            return lax.dot_general(kb, q8_ref[g * grp:(g + 1) * grp, :], _NT,
                                   preferred_element_type=F32) + bias

        ahead = [scores(g) for g in range(min(QK_AHEAD, n_groups))]
        for g in range(n_groups):
            s = ahead.pop(0)
            if g + QK_AHEAD < n_groups:
                ahead.append(scores(g + QK_AHEAD))
            sl = slice(g * grp, (g + 1) * grp)
            m_old = m_ref[:, sl]
            m_new = jnp.maximum(m_old, jnp.max(s, axis=0, keepdims=True))
            m_safe = jnp.where(m_new == -jnp.inf, 0.0, m_new)
            p = jnp.exp2((s - m_safe).astype(BF16))
            alpha = jnp.exp2(m_old - m_safe)
            m_ref[:, sl] = m_new
            acc_ref[:, sl] = alpha * acc_ref[:, sl] + jnp.dot(vt, p, preferred_element_type=F32)
        return c

    sweep(attend, 0)
    out = acc_ref[0:HEAD_DIM, :] / acc_ref[HEAD_DIM:HEAD_DIM + 1, :]
    for h in range(DSA_HEADS):
        o_ref[0, h * HEAD_DIM:(h + 1) * HEAD_DIM, :] = out[:, h * tq:(h + 1) * tq].astype(BF16)


def _dsa_call(dq, iq, wt, dk4, dvt, ik8, hm8, hm4, top_k):
    bsz, s, _ = dq.shape
    tq = min(DSA_Q_TILE, s)

    def tok(w):
        return pl.BlockSpec((1, tq, w), lambda b, i: (b, i, 0))

    def seq(w):
        return pl.BlockSpec((1, s, w), lambda b, i: (b, 0, 0))

    def seqt(r):
        return pl.BlockSpec((1, r, s), lambda b, i: (b, 0, 0))

    return pl.pallas_call(
        functools.partial(_dsa_kernel, top_k=top_k),
        grid=(bsz, s // tq),
        in_specs=[tok(DSA_WIDTH), tok(IDX_WIDTH), seqt(2 * IDX_HEADS), seq(256), seqt(V_ROWS), seq(256),
                  pl.BlockSpec(hm8.shape, lambda b, i: (0, 0)), pl.BlockSpec(hm4.shape, lambda b, i: (0, 0))],
        out_specs=pl.BlockSpec((1, DSA_WIDTH, tq), lambda b, i: (b, 0, i)),
        out_shape=jax.ShapeDtypeStruct((bsz, DSA_WIDTH, s), BF16),
        scratch_shapes=[pltpu.VMEM((IDX_HEADS * tq, 256), BF16), pltpu.VMEM((DSA_HEADS * tq, 256), BF16),
                        pltpu.VMEM((s, tq), jnp.int32), pltpu.VMEM((s, tq), BF16),
                        pltpu.VMEM((1, DSA_HEADS * tq), F32),
                        pltpu.VMEM((V_ROWS, DSA_HEADS * tq), F32)],
        compiler_params=_cparams(("parallel", "arbitrary")),
        name="dsa_attn",
    )(dq, iq, wt, dk4, dvt, ik8, hm8, hm4)


def _merge_kernel(x_ref, mod_ref, g_ref, ya_ref, yb_ref, wg_ref, wfa_ref, wfb_ref, wo_ref, o_ref):
    x = x_ref[0]
    d = x.shape[1]
    h = _modulated(x, mod_ref, g_ref, 1).astype(BF16)
    zg = jnp.dot(h, wg_ref[...], preferred_element_type=F32)
    pa = lax.dot_general(ya_ref[0], wfa_ref[...], _TN, preferred_element_type=F32)
    pb = lax.dot_general(yb_ref[0], wfb_ref[...], _TN, preferred_element_type=F32)
    merged = jax.nn.sigmoid(zg[:, :d]) * pa + jax.nn.sigmoid(zg[:, d:]) * pb
    y = jnp.dot(merged.astype(BF16), wo_ref[...], preferred_element_type=F32)
    o_ref[0] = x + mod_ref[0, 5:6, :] * y


def _merge_call(x, mod, g, ya, yb, wg, wfa, wfb, wo):
    bsz, s, d = x.shape
    tm = min(TOKEN_TILE, s)

    def tok(w):
        return pl.BlockSpec((1, tm, w), lambda b, i: (b, i, 0))

    def tokt(w):
        return pl.BlockSpec((1, w, tm), lambda b, i: (b, 0, i))

    return pl.pallas_call(
        _merge_kernel,
        grid=(bsz, s // tm),
        in_specs=[tok(d),
                  pl.BlockSpec((1, 3 * N_SUBLAYERS, d), lambda b, i: (b, 0, 0)),
                  _const_spec((N_SUBLAYERS, d)),
                  tokt(FOX_WIDTH), tokt(DSA_WIDTH),
                  _const_spec(wg.shape), _const_spec(wfa.shape), _const_spec(wfb.shape), _const_spec(wo.shape)],
        out_specs=tok(d),
        out_shape=jax.ShapeDtypeStruct(x.shape, F32),
        compiler_params=_cparams(("parallel", "parallel")),
        name="merge_out",
    )(x, mod, g, ya, yb, wg, wfa, wfb, wo)


def _head_mask(n_heads, width):
    return jnp.asarray(np.kron(np.eye(n_heads), np.ones((1, width))), BF16)


def _rope_freq_row(rot_dim, period):
    inv_freq = ROPE_THETA ** (-jnp.arange(0, rot_dim, 2, dtype=F32) / rot_dim)
    half = rot_dim // 2
    head = jnp.concatenate([-inv_freq, inv_freq, jnp.zeros((period - 2 * half,), F32)])
    return jnp.tile(head, LANES // period)


def kernel(x, c, positions, ada_w, ada_b, norm_g, ffn1_w1, ffn1_w3, ffn1_w2, w_in, fox_f_bias, fox_qk_g, dsa_qk_g,
           w_br_fox, w_br_dsa, w_out, ffn2_w1, ffn2_w3, ffn2_w2):
    bsz, s, d = x.shape
    top_k = min(TOPK_MAX, s // 4)
    depth = ada_w.shape[0]
    pos = jnp.broadcast_to(positions.astype(F32)[:, :, None], (bsz, s, LANES))
    freq = jnp.stack([_rope_freq_row(HEAD_DIM // ROPE_FRACTION, HEAD_DIM),
                      _rope_freq_row(IDX_DIM // ROPE_FRACTION, IDX_DIM)])
    gmat = jnp.asarray(np.kron(np.eye(FOX_HEADS), np.ones((HEAD_DIM, HEAD_DIM))), BF16)
    hm2 = _head_mask(2, HEAD_DIM)
    hm4 = _head_mask(4, HEAD_DIM)
    hm8 = _head_mask(IDX_HEADS, IDX_DIM)
    fsel = jnp.asarray(np.tile(np.eye(FOX_HEADS), (1, LANES // FOX_HEADS))
                       * (np.arange(LANES) < 3 * FOX_HEADS), BF16).reshape(FOX_HEADS // 2, 2, LANES)
    o_fq, o_fk, o_fv = 0, FOX_WIDTH, 2 * FOX_WIDTH
    o_ff = 3 * FOX_WIDTH
    o_dq = o_ff + FOX_HEADS
    o_dk = o_dq + DSA_WIDTH
    o_dv = o_dk + HEAD_DIM
    o_iq = o_dv + HEAD_DIM
    o_ik = o_iq + IDX_WIDTH
    o_iw = o_ik + IDX_DIM
    o_ga = o_iw + IDX_HEADS

    for l in range(depth):
        mod = _ada_call(c, ada_w[l], ada_b[l]).reshape(bsz, 3 * N_SUBLAYERS, d)
        g = norm_g[l]
        x = _ffn_call(x, mod, g, ffn1_w1[l].astype(BF16), ffn1_w3[l].astype(BF16), ffn1_w2[l].astype(BF16), 0)

        w = w_in[l]
        wa = jnp.concatenate([w[:, o_fq:o_fq + 2 * FOX_WIDTH], w[:, o_dq:o_dq + DSA_WIDTH]], axis=1).astype(BF16)
        pad = jnp.zeros((d, LANES - FOX_HEADS), F32)
        wb = jnp.concatenate([jnp.tile(w[:, o_dk:o_dk + HEAD_DIM], (1, 4)),
                              jnp.tile(w[:, o_ik:o_ik + IDX_DIM], (1, IDX_HEADS)),
                              w[:, o_iq:o_iq + IDX_WIDTH],
                              w[:, o_ff:o_ff + FOX_HEADS], pad], axis=1).astype(BF16)
        zrows = jnp.zeros((FOX_HEADS, V_ROWS - HEAD_DIM, d), F32)
        wfv = jnp.transpose(w[:, o_fv:o_fv + FOX_WIDTH]).reshape(FOX_HEADS, HEAD_DIM, d)
        wt = jnp.concatenate([jnp.concatenate([wfv, zrows], axis=1).reshape(FOX_HEADS * V_ROWS, d),
                              jnp.transpose(w[:, o_dv:o_dv + HEAD_DIM]), zrows[0],
                              jnp.zeros((IDX_HEADS, d), F32), jnp.transpose(w[:, o_iw:o_iw + IDX_HEADS])],
                             axis=0).astype(BF16)
        ones = jnp.asarray(np.tile((np.arange(FOX_HEADS * V_ROWS) % V_ROWS >= HEAD_DIM)[:, None], (1, LANES)), F32)
        wg = w[:, o_ga:o_ga + 2 * d].astype(BF16)
        gains = jnp.stack([jnp.tile(fox_qk_g[l, 0], FOX_HEADS), jnp.tile(fox_qk_g[l, 1], FOX_HEADS),
                           jnp.tile(dsa_qk_g[l, 0], DSA_HEADS), jnp.tile(dsa_qk_g[l, 1], DSA_HEADS)]).astype(F32)
        fbias = jnp.concatenate([fox_f_bias[l].astype(F32), jnp.zeros((LANES - FOX_HEADS,), F32)]).reshape(1, LANES)

        fq, fk, dq, dk4, ik8, iq, faug, fvt, dvt, iwt = _mix_in_call(x, mod, g, pos, wa, wb, wt, ones, gmat, gains,
                                                                       freq, fbias)
        yat = _fox_call(fq, fk, faug, fvt, hm2, fsel)
        ybt = _dsa_call(dq, iq, iwt, dk4, dvt, ik8, hm8, hm4, top_k)
        x = _merge_call(x, mod, g, yat, ybt, wg, w_br_fox[l].astype(BF16), w_br_dsa[l].astype(BF16),
                        w_out[l].astype(BF16))
        x = _ffn_call(x, mod, g, ffn2_w1[l].astype(BF16), ffn2_w3[l].astype(BF16), ffn2_w2[l].astype(BF16), 2)
    return x
```

```python
import functools

import numpy as np
import jax
import jax.numpy as jnp
from jax import lax
from jax.experimental import pallas as pl
from jax.experimental.pallas import tpu as pltpu

D_MODEL = 1024
CHUNK = 64
HEAD_DIM = 64
FOX_HEADS = 8
DSA_HEADS = 8
IDX_HEADS = 8
IDX_DIM = 32
TOPK_MAX = 256
ROPE_THETA = 500000.0
ROPE_FRACTION = 4
D_FF = 2816
N_SUBLAYERS = 3
NORM_EPS = 1e-6
FOX_WIDTH = FOX_HEADS * HEAD_DIM
DSA_WIDTH = DSA_HEADS * HEAD_DIM
IDX_WIDTH = IDX_HEADS * IDX_DIM

LANES = 128
VMEM_LIMIT_BYTES = 56 * 1024 * 1024

TOKEN_TILE = 512
FOX_TILE = 1024
DSA_Q_TILE = 256
DSA_K_TILE = 1024
FOX_GROUP = 512
DSA_GROUP = 1024
QK_AHEAD = 3
LOW_PASSES_PER_CHECK = 4
COUNT_ROWS = 32

V_ROWS = HEAD_DIM + 16
LOG2E = 1.4426950408889634
INT_MIN = -2 ** 31
F32_MIN_NORMAL = 1.1754943508222875e-38
F32 = jnp.float32
BF16 = jnp.bfloat16
_NT = (((1,), (1,)), ((), ()))
_TN = (((0,), (0,)), ((), ()))


def _cparams(sem):
    return pltpu.CompilerParams(dimension_semantics=sem, vmem_limit_bytes=VMEM_LIMIT_BYTES)


def _const_spec(shape):
    nd = len(shape)
    return pl.BlockSpec(shape, lambda *_: (0,) * nd, pipeline_mode=pl.Buffered(1))


def _modulated(x, mod_ref, g_ref, sub):
    ms = jnp.mean(x * x, axis=-1, keepdims=True)
    y = x * lax.rsqrt(ms + NORM_EPS) * g_ref[sub:sub + 1, :]
    shift = mod_ref[0, 3 * sub:3 * sub + 1, :]
    scale = mod_ref[0, 3 * sub + 1:3 * sub + 2, :]
    return y * (1.0 + scale) + shift


def _ada_kernel(c_ref, w_ref, b_ref, o_ref):
    c = c_ref[...]
    cond = c * jax.nn.sigmoid(c)
    o_ref[...] = jnp.dot(cond, w_ref[...], preferred_element_type=F32,
                         precision=lax.Precision.HIGHEST) + b_ref[...]


def _ada_call(c, w, b):
    bsz, d = c.shape
    n = w.shape[1]
    tn = 1152
    return pl.pallas_call(
        _ada_kernel,
        grid=(n // tn,),
        in_specs=[pl.BlockSpec((bsz, d), lambda j: (0, 0)),
                  pl.BlockSpec((d, tn), lambda j: (0, j)),
                  pl.BlockSpec((1, tn), lambda j: (0, j))],
        out_specs=pl.BlockSpec((bsz, tn), lambda j: (0, j)),
        out_shape=jax.ShapeDtypeStruct((bsz, n), F32),
        compiler_params=_cparams(("arbitrary",)),
        name="ada_ln",
    )(c, w, b.reshape(1, n))


def _ffn_kernel(x_ref, mod_ref, g_ref, w1_ref, w3_ref, w2_ref, o_ref, *, sub):
    x = x_ref[0]
    h = _modulated(x, mod_ref, g_ref, sub).astype(BF16)
    a = jnp.dot(h, w1_ref[...], preferred_element_type=F32)
    b = jnp.dot(h, w3_ref[...], preferred_element_type=F32)
    act = (a * jax.nn.sigmoid(a) * b).astype(BF16)
    y = jnp.dot(act, w2_ref[...], preferred_element_type=F32)
    gate = mod_ref[0, 3 * sub + 2:3 * sub + 3, :]
    o_ref[0] = x + (0.5 * gate) * y


def _ffn_call(x, mod, g, w1, w3, w2, sub):
    bsz, s, d = x.shape
    tm = min(TOKEN_TILE, s)
    f = w1.shape[1]
    tok = pl.BlockSpec((1, tm, d), lambda b, i: (b, i, 0))
    return pl.pallas_call(
        functools.partial(_ffn_kernel, sub=sub),
        grid=(bsz, s // tm),
        in_specs=[tok,
                  pl.BlockSpec((1, 3 * N_SUBLAYERS, d), lambda b, i: (b, 0, 0)),
                  _const_spec((N_SUBLAYERS, d)),
                  _const_spec((d, f)), _const_spec((d, f)), _const_spec((f, d))],
        out_specs=tok,
        out_shape=jax.ShapeDtypeStruct(x.shape, F32),
        compiler_params=_cparams(("parallel", "parallel")),
        name=f"ffn{sub}",
    )(x, mod, g, w1, w3, w2)


def _split3(v):
    hi = v.astype(BF16)
    r = v - hi.astype(F32)
    mid = r.astype(BF16)
    lo = (r - mid.astype(F32)).astype(BF16)
    return hi, mid, lo


def _group_norm(z, gmat, gain, group):
    ssq = jnp.dot((z * z).astype(BF16), gmat, preferred_element_type=F32)
    return z * lax.rsqrt(ssq * (1.0 / group) + NORM_EPS) * gain


def _rope(z, cos, sin, half, period):
    width = z.shape[1]
    lane = lax.broadcasted_iota(jnp.int32, z.shape, 1) % period
    up = pltpu.roll(z, width - half, axis=1)
    dn = pltpu.roll(z, half, axis=1)
    return z * cos + jnp.where(lane < half, up, dn) * sin


def _mix_in_kernel(x_ref, mod_ref, g_ref, pos_ref, wa_ref, wb_ref, wt_ref, ones_ref, gmat_ref, gains_ref, freq_ref,
                   fbias_ref, fq_ref, fk_ref, dq_ref, dk4_ref, ik8_ref, iq_ref, faug_ref, fvt_ref, dvt_ref, iwt_ref,
                   carry_ref):
    @pl.when(pl.program_id(1) == 0)
    def _():
        carry_ref[...] = jnp.zeros_like(carry_ref)

    x = x_ref[0]
    tm = x.shape[0]
    h = _modulated(x, mod_ref, g_ref, 1).astype(BF16)
    za = jnp.dot(h, wa_ref[...], preferred_element_type=F32)
    zb = jnp.dot(h, wb_ref[...], preferred_element_type=F32)
    zt = lax.dot_general(wt_ref[...], h, _NT, preferred_element_type=F32)
    n_v = FOX_HEADS * V_ROWS
    fvt_ref[0] = (zt[0:n_v] + jnp.concatenate([ones_ref[...]] * (tm // LANES), axis=1)).astype(BF16)
    dvt_ref[0] = (zt[n_v:n_v + V_ROWS]
                  + jnp.concatenate([ones_ref[0:V_ROWS, :]] * (tm // LANES), axis=1)).astype(BF16)
    iwt_ref[0] = zt[n_v + V_ROWS:, :] * ((IDX_HEADS * IDX_DIM) ** -0.5)

    gmat = gmat_ref[...]
    scale = HEAD_DIM ** -0.5 * LOG2E
    fq = _group_norm(za[:, 0:512], gmat, gains_ref[0:1, :], HEAD_DIM) * scale
    fk = _group_norm(za[:, 512:1024], gmat, gains_ref[1:2, :], HEAD_DIM)
    fq_ref[0] = fq.astype(BF16)
    fk_ref[0] = fk.astype(BF16)

    pos = pos_ref[0]
    ang_a = pos * freq_ref[0:1, :]
    ang_i = pos * freq_ref[1:2, :]
    cos_a, sin_a = jnp.cos(ang_a), jnp.sin(ang_a)
    cos_i, sin_i = jnp.cos(ang_i), jnp.sin(ang_i)
    rot_a = HEAD_DIM // ROPE_FRACTION // 2
    rot_i = IDX_DIM // ROPE_FRACTION // 2

    dq = _group_norm(za[:, 1024:1536], gmat, gains_ref[2:3, :], HEAD_DIM)
    dq = _rope(dq, jnp.concatenate([cos_a] * 4, axis=1), jnp.concatenate([sin_a] * 4, axis=1), rot_a, HEAD_DIM)
    dq_ref[0] = (dq * scale).astype(BF16)

    dk4 = _group_norm(zb[:, 0:256], gmat[0:256, 0:256], gains_ref[3:4, 0:256], HEAD_DIM)
    dk4 = _rope(dk4, jnp.concatenate([cos_a] * 2, axis=1), jnp.concatenate([sin_a] * 2, axis=1), rot_a, HEAD_DIM)
    dk4_ref[0] = dk4.astype(BF16)

    cos_i2 = jnp.concatenate([cos_i] * 2, axis=1)
    sin_i2 = jnp.concatenate([sin_i] * 2, axis=1)
    ik8_ref[0] = _rope(zb[:, 256:512], cos_i2, sin_i2, rot_i, IDX_DIM).astype(BF16)
    iq_ref[0] = _rope(zb[:, 512:768], cos_i2, sin_i2, rot_i, IDX_DIM).astype(BF16)

    sm = zb[:, 768:896]
    v = sm + fbias_ref[...]
    logf = jnp.minimum(v, 0.0) - jnp.log1p(jnp.exp(-jnp.abs(v)))
    row = lax.broadcasted_iota(jnp.int32, (tm, tm), 0)
    col = lax.broadcasted_iota(jnp.int32, (tm, tm), 1)
    tri = jnp.where(row >= col, 1.0, 0.0).astype(BF16)
    hi, mid, lo = _split3(logf)
    csum = (jnp.dot(tri, hi, preferred_element_type=F32) + jnp.dot(tri, mid, preferred_element_type=F32)
            + jnp.dot(tri, lo, preferred_element_type=F32)) + carry_ref[...]
    carry_ref[...] = csum[tm - 1:tm, :]
    lane = lax.broadcasted_iota(jnp.int32, sm.shape, 1)

    nf = -csum * LOG2E
    t0 = nf.astype(BF16).astype(F32)
    r1 = nf - t0
    t1 = r1.astype(BF16).astype(F32)
    t2 = r1 - t1
    aug = jnp.where(lane < FOX_HEADS, t0,
                    jnp.where(lane < 2 * FOX_HEADS, pltpu.roll(t1, FOX_HEADS, axis=1),
                              jnp.where(lane < 3 * FOX_HEADS, pltpu.roll(t2, 2 * FOX_HEADS, axis=1), 0.0)))
    faug_ref[0] = aug.astype(BF16)


def _mix_in_call(x, mod, g, pos, wa, wb, wt, ones, gmat, gains, freq, fbias):
    bsz, s, d = x.shape
    tm = min(TOKEN_TILE, s)

    def tok(w):
        return pl.BlockSpec((1, tm, w), lambda b, i: (b, i, 0))

    def tokt(r):
        return pl.BlockSpec((1, r, tm), lambda b, i: (b, 0, i))

    widths = (512, 512, 512, 256, 256, 256, 128)
    rows = ((FOX_HEADS * V_ROWS, BF16), (V_ROWS, BF16), (2 * IDX_HEADS, F32))
    out_shapes = [jax.ShapeDtypeStruct((bsz, s, w), BF16) for w in widths]
    out_shapes += [jax.ShapeDtypeStruct((bsz, r, s), dt) for r, dt in rows]
    return pl.pallas_call(
        _mix_in_kernel,
        grid=(bsz, s // tm),
        in_specs=[tok(d),
                  pl.BlockSpec((1, 3 * N_SUBLAYERS, d), lambda b, i: (b, 0, 0)),
                  _const_spec((N_SUBLAYERS, d)),
                  tok(LANES),
                  _const_spec(wa.shape), _const_spec(wb.shape), _const_spec(wt.shape), _const_spec(ones.shape),
                  _const_spec(gmat.shape), _const_spec(gains.shape), _const_spec(freq.shape), _const_spec(fbias.shape)],
        out_specs=[tok(w) for w in widths] + [tokt(r) for r, _ in rows],
        out_shape=out_shapes,
        scratch_shapes=[pltpu.VMEM((1, LANES), F32)],
        compiler_params=_cparams(("arbitrary", "arbitrary")),
        name="mix_in",
    )(x, mod, g, pos, wa, wb, wt, ones, gmat, gains, freq, fbias)


def _fox_kernel(q_ref, k_ref, faug_ref, vt_ref, hm_ref, fsel_ref, o_ref, qf_ref, m_ref, acc_ref):
    i = pl.program_id(2)
    tq = q_ref.shape[1]
    tk = tq
    q = q_ref[0]
    for a in range(2):
        sel_row = fsel_ref[0, a:a + 1, :]
        qf_ref[a * tq:(a + 1) * tq, 0:LANES] = q * hm_ref[a:a + 1, :]
        qf_ref[a * tq:(a + 1) * tq, LANES:2 * LANES] = jnp.broadcast_to(sel_row, (tq, LANES))
    m_ref[...] = jnp.full_like(m_ref, -jnp.inf)
    acc_ref[...] = jnp.zeros_like(acc_ref)

    def block(j, masked):
        start = pl.multiple_of(j * tk, tk)
        kf = jnp.concatenate([k_ref[0, pl.ds(start, tk), :], faug_ref[0, pl.ds(start, tk), :]], axis=1)
        grp = min(FOX_GROUP, tq)
        n_groups = 2 * tq // grp

        def rows(g):
            return min(tk, (g * grp) % tq + grp) if masked else tk

        def scores(g):
            s = lax.dot_general(kf[:rows(g)], qf_ref[g * grp:(g + 1) * grp, :], _NT,
                                preferred_element_type=F32)
            if masked:
                krow = lax.broadcasted_iota(jnp.int32, s.shape, 0)
                qcol = lax.broadcasted_iota(jnp.int32, s.shape, 1) + (g * grp) % tq
                s = jnp.where(krow <= qcol, s, -jnp.inf)
            return s

        ahead = [scores(g) for g in range(min(QK_AHEAD, n_groups))]
        for g in range(n_groups):
            s = ahead.pop(0)
            if g + QK_AHEAD < n_groups:
                ahead.append(scores(g + QK_AHEAD))
            sl = slice(g * grp, (g + 1) * grp)
            m_old = m_ref[:, sl]
            m_new = jnp.maximum(m_old, jnp.max(s, axis=0, keepdims=True))
            p = jnp.exp2((s - m_new).astype(BF16))
            alpha = jnp.exp2(m_old - m_new)
            m_ref[:, sl] = m_new
            a = (g * grp) // tq
            vt = vt_ref[0, a * V_ROWS:(a + 1) * V_ROWS, pl.ds(start, rows(g))]
            acc_ref[:, sl] = alpha * acc_ref[:, sl] + jnp.dot(vt, p, preferred_element_type=F32)

    def body(j, c):
        block(j, False)
        return c

    lax.fori_loop(0, i, body, 0)
    block(i, True)
    out = acc_ref[0:HEAD_DIM, :] / acc_ref[HEAD_DIM:HEAD_DIM + 1, :]
    for a in range(2):
        o_ref[0, a * HEAD_DIM:(a + 1) * HEAD_DIM, :] = out[:, a * tq:(a + 1) * tq].astype(BF16)


def _fox_call(fq, fk, faug, fvt, hm2, fsel):
    bsz, s, _ = fq.shape
    tq = min(FOX_TILE, s)
    pairs = FOX_HEADS // 2
    return pl.pallas_call(
        _fox_kernel,
        grid=(bsz, pairs, s // tq),
        in_specs=[pl.BlockSpec((1, tq, LANES), lambda b, h, i: (b, i, h)),
                  pl.BlockSpec((1, s, LANES), lambda b, h, i: (b, 0, h)),
                  pl.BlockSpec((1, s, LANES), lambda b, h, i: (b, 0, 0)),
                  pl.BlockSpec((1, 2 * V_ROWS, s), lambda b, h, i: (b, h, 0)),
                  pl.BlockSpec((2, LANES), lambda b, h, i: (0, 0)),
                  pl.BlockSpec((1, 2, LANES), lambda b, h, i: (h, 0, 0))],
        out_specs=pl.BlockSpec((1, LANES, tq), lambda b, h, i: (b, h, i)),
        out_shape=jax.ShapeDtypeStruct((bsz, FOX_WIDTH, s), BF16),
        scratch_shapes=[pltpu.VMEM((2 * tq, 2 * LANES), BF16),
                        pltpu.VMEM((1, 2 * tq), F32),
                        pltpu.VMEM((V_ROWS, 2 * tq), F32)],
        compiler_params=_cparams(("parallel", "parallel", "arbitrary")),
        name="fox_attn",
    )(fq, fk, faug, fvt, hm2, fsel)


def _dsa_kernel(dq_ref, iq_ref, wt_ref, dk4_ref, dvt_ref, ik8_ref, hm8_ref, hm4_ref, o_ref,
                iq8_ref, q8_ref, keys_ref, hi_ref, m_ref, acc_ref, s_ref, p_ref, *, top_k):
    i = pl.program_id(1)
    tq = dq_ref.shape[1]
    s_len = dk4_ref.shape[1]
    tk = min(DSA_K_TILE, s_len)
    half_tk = tk // 2
    q_start = i * tq
    rem = (q_start + tq) % tk
    n_full = (q_start + tq) // tk + jnp.where(rem > half_tk, 1, 0)
    has_tail = (rem > 0) & (rem <= half_tk)

    def sweep(fn, init):
        carry = lax.fori_loop(0, n_full, lambda j, c: fn(pl.multiple_of(j * tk, tk), tk, c), init)
        return lax.cond(has_tail, lambda c: fn(pl.multiple_of(n_full * tk, tk), half_tk, c), lambda c: c, carry)

    iq = iq_ref[0]
    dq = dq_ref[0]
    half = dq.shape[1] // 2
    for h in range(IDX_HEADS):
        iq8_ref[h * tq:(h + 1) * tq, :] = iq * hm8_ref[h:h + 1, :]
        q8_ref[h * tq:(h + 1) * tq, :] = dq[:, (h // 4) * half:(h // 4 + 1) * half] * hm4_ref[h % 4:h % 4 + 1, :]

    adm_end = q_start + (lax.broadcasted_iota(jnp.int32, (1, tq), 1) // CHUNK + 1) * CHUNK
    wt = wt_ref[0, :, pl.ds(pl.multiple_of(q_start, tq), tq)]

    def score_block(start, size, c):
        d = lax.dot_general(ik8_ref[0, pl.ds(start, size), :], iq8_ref[...], _NT, preferred_element_type=F32)
        sc = jnp.zeros((size, tq), F32)
        for h in range(IDX_HEADS):
            sc = sc + jnp.maximum(d[:, h * tq:(h + 1) * tq], 0.0) * wt[IDX_HEADS + h:IDX_HEADS + h + 1, :]
        sc = jnp.where(jnp.abs(sc) < F32_MIN_NORMAL, 0.0, sc)
        bits = lax.bitcast_convert_type(sc, jnp.int32)
        key = bits ^ ((bits >> 31) & 0x7FFFFFFF)
        adm = lax.broadcasted_iota(jnp.int32, (size, tq), 0) + start < adm_end
        keys_ref[pl.ds(start, size), :] = jnp.where(adm, key, INT_MIN)
        top = lax.bitcast_convert_type(bits & -65536, F32)
        hi_ref[pl.ds(start, size), :] = jnp.where(adm, top, jnp.nan).astype(BF16)
        return c

    sweep(score_block, 0)

    def count(pred):
        def body(start, size, acc):
            hit = jnp.where(pred(keys_ref[pl.ds(start, size), :]), 1, 0)
            return acc + jnp.sum(hit.reshape(size // COUNT_ROWS, COUNT_ROWS, tq), axis=0)
        return jnp.sum(sweep(body, jnp.zeros((COUNT_ROWS, tq), jnp.int32)), axis=0, keepdims=True)

    def count_hi(cand):
        def body(start, size, acc):
            hit = jnp.where(hi_ref[pl.ds(start, size), :] >= cand, jnp.ones((), BF16), jnp.zeros((), BF16))
            part = hit[0:COUNT_ROWS]
            for r in range(1, size // COUNT_ROWS):
                part = part + hit[r * COUNT_ROWS:(r + 1) * COUNT_ROWS]
            return acc + part.astype(F32)
        return jnp.sum(sweep(body, jnp.zeros((COUNT_ROWS, tq), F32)), axis=0, keepdims=True)

    def step(cand, cnt, thr, done):
        ok = (cnt >= top_k) & (done == 0)
        return jnp.where(ok, cand, thr), jnp.where(ok & (cnt == top_k), 1, done)

    def bisect_hi(it, state):
        t16, done = state
        cand = t16 + jnp.left_shift(jnp.int32(1), 15 - it)
        pattern = jnp.where(cand >= 0, cand, cand ^ 0x7FFF)
        pattern = jnp.where((pattern & 0x7F80) == 0, jnp.where(cand > 0, 0x0080, 0), pattern)
        cand_f = lax.bitcast_convert_type(jnp.left_shift(pattern, 16), F32).astype(BF16)
        return step(cand, count_hi(cand_f), t16, done)

    def bisect_lo(state):
        g, thr, done = state
        for u in range(LOW_PASSES_PER_CHECK):
            cand = thr + jnp.left_shift(jnp.int32(1), 15 - (g * LOW_PASSES_PER_CHECK + u))
            thr, done = step(cand, count(lambda kblk: kblk >= cand), thr, done)
        return g + 1, thr, done

    done = jnp.where(adm_end <= top_k, 1, 0)
    t16, done = lax.fori_loop(0, 16, bisect_hi, (jnp.full((1, tq), -2 ** 15, jnp.int32), done))
    _, thr, _ = lax.while_loop(lambda st: (st[0] < 16 // LOW_PASSES_PER_CHECK) & (jnp.min(st[2]) == 0),
                               bisect_lo, (jnp.int32(0), jnp.left_shift(t16, 16), done))
    thr = jnp.maximum(thr, INT_MIN + 1)
    n_ge = count(lambda kblk: kblk >= thr)

    @pl.when(jnp.max(n_ge) > top_k)
    def _():
        n_gt = count(lambda kblk: kblk > thr)
        need = (top_k - n_gt).astype(F32)

        def demote(start, size, seen):
            r2 = lax.broadcasted_iota(jnp.int32, (size, size), 0)
            c2 = lax.broadcasted_iota(jnp.int32, (size, size), 1)
            lower = jnp.where(c2 < r2, 1.0, 0.0).astype(BF16)
            kblk = keys_ref[pl.ds(start, size), :]
            eq = kblk == thr
            eqf = jnp.where(eq, 1.0, 0.0)
            before = jnp.dot(lower, eqf.astype(BF16), preferred_element_type=F32) + seen
            keys_ref[pl.ds(start, size), :] = jnp.where(eq & (before >= need), thr - 1, kblk)
            return seen + jnp.sum(eqf, axis=0, keepdims=True)

        sweep(demote, jnp.zeros((1, tq), F32))

    m_ref[...] = jnp.full_like(m_ref, -jnp.inf)
    acc_ref[...] = jnp.zeros_like(acc_ref)

    def attend(start, size, c):
        bias = jnp.where(keys_ref[pl.ds(start, size), :] >= thr, 0.0, -jnp.inf)
        grp = min(DSA_GROUP, DSA_HEADS * tq)
        bias = jnp.concatenate([bias] * (grp // tq), axis=1)
        kb = dk4_ref[0, pl.ds(start, size), :]
        vt = dvt_ref[0, :, pl.ds(start, size)]
        n_groups = DSA_HEADS * tq // grp

        for g in range(n_groups):
            s_ref[g, 0:size, :] = lax.dot_general(kb, q8_ref[g * grp:(g + 1) * grp, :], _NT,
                                                  preferred_element_type=F32) + bias
        ch = 128
        for g in range(n_groups):
            sl = slice(g * grp, (g + 1) * grp)
            m_old = m_ref[:, sl]
            m_new = m_old
            for c0 in range(0, size, ch):
                m_new = jnp.maximum(m_new, jnp.max(s_ref[g, c0:c0 + ch, :], axis=0, keepdims=True))
            m_safe = jnp.where(m_new == -jnp.inf, 0.0, m_new)
            for c0 in range(0, size, ch):
                p_ref[c0:c0 + ch, :] = jnp.exp2((s_ref[g, c0:c0 + ch, :] - m_safe).astype(BF16))
            alpha = jnp.exp2(m_old - m_safe)
            m_ref[:, sl] = m_new
            acc_ref[:, sl] = alpha * acc_ref[:, sl] + jnp.dot(vt, p_ref[0:size, :], preferred_element_type=F32)
        return c

    sweep(attend, 0)
    out = acc_ref[0:HEAD_DIM, :] / acc_ref[HEAD_DIM:HEAD_DIM + 1, :]
    for h in range(DSA_HEADS):
        o_ref[0, h * HEAD_DIM:(h + 1) * HEAD_DIM, :] = out[:, h * tq:(h + 1) * tq].astype(BF16)


def _dsa_call(dq, iq, wt, dk4, dvt, ik8, hm8, hm4, top_k):
    bsz, s, _ = dq.shape
    tq = min(DSA_Q_TILE, s)

    def tok(w):
        return pl.BlockSpec((1, tq, w), lambda b, i: (b, i, 0))

    def seq(w):
        return pl.BlockSpec((1, s, w), lambda b, i: (b, 0, 0))

    def seqt(r):
        return pl.BlockSpec((1, r, s), lambda b, i: (b, 0, 0))

    return pl.pallas_call(
        functools.partial(_dsa_kernel, top_k=top_k),
        grid=(bsz, s // tq),
        in_specs=[tok(DSA_WIDTH), tok(IDX_WIDTH), seqt(2 * IDX_HEADS), seq(256), seqt(V_ROWS), seq(256),
                  pl.BlockSpec(hm8.shape, lambda b, i: (0, 0)), pl.BlockSpec(hm4.shape, lambda b, i: (0, 0))],
        out_specs=pl.BlockSpec((1, DSA_WIDTH, tq), lambda b, i: (b, 0, i)),
        out_shape=jax.ShapeDtypeStruct((bsz, DSA_WIDTH, s), BF16),
        scratch_shapes=[pltpu.VMEM((IDX_HEADS * tq, 256), BF16), pltpu.VMEM((DSA_HEADS * tq, 256), BF16),
                        pltpu.VMEM((s, tq), jnp.int32), pltpu.VMEM((s, tq), BF16),
                        pltpu.VMEM((1, DSA_HEADS * tq), F32),
                        pltpu.VMEM((V_ROWS, DSA_HEADS * tq), F32),
                        pltpu.VMEM((DSA_HEADS * tq // DSA_GROUP, DSA_K_TILE, DSA_GROUP), F32),
                        pltpu.VMEM((DSA_K_TILE, DSA_GROUP), BF16)],
        compiler_params=_cparams(("parallel", "arbitrary")),
        name="dsa_attn",
    )(dq, iq, wt, dk4, dvt, ik8, hm8, hm4)


def _merge_kernel(x_ref, mod_ref, g_ref, ya_ref, yb_ref, wg_ref, wfa_ref, wfb_ref, wo_ref, o_ref):
    x = x_ref[0]
    d = x.shape[1]
    h = _modulated(x, mod_ref, g_ref, 1).astype(BF16)
    zg = jnp.dot(h, wg_ref[...], preferred_element_type=F32)
    pa = lax.dot_general(ya_ref[0], wfa_ref[...], _TN, preferred_element_type=F32)
    pb = lax.dot_general(yb_ref[0], wfb_ref[...], _TN, preferred_element_type=F32)
    merged = jax.nn.sigmoid(zg[:, :d]) * pa + jax.nn.sigmoid(zg[:, d:]) * pb
    y = jnp.dot(merged.astype(BF16), wo_ref[...], preferred_element_type=F32)
    o_ref[0] = x + mod_ref[0, 5:6, :] * y


def _merge_call(x, mod, g, ya, yb, wg, wfa, wfb, wo):
    bsz, s, d = x.shape
    tm = min(TOKEN_TILE, s)

    def tok(w):
        return pl.BlockSpec((1, tm, w), lambda b, i: (b, i, 0))

    def tokt(w):
        return pl.BlockSpec((1, w, tm), lambda b, i: (b, 0, i))

    return pl.pallas_call(
        _merge_kernel,
        grid=(bsz, s // tm),
        in_specs=[tok(d),
                  pl.BlockSpec((1, 3 * N_SUBLAYERS, d), lambda b, i: (b, 0, 0)),
                  _const_spec((N_SUBLAYERS, d)),
                  tokt(FOX_WIDTH), tokt(DSA_WIDTH),
                  _const_spec(wg.shape), _const_spec(wfa.shape), _const_spec(wfb.shape), _const_spec(wo.shape)],
        out_specs=tok(d),
        out_shape=jax.ShapeDtypeStruct(x.shape, F32),
        compiler_params=_cparams(("parallel", "parallel")),
        name="merge_out",
    )(x, mod, g, ya, yb, wg, wfa, wfb, wo)


def _head_mask(n_heads, width):
    return jnp.asarray(np.kron(np.eye(n_heads), np.ones((1, width))), BF16)


def _rope_freq_row(rot_dim, period):
    inv_freq = ROPE_THETA ** (-jnp.arange(0, rot_dim, 2, dtype=F32) / rot_dim)
    half = rot_dim // 2
    head = jnp.concatenate([-inv_freq, inv_freq, jnp.zeros((period - 2 * half,), F32)])
    return jnp.tile(head, LANES // period)


def kernel(x, c, positions, ada_w, ada_b, norm_g, ffn1_w1, ffn1_w3, ffn1_w2, w_in, fox_f_bias, fox_qk_g, dsa_qk_g,
           w_br_fox, w_br_dsa, w_out, ffn2_w1, ffn2_w3, ffn2_w2):
    bsz, s, d = x.shape
    top_k = min(TOPK_MAX, s // 4)
    depth = ada_w.shape[0]
    pos = jnp.broadcast_to(positions.astype(F32)[:, :, None], (bsz, s, LANES))
    freq = jnp.stack([_rope_freq_row(HEAD_DIM // ROPE_FRACTION, HEAD_DIM),
                      _rope_freq_row(IDX_DIM // ROPE_FRACTION, IDX_DIM)])
    gmat = jnp.asarray(np.kron(np.eye(FOX_HEADS), np.ones((HEAD_DIM, HEAD_DIM))), BF16)
    hm2 = _head_mask(2, HEAD_DIM)
    hm4 = _head_mask(4, HEAD_DIM)
    hm8 = _head_mask(IDX_HEADS, IDX_DIM)
    fsel = jnp.asarray(np.tile(np.eye(FOX_HEADS), (1, LANES // FOX_HEADS))
                       * (np.arange(LANES) < 3 * FOX_HEADS), BF16).reshape(FOX_HEADS // 2, 2, LANES)
    o_fq, o_fk, o_fv = 0, FOX_WIDTH, 2 * FOX_WIDTH
    o_ff = 3 * FOX_WIDTH
    o_dq = o_ff + FOX_HEADS
    o_dk = o_dq + DSA_WIDTH
    o_dv = o_dk + HEAD_DIM
    o_iq = o_dv + HEAD_DIM
    o_ik = o_iq + IDX_WIDTH
    o_iw = o_ik + IDX_DIM
    o_ga = o_iw + IDX_HEADS

    for l in range(depth):
        mod = _ada_call(c, ada_w[l], ada_b[l]).reshape(bsz, 3 * N_SUBLAYERS, d)
        g = norm_g[l]
        x = _ffn_call(x, mod, g, ffn1_w1[l].astype(BF16), ffn1_w3[l].astype(BF16), ffn1_w2[l].astype(BF16), 0)

        w = w_in[l]
        wa = jnp.concatenate([w[:, o_fq:o_fq + 2 * FOX_WIDTH], w[:, o_dq:o_dq + DSA_WIDTH]], axis=1).astype(BF16)
        pad = jnp.zeros((d, LANES - FOX_HEADS), F32)
        wb = jnp.concatenate([jnp.tile(w[:, o_dk:o_dk + HEAD_DIM], (1, 4)),
                              jnp.tile(w[:, o_ik:o_ik + IDX_DIM], (1, IDX_HEADS)),
                              w[:, o_iq:o_iq + IDX_WIDTH],
                              w[:, o_ff:o_ff + FOX_HEADS], pad], axis=1).astype(BF16)
        zrows = jnp.zeros((FOX_HEADS, V_ROWS - HEAD_DIM, d), F32)
        wfv = jnp.transpose(w[:, o_fv:o_fv + FOX_WIDTH]).reshape(FOX_HEADS, HEAD_DIM, d)
        wt = jnp.concatenate([jnp.concatenate([wfv, zrows], axis=1).reshape(FOX_HEADS * V_ROWS, d),
                              jnp.transpose(w[:, o_dv:o_dv + HEAD_DIM]), zrows[0],
                              jnp.zeros((IDX_HEADS, d), F32), jnp.transpose(w[:, o_iw:o_iw + IDX_HEADS])],
                             axis=0).astype(BF16)
        ones = jnp.asarray(np.tile((np.arange(FOX_HEADS * V_ROWS) % V_ROWS >= HEAD_DIM)[:, None], (1, LANES)), F32)
        wg = w[:, o_ga:o_ga + 2 * d].astype(BF16)
        gains = jnp.stack([jnp.tile(fox_qk_g[l, 0], FOX_HEADS), jnp.tile(fox_qk_g[l, 1], FOX_HEADS),
                           jnp.tile(dsa_qk_g[l, 0], DSA_HEADS), jnp.tile(dsa_qk_g[l, 1], DSA_HEADS)]).astype(F32)
        fbias = jnp.concatenate([fox_f_bias[l].astype(F32), jnp.zeros((LANES - FOX_HEADS,), F32)]).reshape(1, LANES)

        fq, fk, dq, dk4, ik8, iq, faug, fvt, dvt, iwt = _mix_in_call(x, mod, g, pos, wa, wb, wt, ones, gmat, gains,
                                                                       freq, fbias)
        yat = _fox_call(fq, fk, faug, fvt, hm2, fsel)
        ybt = _dsa_call(dq, iq, iwt, dk4, dvt, ik8, hm8, hm4, top_k)
        x = _merge_call(x, mod, g, yat, ybt, wg, w_br_fox[l].astype(BF16), w_br_dsa[l].astype(BF16),
                        w_out[l].astype(BF16))
        x = _ffn_call(x, mod, g, ffn2_w1[l].astype(BF16), ffn2_w3[l].astype(BF16), ffn2_w2[l].astype(BF16), 2)
    return x
```

```python
import functools

import numpy as np
import jax
import jax.numpy as jnp
from jax import lax
from jax.experimental import pallas as pl
from jax.experimental.pallas import tpu as pltpu

D_MODEL = 1024
CHUNK = 64
HEAD_DIM = 64
FOX_HEADS = 8
DSA_HEADS = 8
IDX_HEADS = 8
IDX_DIM = 32
TOPK_MAX = 256
ROPE_THETA = 500000.0
ROPE_FRACTION = 4
D_FF = 2816
N_SUBLAYERS = 3
NORM_EPS = 1e-6
FOX_WIDTH = FOX_HEADS * HEAD_DIM
DSA_WIDTH = DSA_HEADS * HEAD_DIM
IDX_WIDTH = IDX_HEADS * IDX_DIM

LANES = 128
VMEM_LIMIT_BYTES = 56 * 1024 * 1024

TOKEN_TILE = 512
FOX_TILE = 1024
DSA_Q_TILE = 256
DSA_K_TILE = 1024
FOX_GROUP = 512
DSA_GROUP = 1024
QK_AHEAD = 3
LOW_PASSES_PER_CHECK = 4
COUNT_ROWS = 32

V_ROWS = HEAD_DIM + 16
LOG2E = 1.4426950408889634
INT_MIN = -2 ** 31
F32_MIN_NORMAL = 1.1754943508222875e-38
F32 = jnp.float32
BF16 = jnp.bfloat16
_NT = (((1,), (1,)), ((), ()))
_TN = (((0,), (0,)), ((), ()))


def _cparams(sem):
    return pltpu.CompilerParams(dimension_semantics=sem, vmem_limit_bytes=VMEM_LIMIT_BYTES)


def _const_spec(shape):
    nd = len(shape)
    return pl.BlockSpec(shape, lambda *_: (0,) * nd, pipeline_mode=pl.Buffered(1))


def _modulated(x, mod_ref, g_ref, sub):
    ms = jnp.mean(x * x, axis=-1, keepdims=True)
    y = x * lax.rsqrt(ms + NORM_EPS) * g_ref[sub:sub + 1, :]
    shift = mod_ref[0, 3 * sub:3 * sub + 1, :]
    scale = mod_ref[0, 3 * sub + 1:3 * sub + 2, :]
    return y * (1.0 + scale) + shift


def _ada_kernel(c_ref, w_ref, b_ref, o_ref):
    c = c_ref[...]
    cond = c * jax.nn.sigmoid(c)
    o_ref[...] = jnp.dot(cond, w_ref[...], preferred_element_type=F32,
                         precision=lax.Precision.HIGHEST) + b_ref[...]


def _ada_call(c, w, b):
    bsz, d = c.shape
    n = w.shape[1]
    tn = 1152
    return pl.pallas_call(
        _ada_kernel,
        grid=(n // tn,),
        in_specs=[pl.BlockSpec((bsz, d), lambda j: (0, 0)),
                  pl.BlockSpec((d, tn), lambda j: (0, j)),
                  pl.BlockSpec((1, tn), lambda j: (0, j))],
        out_specs=pl.BlockSpec((bsz, tn), lambda j: (0, j)),
        out_shape=jax.ShapeDtypeStruct((bsz, n), F32),
        compiler_params=_cparams(("arbitrary",)),
        name="ada_ln",
    )(c, w, b.reshape(1, n))


def _ffn_kernel(x_ref, mod_ref, g_ref, w1_ref, w3_ref, w2_ref, o_ref, *, sub):
    x = x_ref[0]
    h = _modulated(x, mod_ref, g_ref, sub).astype(BF16)
    a = jnp.dot(h, w1_ref[...], preferred_element_type=F32)
    b = jnp.dot(h, w3_ref[...], preferred_element_type=F32)
    act = (a * jax.nn.sigmoid(a) * b).astype(BF16)
    y = jnp.dot(act, w2_ref[...], preferred_element_type=F32)
    gate = mod_ref[0, 3 * sub + 2:3 * sub + 3, :]
    o_ref[0] = x + (0.5 * gate) * y


def _ffn_call(x, mod, g, w1, w3, w2, sub):
    bsz, s, d = x.shape
    tm = min(TOKEN_TILE, s)
    f = w1.shape[1]
    tok = pl.BlockSpec((1, tm, d), lambda b, i: (b, i, 0))
    return pl.pallas_call(
        functools.partial(_ffn_kernel, sub=sub),
        grid=(bsz, s // tm),
        in_specs=[tok,
                  pl.BlockSpec((1, 3 * N_SUBLAYERS, d), lambda b, i: (b, 0, 0)),
                  _const_spec((N_SUBLAYERS, d)),
                  _const_spec((d, f)), _const_spec((d, f)), _const_spec((f, d))],
        out_specs=tok,
        out_shape=jax.ShapeDtypeStruct(x.shape, F32),
        compiler_params=_cparams(("parallel", "parallel")),
        name=f"ffn{sub}",
    )(x, mod, g, w1, w3, w2)


def _split3(v):
    hi = v.astype(BF16)
    r = v - hi.astype(F32)
    mid = r.astype(BF16)
    lo = (r - mid.astype(F32)).astype(BF16)
    return hi, mid, lo


def _group_norm(z, gmat, gain, group):
    ssq = jnp.dot((z * z).astype(BF16), gmat, preferred_element_type=F32)
    return z * lax.rsqrt(ssq * (1.0 / group) + NORM_EPS) * gain


def _rope(z, cos, sin, half, period):
    width = z.shape[1]
    lane = lax.broadcasted_iota(jnp.int32, z.shape, 1) % period
    up = pltpu.roll(z, width - half, axis=1)
    dn = pltpu.roll(z, half, axis=1)
    return z * cos + jnp.where(lane < half, up, dn) * sin


def _mix_in_kernel(x_ref, mod_ref, g_ref, pos_ref, wa_ref, wb_ref, wt_ref, ones_ref, gmat_ref, gains_ref, freq_ref,
                   fbias_ref, fq_ref, fk_ref, dq_ref, dk4_ref, ik8_ref, iq_ref, faug_ref, fvt_ref, dvt_ref, iwt_ref,
                   carry_ref):
    @pl.when(pl.program_id(1) == 0)
    def _():
        carry_ref[...] = jnp.zeros_like(carry_ref)

    x = x_ref[0]
    tm = x.shape[0]
    h = _modulated(x, mod_ref, g_ref, 1).astype(BF16)
    za = jnp.dot(h, wa_ref[...], preferred_element_type=F32)
    zb = jnp.dot(h, wb_ref[...], preferred_element_type=F32)
    zt = lax.dot_general(wt_ref[...], h, _NT, preferred_element_type=F32)
    n_v = FOX_HEADS * V_ROWS
    fvt_ref[0] = (zt[0:n_v] + jnp.concatenate([ones_ref[...]] * (tm // LANES), axis=1)).astype(BF16)
    dvt_ref[0] = (zt[n_v:n_v + V_ROWS]
                  + jnp.concatenate([ones_ref[0:V_ROWS, :]] * (tm // LANES), axis=1)).astype(BF16)
    iwt_ref[0] = zt[n_v + V_ROWS:, :] * ((IDX_HEADS * IDX_DIM) ** -0.5)

    gmat = gmat_ref[...]
    scale = HEAD_DIM ** -0.5 * LOG2E
    fq = _group_norm(za[:, 0:512], gmat, gains_ref[0:1, :], HEAD_DIM) * scale
    fk = _group_norm(za[:, 512:1024], gmat, gains_ref[1:2, :], HEAD_DIM)
    fq_ref[0] = fq.astype(BF16)
    fk_ref[0] = fk.astype(BF16)

    pos = pos_ref[0]
    ang_a = pos * freq_ref[0:1, :]
    ang_i = pos * freq_ref[1:2, :]
    cos_a, sin_a = jnp.cos(ang_a), jnp.sin(ang_a)
    cos_i, sin_i = jnp.cos(ang_i), jnp.sin(ang_i)
    rot_a = HEAD_DIM // ROPE_FRACTION // 2
    rot_i = IDX_DIM // ROPE_FRACTION // 2

    dq = _group_norm(za[:, 1024:1536], gmat, gains_ref[2:3, :], HEAD_DIM)
    dq = _rope(dq, jnp.concatenate([cos_a] * 4, axis=1), jnp.concatenate([sin_a] * 4, axis=1), rot_a, HEAD_DIM)
    dq_ref[0] = (dq * scale).astype(BF16)

    dk4 = _group_norm(zb[:, 0:256], gmat[0:256, 0:256], gains_ref[3:4, 0:256], HEAD_DIM)
    dk4 = _rope(dk4, jnp.concatenate([cos_a] * 2, axis=1), jnp.concatenate([sin_a] * 2, axis=1), rot_a, HEAD_DIM)
    dk4_ref[0] = dk4.astype(BF16)

    cos_i2 = jnp.concatenate([cos_i] * 2, axis=1)
    sin_i2 = jnp.concatenate([sin_i] * 2, axis=1)
    ik8_ref[0] = _rope(zb[:, 256:512], cos_i2, sin_i2, rot_i, IDX_DIM).astype(BF16)
    iq_ref[0] = _rope(zb[:, 512:768], cos_i2, sin_i2, rot_i, IDX_DIM).astype(BF16)

    sm = zb[:, 768:896]
    v = sm + fbias_ref[...]
    logf = jnp.minimum(v, 0.0) - jnp.log1p(jnp.exp(-jnp.abs(v)))
    row = lax.broadcasted_iota(jnp.int32, (tm, tm), 0)
    col = lax.broadcasted_iota(jnp.int32, (tm, tm), 1)
    tri = jnp.where(row >= col, 1.0, 0.0).astype(BF16)
    hi, mid, lo = _split3(logf)
    csum = (jnp.dot(tri, hi, preferred_element_type=F32) + jnp.dot(tri, mid, preferred_element_type=F32)
            + jnp.dot(tri, lo, preferred_element_type=F32)) + carry_ref[...]
    carry_ref[...] = csum[tm - 1:tm, :]
    lane = lax.broadcasted_iota(jnp.int32, sm.shape, 1)

    nf = -csum * LOG2E
    t0 = nf.astype(BF16).astype(F32)
    r1 = nf - t0
    t1 = r1.astype(BF16).astype(F32)
    t2 = r1 - t1
    aug = jnp.where(lane < FOX_HEADS, t0,
                    jnp.where(lane < 2 * FOX_HEADS, pltpu.roll(t1, FOX_HEADS, axis=1),
                              jnp.where(lane < 3 * FOX_HEADS, pltpu.roll(t2, 2 * FOX_HEADS, axis=1), 0.0)))
    faug_ref[0] = aug.astype(BF16)


def _mix_in_call(x, mod, g, pos, wa, wb, wt, ones, gmat, gains, freq, fbias):
    bsz, s, d = x.shape
    tm = min(TOKEN_TILE, s)

    def tok(w):
        return pl.BlockSpec((1, tm, w), lambda b, i: (b, i, 0))

    def tokt(r):
        return pl.BlockSpec((1, r, tm), lambda b, i: (b, 0, i))

    widths = (512, 512, 512, 256, 256, 256, 128)
    rows = ((FOX_HEADS * V_ROWS, BF16), (V_ROWS, BF16), (2 * IDX_HEADS, F32))
    out_shapes = [jax.ShapeDtypeStruct((bsz, s, w), BF16) for w in widths]
    out_shapes += [jax.ShapeDtypeStruct((bsz, r, s), dt) for r, dt in rows]
    return pl.pallas_call(
        _mix_in_kernel,
        grid=(bsz, s // tm),
        in_specs=[tok(d),
                  pl.BlockSpec((1, 3 * N_SUBLAYERS, d), lambda b, i: (b, 0, 0)),
                  _const_spec((N_SUBLAYERS, d)),
                  tok(LANES),
                  _const_spec(wa.shape), _const_spec(wb.shape), _const_spec(wt.shape), _const_spec(ones.shape),
                  _const_spec(gmat.shape), _const_spec(gains.shape), _const_spec(freq.shape), _const_spec(fbias.shape)],
        out_specs=[tok(w) for w in widths] + [tokt(r) for r, _ in rows],
        out_shape=out_shapes,
        scratch_shapes=[pltpu.VMEM((1, LANES), F32)],
        compiler_params=_cparams(("arbitrary", "arbitrary")),
        name="mix_in",
    )(x, mod, g, pos, wa, wb, wt, ones, gmat, gains, freq, fbias)


def _fox_kernel(q_ref, k_ref, faug_ref, vt_ref, hm_ref, fsel_ref, o_ref, qf_ref, m_ref, acc_ref):
    i = pl.program_id(2)
    tq = q_ref.shape[1]
    tk = tq
    q = q_ref[0]
    for a in range(2):
        sel_row = fsel_ref[0, a:a + 1, :]
        qf_ref[a * tq:(a + 1) * tq, 0:LANES] = q * hm_ref[a:a + 1, :]
        qf_ref[a * tq:(a + 1) * tq, LANES:2 * LANES] = jnp.broadcast_to(sel_row, (tq, LANES))
    m_ref[...] = jnp.full_like(m_ref, -jnp.inf)
    acc_ref[...] = jnp.zeros_like(acc_ref)

    def block(j, masked):
        start = pl.multiple_of(j * tk, tk)
        kf = jnp.concatenate([k_ref[0, pl.ds(start, tk), :], faug_ref[0, pl.ds(start, tk), :]], axis=1)
        grp = min(FOX_GROUP, tq)
        n_groups = 2 * tq // grp

        def rows(g):
            return min(tk, (g * grp) % tq + grp) if masked else tk

        def scores(g):
            s = lax.dot_general(kf[:rows(g)], qf_ref[g * grp:(g + 1) * grp, :], _NT,
                                preferred_element_type=F32)
            if masked:
                krow = lax.broadcasted_iota(jnp.int32, s.shape, 0)
                qcol = lax.broadcasted_iota(jnp.int32, s.shape, 1) + (g * grp) % tq
                s = jnp.where(krow <= qcol, s, -jnp.inf)
            return s

        ahead = [scores(g) for g in range(min(QK_AHEAD, n_groups))]
        for g in range(n_groups):
            s = ahead.pop(0)
            if g + QK_AHEAD < n_groups:
                ahead.append(scores(g + QK_AHEAD))
            sl = slice(g * grp, (g + 1) * grp)
            m_old = m_ref[:, sl]
            m_new = jnp.maximum(m_old, jnp.max(s, axis=0, keepdims=True))
            p = jnp.exp2((s - m_new).astype(BF16))
            alpha = jnp.exp2(m_old - m_new)
            m_ref[:, sl] = m_new
            a = (g * grp) // tq
            vt = vt_ref[0, a * V_ROWS:(a + 1) * V_ROWS, pl.ds(start, rows(g))]
            acc_ref[:, sl] = alpha * acc_ref[:, sl] + jnp.dot(vt, p, preferred_element_type=F32)

    def body(j, c):
        block(j, False)
        return c

    lax.fori_loop(0, i, body, 0)
    block(i, True)
    out = acc_ref[0:HEAD_DIM, :] / acc_ref[HEAD_DIM:HEAD_DIM + 1, :]
    for a in range(2):
        o_ref[0, a * HEAD_DIM:(a + 1) * HEAD_DIM, :] = out[:, a * tq:(a + 1) * tq].astype(BF16)


def _fox_call(fq, fk, faug, fvt, hm2, fsel):
    bsz, s, _ = fq.shape
    tq = min(FOX_TILE, s)
    pairs = FOX_HEADS // 2
    return pl.pallas_call(
        _fox_kernel,
        grid=(bsz, pairs, s // tq),
        in_specs=[pl.BlockSpec((1, tq, LANES), lambda b, h, i: (b, i, h)),
                  pl.BlockSpec((1, s, LANES), lambda b, h, i: (b, 0, h)),
                  pl.BlockSpec((1, s, LANES), lambda b, h, i: (b, 0, 0)),
                  pl.BlockSpec((1, 2 * V_ROWS, s), lambda b, h, i: (b, h, 0)),
                  pl.BlockSpec((2, LANES), lambda b, h, i: (0, 0)),
                  pl.BlockSpec((1, 2, LANES), lambda b, h, i: (h, 0, 0))],
        out_specs=pl.BlockSpec((1, LANES, tq), lambda b, h, i: (b, h, i)),
        out_shape=jax.ShapeDtypeStruct((bsz, FOX_WIDTH, s), BF16),
        scratch_shapes=[pltpu.VMEM((2 * tq, 2 * LANES), BF16),
                        pltpu.VMEM((1, 2 * tq), F32),
                        pltpu.VMEM((V_ROWS, 2 * tq), F32)],
        compiler_params=_cparams(("parallel", "parallel", "arbitrary")),
        name="fox_attn",
    )(fq, fk, faug, fvt, hm2, fsel)


def _dsa_kernel(dq_ref, iq_ref, wt_ref, dk4_ref, dvt_ref, ik8_ref, hm8_ref, hm4_ref, o_ref,
                iq8_ref, q8_ref, keys_ref, hi_ref, m_ref, acc_ref, *, top_k):
    i = pl.program_id(1)
    tq = dq_ref.shape[1]
    s_len = dk4_ref.shape[1]
    tk = min(DSA_K_TILE, s_len)
    half_tk = tk // 2
    q_start = i * tq
    rem = (q_start + tq) % tk
    n_full = (q_start + tq) // tk + jnp.where(rem > half_tk, 1, 0)
    has_tail = (rem > 0) & (rem <= half_tk)

    def sweep(fn, init):
        carry = lax.fori_loop(0, n_full, lambda j, c: fn(pl.multiple_of(j * tk, tk), tk, c), init)
        return lax.cond(has_tail, lambda c: fn(pl.multiple_of(n_full * tk, tk), half_tk, c), lambda c: c, carry)

    iq = iq_ref[0]
    dq = dq_ref[0]
    half = dq.shape[1] // 2
    for h in range(IDX_HEADS):
        iq8_ref[h * tq:(h + 1) * tq, :] = iq * hm8_ref[h:h + 1, :]
        q8_ref[h * tq:(h + 1) * tq, :] = dq[:, (h // 4) * half:(h // 4 + 1) * half] * hm4_ref[h % 4:h % 4 + 1, :]

    adm_end = q_start + (lax.broadcasted_iota(jnp.int32, (1, tq), 1) // CHUNK + 1) * CHUNK
    wt = wt_ref[0, :, pl.ds(pl.multiple_of(q_start, tq), tq)]

    def score_block(start, size, c):
        d = lax.dot_general(ik8_ref[0, pl.ds(start, size), :], iq8_ref[...], _NT, preferred_element_type=F32)
        sc = jnp.zeros((size, tq), F32)
        for h in range(IDX_HEADS):
            sc = sc + jnp.maximum(d[:, h * tq:(h + 1) * tq], 0.0) * wt[IDX_HEADS + h:IDX_HEADS + h + 1, :]
        sc = jnp.where(jnp.abs(sc) < F32_MIN_NORMAL, 0.0, sc)
        bits = lax.bitcast_convert_type(sc, jnp.int32)
        key = bits ^ ((bits >> 31) & 0x7FFFFFFF)
        adm = lax.broadcasted_iota(jnp.int32, (size, tq), 0) + start < adm_end
        keys_ref[pl.ds(start, size), :] = jnp.where(adm, key, INT_MIN)
        top = lax.bitcast_convert_type(bits & -65536, F32)
        hi_ref[pl.ds(start, size), :] = jnp.where(adm, top, jnp.nan).astype(BF16)
        return c

    sweep(score_block, 0)

    def count(pred):
        def body(start, size, acc):
            hit = jnp.where(pred(keys_ref[pl.ds(start, size), :]), 1, 0)
            return acc + jnp.sum(hit.reshape(size // COUNT_ROWS, COUNT_ROWS, tq), axis=0)
        return jnp.sum(sweep(body, jnp.zeros((COUNT_ROWS, tq), jnp.int32)), axis=0, keepdims=True)

    def count_hi(cand):
        def body(start, size, acc):
            hit = jnp.where(hi_ref[pl.ds(start, size), :] >= cand, jnp.ones((), BF16), jnp.zeros((), BF16))
            part = hit[0:COUNT_ROWS]
            for r in range(1, size // COUNT_ROWS):
                part = part + hit[r * COUNT_ROWS:(r + 1) * COUNT_ROWS]
            return acc + part.astype(F32)
        return jnp.sum(sweep(body, jnp.zeros((COUNT_ROWS, tq), F32)), axis=0, keepdims=True)

    def step(cand, cnt, thr, done):
        ok = (cnt >= top_k) & (done == 0)
        return jnp.where(ok, cand, thr), jnp.where(ok & (cnt == top_k), 1, done)

    def bisect_hi(it, state):
        t16, done = state
        cand = t16 + jnp.left_shift(jnp.int32(1), 15 - it)
        pattern = jnp.where(cand >= 0, cand, cand ^ 0x7FFF)
        pattern = jnp.where((pattern & 0x7F80) == 0, jnp.where(cand > 0, 0x0080, 0), pattern)
        cand_f = lax.bitcast_convert_type(jnp.left_shift(pattern, 16), F32).astype(BF16)
        return step(cand, count_hi(cand_f), t16, done)

    def bisect_lo(state):
        g, thr, done = state
        for u in range(LOW_PASSES_PER_CHECK):
            cand = thr + jnp.left_shift(jnp.int32(1), 15 - (g * LOW_PASSES_PER_CHECK + u))
            thr, done = step(cand, count(lambda kblk: kblk >= cand), thr, done)
        return g + 1, thr, done

    done = jnp.where(adm_end <= top_k, 1, 0)
    t16, done = lax.fori_loop(0, 16, bisect_hi, (jnp.full((1, tq), -2 ** 15, jnp.int32), done))
    _, thr, _ = lax.while_loop(lambda st: (st[0] < 16 // LOW_PASSES_PER_CHECK) & (jnp.min(st[2]) == 0),
                               bisect_lo, (jnp.int32(0), jnp.left_shift(t16, 16), done))
    thr = jnp.maximum(thr, INT_MIN + 1)
    n_ge = count(lambda kblk: kblk >= thr)

    @pl.when(jnp.max(n_ge) > top_k)
    def _():
        n_gt = count(lambda kblk: kblk > thr)
        need = (top_k - n_gt).astype(F32)

        def demote(start, size, seen):
            r2 = lax.broadcasted_iota(jnp.int32, (size, size), 0)
            c2 = lax.broadcasted_iota(jnp.int32, (size, size), 1)
            lower = jnp.where(c2 < r2, 1.0, 0.0).astype(BF16)
            kblk = keys_ref[pl.ds(start, size), :]
            eq = kblk == thr
            eqf = jnp.where(eq, 1.0, 0.0)
            before = jnp.dot(lower, eqf.astype(BF16), preferred_element_type=F32) + seen
            keys_ref[pl.ds(start, size), :] = jnp.where(eq & (before >= need), thr - 1, kblk)
            return seen + jnp.sum(eqf, axis=0, keepdims=True)

        sweep(demote, jnp.zeros((1, tq), F32))

    m_ref[...] = jnp.full_like(m_ref, -jnp.inf)
    acc_ref[...] = jnp.zeros_like(acc_ref)

    def attend(start, size, c):
        bias = jnp.where(keys_ref[pl.ds(start, size), :] >= thr, 0.0, -jnp.inf)
        grp = min(DSA_GROUP, DSA_HEADS * tq)
        bias = jnp.concatenate([bias] * (grp // tq), axis=1)
        kb = dk4_ref[0, pl.ds(start, size), :]
        vt = dvt_ref[0, :, pl.ds(start, size)]
        n_groups = DSA_HEADS * tq // grp

        def scores(g):
            return lax.dot_general(kb, q8_ref[g * grp:(g + 1) * grp, :], _NT,
                                   preferred_element_type=F32) + bias

        ahead = [scores(g) for g in range(min(QK_AHEAD, n_groups))]
        for g in range(n_groups):
            s = ahead.pop(0)
            if g + QK_AHEAD < n_groups:
                ahead.append(scores(g + QK_AHEAD))
            sl = slice(g * grp, (g + 1) * grp)
            m_old = m_ref[:, sl]
            m_new = jnp.maximum(m_old, jnp.max(s, axis=0, keepdims=True))
            m_safe = jnp.where(m_new == -jnp.inf, 0.0, m_new)
            p = jnp.exp2((s - m_safe).astype(BF16))
            alpha = jnp.exp2(m_old - m_safe)
            m_ref[:, sl] = m_new
            acc_ref[:, sl] = alpha * acc_ref[:, sl] + jnp.dot(vt, p, preferred_element_type=F32)
        return c

    sweep(attend, 0)
    out = acc_ref[0:HEAD_DIM, :] / acc_ref[HEAD_DIM:HEAD_DIM + 1, :]
    for h in range(DSA_HEADS):
        o_ref[0, h * HEAD_DIM:(h + 1) * HEAD_DIM, :] = out[:, h * tq:(h + 1) * tq].astype(BF16)


def _dsa_call(dq, iq, wt, dk4, dvt, ik8, hm8, hm4, top_k):
    bsz, s, _ = dq.shape
    tq = min(DSA_Q_TILE, s)

    def tok(w):
        return pl.BlockSpec((1, tq, w), lambda b, i: (b, i, 0))

    def seq(w):
        return pl.BlockSpec((1, s, w), lambda b, i: (b, 0, 0))

    def seqt(r):
        return pl.BlockSpec((1, r, s), lambda b, i: (b, 0, 0))

    return pl.pallas_call(
        functools.partial(_dsa_kernel, top_k=top_k),
        grid=(bsz, s // tq),
        in_specs=[tok(DSA_WIDTH), tok(IDX_WIDTH), seqt(2 * IDX_HEADS), seq(256), seqt(V_ROWS), seq(256),
                  pl.BlockSpec(hm8.shape, lambda b, i: (0, 0)), pl.BlockSpec(hm4.shape, lambda b, i: (0, 0))],
        out_specs=pl.BlockSpec((1, DSA_WIDTH, tq), lambda b, i: (b, 0, i)),
        out_shape=jax.ShapeDtypeStruct((bsz, DSA_WIDTH, s), BF16),
        scratch_shapes=[pltpu.VMEM((IDX_HEADS * tq, 256), BF16), pltpu.VMEM((DSA_HEADS * tq, 256), BF16),
                        pltpu.VMEM((s, tq), jnp.int32), pltpu.VMEM((s, tq), BF16),
                        pltpu.VMEM((1, DSA_HEADS * tq), F32),
                        pltpu.VMEM((V_ROWS, DSA_HEADS * tq), F32)],
        compiler_params=_cparams(("parallel", "arbitrary")),
        name="dsa_attn",
    )(dq, iq, wt, dk4, dvt, ik8, hm8, hm4)


def _merge_kernel(x_ref, mod_ref, g_ref, ya_ref, yb_ref, wg_ref, wfa_ref, wfb_ref, wo_ref, o_ref):
    x = x_ref[0]
    d = x.shape[1]
    h = _modulated(x, mod_ref, g_ref, 1).astype(BF16)
    zg = jnp.dot(h, wg_ref[...], preferred_element_type=F32)
    pa = lax.dot_general(ya_ref[0], wfa_ref[...], _TN, preferred_element_type=F32)
    pb = lax.dot_general(yb_ref[0], wfb_ref[...], _TN, preferred_element_type=F32)
    merged = jax.nn.sigmoid(zg[:, :d]) * pa + jax.nn.sigmoid(zg[:, d:]) * pb
    y = jnp.dot(merged.astype(BF16), wo_ref[...], preferred_element_type=F32)
    o_ref[0] = x + mod_ref[0, 5:6, :] * y


def _merge_call(x, mod, g, ya, yb, wg, wfa, wfb, wo):
    bsz, s, d = x.shape
    tm = min(TOKEN_TILE, s)

    def tok(w):
        return pl.BlockSpec((1, tm, w), lambda b, i: (b, i, 0))

    def tokt(w):
        return pl.BlockSpec((1, w, tm), lambda b, i: (b, 0, i))

    return pl.pallas_call(
        _merge_kernel,
        grid=(bsz, s // tm),
        in_specs=[tok(d),
                  pl.BlockSpec((1, 3 * N_SUBLAYERS, d), lambda b, i: (b, 0, 0)),
                  _const_spec((N_SUBLAYERS, d)),
                  tokt(FOX_WIDTH), tokt(DSA_WIDTH),
                  _const_spec(wg.shape), _const_spec(wfa.shape), _const_spec(wfb.shape), _const_spec(wo.shape)],
        out_specs=tok(d),
        out_shape=jax.ShapeDtypeStruct(x.shape, F32),
        compiler_params=_cparams(("parallel", "parallel")),
        name="merge_out",
    )(x, mod, g, ya, yb, wg, wfa, wfb, wo)


def _head_mask(n_heads, width):
    return jnp.asarray(np.kron(np.eye(n_heads), np.ones((1, width))), BF16)


def _rope_freq_row(rot_dim, period):
    inv_freq = ROPE_THETA ** (-jnp.arange(0, rot_dim, 2, dtype=F32) / rot_dim)
    half = rot_dim // 2
    head = jnp.concatenate([-inv_freq, inv_freq, jnp.zeros((period - 2 * half,), F32)])
    return jnp.tile(head, LANES // period)


def kernel(x, c, positions, ada_w, ada_b, norm_g, ffn1_w1, ffn1_w3, ffn1_w2, w_in, fox_f_bias, fox_qk_g, dsa_qk_g,
           w_br_fox, w_br_dsa, w_out, ffn2_w1, ffn2_w3, ffn2_w2):
    bsz, s, d = x.shape
    top_k = min(TOPK_MAX, s // 4)
    depth = ada_w.shape[0]
    pos = jnp.broadcast_to(positions.astype(F32)[:, :, None], (bsz, s, LANES))
    freq = jnp.stack([_rope_freq_row(HEAD_DIM // ROPE_FRACTION, HEAD_DIM),
                      _rope_freq_row(IDX_DIM // ROPE_FRACTION, IDX_DIM)])
    gmat = jnp.asarray(np.kron(np.eye(FOX_HEADS), np.ones((HEAD_DIM, HEAD_DIM))), BF16)
    hm2 = _head_mask(2, HEAD_DIM)
    hm4 = _head_mask(4, HEAD_DIM)
    hm8 = _head_mask(IDX_HEADS, IDX_DIM)
    fsel = jnp.asarray(np.tile(np.eye(FOX_HEADS), (1, LANES // FOX_HEADS))
                       * (np.arange(LANES) < 3 * FOX_HEADS), BF16).reshape(FOX_HEADS // 2, 2, LANES)
    o_fq, o_fk, o_fv = 0, FOX_WIDTH, 2 * FOX_WIDTH
    o_ff = 3 * FOX_WIDTH
    o_dq = o_ff + FOX_HEADS
    o_dk = o_dq + DSA_WIDTH
    o_dv = o_dk + HEAD_DIM
    o_iq = o_dv + HEAD_DIM
    o_ik = o_iq + IDX_WIDTH
    o_iw = o_ik + IDX_DIM
    o_ga = o_iw + IDX_HEADS

    for l in range(depth):
        mod = _ada_call(c, ada_w[l], ada_b[l]).reshape(bsz, 3 * N_SUBLAYERS, d)
        g = norm_g[l]
        x = _ffn_call(x, mod, g, ffn1_w1[l].astype(BF16), ffn1_w3[l].astype(BF16), ffn1_w2[l].astype(BF16), 0)

        w = w_in[l]
        wa = jnp.concatenate([w[:, o_fq:o_fq + 2 * FOX_WIDTH], w[:, o_dq:o_dq + DSA_WIDTH]], axis=1).astype(BF16)
        pad = jnp.zeros((d, LANES - FOX_HEADS), F32)
        wb = jnp.concatenate([jnp.tile(w[:, o_dk:o_dk + HEAD_DIM], (1, 4)),
                              jnp.tile(w[:, o_ik:o_ik + IDX_DIM], (1, IDX_HEADS)),
                              w[:, o_iq:o_iq + IDX_WIDTH],
                              w[:, o_ff:o_ff + FOX_HEADS], pad], axis=1).astype(BF16)
        zrows = jnp.zeros((FOX_HEADS, V_ROWS - HEAD_DIM, d), F32)
        wfv = jnp.transpose(w[:, o_fv:o_fv + FOX_WIDTH]).reshape(FOX_HEADS, HEAD_DIM, d)
        wt = jnp.concatenate([jnp.concatenate([wfv, zrows], axis=1).reshape(FOX_HEADS * V_ROWS, d),
                              jnp.transpose(w[:, o_dv:o_dv + HEAD_DIM]), zrows[0],
                              jnp.zeros((IDX_HEADS, d), F32), jnp.transpose(w[:, o_iw:o_iw + IDX_HEADS])],
                             axis=0).astype(BF16)
        ones = jnp.asarray(np.tile((np.arange(FOX_HEADS * V_ROWS) % V_ROWS >= HEAD_DIM)[:, None], (1, LANES)), F32)
        wg = w[:, o_ga:o_ga + 2 * d].astype(BF16)
        gains = jnp.stack([jnp.tile(fox_qk_g[l, 0], FOX_HEADS), jnp.tile(fox_qk_g[l, 1], FOX_HEADS),
                           jnp.tile(dsa_qk_g[l, 0], DSA_HEADS), jnp.tile(dsa_qk_g[l, 1], DSA_HEADS)]).astype(F32)
        fbias = jnp.concatenate([fox_f_bias[l].astype(F32), jnp.zeros((LANES - FOX_HEADS,), F32)]).reshape(1, LANES)

        fq, fk, dq, dk4, ik8, iq, faug, fvt, dvt, iwt = _mix_in_call(x, mod, g, pos, wa, wb, wt, ones, gmat, gains,
                                                                       freq, fbias)
        yat = _fox_call(fq, fk, faug, fvt, hm2, fsel)
        ybt = _dsa_call(dq, iq, iwt, dk4, dvt, ik8, hm8, hm4, top_k)
        x = _merge_call(x, mod, g, yat, ybt, wg, w_br_fox[l].astype(BF16), w_br_dsa[l].astype(BF16),
                        w_out[l].astype(BF16))
        x = _ffn_call(x, mod, g, ffn2_w1[l].astype(BF16), ffn2_w3[l].astype(BF16), ffn2_w2[l].astype(BF16), 2)
    return x
```

```python
import functools

import numpy as np
import jax
import jax.numpy as jnp
from jax import lax
from jax.experimental import pallas as pl
from jax.experimental.pallas import tpu as pltpu

D_MODEL = 1024
CHUNK = 64
HEAD_DIM = 64
FOX_HEADS = 8
DSA_HEADS = 8
IDX_HEADS = 8
IDX_DIM = 32
TOPK_MAX = 256
ROPE_THETA = 500000.0
ROPE_FRACTION = 4
D_FF = 2816
N_SUBLAYERS = 3
NORM_EPS = 1e-6
FOX_WIDTH = FOX_HEADS * HEAD_DIM
DSA_WIDTH = DSA_HEADS * HEAD_DIM
IDX_WIDTH = IDX_HEADS * IDX_DIM

LANES = 128
VMEM_LIMIT_BYTES = 56 * 1024 * 1024

TOKEN_TILE = 512
FOX_TILE = 1024
DSA_Q_TILE = 256
DSA_K_TILE = 1024
FOX_GROUP = 512
DSA_GROUP = 1024
QK_AHEAD = 3
LOW_PASSES_PER_CHECK = 2
DEMOTE_ROWS = 128
COUNT_ROWS = 32

V_ROWS = HEAD_DIM + 16
LOG2E = 1.4426950408889634
INT_MIN = -2 ** 31
F32_MIN_NORMAL = 1.1754943508222875e-38
F32 = jnp.float32
BF16 = jnp.bfloat16
_NT = (((1,), (1,)), ((), ()))
_TN = (((0,), (0,)), ((), ()))


def _cparams(sem):
    return pltpu.CompilerParams(dimension_semantics=sem, vmem_limit_bytes=VMEM_LIMIT_BYTES)


def _const_spec(shape):
    nd = len(shape)
    return pl.BlockSpec(shape, lambda *_: (0,) * nd, pipeline_mode=pl.Buffered(1))


def _modulated(x, mod_ref, g_ref, sub):
    ms = jnp.mean(x * x, axis=-1, keepdims=True)
    y = x * lax.rsqrt(ms + NORM_EPS) * g_ref[sub:sub + 1, :]
    shift = mod_ref[0, 3 * sub:3 * sub + 1, :]
    scale = mod_ref[0, 3 * sub + 1:3 * sub + 2, :]
    return y * (1.0 + scale) + shift


def _ada_kernel(c_ref, w_ref, b_ref, o_ref):
    c = c_ref[...]
    cond = c * jax.nn.sigmoid(c)
    o_ref[...] = jnp.dot(cond, w_ref[...], preferred_element_type=F32,
                         precision=lax.Precision.HIGHEST) + b_ref[...]


def _ada_call(c, w, b):
    bsz, d = c.shape
    n = w.shape[1]
    tn = 1152
    return pl.pallas_call(
        _ada_kernel,
        grid=(n // tn,),
        in_specs=[pl.BlockSpec((bsz, d), lambda j: (0, 0)),
                  pl.BlockSpec((d, tn), lambda j: (0, j)),
                  pl.BlockSpec((1, tn), lambda j: (0, j))],
        out_specs=pl.BlockSpec((bsz, tn), lambda j: (0, j)),
        out_shape=jax.ShapeDtypeStruct((bsz, n), F32),
        compiler_params=_cparams(("arbitrary",)),
        name="ada_ln",
    )(c, w, b.reshape(1, n))


def _ffn_kernel(x_ref, mod_ref, g_ref, w1_ref, w3_ref, w2_ref, o_ref, *, sub):
    x = x_ref[0]
    h = _modulated(x, mod_ref, g_ref, sub).astype(BF16)
    a = jnp.dot(h, w1_ref[...], preferred_element_type=F32)
    b = jnp.dot(h, w3_ref[...], preferred_element_type=F32)
    act = (a * jax.nn.sigmoid(a) * b).astype(BF16)
    y = jnp.dot(act, w2_ref[...], preferred_element_type=F32)
    gate = mod_ref[0, 3 * sub + 2:3 * sub + 3, :]
    o_ref[0] = x + (0.5 * gate) * y


def _ffn_call(x, mod, g, w1, w3, w2, sub):
    bsz, s, d = x.shape
    tm = min(TOKEN_TILE, s)
    f = w1.shape[1]
    tok = pl.BlockSpec((1, tm, d), lambda b, i: (b, i, 0))
    return pl.pallas_call(
        functools.partial(_ffn_kernel, sub=sub),
        grid=(bsz, s // tm),
        in_specs=[tok,
                  pl.BlockSpec((1, 3 * N_SUBLAYERS, d), lambda b, i: (b, 0, 0)),
                  _const_spec((N_SUBLAYERS, d)),
                  _const_spec((d, f)), _const_spec((d, f)), _const_spec((f, d))],
        out_specs=tok,
        out_shape=jax.ShapeDtypeStruct(x.shape, F32),
        compiler_params=_cparams(("parallel", "parallel")),
        name=f"ffn{sub}",
    )(x, mod, g, w1, w3, w2)


def _split3(v):
    hi = v.astype(BF16)
    r = v - hi.astype(F32)
    mid = r.astype(BF16)
    lo = (r - mid.astype(F32)).astype(BF16)
    return hi, mid, lo


def _group_norm(z, gmat, gain, group):
    ssq = jnp.dot((z * z).astype(BF16), gmat, preferred_element_type=F32)
    return z * lax.rsqrt(ssq * (1.0 / group) + NORM_EPS) * gain


def _rope(z, cos, sin, half, period):
    width = z.shape[1]
    lane = lax.broadcasted_iota(jnp.int32, z.shape, 1) % period
    up = pltpu.roll(z, width - half, axis=1)
    dn = pltpu.roll(z, half, axis=1)
    return z * cos + jnp.where(lane < half, up, dn) * sin


def _mix_in_kernel(x_ref, mod_ref, g_ref, pos_ref, wa_ref, wb_ref, wt_ref, ones_ref, gmat_ref, gains_ref, freq_ref,
                   fbias_ref, fq_ref, fk_ref, dq_ref, dk4_ref, ik8_ref, iq_ref, faug_ref, fvt_ref, dvt_ref, iwt_ref,
                   carry_ref):
    @pl.when(pl.program_id(1) == 0)
    def _():
        carry_ref[...] = jnp.zeros_like(carry_ref)

    x = x_ref[0]
    tm = x.shape[0]
    h = _modulated(x, mod_ref, g_ref, 1).astype(BF16)
    za = jnp.dot(h, wa_ref[...], preferred_element_type=F32)
    zb = jnp.dot(h, wb_ref[...], preferred_element_type=F32)
    zt = lax.dot_general(wt_ref[...], h, _NT, preferred_element_type=F32)
    n_v = FOX_HEADS * V_ROWS
    fvt_ref[0] = (zt[0:n_v] + jnp.concatenate([ones_ref[...]] * (tm // LANES), axis=1)).astype(BF16)
    dvt_ref[0] = (zt[n_v:n_v + V_ROWS]
                  + jnp.concatenate([ones_ref[0:V_ROWS, :]] * (tm // LANES), axis=1)).astype(BF16)
    iwt_ref[0] = zt[n_v + V_ROWS:, :] * ((IDX_HEADS * IDX_DIM) ** -0.5)

    gmat = gmat_ref[...]
    scale = HEAD_DIM ** -0.5 * LOG2E
    fq = _group_norm(za[:, 0:512], gmat, gains_ref[0:1, :], HEAD_DIM) * scale
    fk = _group_norm(za[:, 512:1024], gmat, gains_ref[1:2, :], HEAD_DIM)
    fq_ref[0] = fq.astype(BF16)
    fk_ref[0] = fk.astype(BF16)

    pos = pos_ref[0]
    ang_a = pos * freq_ref[0:1, :]
    ang_i = pos * freq_ref[1:2, :]
    cos_a, sin_a = jnp.cos(ang_a), jnp.sin(ang_a)
    cos_i, sin_i = jnp.cos(ang_i), jnp.sin(ang_i)
    rot_a = HEAD_DIM // ROPE_FRACTION // 2
    rot_i = IDX_DIM // ROPE_FRACTION // 2

    dq = _group_norm(za[:, 1024:1536], gmat, gains_ref[2:3, :], HEAD_DIM)
    dq = _rope(dq, jnp.concatenate([cos_a] * 4, axis=1), jnp.concatenate([sin_a] * 4, axis=1), rot_a, HEAD_DIM)
    dq_ref[0] = (dq * scale).astype(BF16)

    dk4 = _group_norm(zb[:, 0:256], gmat[0:256, 0:256], gains_ref[3:4, 0:256], HEAD_DIM)
    dk4 = _rope(dk4, jnp.concatenate([cos_a] * 2, axis=1), jnp.concatenate([sin_a] * 2, axis=1), rot_a, HEAD_DIM)
    dk4_ref[0] = dk4.astype(BF16)

    cos_i2 = jnp.concatenate([cos_i] * 2, axis=1)
    sin_i2 = jnp.concatenate([sin_i] * 2, axis=1)
    ik8_ref[0] = _rope(zb[:, 256:512], cos_i2, sin_i2, rot_i, IDX_DIM).astype(BF16)
    iq_ref[0] = _rope(zb[:, 512:768], cos_i2, sin_i2, rot_i, IDX_DIM).astype(BF16)

    sm = zb[:, 768:896]
    v = sm + fbias_ref[...]
    logf = jnp.minimum(v, 0.0) - jnp.log1p(jnp.exp(-jnp.abs(v)))
    row = lax.broadcasted_iota(jnp.int32, (tm, tm), 0)
    col = lax.broadcasted_iota(jnp.int32, (tm, tm), 1)
    tri = jnp.where(row >= col, 1.0, 0.0).astype(BF16)
    hi, mid, lo = _split3(logf)
    csum = (jnp.dot(tri, hi, preferred_element_type=F32) + jnp.dot(tri, mid, preferred_element_type=F32)
            + jnp.dot(tri, lo, preferred_element_type=F32)) + carry_ref[...]
    carry_ref[...] = csum[tm - 1:tm, :]
    lane = lax.broadcasted_iota(jnp.int32, sm.shape, 1)

    nf = -csum * LOG2E
    t0 = nf.astype(BF16).astype(F32)
    r1 = nf - t0
    t1 = r1.astype(BF16).astype(F32)
    t2 = r1 - t1
    aug = jnp.where(lane < FOX_HEADS, t0,
                    jnp.where(lane < 2 * FOX_HEADS, pltpu.roll(t1, FOX_HEADS, axis=1),
                              jnp.where(lane < 3 * FOX_HEADS, pltpu.roll(t2, 2 * FOX_HEADS, axis=1), 0.0)))
    faug_ref[0] = aug.astype(BF16)


def _mix_in_call(x, mod, g, pos, wa, wb, wt, ones, gmat, gains, freq, fbias):
    bsz, s, d = x.shape
    tm = min(TOKEN_TILE, s)

    def tok(w):
        return pl.BlockSpec((1, tm, w), lambda b, i: (b, i, 0))

    def tokt(r):
        return pl.BlockSpec((1, r, tm), lambda b, i: (b, 0, i))

    widths = (512, 512, 512, 256, 256, 256, 128)
    rows = ((FOX_HEADS * V_ROWS, BF16), (V_ROWS, BF16), (2 * IDX_HEADS, F32))
    out_shapes = [jax.ShapeDtypeStruct((bsz, s, w), BF16) for w in widths]
    out_shapes += [jax.ShapeDtypeStruct((bsz, r, s), dt) for r, dt in rows]
    return pl.pallas_call(
        _mix_in_kernel,
        grid=(bsz, s // tm),
        in_specs=[tok(d),
                  pl.BlockSpec((1, 3 * N_SUBLAYERS, d), lambda b, i: (b, 0, 0)),
                  _const_spec((N_SUBLAYERS, d)),
                  tok(LANES),
                  _const_spec(wa.shape), _const_spec(wb.shape), _const_spec(wt.shape), _const_spec(ones.shape),
                  _const_spec(gmat.shape), _const_spec(gains.shape), _const_spec(freq.shape), _const_spec(fbias.shape)],
        out_specs=[tok(w) for w in widths] + [tokt(r) for r, _ in rows],
        out_shape=out_shapes,
        scratch_shapes=[pltpu.VMEM((1, LANES), F32)],
        compiler_params=_cparams(("arbitrary", "arbitrary")),
        name="mix_in",
    )(x, mod, g, pos, wa, wb, wt, ones, gmat, gains, freq, fbias)


def _fox_kernel(q_ref, k_ref, faug_ref, vt_ref, hm_ref, fsel_ref, o_ref, qf_ref, m_ref, acc_ref):
    i = pl.program_id(2)
    tq = q_ref.shape[1]
    tk = tq
    q = q_ref[0]
    for a in range(2):
        sel_row = fsel_ref[0, a:a + 1, :]
        qf_ref[a * tq:(a + 1) * tq, 0:LANES] = q * hm_ref[a:a + 1, :]
        qf_ref[a * tq:(a + 1) * tq, LANES:2 * LANES] = jnp.broadcast_to(sel_row, (tq, LANES))
    m_ref[...] = jnp.full_like(m_ref, -jnp.inf)
    acc_ref[...] = jnp.zeros_like(acc_ref)

    def block(j, masked):
        start = pl.multiple_of(j * tk, tk)
        kf = jnp.concatenate([k_ref[0, pl.ds(start, tk), :], faug_ref[0, pl.ds(start, tk), :]], axis=1)
        grp = min(FOX_GROUP, tq)
        n_groups = 2 * tq // grp

        def rows(g):
            return min(tk, (g * grp) % tq + grp) if masked else tk

        def scores(g):
            s = lax.dot_general(kf[:rows(g)], qf_ref[g * grp:(g + 1) * grp, :], _NT,
                                preferred_element_type=F32)
            if masked:
                krow = lax.broadcasted_iota(jnp.int32, s.shape, 0)
                qcol = lax.broadcasted_iota(jnp.int32, s.shape, 1) + (g * grp) % tq
                s = jnp.where(krow <= qcol, s, -jnp.inf)
            return s

        ahead = [scores(g) for g in range(min(QK_AHEAD, n_groups))]
        for g in range(n_groups):
            s = ahead.pop(0)
            if g + QK_AHEAD < n_groups:
                ahead.append(scores(g + QK_AHEAD))
            sl = slice(g * grp, (g + 1) * grp)
            m_old = m_ref[:, sl]
            m_new = jnp.maximum(m_old, jnp.max(s, axis=0, keepdims=True))
            p = jnp.exp2((s - m_new).astype(BF16))
            alpha = jnp.exp2(m_old - m_new)
            m_ref[:, sl] = m_new
            a = (g * grp) // tq
            vt = vt_ref[0, a * V_ROWS:(a + 1) * V_ROWS, pl.ds(start, rows(g))]
            acc_ref[:, sl] = alpha * acc_ref[:, sl] + jnp.dot(vt, p, preferred_element_type=F32)

    def body(j, c):
        block(j, False)
        return c

    lax.fori_loop(0, i, body, 0)
    block(i, True)
    out = acc_ref[0:HEAD_DIM, :] / acc_ref[HEAD_DIM:HEAD_DIM + 1, :]
    for a in range(2):
        o_ref[0, a * HEAD_DIM:(a + 1) * HEAD_DIM, :] = out[:, a * tq:(a + 1) * tq].astype(BF16)


def _fox_call(fq, fk, faug, fvt, hm2, fsel):
    bsz, s, _ = fq.shape
    tq = min(FOX_TILE, s)
    pairs = FOX_HEADS // 2
    return pl.pallas_call(
        _fox_kernel,
        grid=(bsz, pairs, s // tq),
        in_specs=[pl.BlockSpec((1, tq, LANES), lambda b, h, i: (b, i, h)),
                  pl.BlockSpec((1, s, LANES), lambda b, h, i: (b, 0, h)),
                  pl.BlockSpec((1, s, LANES), lambda b, h, i: (b, 0, 0)),
                  pl.BlockSpec((1, 2 * V_ROWS, s), lambda b, h, i: (b, h, 0)),
                  pl.BlockSpec((2, LANES), lambda b, h, i: (0, 0)),
                  pl.BlockSpec((1, 2, LANES), lambda b, h, i: (h, 0, 0))],
        out_specs=pl.BlockSpec((1, LANES, tq), lambda b, h, i: (b, h, i)),
        out_shape=jax.ShapeDtypeStruct((bsz, FOX_WIDTH, s), BF16),
        scratch_shapes=[pltpu.VMEM((2 * tq, 2 * LANES), BF16),
                        pltpu.VMEM((1, 2 * tq), F32),
                        pltpu.VMEM((V_ROWS, 2 * tq), F32)],
        compiler_params=_cparams(("parallel", "parallel", "arbitrary")),
        name="fox_attn",
    )(fq, fk, faug, fvt, hm2, fsel)


def _dsa_kernel(dq_ref, iq_ref, wt_ref, dk4_ref, dvt_ref, ik8_ref, hm8_ref, hm4_ref, o_ref,
                iq8_ref, q8_ref, keys_ref, hi_ref, m_ref, acc_ref, *, top_k):
    i = pl.program_id(1)
    tq = dq_ref.shape[1]
    s_len = dk4_ref.shape[1]
    tk = min(DSA_K_TILE, s_len)
    half_tk = tk // 2
    q_start = i * tq
    rem = (q_start + tq) % tk
    n_full = (q_start + tq) // tk + jnp.where(rem > half_tk, 1, 0)
    has_tail = (rem > 0) & (rem <= half_tk)

    def sweep(fn, init):
        carry = lax.fori_loop(0, n_full, lambda j, c: fn(pl.multiple_of(j * tk, tk), tk, c), init)
        return lax.cond(has_tail, lambda c: fn(pl.multiple_of(n_full * tk, tk), half_tk, c), lambda c: c, carry)

    iq = iq_ref[0]
    dq = dq_ref[0]
    half = dq.shape[1] // 2
    for h in range(IDX_HEADS):
        iq8_ref[h * tq:(h + 1) * tq, :] = iq * hm8_ref[h:h + 1, :]
        q8_ref[h * tq:(h + 1) * tq, :] = dq[:, (h // 4) * half:(h // 4 + 1) * half] * hm4_ref[h % 4:h % 4 + 1, :]

    adm_end = q_start + (lax.broadcasted_iota(jnp.int32, (1, tq), 1) // CHUNK + 1) * CHUNK
    wt = wt_ref[0, :, pl.ds(pl.multiple_of(q_start, tq), tq)]

    def score_block(start, size, c):
        d = lax.dot_general(ik8_ref[0, pl.ds(start, size), :], iq8_ref[...], _NT, preferred_element_type=F32)
        sc = jnp.zeros((size, tq), F32)
        for h in range(IDX_HEADS):
            sc = sc + jnp.maximum(d[:, h * tq:(h + 1) * tq], 0.0) * wt[IDX_HEADS + h:IDX_HEADS + h + 1, :]
        sc = jnp.where(jnp.abs(sc) < F32_MIN_NORMAL, 0.0, sc)
        bits = lax.bitcast_convert_type(sc, jnp.int32)
        key = bits ^ ((bits >> 31) & 0x7FFFFFFF)
        adm = lax.broadcasted_iota(jnp.int32, (size, tq), 0) + start < adm_end
        keys_ref[pl.ds(start, size), :] = jnp.where(adm, key, INT_MIN)
        top = lax.bitcast_convert_type(bits & -65536, F32)
        hi_ref[pl.ds(start, size), :] = jnp.where(adm, top, jnp.nan).astype(BF16)
        return c

    sweep(score_block, 0)

    def count(pred):
        def body(start, size, acc):
            hit = jnp.where(pred(keys_ref[pl.ds(start, size), :]), 1, 0)
            return acc + jnp.sum(hit.reshape(size // COUNT_ROWS, COUNT_ROWS, tq), axis=0)
        return jnp.sum(sweep(body, jnp.zeros((COUNT_ROWS, tq), jnp.int32)), axis=0, keepdims=True)

    def count_hi(cand):
        def body(start, size, acc):
            hit = jnp.where(hi_ref[pl.ds(start, size), :] >= cand, jnp.ones((), BF16), jnp.zeros((), BF16))
            part = hit[0:COUNT_ROWS]
            for r in range(1, size // COUNT_ROWS):
                part = part + hit[r * COUNT_ROWS:(r + 1) * COUNT_ROWS]
            return acc + part.astype(F32)
        return jnp.sum(sweep(body, jnp.zeros((COUNT_ROWS, tq), F32)), axis=0, keepdims=True)

    def step(cand, cnt, thr, done):
        ok = (cnt >= top_k) & (done == 0)
        return jnp.where(ok, cand, thr), jnp.where(ok & (cnt == top_k), 1, done)

    def bisect_hi(it, state):
        t16, done = state
        cand = t16 + jnp.left_shift(jnp.int32(1), 15 - it)
        pattern = jnp.where(cand >= 0, cand, cand ^ 0x7FFF)
        pattern = jnp.where((pattern & 0x7F80) == 0, jnp.where(cand > 0, 0x0080, 0), pattern)
        cand_f = lax.bitcast_convert_type(jnp.left_shift(pattern, 16), F32).astype(BF16)
        return step(cand, count_hi(cand_f), t16, done)

    def bisect_lo(state):
        g, thr, done = state
        for u in range(LOW_PASSES_PER_CHECK):
            cand = thr + jnp.left_shift(jnp.int32(1), 15 - (g * LOW_PASSES_PER_CHECK + u))
            thr, done = step(cand, count(lambda kblk: kblk >= cand), thr, done)
        return g + 1, thr, done

    done = jnp.where(adm_end <= top_k, 1, 0)
    t16, done = lax.fori_loop(0, 16, bisect_hi, (jnp.full((1, tq), -2 ** 15, jnp.int32), done))
    base = jnp.left_shift(t16, 16)
    above = count(lambda kblk: kblk >= base + 1)
    thr = jnp.where((done == 0) & (above == top_k), base + 1, base)
    done = jnp.where(above <= top_k, 1, done)
    _, thr, _ = lax.while_loop(lambda st: (st[0] < 16 // LOW_PASSES_PER_CHECK) & (jnp.min(st[2]) == 0),
                               bisect_lo, (jnp.int32(0), thr, done))
    thr = jnp.maximum(thr, INT_MIN + 1)
    n_ge = count(lambda kblk: kblk >= thr)

    @pl.when(jnp.max(n_ge) > top_k)
    def _():
        n_gt = count(lambda kblk: kblk > thr)
        need = (top_k - n_gt).astype(F32)

        r2 = lax.broadcasted_iota(jnp.int32, (DEMOTE_ROWS, DEMOTE_ROWS), 0)
        c2 = lax.broadcasted_iota(jnp.int32, (DEMOTE_ROWS, DEMOTE_ROWS), 1)
        lower = jnp.where(c2 < r2, 1.0, 0.0).astype(BF16)

        def demote(start, size, seen):
            chunks = []
            for c0 in range(0, size, DEMOTE_ROWS):
                kblk = keys_ref[pl.ds(start + c0, DEMOTE_ROWS), :]
                eqf = jnp.where(kblk == thr, 1.0, 0.0)
                chunks.append((c0, kblk, eqf, jnp.dot(lower, eqf.astype(BF16), preferred_element_type=F32),
                               jnp.sum(eqf, axis=0, keepdims=True)))
            for c0, kblk, eqf, within, total in chunks:
                drop = (eqf > 0.0) & (within + seen >= need)
                keys_ref[pl.ds(start + c0, DEMOTE_ROWS), :] = jnp.where(drop, thr - 1, kblk)
                seen = seen + total
            return seen

        sweep(demote, jnp.zeros((1, tq), F32))

    m_ref[...] = jnp.full_like(m_ref, -jnp.inf)
    acc_ref[...] = jnp.zeros_like(acc_ref)

    def attend(start, size, c):
        bias = jnp.where(keys_ref[pl.ds(start, size), :] >= thr, 0.0, -jnp.inf)
        grp = min(DSA_GROUP, DSA_HEADS * tq)
        bias = jnp.concatenate([bias] * (grp // tq), axis=1)
        kb = dk4_ref[0, pl.ds(start, size), :]
        vt = dvt_ref[0, :, pl.ds(start, size)]
        n_groups = DSA_HEADS * tq // grp

        def scores(g):
            return lax.dot_general(kb, q8_ref[g * grp:(g + 1) * grp, :], _NT,
                                   preferred_element_type=F32) + bias

        ahead = [scores(g) for g in range(min(QK_AHEAD, n_groups))]
        for g in range(n_groups):
            s = ahead.pop(0)
            if g + QK_AHEAD < n_groups:
                ahead.append(scores(g + QK_AHEAD))
            sl = slice(g * grp, (g + 1) * grp)
            m_old = m_ref[:, sl]
            m_new = jnp.maximum(m_old, jnp.max(s, axis=0, keepdims=True))
            m_safe = jnp.where(m_new == -jnp.inf, 0.0, m_new)
            p = jnp.exp2((s - m_safe).astype(BF16))
            alpha = jnp.exp2(m_old - m_safe)
            m_ref[:, sl] = m_new
            acc_ref[:, sl] = alpha * acc_ref[:, sl] + jnp.dot(vt, p, preferred_element_type=F32)
        return c

    sweep(attend, 0)
    out = acc_ref[0:HEAD_DIM, :] / acc_ref[HEAD_DIM:HEAD_DIM + 1, :]
    for h in range(DSA_HEADS):
        o_ref[0, h * HEAD_DIM:(h + 1) * HEAD_DIM, :] = out[:, h * tq:(h + 1) * tq].astype(BF16)


def _dsa_call(dq, iq, wt, dk4, dvt, ik8, hm8, hm4, top_k):
    bsz, s, _ = dq.shape
    tq = min(DSA_Q_TILE, s)

    def tok(w):
        return pl.BlockSpec((1, tq, w), lambda b, i: (b, i, 0))

    def seq(w):
        return pl.BlockSpec((1, s, w), lambda b, i: (b, 0, 0))

    def seqt(r):
        return pl.BlockSpec((1, r, s), lambda b, i: (b, 0, 0))

    return pl.pallas_call(
        functools.partial(_dsa_kernel, top_k=top_k),
        grid=(bsz, s // tq),
        in_specs=[tok(DSA_WIDTH), tok(IDX_WIDTH), seqt(2 * IDX_HEADS), seq(256), seqt(V_ROWS), seq(256),
                  pl.BlockSpec(hm8.shape, lambda b, i: (0, 0)), pl.BlockSpec(hm4.shape, lambda b, i: (0, 0))],
        out_specs=pl.BlockSpec((1, DSA_WIDTH, tq), lambda b, i: (b, 0, i)),
        out_shape=jax.ShapeDtypeStruct((bsz, DSA_WIDTH, s), BF16),
        scratch_shapes=[pltpu.VMEM((IDX_HEADS * tq, 256), BF16), pltpu.VMEM((DSA_HEADS * tq, 256), BF16),
                        pltpu.VMEM((s, tq), jnp.int32), pltpu.VMEM((s, tq), BF16),
                        pltpu.VMEM((1, DSA_HEADS * tq), F32),
                        pltpu.VMEM((V_ROWS, DSA_HEADS * tq), F32)],
        compiler_params=_cparams(("parallel", "arbitrary")),
        name="dsa_attn",
    )(dq, iq, wt, dk4, dvt, ik8, hm8, hm4)


def _merge_kernel(x_ref, mod_ref, g_ref, ya_ref, yb_ref, wg_ref, wfa_ref, wfb_ref, wo_ref, o_ref):
    x = x_ref[0]
    d = x.shape[1]
    h = _modulated(x, mod_ref, g_ref, 1).astype(BF16)
    zg = jnp.dot(h, wg_ref[...], preferred_element_type=F32)
    pa = lax.dot_general(ya_ref[0], wfa_ref[...], _TN, preferred_element_type=F32)
    pb = lax.dot_general(yb_ref[0], wfb_ref[...], _TN, preferred_element_type=F32)
    merged = jax.nn.sigmoid(zg[:, :d]) * pa + jax.nn.sigmoid(zg[:, d:]) * pb
    y = jnp.dot(merged.astype(BF16), wo_ref[...], preferred_element_type=F32)
    o_ref[0] = x + mod_ref[0, 5:6, :] * y


def _merge_call(x, mod, g, ya, yb, wg, wfa, wfb, wo):
    bsz, s, d = x.shape
    tm = min(TOKEN_TILE, s)

    def tok(w):
        return pl.BlockSpec((1, tm, w), lambda b, i: (b, i, 0))

    def tokt(w):
        return pl.BlockSpec((1, w, tm), lambda b, i: (b, 0, i))

    return pl.pallas_call(
        _merge_kernel,
        grid=(bsz, s // tm),
        in_specs=[tok(d),
                  pl.BlockSpec((1, 3 * N_SUBLAYERS, d), lambda b, i: (b, 0, 0)),
                  _const_spec((N_SUBLAYERS, d)),
                  tokt(FOX_WIDTH), tokt(DSA_WIDTH),
                  _const_spec(wg.shape), _const_spec(wfa.shape), _const_spec(wfb.shape), _const_spec(wo.shape)],
        out_specs=tok(d),
        out_shape=jax.ShapeDtypeStruct(x.shape, F32),
        compiler_params=_cparams(("parallel", "parallel")),
        name="merge_out",
    )(x, mod, g, ya, yb, wg, wfa, wfb, wo)


def _head_mask(n_heads, width):
    return jnp.asarray(np.kron(np.eye(n_heads), np.ones((1, width))), BF16)


def _rope_freq_row(rot_dim, period):
    inv_freq = ROPE_THETA ** (-jnp.arange(0, rot_dim, 2, dtype=F32) / rot_dim)
    half = rot_dim // 2
    head = jnp.concatenate([-inv_freq, inv_freq, jnp.zeros((period - 2 * half,), F32)])
    return jnp.tile(head, LANES // period)


def kernel(x, c, positions, ada_w, ada_b, norm_g, ffn1_w1, ffn1_w3, ffn1_w2, w_in, fox_f_bias, fox_qk_g, dsa_qk_g,
           w_br_fox, w_br_dsa, w_out, ffn2_w1, ffn2_w3, ffn2_w2):
    bsz, s, d = x.shape
    top_k = min(TOPK_MAX, s // 4)
    depth = ada_w.shape[0]
    pos = jnp.broadcast_to(positions.astype(F32)[:, :, None], (bsz, s, LANES))
    freq = jnp.stack([_rope_freq_row(HEAD_DIM // ROPE_FRACTION, HEAD_DIM),
                      _rope_freq_row(IDX_DIM // ROPE_FRACTION, IDX_DIM)])
    gmat = jnp.asarray(np.kron(np.eye(FOX_HEADS), np.ones((HEAD_DIM, HEAD_DIM))), BF16)
    hm2 = _head_mask(2, HEAD_DIM)
    hm4 = _head_mask(4, HEAD_DIM)
    hm8 = _head_mask(IDX_HEADS, IDX_DIM)
    fsel = jnp.asarray(np.tile(np.eye(FOX_HEADS), (1, LANES // FOX_HEADS))
                       * (np.arange(LANES) < 3 * FOX_HEADS), BF16).reshape(FOX_HEADS // 2, 2, LANES)
    o_fq, o_fk, o_fv = 0, FOX_WIDTH, 2 * FOX_WIDTH
    o_ff = 3 * FOX_WIDTH
    o_dq = o_ff + FOX_HEADS
    o_dk = o_dq + DSA_WIDTH
    o_dv = o_dk + HEAD_DIM
    o_iq = o_dv + HEAD_DIM
    o_ik = o_iq + IDX_WIDTH
    o_iw = o_ik + IDX_DIM
    o_ga = o_iw + IDX_HEADS

    for l in range(depth):
        mod = _ada_call(c, ada_w[l], ada_b[l]).reshape(bsz, 3 * N_SUBLAYERS, d)
        g = norm_g[l]
        x = _ffn_call(x, mod, g, ffn1_w1[l].astype(BF16), ffn1_w3[l].astype(BF16), ffn1_w2[l].astype(BF16), 0)

        w = w_in[l]
        wa = jnp.concatenate([w[:, o_fq:o_fq + 2 * FOX_WIDTH], w[:, o_dq:o_dq + DSA_WIDTH]], axis=1).astype(BF16)
        pad = jnp.zeros((d, LANES - FOX_HEADS), F32)
        wb = jnp.concatenate([jnp.tile(w[:, o_dk:o_dk + HEAD_DIM], (1, 4)),
                              jnp.tile(w[:, o_ik:o_ik + IDX_DIM], (1, IDX_HEADS)),
                              w[:, o_iq:o_iq + IDX_WIDTH],
                              w[:, o_ff:o_ff + FOX_HEADS], pad], axis=1).astype(BF16)
        zrows = jnp.zeros((FOX_HEADS, V_ROWS - HEAD_DIM, d), F32)
        wfv = jnp.transpose(w[:, o_fv:o_fv + FOX_WIDTH]).reshape(FOX_HEADS, HEAD_DIM, d)
        wt = jnp.concatenate([jnp.concatenate([wfv, zrows], axis=1).reshape(FOX_HEADS * V_ROWS, d),
                              jnp.transpose(w[:, o_dv:o_dv + HEAD_DIM]), zrows[0],
                              jnp.zeros((IDX_HEADS, d), F32), jnp.transpose(w[:, o_iw:o_iw + IDX_HEADS])],
                             axis=0).astype(BF16)
        ones = jnp.asarray(np.tile((np.arange(FOX_HEADS * V_ROWS) % V_ROWS >= HEAD_DIM)[:, None], (1, LANES)), F32)
        wg = w[:, o_ga:o_ga + 2 * d].astype(BF16)
        gains = jnp.stack([jnp.tile(fox_qk_g[l, 0], FOX_HEADS), jnp.tile(fox_qk_g[l, 1], FOX_HEADS),
                           jnp.tile(dsa_qk_g[l, 0], DSA_HEADS), jnp.tile(dsa_qk_g[l, 1], DSA_HEADS)]).astype(F32)
        fbias = jnp.concatenate([fox_f_bias[l].astype(F32), jnp.zeros((LANES - FOX_HEADS,), F32)]).reshape(1, LANES)

        fq, fk, dq, dk4, ik8, iq, faug, fvt, dvt, iwt = _mix_in_call(x, mod, g, pos, wa, wb, wt, ones, gmat, gains,
                                                                       freq, fbias)
        yat = _fox_call(fq, fk, faug, fvt, hm2, fsel)
        ybt = _dsa_call(dq, iq, iwt, dk4, dvt, ik8, hm8, hm4, top_k)
        x = _merge_call(x, mod, g, yat, ybt, wg, w_br_fox[l].astype(BF16), w_br_dsa[l].astype(BF16),
                        w_out[l].astype(BF16))
        x = _ffn_call(x, mod, g, ffn2_w1[l].astype(BF16), ffn2_w3[l].astype(BF16), ffn2_w2[l].astype(BF16), 2)
    return x
```

```python
import functools

import numpy as np
import jax
import jax.numpy as jnp
from jax import lax
from jax.experimental import pallas as pl
from jax.experimental.pallas import tpu as pltpu

D_MODEL = 1024
CHUNK = 64
HEAD_DIM = 64
FOX_HEADS = 8
DSA_HEADS = 8
IDX_HEADS = 8
IDX_DIM = 32
TOPK_MAX = 256
ROPE_THETA = 500000.0
ROPE_FRACTION = 4
D_FF = 2816
N_SUBLAYERS = 3
NORM_EPS = 1e-6
FOX_WIDTH = FOX_HEADS * HEAD_DIM
DSA_WIDTH = DSA_HEADS * HEAD_DIM
IDX_WIDTH = IDX_HEADS * IDX_DIM

LANES = 128
VMEM_LIMIT_BYTES = 56 * 1024 * 1024

TOKEN_TILE = 512
FOX_TILE = 1024
DSA_Q_TILE = 256
DSA_K_TILE = 1024
FOX_GROUP = 512
DSA_GROUP = 1024
QK_AHEAD = 3
LOW_PASSES_PER_CHECK = 2
DEMOTE_ROWS = 128
COUNT_ROWS = 32

V_ROWS = HEAD_DIM + 16
LOG2E = 1.4426950408889634
INT_MIN = -2 ** 31
F32_MIN_NORMAL = 1.1754943508222875e-38
F32 = jnp.float32
BF16 = jnp.bfloat16
_NT = (((1,), (1,)), ((), ()))
_TN = (((0,), (0,)), ((), ()))


def _cparams(sem):
    return pltpu.CompilerParams(dimension_semantics=sem, vmem_limit_bytes=VMEM_LIMIT_BYTES)


def _const_spec(shape):
    nd = len(shape)
    return pl.BlockSpec(shape, lambda *_: (0,) * nd, pipeline_mode=pl.Buffered(1))


def _modulated(x, mod_ref, g_ref, sub):
    ms = jnp.mean(x * x, axis=-1, keepdims=True)
    y = x * lax.rsqrt(ms + NORM_EPS) * g_ref[sub:sub + 1, :]
    shift = mod_ref[0, 3 * sub:3 * sub + 1, :]
    scale = mod_ref[0, 3 * sub + 1:3 * sub + 2, :]
    return y * (1.0 + scale) + shift


def _ada_kernel(c_ref, w_ref, b_ref, o_ref):
    c = c_ref[...]
    cond = c * jax.nn.sigmoid(c)
    o_ref[...] = jnp.dot(cond, w_ref[...], preferred_element_type=F32,
                         precision=lax.Precision.HIGHEST) + b_ref[...]


def _ada_call(c, w, b):
    bsz, d = c.shape
    n = w.shape[1]
    tn = 1152
    return pl.pallas_call(
        _ada_kernel,
        grid=(n // tn,),
        in_specs=[pl.BlockSpec((bsz, d), lambda j: (0, 0)),
                  pl.BlockSpec((d, tn), lambda j: (0, j)),
                  pl.BlockSpec((1, tn), lambda j: (0, j))],
        out_specs=pl.BlockSpec((bsz, tn), lambda j: (0, j)),
        out_shape=jax.ShapeDtypeStruct((bsz, n), F32),
        compiler_params=_cparams(("arbitrary",)),
        name="ada_ln",
    )(c, w, b.reshape(1, n))


def _ffn_kernel(x_ref, mod_ref, g_ref, w1_ref, w3_ref, w2_ref, o_ref, *, sub):
    x = x_ref[0]
    h = _modulated(x, mod_ref, g_ref, sub).astype(BF16)
    a = jnp.dot(h, w1_ref[...], preferred_element_type=F32)
    b = jnp.dot(h, w3_ref[...], preferred_element_type=F32)
    act = (a * jax.nn.sigmoid(a) * b).astype(BF16)
    y = jnp.dot(act, w2_ref[...], preferred_element_type=F32)
    gate = mod_ref[0, 3 * sub + 2:3 * sub + 3, :]
    o_ref[0] = x + (0.5 * gate) * y


def _ffn_call(x, mod, g, w1, w3, w2, sub):
    bsz, s, d = x.shape
    tm = min(TOKEN_TILE, s)
    f = w1.shape[1]
    tok = pl.BlockSpec((1, tm, d), lambda b, i: (b, i, 0))
    return pl.pallas_call(
        functools.partial(_ffn_kernel, sub=sub),
        grid=(bsz, s // tm),
        in_specs=[tok,
                  pl.BlockSpec((1, 3 * N_SUBLAYERS, d), lambda b, i: (b, 0, 0)),
                  _const_spec((N_SUBLAYERS, d)),
                  _const_spec((d, f)), _const_spec((d, f)), _const_spec((f, d))],
        out_specs=tok,
        out_shape=jax.ShapeDtypeStruct(x.shape, F32),
        compiler_params=_cparams(("parallel", "parallel")),
        name=f"ffn{sub}",
    )(x, mod, g, w1, w3, w2)


def _split3(v):
    hi = v.astype(BF16)
    r = v - hi.astype(F32)
    mid = r.astype(BF16)
    lo = (r - mid.astype(F32)).astype(BF16)
    return hi, mid, lo


def _group_norm(z, gmat, gain, group):
    ssq = jnp.dot((z * z).astype(BF16), gmat, preferred_element_type=F32)
    return z * lax.rsqrt(ssq * (1.0 / group) + NORM_EPS) * gain


def _rope(z, cos, sin, half, period):
    width = z.shape[1]
    lane = lax.broadcasted_iota(jnp.int32, z.shape, 1) % period
    up = pltpu.roll(z, width - half, axis=1)
    dn = pltpu.roll(z, half, axis=1)
    return z * cos + jnp.where(lane < half, up, dn) * sin


def _mix_in_kernel(x_ref, mod_ref, g_ref, pos_ref, wa_ref, wb_ref, wt_ref, ones_ref, gmat_ref, gains_ref, freq_ref,
                   fbias_ref, fq_ref, fk_ref, dq_ref, dk4_ref, ik8_ref, iq_ref, faug_ref, fvt_ref, dvt_ref, iwt_ref,
                   carry_ref):
    @pl.when(pl.program_id(1) == 0)
    def _():
        carry_ref[...] = jnp.zeros_like(carry_ref)

    x = x_ref[0]
    tm = x.shape[0]
    h = _modulated(x, mod_ref, g_ref, 1).astype(BF16)
    za = jnp.dot(h, wa_ref[...], preferred_element_type=F32)
    zb = jnp.dot(h, wb_ref[...], preferred_element_type=F32)
    zt = lax.dot_general(wt_ref[...], h, _NT, preferred_element_type=F32)
    n_v = FOX_HEADS * V_ROWS
    fvt_ref[0] = (zt[0:n_v] + jnp.concatenate([ones_ref[...]] * (tm // LANES), axis=1)).astype(BF16)
    dvt_ref[0] = (zt[n_v:n_v + V_ROWS]
                  + jnp.concatenate([ones_ref[0:V_ROWS, :]] * (tm // LANES), axis=1)).astype(BF16)
    iwt_ref[0] = zt[n_v + V_ROWS:, :] * ((IDX_HEADS * IDX_DIM) ** -0.5)

    gmat = gmat_ref[...]
    scale = HEAD_DIM ** -0.5 * LOG2E
    fq = _group_norm(za[:, 0:512], gmat, gains_ref[0:1, :], HEAD_DIM) * scale
    fk = _group_norm(za[:, 512:1024], gmat, gains_ref[1:2, :], HEAD_DIM)
    fq_ref[0] = fq.astype(BF16)
    fk_ref[0] = fk.astype(BF16)

    pos = pos_ref[0]
    ang_a = pos * freq_ref[0:1, :]
    ang_i = pos * freq_ref[1:2, :]
    cos_a, sin_a = jnp.cos(ang_a), jnp.sin(ang_a)
    cos_i, sin_i = jnp.cos(ang_i), jnp.sin(ang_i)
    rot_a = HEAD_DIM // ROPE_FRACTION // 2
    rot_i = IDX_DIM // ROPE_FRACTION // 2

    dq = _group_norm(za[:, 1024:1536], gmat, gains_ref[2:3, :], HEAD_DIM)
    dq = _rope(dq, jnp.concatenate([cos_a] * 4, axis=1), jnp.concatenate([sin_a] * 4, axis=1), rot_a, HEAD_DIM)
    dq_ref[0] = (dq * scale).astype(BF16)

    dk4 = _group_norm(zb[:, 0:256], gmat[0:256, 0:256], gains_ref[3:4, 0:256], HEAD_DIM)
    dk4 = _rope(dk4, jnp.concatenate([cos_a] * 2, axis=1), jnp.concatenate([sin_a] * 2, axis=1), rot_a, HEAD_DIM)
    dk4_ref[0] = dk4.astype(BF16)

    cos_i2 = jnp.concatenate([cos_i] * 2, axis=1)
    sin_i2 = jnp.concatenate([sin_i] * 2, axis=1)
    ik8_ref[0] = _rope(zb[:, 256:512], cos_i2, sin_i2, rot_i, IDX_DIM).astype(BF16)
    iq_ref[0] = _rope(zb[:, 512:768], cos_i2, sin_i2, rot_i, IDX_DIM).astype(BF16)

    sm = zb[:, 768:896]
    v = sm + fbias_ref[...]
    logf = jnp.minimum(v, 0.0) - jnp.log1p(jnp.exp(-jnp.abs(v)))
    row = lax.broadcasted_iota(jnp.int32, (tm, tm), 0)
    col = lax.broadcasted_iota(jnp.int32, (tm, tm), 1)
    tri = jnp.where(row >= col, 1.0, 0.0).astype(BF16)
    hi, mid, lo = _split3(logf)
    csum = (jnp.dot(tri, hi, preferred_element_type=F32) + jnp.dot(tri, mid, preferred_element_type=F32)
            + jnp.dot(tri, lo, preferred_element_type=F32)) + carry_ref[...]
    carry_ref[...] = csum[tm - 1:tm, :]
    lane = lax.broadcasted_iota(jnp.int32, sm.shape, 1)

    nf = -csum * LOG2E
    t0 = nf.astype(BF16).astype(F32)
    r1 = nf - t0
    t1 = r1.astype(BF16).astype(F32)
    t2 = r1 - t1
    aug = jnp.where(lane < FOX_HEADS, t0,
                    jnp.where(lane < 2 * FOX_HEADS, pltpu.roll(t1, FOX_HEADS, axis=1),
                              jnp.where(lane < 3 * FOX_HEADS, pltpu.roll(t2, 2 * FOX_HEADS, axis=1), 0.0)))
    faug_ref[0] = aug.astype(BF16)


def _mix_in_call(x, mod, g, pos, wa, wb, wt, ones, gmat, gains, freq, fbias):
    bsz, s, d = x.shape
    tm = min(TOKEN_TILE, s)

    def tok(w):
        return pl.BlockSpec((1, tm, w), lambda b, i: (b, i, 0))

    def tokt(r):
        return pl.BlockSpec((1, r, tm), lambda b, i: (b, 0, i))

    widths = (512, 512, 512, 256, 256, 256, 128)
    rows = ((FOX_HEADS * V_ROWS, BF16), (V_ROWS, BF16), (2 * IDX_HEADS, F32))
    out_shapes = [jax.ShapeDtypeStruct((bsz, s, w), BF16) for w in widths]
    out_shapes += [jax.ShapeDtypeStruct((bsz, r, s), dt) for r, dt in rows]
    return pl.pallas_call(
        _mix_in_kernel,
        grid=(bsz, s // tm),
        in_specs=[tok(d),
                  pl.BlockSpec((1, 3 * N_SUBLAYERS, d), lambda b, i: (b, 0, 0)),
                  _const_spec((N_SUBLAYERS, d)),
                  tok(LANES),
                  _const_spec(wa.shape), _const_spec(wb.shape), _const_spec(wt.shape), _const_spec(ones.shape),
                  _const_spec(gmat.shape), _const_spec(gains.shape), _const_spec(freq.shape), _const_spec(fbias.shape)],
        out_specs=[tok(w) for w in widths] + [tokt(r) for r, _ in rows],
        out_shape=out_shapes,
        scratch_shapes=[pltpu.VMEM((1, LANES), F32)],
        compiler_params=_cparams(("arbitrary", "arbitrary")),
        name="mix_in",
    )(x, mod, g, pos, wa, wb, wt, ones, gmat, gains, freq, fbias)


def _fox_kernel(q_ref, k_ref, faug_ref, vt_ref, hm_ref, fsel_ref, o_ref, qf_ref, m_ref, acc_ref):
    i = pl.program_id(2)
    tq = q_ref.shape[1]
    tk = tq
    q = q_ref[0]
    for a in range(2):
        sel_row = fsel_ref[0, a:a + 1, :]
        qf_ref[a * tq:(a + 1) * tq, 0:LANES] = q * hm_ref[a:a + 1, :]
        qf_ref[a * tq:(a + 1) * tq, LANES:2 * LANES] = jnp.broadcast_to(sel_row, (tq, LANES))
    m_ref[...] = jnp.full_like(m_ref, -jnp.inf)
    acc_ref[...] = jnp.zeros_like(acc_ref)

    def block(j, masked):
        start = pl.multiple_of(j * tk, tk)
        kf = jnp.concatenate([k_ref[0, pl.ds(start, tk), :], faug_ref[0, pl.ds(start, tk), :]], axis=1)
        grp = min(FOX_GROUP, tq)
        n_groups = 2 * tq // grp

        def rows(g):
            return min(tk, (g * grp) % tq + grp) if masked else tk

        def scores(g):
            s = lax.dot_general(kf[:rows(g)], qf_ref[g * grp:(g + 1) * grp, :], _NT,
                                preferred_element_type=F32)
            if masked:
                krow = lax.broadcasted_iota(jnp.int32, s.shape, 0)
                qcol = lax.broadcasted_iota(jnp.int32, s.shape, 1) + (g * grp) % tq
                s = jnp.where(krow <= qcol, s, -jnp.inf)
            return s

        ahead = [scores(g) for g in range(min(QK_AHEAD, n_groups))]
        for g in range(n_groups):
            s = ahead.pop(0)
            if g + QK_AHEAD < n_groups:
                ahead.append(scores(g + QK_AHEAD))
            sl = slice(g * grp, (g + 1) * grp)
            m_old = m_ref[:, sl]
            m_new = jnp.maximum(m_old, jnp.max(s, axis=0, keepdims=True))
            p = jnp.exp2((s - m_new).astype(BF16))
            alpha = jnp.exp2(m_old - m_new)
            m_ref[:, sl] = m_new
            a = (g * grp) // tq
            vt = vt_ref[0, a * V_ROWS:(a + 1) * V_ROWS, pl.ds(start, rows(g))]
            acc_ref[:, sl] = alpha * acc_ref[:, sl] + jnp.dot(vt, p, preferred_element_type=F32)

    def body(j, c):
        block(j, False)
        return c

    lax.fori_loop(0, i, body, 0)
    block(i, True)
    out = acc_ref[0:HEAD_DIM, :] / acc_ref[HEAD_DIM:HEAD_DIM + 1, :]
    for a in range(2):
        o_ref[0, a * HEAD_DIM:(a + 1) * HEAD_DIM, :] = out[:, a * tq:(a + 1) * tq].astype(BF16)


def _fox_call(fq, fk, faug, fvt, hm2, fsel):
    bsz, s, _ = fq.shape
    tq = min(FOX_TILE, s)
    pairs = FOX_HEADS // 2
    return pl.pallas_call(
        _fox_kernel,
        grid=(bsz, pairs, s // tq),
        in_specs=[pl.BlockSpec((1, tq, LANES), lambda b, h, i: (b, i, h)),
                  pl.BlockSpec((1, s, LANES), lambda b, h, i: (b, 0, h)),
                  pl.BlockSpec((1, s, LANES), lambda b, h, i: (b, 0, 0)),
                  pl.BlockSpec((1, 2 * V_ROWS, s), lambda b, h, i: (b, h, 0)),
                  pl.BlockSpec((2, LANES), lambda b, h, i: (0, 0)),
                  pl.BlockSpec((1, 2, LANES), lambda b, h, i: (h, 0, 0))],
        out_specs=pl.BlockSpec((1, LANES, tq), lambda b, h, i: (b, h, i)),
        out_shape=jax.ShapeDtypeStruct((bsz, FOX_WIDTH, s), BF16),
        scratch_shapes=[pltpu.VMEM((2 * tq, 2 * LANES), BF16),
                        pltpu.VMEM((1, 2 * tq), F32),
                        pltpu.VMEM((V_ROWS, 2 * tq), F32)],
        compiler_params=_cparams(("parallel", "parallel", "arbitrary")),
        name="fox_attn",
    )(fq, fk, faug, fvt, hm2, fsel)


def _dsa_kernel(dq_ref, iq_ref, wt_ref, dk4_ref, dvt_ref, ik8_ref, hm8_ref, hm4_ref, o_ref,
                iq8_ref, q8_ref, keys_ref, hi_ref, m_ref, acc_ref, *, top_k):
    i = pl.program_id(1)
    tq = dq_ref.shape[1]
    s_len = dk4_ref.shape[1]
    tk = min(DSA_K_TILE, s_len)
    half_tk = tk // 2
    q_start = i * tq
    rem = (q_start + tq) % tk
    n_full = (q_start + tq) // tk + jnp.where(rem > half_tk, 1, 0)
    has_tail = (rem > 0) & (rem <= half_tk)

    def sweep(fn, init):
        carry = lax.fori_loop(0, n_full, lambda j, c: fn(pl.multiple_of(j * tk, tk), tk, c), init)
        return lax.cond(has_tail, lambda c: fn(pl.multiple_of(n_full * tk, tk), half_tk, c), lambda c: c, carry)

    iq = iq_ref[0]
    dq = dq_ref[0]
    half = dq.shape[1] // 2
    for h in range(IDX_HEADS):
        iq8_ref[h * tq:(h + 1) * tq, :] = iq * hm8_ref[h:h + 1, :]
        q8_ref[h * tq:(h + 1) * tq, :] = dq[:, (h // 4) * half:(h // 4 + 1) * half] * hm4_ref[h % 4:h % 4 + 1, :]

    adm_end = q_start + (lax.broadcasted_iota(jnp.int32, (1, tq), 1) // CHUNK + 1) * CHUNK
    wt = wt_ref[0, :, pl.ds(pl.multiple_of(q_start, tq), tq)]

    def score_block(start, size, c):
        d = lax.dot_general(ik8_ref[0, pl.ds(start, size), :], iq8_ref[...], _NT, preferred_element_type=F32)
        sc = jnp.zeros((size, tq), F32)
        for h in range(IDX_HEADS):
            sc = sc + jnp.maximum(d[:, h * tq:(h + 1) * tq], 0.0) * wt[IDX_HEADS + h:IDX_HEADS + h + 1, :]
        sc = jnp.where(jnp.abs(sc) < F32_MIN_NORMAL, 0.0, sc)
        bits = lax.bitcast_convert_type(sc, jnp.int32)
        key = bits ^ ((bits >> 31) & 0x7FFFFFFF)
        adm = lax.broadcasted_iota(jnp.int32, (size, tq), 0) + start < adm_end
        keys_ref[pl.ds(start, size), :] = jnp.where(adm, key, INT_MIN)
        top = lax.bitcast_convert_type(bits & -65536, F32)
        hi_ref[pl.ds(start, size), :] = jnp.where(adm, top, jnp.nan).astype(BF16)
        return c

    sweep(score_block, 0)

    def count(pred):
        def body(start, size, acc):
            hit = jnp.where(pred(keys_ref[pl.ds(start, size), :]), 1, 0)
            return acc + jnp.sum(hit.reshape(size // COUNT_ROWS, COUNT_ROWS, tq), axis=0)
        return jnp.sum(sweep(body, jnp.zeros((COUNT_ROWS, tq), jnp.int32)), axis=0, keepdims=True)

    def count_hi(cand):
        def body(start, size, acc):
            hit = jnp.where(hi_ref[pl.ds(start, size), :] >= cand, jnp.ones((), BF16), jnp.zeros((), BF16))
            part = hit[0:COUNT_ROWS]
            for r in range(1, size // COUNT_ROWS):
                part = part + hit[r * COUNT_ROWS:(r + 1) * COUNT_ROWS]
            return acc + part.astype(F32)
        return jnp.sum(sweep(body, jnp.zeros((COUNT_ROWS, tq), F32)), axis=0, keepdims=True)

    def step(cand, cnt, state):
        thr, done, n_ge, n_gt = state
        live = done == 0
        ok = (cnt >= top_k) & live
        return (jnp.where(ok, cand, thr), jnp.where(ok & (cnt == top_k), 1, done),
                jnp.where(ok, cnt, n_ge), jnp.where((cnt < top_k) & live, cnt, n_gt))

    def bisect_hi(it, state):
        cand = state[0] + jnp.left_shift(jnp.int32(1), 15 - it)
        pattern = jnp.where(cand >= 0, cand, cand ^ 0x7FFF)
        pattern = jnp.where((pattern & 0x7F80) == 0, jnp.where(cand > 0, 0x0080, 0), pattern)
        cand_f = lax.bitcast_convert_type(jnp.left_shift(pattern, 16), F32).astype(BF16)
        return step(cand, count_hi(cand_f).astype(jnp.int32), state)

    def bisect_lo(carry):
        g, state = carry
        for u in range(LOW_PASSES_PER_CHECK):
            cand = state[0] + jnp.left_shift(jnp.int32(1), 15 - (g * LOW_PASSES_PER_CHECK + u))
            state = step(cand, count(lambda kblk: kblk >= cand), state)
        return g + 1, state

    state = (jnp.full((1, tq), -2 ** 15, jnp.int32), jnp.where(adm_end <= top_k, 1, 0), adm_end,
             jnp.zeros((1, tq), jnp.int32))
    t16, done, n_ge, n_gt = lax.fori_loop(0, 16, bisect_hi, state)
    base = jnp.left_shift(t16, 16)
    above = count(lambda kblk: kblk >= base + 1)
    exact = (done == 0) & (above == top_k)
    short = (done == 0) & (above < top_k)
    state = (jnp.where(exact, base + 1, base), jnp.where(above <= top_k, 1, done),
             jnp.where(exact, above, n_ge), jnp.where(short, above, n_gt))
    _, (thr, _, n_ge, n_gt) = lax.while_loop(
        lambda c: (c[0] < 16 // LOW_PASSES_PER_CHECK) & (jnp.min(c[1][1]) == 0), bisect_lo, (jnp.int32(0), state))
    thr = jnp.maximum(thr, INT_MIN + 1)

    @pl.when(jnp.max(n_ge) > top_k)
    def _():
        need = (top_k - n_gt).astype(F32)

        r2 = lax.broadcasted_iota(jnp.int32, (DEMOTE_ROWS, DEMOTE_ROWS), 0)
        c2 = lax.broadcasted_iota(jnp.int32, (DEMOTE_ROWS, DEMOTE_ROWS), 1)
        lower = jnp.where(c2 < r2, 1.0, 0.0).astype(BF16)

        def demote(start, size, seen):
            chunks = []
            for c0 in range(0, size, DEMOTE_ROWS):
                kblk = keys_ref[pl.ds(start + c0, DEMOTE_ROWS), :]
                eqf = jnp.where(kblk == thr, 1.0, 0.0)
                chunks.append((c0, kblk, eqf, jnp.dot(lower, eqf.astype(BF16), preferred_element_type=F32),
                               jnp.sum(eqf, axis=0, keepdims=True)))
            for c0, kblk, eqf, within, total in chunks:
                drop = (eqf > 0.0) & (within + seen >= need)
                keys_ref[pl.ds(start + c0, DEMOTE_ROWS), :] = jnp.where(drop, thr - 1, kblk)
                seen = seen + total
            return seen

        sweep(demote, jnp.zeros((1, tq), F32))

    m_ref[...] = jnp.full_like(m_ref, -jnp.inf)
    acc_ref[...] = jnp.zeros_like(acc_ref)

    def attend(start, size, c):
        bias = jnp.where(keys_ref[pl.ds(start, size), :] >= thr, 0.0, -jnp.inf)
        grp = min(DSA_GROUP, DSA_HEADS * tq)
        bias = jnp.concatenate([bias] * (grp // tq), axis=1)
        kb = dk4_ref[0, pl.ds(start, size), :]
        vt = dvt_ref[0, :, pl.ds(start, size)]
        n_groups = DSA_HEADS * tq // grp

        def scores(g):
            return lax.dot_general(kb, q8_ref[g * grp:(g + 1) * grp, :], _NT,
                                   preferred_element_type=F32) + bias

        ahead = [scores(g) for g in range(min(QK_AHEAD, n_groups))]
        for g in range(n_groups):
            s = ahead.pop(0)
            if g + QK_AHEAD < n_groups:
                ahead.append(scores(g + QK_AHEAD))
            sl = slice(g * grp, (g + 1) * grp)
            m_old = m_ref[:, sl]
            m_new = jnp.maximum(m_old, jnp.max(s, axis=0, keepdims=True))
            m_safe = jnp.where(m_new == -jnp.inf, 0.0, m_new)
            p = jnp.exp2((s - m_safe).astype(BF16))
            alpha = jnp.exp2(m_old - m_safe)
            m_ref[:, sl] = m_new
            acc_ref[:, sl] = alpha * acc_ref[:, sl] + jnp.dot(vt, p, preferred_element_type=F32)
        return c

    sweep(attend, 0)
    out = acc_ref[0:HEAD_DIM, :] / acc_ref[HEAD_DIM:HEAD_DIM + 1, :]
    for h in range(DSA_HEADS):
        o_ref[0, h * HEAD_DIM:(h + 1) * HEAD_DIM, :] = out[:, h * tq:(h + 1) * tq].astype(BF16)


def _dsa_call(dq, iq, wt, dk4, dvt, ik8, hm8, hm4, top_k):
    bsz, s, _ = dq.shape
    tq = min(DSA_Q_TILE, s)

    def tok(w):
        return pl.BlockSpec((1, tq, w), lambda b, i: (b, i, 0))

    def seq(w):
        return pl.BlockSpec((1, s, w), lambda b, i: (b, 0, 0))

    def seqt(r):
        return pl.BlockSpec((1, r, s), lambda b, i: (b, 0, 0))

    return pl.pallas_call(
        functools.partial(_dsa_kernel, top_k=top_k),
        grid=(bsz, s // tq),
        in_specs=[tok(DSA_WIDTH), tok(IDX_WIDTH), seqt(2 * IDX_HEADS), seq(256), seqt(V_ROWS), seq(256),
                  pl.BlockSpec(hm8.shape, lambda b, i: (0, 0)), pl.BlockSpec(hm4.shape, lambda b, i: (0, 0))],
        out_specs=pl.BlockSpec((1, DSA_WIDTH, tq), lambda b, i: (b, 0, i)),
        out_shape=jax.ShapeDtypeStruct((bsz, DSA_WIDTH, s), BF16),
        scratch_shapes=[pltpu.VMEM((IDX_HEADS * tq, 256), BF16), pltpu.VMEM((DSA_HEADS * tq, 256), BF16),
                        pltpu.VMEM((s, tq), jnp.int32), pltpu.VMEM((s, tq), BF16),
                        pltpu.VMEM((1, DSA_HEADS * tq), F32),
                        pltpu.VMEM((V_ROWS, DSA_HEADS * tq), F32)],
        compiler_params=_cparams(("parallel", "arbitrary")),
        name="dsa_attn",
    )(dq, iq, wt, dk4, dvt, ik8, hm8, hm4)


def _merge_kernel(x_ref, mod_ref, g_ref, ya_ref, yb_ref, wg_ref, wfa_ref, wfb_ref, wo_ref, o_ref):
    x = x_ref[0]
    d = x.shape[1]
    h = _modulated(x, mod_ref, g_ref, 1).astype(BF16)
    zg = jnp.dot(h, wg_ref[...], preferred_element_type=F32)
    pa = lax.dot_general(ya_ref[0], wfa_ref[...], _TN, preferred_element_type=F32)
    pb = lax.dot_general(yb_ref[0], wfb_ref[...], _TN, preferred_element_type=F32)
    merged = jax.nn.sigmoid(zg[:, :d]) * pa + jax.nn.sigmoid(zg[:, d:]) * pb
    y = jnp.dot(merged.astype(BF16), wo_ref[...], preferred_element_type=F32)
    o_ref[0] = x + mod_ref[0, 5:6, :] * y


def _merge_call(x, mod, g, ya, yb, wg, wfa, wfb, wo):
    bsz, s, d = x.shape
    tm = min(TOKEN_TILE, s)

    def tok(w):
        return pl.BlockSpec((1, tm, w), lambda b, i: (b, i, 0))

    def tokt(w):
        return pl.BlockSpec((1, w, tm), lambda b, i: (b, 0, i))

    return pl.pallas_call(
        _merge_kernel,
        grid=(bsz, s // tm),
        in_specs=[tok(d),
                  pl.BlockSpec((1, 3 * N_SUBLAYERS, d), lambda b, i: (b, 0, 0)),
                  _const_spec((N_SUBLAYERS, d)),
                  tokt(FOX_WIDTH), tokt(DSA_WIDTH),
                  _const_spec(wg.shape), _const_spec(wfa.shape), _const_spec(wfb.shape), _const_spec(wo.shape)],
        out_specs=tok(d),
        out_shape=jax.ShapeDtypeStruct(x.shape, F32),
        compiler_params=_cparams(("parallel", "parallel")),
        name="merge_out",
    )(x, mod, g, ya, yb, wg, wfa, wfb, wo)


def _head_mask(n_heads, width):
    return jnp.asarray(np.kron(np.eye(n_heads), np.ones((1, width))), BF16)


def _rope_freq_row(rot_dim, period):
    inv_freq = ROPE_THETA ** (-jnp.arange(0, rot_dim, 2, dtype=F32) / rot_dim)
    half = rot_dim // 2
    head = jnp.concatenate([-inv_freq, inv_freq, jnp.zeros((period - 2 * half,), F32)])
    return jnp.tile(head, LANES // period)


def kernel(x, c, positions, ada_w, ada_b, norm_g, ffn1_w1, ffn1_w3, ffn1_w2, w_in, fox_f_bias, fox_qk_g, dsa_qk_g,
           w_br_fox, w_br_dsa, w_out, ffn2_w1, ffn2_w3, ffn2_w2):
    bsz, s, d = x.shape
    top_k = min(TOPK_MAX, s // 4)
    depth = ada_w.shape[0]
    pos = jnp.broadcast_to(positions.astype(F32)[:, :, None], (bsz, s, LANES))
    freq = jnp.stack([_rope_freq_row(HEAD_DIM // ROPE_FRACTION, HEAD_DIM),
                      _rope_freq_row(IDX_DIM // ROPE_FRACTION, IDX_DIM)])
    gmat = jnp.asarray(np.kron(np.eye(FOX_HEADS), np.ones((HEAD_DIM, HEAD_DIM))), BF16)
    hm2 = _head_mask(2, HEAD_DIM)
    hm4 = _head_mask(4, HEAD_DIM)
    hm8 = _head_mask(IDX_HEADS, IDX_DIM)
    fsel = jnp.asarray(np.tile(np.eye(FOX_HEADS), (1, LANES // FOX_HEADS))
                       * (np.arange(LANES) < 3 * FOX_HEADS), BF16).reshape(FOX_HEADS // 2, 2, LANES)
    o_fq, o_fk, o_fv = 0, FOX_WIDTH, 2 * FOX_WIDTH
    o_ff = 3 * FOX_WIDTH
    o_dq = o_ff + FOX_HEADS
    o_dk = o_dq + DSA_WIDTH
    o_dv = o_dk + HEAD_DIM
    o_iq = o_dv + HEAD_DIM
    o_ik = o_iq + IDX_WIDTH
    o_iw = o_ik + IDX_DIM
    o_ga = o_iw + IDX_HEADS

    for l in range(depth):
        mod = _ada_call(c, ada_w[l], ada_b[l]).reshape(bsz, 3 * N_SUBLAYERS, d)
        g = norm_g[l]
        x = _ffn_call(x, mod, g, ffn1_w1[l].astype(BF16), ffn1_w3[l].astype(BF16), ffn1_w2[l].astype(BF16), 0)

        w = w_in[l]
        wa = jnp.concatenate([w[:, o_fq:o_fq + 2 * FOX_WIDTH], w[:, o_dq:o_dq + DSA_WIDTH]], axis=1).astype(BF16)
        pad = jnp.zeros((d, LANES - FOX_HEADS), F32)
        wb = jnp.concatenate([jnp.tile(w[:, o_dk:o_dk + HEAD_DIM], (1, 4)),
                              jnp.tile(w[:, o_ik:o_ik + IDX_DIM], (1, IDX_HEADS)),
                              w[:, o_iq:o_iq + IDX_WIDTH],
                              w[:, o_ff:o_ff + FOX_HEADS], pad], axis=1).astype(BF16)
        zrows = jnp.zeros((FOX_HEADS, V_ROWS - HEAD_DIM, d), F32)
        wfv = jnp.transpose(w[:, o_fv:o_fv + FOX_WIDTH]).reshape(FOX_HEADS, HEAD_DIM, d)
        wt = jnp.concatenate([jnp.concatenate([wfv, zrows], axis=1).reshape(FOX_HEADS * V_ROWS, d),
                              jnp.transpose(w[:, o_dv:o_dv + HEAD_DIM]), zrows[0],
                              jnp.zeros((IDX_HEADS, d), F32), jnp.transpose(w[:, o_iw:o_iw + IDX_HEADS])],
                             axis=0).astype(BF16)
        ones = jnp.asarray(np.tile((np.arange(FOX_HEADS * V_ROWS) % V_ROWS >= HEAD_DIM)[:, None], (1, LANES)), F32)
        wg = w[:, o_ga:o_ga + 2 * d].astype(BF16)
        gains = jnp.stack([jnp.tile(fox_qk_g[l, 0], FOX_HEADS), jnp.tile(fox_qk_g[l, 1], FOX_HEADS),
                           jnp.tile(dsa_qk_g[l, 0], DSA_HEADS), jnp.tile(dsa_qk_g[l, 1], DSA_HEADS)]).astype(F32)
        fbias = jnp.concatenate([fox_f_bias[l].astype(F32), jnp.zeros((LANES - FOX_HEADS,), F32)]).reshape(1, LANES)

        fq, fk, dq, dk4, ik8, iq, faug, fvt, dvt, iwt = _mix_in_call(x, mod, g, pos, wa, wb, wt, ones, gmat, gains,
                                                                       freq, fbias)
        yat = _fox_call(fq, fk, faug, fvt, hm2, fsel)
        ybt = _dsa_call(dq, iq, iwt, dk4, dvt, ik8, hm8, hm4, top_k)
        x = _merge_call(x, mod, g, yat, ybt, wg, w_br_fox[l].astype(BF16), w_br_dsa[l].astype(BF16),
                        w_out[l].astype(BF16))
        x = _ffn_call(x, mod, g, ffn2_w1[l].astype(BF16), ffn2_w3[l].astype(BF16), ffn2_w2[l].astype(BF16), 2)
    return x
```

```python
import functools

import numpy as np
import jax
import jax.numpy as jnp
from jax import lax
from jax.experimental import pallas as pl
from jax.experimental.pallas import tpu as pltpu

D_MODEL = 1024
CHUNK = 64
HEAD_DIM = 64
FOX_HEADS = 8
DSA_HEADS = 8
IDX_HEADS = 8
IDX_DIM = 32
TOPK_MAX = 256
ROPE_THETA = 500000.0
ROPE_FRACTION = 4
D_FF = 2816
N_SUBLAYERS = 3
NORM_EPS = 1e-6
FOX_WIDTH = FOX_HEADS * HEAD_DIM
DSA_WIDTH = DSA_HEADS * HEAD_DIM
IDX_WIDTH = IDX_HEADS * IDX_DIM

LANES = 128
VMEM_LIMIT_BYTES = 56 * 1024 * 1024

TOKEN_TILE = 512
FOX_TILE = 1024
DSA_Q_TILE = 256
DSA_K_TILE = 1024
FOX_GROUP = 512
DSA_GROUP = 1024
QK_AHEAD = 3
LOW_PASSES_PER_CHECK = 2
DEMOTE_ROWS = 128
COUNT_ROWS = 32

FOX_ZERO_EXP = 160.0
V_ROWS = HEAD_DIM + 16
LOG2E = 1.4426950408889634
INT_MIN = -2 ** 31
F32_MIN_NORMAL = 1.1754943508222875e-38
F32 = jnp.float32
BF16 = jnp.bfloat16
_NT = (((1,), (1,)), ((), ()))
_TN = (((0,), (0,)), ((), ()))


def _cparams(sem):
    return pltpu.CompilerParams(dimension_semantics=sem, vmem_limit_bytes=VMEM_LIMIT_BYTES)


def _const_spec(shape):
    nd = len(shape)
    return pl.BlockSpec(shape, lambda *_: (0,) * nd, pipeline_mode=pl.Buffered(1))


def _modulated(x, mod_ref, g_ref, sub):
    ms = jnp.mean(x * x, axis=-1, keepdims=True)
    y = x * lax.rsqrt(ms + NORM_EPS) * g_ref[sub:sub + 1, :]
    shift = mod_ref[0, 3 * sub:3 * sub + 1, :]
    scale = mod_ref[0, 3 * sub + 1:3 * sub + 2, :]
    return y * (1.0 + scale) + shift


def _ada_kernel(c_ref, w_ref, b_ref, o_ref):
    c = c_ref[...]
    cond = c * jax.nn.sigmoid(c)
    o_ref[...] = jnp.dot(cond, w_ref[...], preferred_element_type=F32,
                         precision=lax.Precision.HIGHEST) + b_ref[...]


def _ada_call(c, w, b):
    bsz, d = c.shape
    n = w.shape[1]
    tn = 1152
    return pl.pallas_call(
        _ada_kernel,
        grid=(n // tn,),
        in_specs=[pl.BlockSpec((bsz, d), lambda j: (0, 0)),
                  pl.BlockSpec((d, tn), lambda j: (0, j)),
                  pl.BlockSpec((1, tn), lambda j: (0, j))],
        out_specs=pl.BlockSpec((bsz, tn), lambda j: (0, j)),
        out_shape=jax.ShapeDtypeStruct((bsz, n), F32),
        compiler_params=_cparams(("arbitrary",)),
        name="ada_ln",
    )(c, w, b.reshape(1, n))


def _ffn_kernel(x_ref, mod_ref, g_ref, w1_ref, w3_ref, w2_ref, o_ref, *, sub):
    x = x_ref[0]
    h = _modulated(x, mod_ref, g_ref, sub).astype(BF16)
    a = jnp.dot(h, w1_ref[...], preferred_element_type=F32)
    b = jnp.dot(h, w3_ref[...], preferred_element_type=F32)
    act = (a * jax.nn.sigmoid(a) * b).astype(BF16)
    y = jnp.dot(act, w2_ref[...], preferred_element_type=F32)
    gate = mod_ref[0, 3 * sub + 2:3 * sub + 3, :]
    o_ref[0] = x + (0.5 * gate) * y


def _ffn_call(x, mod, g, w1, w3, w2, sub):
    bsz, s, d = x.shape
    tm = min(TOKEN_TILE, s)
    f = w1.shape[1]
    tok = pl.BlockSpec((1, tm, d), lambda b, i: (b, i, 0))
    return pl.pallas_call(
        functools.partial(_ffn_kernel, sub=sub),
        grid=(bsz, s // tm),
        in_specs=[tok,
                  pl.BlockSpec((1, 3 * N_SUBLAYERS, d), lambda b, i: (b, 0, 0)),
                  _const_spec((N_SUBLAYERS, d)),
                  _const_spec((d, f)), _const_spec((d, f)), _const_spec((f, d))],
        out_specs=tok,
        out_shape=jax.ShapeDtypeStruct(x.shape, F32),
        compiler_params=_cparams(("parallel", "parallel")),
        name=f"ffn{sub}",
    )(x, mod, g, w1, w3, w2)


def _split3(v):
    hi = v.astype(BF16)
    r = v - hi.astype(F32)
    mid = r.astype(BF16)
    lo = (r - mid.astype(F32)).astype(BF16)
    return hi, mid, lo


def _group_norm(z, gmat, gain, group):
    ssq = jnp.dot((z * z).astype(BF16), gmat, preferred_element_type=F32)
    return z * lax.rsqrt(ssq * (1.0 / group) + NORM_EPS) * gain


def _rope(z, cos, sin, half, period):
    width = z.shape[1]
    lane = lax.broadcasted_iota(jnp.int32, z.shape, 1) % period
    up = pltpu.roll(z, width - half, axis=1)
    dn = pltpu.roll(z, half, axis=1)
    return z * cos + jnp.where(lane < half, up, dn) * sin


def _mix_in_kernel(x_ref, mod_ref, g_ref, pos_ref, wa_ref, wb_ref, wt_ref, ones_ref, gmat_ref, gains_ref, freq_ref,
                   fbias_ref, fq_ref, fk_ref, dq_ref, dk4_ref, ik8_ref, iq_ref, faug_ref, fvt_ref, dvt_ref, iwt_ref,
                   carry_ref):
    @pl.when(pl.program_id(1) == 0)
    def _():
        carry_ref[...] = jnp.zeros_like(carry_ref)

    x = x_ref[0]
    tm = x.shape[0]
    h = _modulated(x, mod_ref, g_ref, 1).astype(BF16)
    za = jnp.dot(h, wa_ref[...], preferred_element_type=F32)
    zb = jnp.dot(h, wb_ref[...], preferred_element_type=F32)
    zt = lax.dot_general(wt_ref[...], h, _NT, preferred_element_type=F32)
    n_v = FOX_HEADS * V_ROWS
    fvt_ref[0] = (zt[0:n_v] + jnp.concatenate([ones_ref[...]] * (tm // LANES), axis=1)).astype(BF16)
    dvt_ref[0] = (zt[n_v:n_v + V_ROWS]
                  + jnp.concatenate([ones_ref[0:V_ROWS, :]] * (tm // LANES), axis=1)).astype(BF16)
    iwt_ref[0] = zt[n_v + V_ROWS:, :] * ((IDX_HEADS * IDX_DIM) ** -0.5)

    gmat = gmat_ref[...]
    scale = HEAD_DIM ** -0.5 * LOG2E
    fq = _group_norm(za[:, 0:512], gmat, gains_ref[0:1, :], HEAD_DIM) * scale
    fk = _group_norm(za[:, 512:1024], gmat, gains_ref[1:2, :], HEAD_DIM)
    fq_ref[0] = fq.astype(BF16)
    fk_ref[0] = fk.astype(BF16)

    pos = pos_ref[0]
    ang_a = pos * freq_ref[0:1, :]
    ang_i = pos * freq_ref[1:2, :]
    cos_a, sin_a = jnp.cos(ang_a), jnp.sin(ang_a)
    cos_i, sin_i = jnp.cos(ang_i), jnp.sin(ang_i)
    rot_a = HEAD_DIM // ROPE_FRACTION // 2
    rot_i = IDX_DIM // ROPE_FRACTION // 2

    dq = _group_norm(za[:, 1024:1536], gmat, gains_ref[2:3, :], HEAD_DIM)
    dq = _rope(dq, jnp.concatenate([cos_a] * 4, axis=1), jnp.concatenate([sin_a] * 4, axis=1), rot_a, HEAD_DIM)
    dq_ref[0] = (dq * scale).astype(BF16)

    dk4 = _group_norm(zb[:, 0:256], gmat[0:256, 0:256], gains_ref[3:4, 0:256], HEAD_DIM)
    dk4 = _rope(dk4, jnp.concatenate([cos_a] * 2, axis=1), jnp.concatenate([sin_a] * 2, axis=1), rot_a, HEAD_DIM)
    dk4_ref[0] = dk4.astype(BF16)

    cos_i2 = jnp.concatenate([cos_i] * 2, axis=1)
    sin_i2 = jnp.concatenate([sin_i] * 2, axis=1)
    ik8_ref[0] = _rope(zb[:, 256:512], cos_i2, sin_i2, rot_i, IDX_DIM).astype(BF16)
    iq_ref[0] = _rope(zb[:, 512:768], cos_i2, sin_i2, rot_i, IDX_DIM).astype(BF16)

    sm = zb[:, 768:896]
    v = sm + fbias_ref[...]
    logf = jnp.minimum(v, 0.0) - jnp.log1p(jnp.exp(-jnp.abs(v)))
    row = lax.broadcasted_iota(jnp.int32, (tm, tm), 0)
    col = lax.broadcasted_iota(jnp.int32, (tm, tm), 1)
    tri = jnp.where(row >= col, 1.0, 0.0).astype(BF16)
    hi, mid, lo = _split3(logf)
    csum = (jnp.dot(tri, hi, preferred_element_type=F32) + jnp.dot(tri, mid, preferred_element_type=F32)
            + jnp.dot(tri, lo, preferred_element_type=F32)) + carry_ref[...]
    carry_ref[...] = csum[tm - 1:tm, :]
    lane = lax.broadcasted_iota(jnp.int32, sm.shape, 1)

    nf = -csum * LOG2E
    t0 = nf.astype(BF16).astype(F32)
    r1 = nf - t0
    t1 = r1.astype(BF16).astype(F32)
    t2 = r1 - t1
    aug = jnp.where(lane < FOX_HEADS, t0,
                    jnp.where(lane < 2 * FOX_HEADS, pltpu.roll(t1, FOX_HEADS, axis=1),
                              jnp.where(lane < 3 * FOX_HEADS, pltpu.roll(t2, 2 * FOX_HEADS, axis=1), 0.0)))
    faug_ref[0] = aug.astype(BF16)


def _mix_in_call(x, mod, g, pos, wa, wb, wt, ones, gmat, gains, freq, fbias):
    bsz, s, d = x.shape
    tm = min(TOKEN_TILE, s)

    def tok(w):
        return pl.BlockSpec((1, tm, w), lambda b, i: (b, i, 0))

    def tokt(r):
        return pl.BlockSpec((1, r, tm), lambda b, i: (b, 0, i))

    widths = (512, 512, 512, 256, 256, 256, 128)
    rows = ((FOX_HEADS * V_ROWS, BF16), (V_ROWS, BF16), (2 * IDX_HEADS, F32))
    out_shapes = [jax.ShapeDtypeStruct((bsz, s, w), BF16) for w in widths]
    out_shapes += [jax.ShapeDtypeStruct((bsz, r, s), dt) for r, dt in rows]
    return pl.pallas_call(
        _mix_in_kernel,
        grid=(bsz, s // tm),
        in_specs=[tok(d),
                  pl.BlockSpec((1, 3 * N_SUBLAYERS, d), lambda b, i: (b, 0, 0)),
                  _const_spec((N_SUBLAYERS, d)),
                  tok(LANES),
                  _const_spec(wa.shape), _const_spec(wb.shape), _const_spec(wt.shape), _const_spec(ones.shape),
                  _const_spec(gmat.shape), _const_spec(gains.shape), _const_spec(freq.shape), _const_spec(fbias.shape)],
        out_specs=[tok(w) for w in widths] + [tokt(r) for r, _ in rows],
        out_shape=out_shapes,
        scratch_shapes=[pltpu.VMEM((1, LANES), F32)],
        compiler_params=_cparams(("arbitrary", "arbitrary")),
        name="mix_in",
    )(x, mod, g, pos, wa, wb, wt, ones, gmat, gains, freq, fbias)


def _fox_kernel(q_ref, k_ref, faug_ref, vt_ref, hm_ref, fsel_ref, bound_ref, o_ref, qf_ref, m_ref, acc_ref):
    i = pl.program_id(2)
    tq = q_ref.shape[1]
    tk = tq
    q = q_ref[0]
    for a in range(2):
        sel_row = fsel_ref[0, a:a + 1, :]
        qf_ref[a * tq:(a + 1) * tq, 0:LANES] = q * hm_ref[a:a + 1, :]
        qf_ref[a * tq:(a + 1) * tq, LANES:2 * LANES] = jnp.broadcast_to(sel_row, (tq, LANES))
    m_ref[...] = jnp.full_like(m_ref, -jnp.inf)
    acc_ref[...] = jnp.zeros_like(acc_ref)

    def block(j, masked):
        start = pl.multiple_of(j * tk, tk)
        kf = jnp.concatenate([k_ref[0, pl.ds(start, tk), :], faug_ref[0, pl.ds(start, tk), :]], axis=1)
        grp = min(FOX_GROUP, tq)
        n_groups = 2 * tq // grp

        def rows(g):
            return min(tk, (g * grp) % tq + grp) if masked else tk

        def scores(g):
            s = lax.dot_general(kf[:rows(g)], qf_ref[g * grp:(g + 1) * grp, :], _NT,
                                preferred_element_type=F32)
            if masked:
                krow = lax.broadcasted_iota(jnp.int32, s.shape, 0)
                qcol = lax.broadcasted_iota(jnp.int32, s.shape, 1) + (g * grp) % tq
                s = jnp.where(krow <= qcol, s, -jnp.inf)
            return s

        ahead = [scores(g) for g in range(min(QK_AHEAD, n_groups))]
        for g in range(n_groups):
            s = ahead.pop(0)
            if g + QK_AHEAD < n_groups:
                ahead.append(scores(g + QK_AHEAD))
            sl = slice(g * grp, (g + 1) * grp)
            m_old = m_ref[:, sl]
            m_new = jnp.maximum(m_old, jnp.max(s, axis=0, keepdims=True))
            p = jnp.exp2((s - m_new).astype(BF16))
            alpha = jnp.exp2(m_old - m_new)
            m_ref[:, sl] = m_new
            a = (g * grp) // tq
            vt = vt_ref[0, a * V_ROWS:(a + 1) * V_ROWS, pl.ds(start, rows(g))]
            acc_ref[:, sl] = alpha * acc_ref[:, sl] + jnp.dot(vt, p, preferred_element_type=F32)

    block(i, True)
    bound = jnp.max(bound_ref[...])

    def older(jj, c):
        j = i - 1 - jj
        last = pl.multiple_of(j * tk + tk - 16, 16)
        tail_rows = faug_ref[0, pl.ds(last, 16), :].astype(F32)
        worst = -jnp.inf
        for a in range(2):
            nf_end = jnp.max(jnp.sum(tail_rows * fsel_ref[0, a:a + 1, :].astype(F32), axis=1, keepdims=True))
            worst = jnp.maximum(worst, nf_end + bound - jnp.min(m_ref[:, a * tq:(a + 1) * tq]))

        @pl.when(worst > -FOX_ZERO_EXP)
        def _():
            block(j, False)
        return c

    lax.fori_loop(0, i, older, 0)
    out = acc_ref[0:HEAD_DIM, :] / acc_ref[HEAD_DIM:HEAD_DIM + 1, :]
    for a in range(2):
        o_ref[0, a * HEAD_DIM:(a + 1) * HEAD_DIM, :] = out[:, a * tq:(a + 1) * tq].astype(BF16)


def _fox_call(fq, fk, faug, fvt, hm2, fsel, bound):
    bsz, s, _ = fq.shape
    tq = min(FOX_TILE, s)
    pairs = FOX_HEADS // 2
    return pl.pallas_call(
        _fox_kernel,
        grid=(bsz, pairs, s // tq),
        in_specs=[pl.BlockSpec((1, tq, LANES), lambda b, h, i: (b, i, h)),
                  pl.BlockSpec((1, s, LANES), lambda b, h, i: (b, 0, h)),
                  pl.BlockSpec((1, s, LANES), lambda b, h, i: (b, 0, 0)),
                  pl.BlockSpec((1, 2 * V_ROWS, s), lambda b, h, i: (b, h, 0)),
                  pl.BlockSpec((2, LANES), lambda b, h, i: (0, 0)),
                  pl.BlockSpec((1, 2, LANES), lambda b, h, i: (h, 0, 0)),
                  pl.BlockSpec((1, LANES), lambda b, h, i: (0, 0))],
        out_specs=pl.BlockSpec((1, LANES, tq), lambda b, h, i: (b, h, i)),
        out_shape=jax.ShapeDtypeStruct((bsz, FOX_WIDTH, s), BF16),
        scratch_shapes=[pltpu.VMEM((2 * tq, 2 * LANES), BF16),
                        pltpu.VMEM((1, 2 * tq), F32),
                        pltpu.VMEM((V_ROWS, 2 * tq), F32)],
        compiler_params=_cparams(("parallel", "parallel", "arbitrary")),
        name="fox_attn",
    )(fq, fk, faug, fvt, hm2, fsel, bound)


def _dsa_kernel(dq_ref, iq_ref, wt_ref, dk4_ref, dvt_ref, ik8_ref, hm8_ref, hm4_ref, o_ref,
                iq8_ref, q8_ref, keys_ref, hi_ref, m_ref, acc_ref, *, top_k):
    i = pl.program_id(1)
    tq = dq_ref.shape[1]
    s_len = dk4_ref.shape[1]
    tk = min(DSA_K_TILE, s_len)
    half_tk = tk // 2
    q_start = i * tq
    rem = (q_start + tq) % tk
    n_full = (q_start + tq) // tk + jnp.where(rem > half_tk, 1, 0)
    has_tail = (rem > 0) & (rem <= half_tk)

    def sweep(fn, init):
        carry = lax.fori_loop(0, n_full, lambda j, c: fn(pl.multiple_of(j * tk, tk), tk, c), init)
        return lax.cond(has_tail, lambda c: fn(pl.multiple_of(n_full * tk, tk), half_tk, c), lambda c: c, carry)

    def half_sweep(fn, init):
        n_half = (q_start + tq + half_tk - 1) // half_tk
        return lax.fori_loop(0, n_half, lambda j, c: fn(pl.multiple_of(j * half_tk, half_tk), half_tk, c), init)

    iq = iq_ref[0]
    dq = dq_ref[0]
    half = dq.shape[1] // 2
    for h in range(IDX_HEADS):
        iq8_ref[h * tq:(h + 1) * tq, :] = iq * hm8_ref[h:h + 1, :]
        q8_ref[h * tq:(h + 1) * tq, :] = dq[:, (h // 4) * half:(h // 4 + 1) * half] * hm4_ref[h % 4:h % 4 + 1, :]

    adm_end = q_start + (lax.broadcasted_iota(jnp.int32, (1, tq), 1) // CHUNK + 1) * CHUNK
    wt = wt_ref[0, :, pl.ds(pl.multiple_of(q_start, tq), tq)]

    def score_block(start, size, c):
        d = lax.dot_general(ik8_ref[0, pl.ds(start, size), :], iq8_ref[...], _NT, preferred_element_type=F32)
        sc = jnp.zeros((size, tq), F32)
        for h in range(IDX_HEADS):
            sc = sc + jnp.maximum(d[:, h * tq:(h + 1) * tq], 0.0) * wt[IDX_HEADS + h:IDX_HEADS + h + 1, :]
        sc = jnp.where(jnp.abs(sc) < F32_MIN_NORMAL, 0.0, sc)
        bits = lax.bitcast_convert_type(sc, jnp.int32)
        key = bits ^ ((bits >> 31) & 0x7FFFFFFF)
        adm = lax.broadcasted_iota(jnp.int32, (size, tq), 0) + start < adm_end
        keys_ref[pl.ds(start, size), :] = jnp.where(adm, key, INT_MIN)
        top = lax.bitcast_convert_type(bits & -65536, F32)
        hi_ref[pl.ds(start, size), :] = jnp.where(adm, top, jnp.nan).astype(BF16)
        return c

    sweep(score_block, 0)

    def count(pred):
        def body(start, size, acc):
            hit = jnp.where(pred(keys_ref[pl.ds(start, size), :]), 1, 0)
            return acc + jnp.sum(hit.reshape(size // COUNT_ROWS, COUNT_ROWS, tq), axis=0)
        return jnp.sum(half_sweep(body, jnp.zeros((COUNT_ROWS, tq), jnp.int32)), axis=0, keepdims=True)

    def count_hi(cand):
        def body(start, size, acc):
            hit = jnp.where(hi_ref[pl.ds(start, size), :] >= cand, jnp.ones((), BF16), jnp.zeros((), BF16))
            part = hit[0:COUNT_ROWS]
            for r in range(1, size // COUNT_ROWS):
                part = part + hit[r * COUNT_ROWS:(r + 1) * COUNT_ROWS]
            return acc + part.astype(F32)
        return jnp.sum(half_sweep(body, jnp.zeros((COUNT_ROWS, tq), F32)), axis=0, keepdims=True)

    def step(cand, cnt, state):
        thr, done, n_ge, n_gt = state
        live = done == 0
        ok = (cnt >= top_k) & live
        return (jnp.where(ok, cand, thr), jnp.where(ok & (cnt == top_k), 1, done),
                jnp.where(ok, cnt, n_ge), jnp.where((cnt < top_k) & live, cnt, n_gt))

    def bisect_hi(it, state):
        cand = state[0] + jnp.left_shift(jnp.int32(1), 15 - it)
        pattern = jnp.where(cand >= 0, cand, cand ^ 0x7FFF)
        pattern = jnp.where((pattern & 0x7F80) == 0, jnp.where(cand > 0, 0x0080, 0), pattern)
        cand_f = lax.bitcast_convert_type(jnp.left_shift(pattern, 16), F32).astype(BF16)
        return step(cand, count_hi(cand_f).astype(jnp.int32), state)

    def bisect_lo(carry):
        g, state = carry
        for u in range(LOW_PASSES_PER_CHECK):
            cand = state[0] + jnp.left_shift(jnp.int32(1), 15 - (g * LOW_PASSES_PER_CHECK + u))
            state = step(cand, count(lambda kblk: kblk >= cand), state)
        return g + 1, state

    state = (jnp.full((1, tq), -2 ** 15, jnp.int32), jnp.where(adm_end <= top_k, 1, 0), adm_end,
             jnp.zeros((1, tq), jnp.int32))
    t16, done, n_ge, n_gt = lax.fori_loop(0, 16, bisect_hi, state)
    base = jnp.left_shift(t16, 16)
    above = count(lambda kblk: kblk >= base + 1)
    exact = (done == 0) & (above == top_k)
    short = (done == 0) & (above < top_k)
    state = (jnp.where(exact, base + 1, base), jnp.where(above <= top_k, 1, done),
             jnp.where(exact, above, n_ge), jnp.where(short, above, n_gt))
    _, (thr, _, n_ge, n_gt) = lax.while_loop(
        lambda c: (c[0] < 16 // LOW_PASSES_PER_CHECK) & (jnp.min(c[1][1]) == 0), bisect_lo, (jnp.int32(0), state))
    thr = jnp.maximum(thr, INT_MIN + 1)

    @pl.when(jnp.max(n_ge) > top_k)
    def _():
        need = (top_k - n_gt).astype(F32)

        r2 = lax.broadcasted_iota(jnp.int32, (DEMOTE_ROWS, DEMOTE_ROWS), 0)
        c2 = lax.broadcasted_iota(jnp.int32, (DEMOTE_ROWS, DEMOTE_ROWS), 1)
        lower = jnp.where(c2 < r2, 1.0, 0.0).astype(BF16)

        def demote(start, size, seen):
            chunks = []
            for c0 in range(0, size, DEMOTE_ROWS):
                kblk = keys_ref[pl.ds(start + c0, DEMOTE_ROWS), :]
                eqf = jnp.where(kblk == thr, 1.0, 0.0)
                chunks.append((c0, kblk, eqf, jnp.dot(lower, eqf.astype(BF16), preferred_element_type=F32),
                               jnp.sum(eqf, axis=0, keepdims=True)))
            for c0, kblk, eqf, within, total in chunks:
                drop = (eqf > 0.0) & (within + seen >= need)
                keys_ref[pl.ds(start + c0, DEMOTE_ROWS), :] = jnp.where(drop, thr - 1, kblk)
                seen = seen + total
            return seen

        sweep(demote, jnp.zeros((1, tq), F32))

    m_ref[...] = jnp.full_like(m_ref, -jnp.inf)
    acc_ref[...] = jnp.zeros_like(acc_ref)

    def attend(start, size, c):
        bias = jnp.where(keys_ref[pl.ds(start, size), :] >= thr, 0.0, -jnp.inf)
        grp = min(DSA_GROUP, DSA_HEADS * tq)
        bias = jnp.concatenate([bias] * (grp // tq), axis=1)
        kb = dk4_ref[0, pl.ds(start, size), :]
        vt = dvt_ref[0, :, pl.ds(start, size)]
        n_groups = DSA_HEADS * tq // grp

        def scores(g):
            return lax.dot_general(kb, q8_ref[g * grp:(g + 1) * grp, :], _NT,
                                   preferred_element_type=F32) + bias

        ahead = [scores(g) for g in range(min(QK_AHEAD, n_groups))]
        for g in range(n_groups):
            s = ahead.pop(0)
            if g + QK_AHEAD < n_groups:
                ahead.append(scores(g + QK_AHEAD))
            sl = slice(g * grp, (g + 1) * grp)
            m_old = m_ref[:, sl]
            m_new = jnp.maximum(m_old, jnp.max(s, axis=0, keepdims=True))
            m_safe = jnp.where(m_new == -jnp.inf, 0.0, m_new)
            p = jnp.exp2((s - m_safe).astype(BF16))
            alpha = jnp.exp2(m_old - m_safe)
            m_ref[:, sl] = m_new
            acc_ref[:, sl] = alpha * acc_ref[:, sl] + jnp.dot(vt, p, preferred_element_type=F32)
        return c

    sweep(attend, 0)
    out = acc_ref[0:HEAD_DIM, :] / acc_ref[HEAD_DIM:HEAD_DIM + 1, :]
    for h in range(DSA_HEADS):
        o_ref[0, h * HEAD_DIM:(h + 1) * HEAD_DIM, :] = out[:, h * tq:(h + 1) * tq].astype(BF16)


def _dsa_call(dq, iq, wt, dk4, dvt, ik8, hm8, hm4, top_k):
    bsz, s, _ = dq.shape
    tq = min(DSA_Q_TILE, s)

    def tok(w):
        return pl.BlockSpec((1, tq, w), lambda b, i: (b, i, 0))

    def seq(w):
        return pl.BlockSpec((1, s, w), lambda b, i: (b, 0, 0))

    def seqt(r):
        return pl.BlockSpec((1, r, s), lambda b, i: (b, 0, 0))

    return pl.pallas_call(
        functools.partial(_dsa_kernel, top_k=top_k),
        grid=(bsz, s // tq),
        in_specs=[tok(DSA_WIDTH), tok(IDX_WIDTH), seqt(2 * IDX_HEADS), seq(256), seqt(V_ROWS), seq(256),
                  pl.BlockSpec(hm8.shape, lambda b, i: (0, 0)), pl.BlockSpec(hm4.shape, lambda b, i: (0, 0))],
        out_specs=pl.BlockSpec((1, DSA_WIDTH, tq), lambda b, i: (b, 0, i)),
        out_shape=jax.ShapeDtypeStruct((bsz, DSA_WIDTH, s), BF16),
        scratch_shapes=[pltpu.VMEM((IDX_HEADS * tq, 256), BF16), pltpu.VMEM((DSA_HEADS * tq, 256), BF16),
                        pltpu.VMEM((s, tq), jnp.int32), pltpu.VMEM((s, tq), BF16),
                        pltpu.VMEM((1, DSA_HEADS * tq), F32),
                        pltpu.VMEM((V_ROWS, DSA_HEADS * tq), F32)],
        compiler_params=_cparams(("parallel", "arbitrary")),
        name="dsa_attn",
    )(dq, iq, wt, dk4, dvt, ik8, hm8, hm4)


def _merge_kernel(x_ref, mod_ref, g_ref, ya_ref, yb_ref, wg_ref, wfa_ref, wfb_ref, wo_ref, o_ref):
    x = x_ref[0]
    d = x.shape[1]
    h = _modulated(x, mod_ref, g_ref, 1).astype(BF16)
    zg = jnp.dot(h, wg_ref[...], preferred_element_type=F32)
    pa = lax.dot_general(ya_ref[0], wfa_ref[...], _TN, preferred_element_type=F32)
    pb = lax.dot_general(yb_ref[0], wfb_ref[...], _TN, preferred_element_type=F32)
    merged = jax.nn.sigmoid(zg[:, :d]) * pa + jax.nn.sigmoid(zg[:, d:]) * pb
    y = jnp.dot(merged.astype(BF16), wo_ref[...], preferred_element_type=F32)
    o_ref[0] = x + mod_ref[0, 5:6, :] * y


def _merge_call(x, mod, g, ya, yb, wg, wfa, wfb, wo):
    bsz, s, d = x.shape
    tm = min(TOKEN_TILE, s)

    def tok(w):
        return pl.BlockSpec((1, tm, w), lambda b, i: (b, i, 0))

    def tokt(w):
        return pl.BlockSpec((1, w, tm), lambda b, i: (b, 0, i))

    return pl.pallas_call(
        _merge_kernel,
        grid=(bsz, s // tm),
        in_specs=[tok(d),
                  pl.BlockSpec((1, 3 * N_SUBLAYERS, d), lambda b, i: (b, 0, 0)),
                  _const_spec((N_SUBLAYERS, d)),
                  tokt(FOX_WIDTH), tokt(DSA_WIDTH),
                  _const_spec(wg.shape), _const_spec(wfa.shape), _const_spec(wfb.shape), _const_spec(wo.shape)],
        out_specs=tok(d),
        out_shape=jax.ShapeDtypeStruct(x.shape, F32),
        compiler_params=_cparams(("parallel", "parallel")),
        name="merge_out",
    )(x, mod, g, ya, yb, wg, wfa, wfb, wo)


def _head_mask(n_heads, width):
    return jnp.asarray(np.kron(np.eye(n_heads), np.ones((1, width))), BF16)


def _rope_freq_row(rot_dim, period):
    inv_freq = ROPE_THETA ** (-jnp.arange(0, rot_dim, 2, dtype=F32) / rot_dim)
    half = rot_dim // 2
    head = jnp.concatenate([-inv_freq, inv_freq, jnp.zeros((period - 2 * half,), F32)])
    return jnp.tile(head, LANES // period)


def kernel(x, c, positions, ada_w, ada_b, norm_g, ffn1_w1, ffn1_w3, ffn1_w2, w_in, fox_f_bias, fox_qk_g, dsa_qk_g,
           w_br_fox, w_br_dsa, w_out, ffn2_w1, ffn2_w3, ffn2_w2):
    bsz, s, d = x.shape
    top_k = min(TOPK_MAX, s // 4)
    depth = ada_w.shape[0]
    pos = jnp.broadcast_to(positions.astype(F32)[:, :, None], (bsz, s, LANES))
    freq = jnp.stack([_rope_freq_row(HEAD_DIM // ROPE_FRACTION, HEAD_DIM),
                      _rope_freq_row(IDX_DIM // ROPE_FRACTION, IDX_DIM)])
    gmat = jnp.asarray(np.kron(np.eye(FOX_HEADS), np.ones((HEAD_DIM, HEAD_DIM))), BF16)
    hm2 = _head_mask(2, HEAD_DIM)
    hm4 = _head_mask(4, HEAD_DIM)
    hm8 = _head_mask(IDX_HEADS, IDX_DIM)
    fsel = jnp.asarray(np.tile(np.eye(FOX_HEADS), (1, LANES // FOX_HEADS))
                       * (np.arange(LANES) < 3 * FOX_HEADS), BF16).reshape(FOX_HEADS // 2, 2, LANES)
    o_fq, o_fk, o_fv = 0, FOX_WIDTH, 2 * FOX_WIDTH
    o_ff = 3 * FOX_WIDTH
    o_dq = o_ff + FOX_HEADS
    o_dk = o_dq + DSA_WIDTH
    o_dv = o_dk + HEAD_DIM
    o_iq = o_dv + HEAD_DIM
    o_ik = o_iq + IDX_WIDTH
    o_iw = o_ik + IDX_DIM
    o_ga = o_iw + IDX_HEADS

    for l in range(depth):
        mod = _ada_call(c, ada_w[l], ada_b[l]).reshape(bsz, 3 * N_SUBLAYERS, d)
        g = norm_g[l]
        x = _ffn_call(x, mod, g, ffn1_w1[l].astype(BF16), ffn1_w3[l].astype(BF16), ffn1_w2[l].astype(BF16), 0)

        w = w_in[l]
        wa = jnp.concatenate([w[:, o_fq:o_fq + 2 * FOX_WIDTH], w[:, o_dq:o_dq + DSA_WIDTH]], axis=1).astype(BF16)
        pad = jnp.zeros((d, LANES - FOX_HEADS), F32)
        wb = jnp.concatenate([jnp.tile(w[:, o_dk:o_dk + HEAD_DIM], (1, 4)),
                              jnp.tile(w[:, o_ik:o_ik + IDX_DIM], (1, IDX_HEADS)),
                              w[:, o_iq:o_iq + IDX_WIDTH],
                              w[:, o_ff:o_ff + FOX_HEADS], pad], axis=1).astype(BF16)
        zrows = jnp.zeros((FOX_HEADS, V_ROWS - HEAD_DIM, d), F32)
        wfv = jnp.transpose(w[:, o_fv:o_fv + FOX_WIDTH]).reshape(FOX_HEADS, HEAD_DIM, d)
        wt = jnp.concatenate([jnp.concatenate([wfv, zrows], axis=1).reshape(FOX_HEADS * V_ROWS, d),
                              jnp.transpose(w[:, o_dv:o_dv + HEAD_DIM]), zrows[0],
                              jnp.zeros((IDX_HEADS, d), F32), jnp.transpose(w[:, o_iw:o_iw + IDX_HEADS])],
                             axis=0).astype(BF16)
        ones = jnp.asarray(np.tile((np.arange(FOX_HEADS * V_ROWS) % V_ROWS >= HEAD_DIM)[:, None], (1, LANES)), F32)
        wg = w[:, o_ga:o_ga + 2 * d].astype(BF16)
        gains = jnp.stack([jnp.tile(fox_qk_g[l, 0], FOX_HEADS), jnp.tile(fox_qk_g[l, 1], FOX_HEADS),
                           jnp.tile(dsa_qk_g[l, 0], DSA_HEADS), jnp.tile(dsa_qk_g[l, 1], DSA_HEADS)]).astype(F32)
        fbias = jnp.concatenate([fox_f_bias[l].astype(F32), jnp.zeros((LANES - FOX_HEADS,), F32)]).reshape(1, LANES)

        fq, fk, dq, dk4, ik8, iq, faug, fvt, dvt, iwt = _mix_in_call(x, mod, g, pos, wa, wb, wt, ones, gmat, gains,
                                                                       freq, fbias)
        bound = jnp.full((1, LANES), HEAD_DIM ** 0.5 * LOG2E * 1.02, F32) * (jnp.max(jnp.abs(fox_qk_g[l, 0]))
                                                                          * jnp.max(jnp.abs(fox_qk_g[l, 1])))
        yat = _fox_call(fq, fk, faug, fvt, hm2, fsel, bound)
        ybt = _dsa_call(dq, iq, iwt, dk4, dvt, ik8, hm8, hm4, top_k)
        x = _merge_call(x, mod, g, yat, ybt, wg, w_br_fox[l].astype(BF16), w_br_dsa[l].astype(BF16),
                        w_out[l].astype(BF16))
        x = _ffn_call(x, mod, g, ffn2_w1[l].astype(BF16), ffn2_w3[l].astype(BF16), ffn2_w2[l].astype(BF16), 2)
    return x
```

```python
import functools

import numpy as np
import jax
import jax.numpy as jnp
from jax import lax
from jax.experimental import pallas as pl
from jax.experimental.pallas import tpu as pltpu

CHUNK = 64
HEAD_DIM = 64
FOX_HEADS = 8
DSA_HEADS = 8
IDX_HEADS = 8
IDX_DIM = 32
TOPK_MAX = 256
ROPE_THETA = 500000.0
ROPE_FRACTION = 4
N_SUBLAYERS = 3
NORM_EPS = 1e-6
FOX_WIDTH = FOX_HEADS * HEAD_DIM
DSA_WIDTH = DSA_HEADS * HEAD_DIM
IDX_WIDTH = IDX_HEADS * IDX_DIM
DK_TILED = 4 * HEAD_DIM

LANES = 128
VMEM_LIMIT_BYTES = 56 * 1024 * 1024

ADA_COLS = 1152
TOKEN_TILE = 512
FOX_TILE = 1024
DSA_Q_TILE = 256
DSA_K_TILE = 1024
FOX_GROUP = 512
DSA_GROUP = 1024
QK_AHEAD = 3
LOW_PASSES_PER_CHECK = 2
DEMOTE_ROWS = 128
COUNT_ROWS = 32

DSA_MAX_FIXED_SHIFT = 60.0
FOX_ZERO_EXP = 160.0
V_ROWS = HEAD_DIM + 16
LOG2E = 1.4426950408889634
INT_MIN = -2 ** 31
F32_MAGNITUDE_BITS = 0x7FFFFFFF
TOP16_MASK = -65536
BF16_MAGNITUDE_BITS = 0x7FFF
BF16_EXPONENT_BITS = 0x7F80
BF16_MIN_NORMAL_BITS = 0x0080
NORM_SLACK = 1.02
F32_MIN_NORMAL = 1.1754943508222875e-38
F32 = jnp.float32
BF16 = jnp.bfloat16
_NT = (((1,), (1,)), ((), ()))
_TN = (((0,), (0,)), ((), ()))


def _cparams(sem):
    return pltpu.CompilerParams(dimension_semantics=sem, vmem_limit_bytes=VMEM_LIMIT_BYTES)


def _const_spec(shape):
    nd = len(shape)
    return pl.BlockSpec(shape, lambda *_: (0,) * nd, pipeline_mode=pl.Buffered(1))


def _modulated(x, mod_ref, g_ref, sub):
    ms = jnp.mean(x * x, axis=-1, keepdims=True)
    y = x * lax.rsqrt(ms + NORM_EPS) * g_ref[sub:sub + 1, :]
    shift = mod_ref[0, 3 * sub:3 * sub + 1, :]
    scale = mod_ref[0, 3 * sub + 1:3 * sub + 2, :]
    return y * (1.0 + scale) + shift


def _ada_kernel(c_ref, w_ref, b_ref, o_ref):
    c = c_ref[...]
    cond = c * jax.nn.sigmoid(c)
    o_ref[...] = jnp.dot(cond, w_ref[...], preferred_element_type=F32,
                         precision=lax.Precision.HIGHEST) + b_ref[...]


def _ada_call(c, w, b):
    bsz, d = c.shape
    n = w.shape[1]
    tn = ADA_COLS
    return pl.pallas_call(
        _ada_kernel,
        grid=(n // tn,),
        in_specs=[pl.BlockSpec((bsz, d), lambda j: (0, 0)),
                  pl.BlockSpec((d, tn), lambda j: (0, j)),
                  pl.BlockSpec((1, tn), lambda j: (0, j))],
        out_specs=pl.BlockSpec((bsz, tn), lambda j: (0, j)),
        out_shape=jax.ShapeDtypeStruct((bsz, n), F32),
        compiler_params=_cparams(("arbitrary",)),
        name="ada_ln",
    )(c, w, b.reshape(1, n))


def _ffn_kernel(x_ref, mod_ref, g_ref, w1_ref, w3_ref, w2_ref, o_ref, *, sub):
    x = x_ref[0]
    h = _modulated(x, mod_ref, g_ref, sub).astype(BF16)
    a = jnp.dot(h, w1_ref[...], preferred_element_type=F32)
    b = jnp.dot(h, w3_ref[...], preferred_element_type=F32)
    act = (a * jax.nn.sigmoid(a) * b).astype(BF16)
    y = jnp.dot(act, w2_ref[...], preferred_element_type=F32)
    gate = mod_ref[0, 3 * sub + 2:3 * sub + 3, :]
    o_ref[0] = x + (0.5 * gate) * y


def _ffn_call(x, mod, g, w1, w3, w2, sub):
    bsz, s, d = x.shape
    tm = min(TOKEN_TILE, s)
    f = w1.shape[1]
    tok = pl.BlockSpec((1, tm, d), lambda b, i: (b, i, 0))
    return pl.pallas_call(
        functools.partial(_ffn_kernel, sub=sub),
        grid=(bsz, s // tm),
        in_specs=[tok,
                  pl.BlockSpec((1, 3 * N_SUBLAYERS, d), lambda b, i: (b, 0, 0)),
                  _const_spec((N_SUBLAYERS, d)),
                  _const_spec((d, f)), _const_spec((d, f)), _const_spec((f, d))],
        out_specs=tok,
        out_shape=jax.ShapeDtypeStruct(x.shape, F32),
        compiler_params=_cparams(("parallel", "parallel")),
        name=f"ffn{sub}",
    )(x, mod, g, w1, w3, w2)


def _split3(v):
    hi = v.astype(BF16)
    r = v - hi.astype(F32)
    mid = r.astype(BF16)
    lo = (r - mid.astype(F32)).astype(BF16)
    return hi, mid, lo


def _group_norm(z, gmat, gain, group):
    ssq = jnp.dot((z * z).astype(BF16), gmat, preferred_element_type=F32)
    return z * lax.rsqrt(ssq * (1.0 / group) + NORM_EPS) * gain


def _rope(z, cos, sin, half, period):
    width = z.shape[1]
    lane = lax.broadcasted_iota(jnp.int32, z.shape, 1) % period
    up = pltpu.roll(z, width - half, axis=1)
    dn = pltpu.roll(z, half, axis=1)
    return z * cos + jnp.where(lane < half, up, dn) * sin


def _mix_in_kernel(x_ref, mod_ref, g_ref, pos_ref, wa_ref, wb_ref, wt_ref, ones_ref, gmat_ref, gains_ref, freq_ref,
                   fbias_ref, fq_ref, fk_ref, dq_ref, dk4_ref, ik8_ref, iq_ref, faug_ref, fvt_ref, dvt_ref, iwt_ref,
                   carry_ref):
    @pl.when(pl.program_id(1) == 0)
    def _():
        carry_ref[...] = jnp.zeros_like(carry_ref)

    x = x_ref[0]
    tm = x.shape[0]
    h = _modulated(x, mod_ref, g_ref, 1).astype(BF16)
    za = jnp.dot(h, wa_ref[...], preferred_element_type=F32)
    zb = jnp.dot(h, wb_ref[...], preferred_element_type=F32)
    zt = lax.dot_general(wt_ref[...], h, _NT, preferred_element_type=F32)
    n_v = FOX_HEADS * V_ROWS
    fvt_ref[0] = (zt[0:n_v] + jnp.concatenate([ones_ref[...]] * (tm // LANES), axis=1)).astype(BF16)
    dvt_ref[0] = (zt[n_v:n_v + V_ROWS]
                  + jnp.concatenate([ones_ref[0:V_ROWS, :]] * (tm // LANES), axis=1)).astype(BF16)
    iwt_ref[0] = zt[n_v + V_ROWS:, :] * ((IDX_HEADS * IDX_DIM) ** -0.5)

    gmat = gmat_ref[...]
    scale = HEAD_DIM ** -0.5 * LOG2E
    fq = _group_norm(za[:, 0:FOX_WIDTH], gmat, gains_ref[0:1, :], HEAD_DIM) * scale
    fk = _group_norm(za[:, FOX_WIDTH:2 * FOX_WIDTH], gmat, gains_ref[1:2, :], HEAD_DIM)
    fq_ref[0] = fq.astype(BF16)
    fk_ref[0] = fk.astype(BF16)

    pos = pos_ref[0]
    ang_a = pos * freq_ref[0:1, :]
    ang_i = pos * freq_ref[1:2, :]
    cos_a, sin_a = jnp.cos(ang_a), jnp.sin(ang_a)
    cos_i, sin_i = jnp.cos(ang_i), jnp.sin(ang_i)
    rot_a = HEAD_DIM // ROPE_FRACTION // 2
    rot_i = IDX_DIM // ROPE_FRACTION // 2

    dq = _group_norm(za[:, 2 * FOX_WIDTH:2 * FOX_WIDTH + DSA_WIDTH], gmat, gains_ref[2:3, :], HEAD_DIM)
    dq = _rope(dq, jnp.concatenate([cos_a] * 4, axis=1), jnp.concatenate([sin_a] * 4, axis=1), rot_a, HEAD_DIM)
    dq_ref[0] = (dq * scale).astype(BF16)

    dk4 = _group_norm(zb[:, 0:DK_TILED], gmat[0:DK_TILED, 0:DK_TILED], gains_ref[3:4, 0:DK_TILED], HEAD_DIM)
    dk4 = _rope(dk4, jnp.concatenate([cos_a] * 2, axis=1), jnp.concatenate([sin_a] * 2, axis=1), rot_a, HEAD_DIM)
    dk4_ref[0] = dk4.astype(BF16)

    cos_i2 = jnp.concatenate([cos_i] * 2, axis=1)
    sin_i2 = jnp.concatenate([sin_i] * 2, axis=1)
    o_ik, o_iq, o_sm = DK_TILED, DK_TILED + IDX_WIDTH, DK_TILED + 2 * IDX_WIDTH
    ik8_ref[0] = _rope(zb[:, o_ik:o_iq], cos_i2, sin_i2, rot_i, IDX_DIM).astype(BF16)
    iq_ref[0] = _rope(zb[:, o_iq:o_sm], cos_i2, sin_i2, rot_i, IDX_DIM).astype(BF16)

    sm = zb[:, o_sm:o_sm + LANES]
    v = sm + fbias_ref[...]
    logf = jnp.minimum(v, 0.0) - jnp.log1p(jnp.exp(-jnp.abs(v)))
    row = lax.broadcasted_iota(jnp.int32, (tm, tm), 0)
    col = lax.broadcasted_iota(jnp.int32, (tm, tm), 1)
    tri = jnp.where(row >= col, 1.0, 0.0).astype(BF16)
    hi, mid, lo = _split3(logf)
    csum = (jnp.dot(tri, hi, preferred_element_type=F32) + jnp.dot(tri, mid, preferred_element_type=F32)
            + jnp.dot(tri, lo, preferred_element_type=F32)) + carry_ref[...]
    carry_ref[...] = csum[tm - 1:tm, :]
    lane = lax.broadcasted_iota(jnp.int32, sm.shape, 1)

    nf = -csum * LOG2E
    t0 = nf.astype(BF16).astype(F32)
    r1 = nf - t0
    t1 = r1.astype(BF16).astype(F32)
    t2 = r1 - t1
    aug = jnp.where(lane < FOX_HEADS, t0,
                    jnp.where(lane < 2 * FOX_HEADS, pltpu.roll(t1, FOX_HEADS, axis=1),
                              jnp.where(lane < 3 * FOX_HEADS, pltpu.roll(t2, 2 * FOX_HEADS, axis=1), 0.0)))
    faug_ref[0] = aug.astype(BF16)


def _mix_in_call(x, mod, g, pos, wa, wb, wt, ones, gmat, gains, freq, fbias):
    bsz, s, d = x.shape
    tm = min(TOKEN_TILE, s)

    def tok(w):
        return pl.BlockSpec((1, tm, w), lambda b, i: (b, i, 0))

    def tokt(r):
        return pl.BlockSpec((1, r, tm), lambda b, i: (b, 0, i))

    widths = (FOX_WIDTH, FOX_WIDTH, DSA_WIDTH, DK_TILED, IDX_WIDTH, IDX_WIDTH, LANES)
    rows = ((FOX_HEADS * V_ROWS, BF16), (V_ROWS, BF16), (2 * IDX_HEADS, F32))
    out_shapes = [jax.ShapeDtypeStruct((bsz, s, w), BF16) for w in widths]
    out_shapes += [jax.ShapeDtypeStruct((bsz, r, s), dt) for r, dt in rows]
    return pl.pallas_call(
        _mix_in_kernel,
        grid=(bsz, s // tm),
        in_specs=[tok(d),
                  pl.BlockSpec((1, 3 * N_SUBLAYERS, d), lambda b, i: (b, 0, 0)),
                  _const_spec((N_SUBLAYERS, d)),
                  tok(LANES),
                  _const_spec(wa.shape), _const_spec(wb.shape), _const_spec(wt.shape), _const_spec(ones.shape),
                  _const_spec(gmat.shape), _const_spec(gains.shape), _const_spec(freq.shape), _const_spec(fbias.shape)],
        out_specs=[tok(w) for w in widths] + [tokt(r) for r, _ in rows],
        out_shape=out_shapes,
        scratch_shapes=[pltpu.VMEM((1, LANES), F32)],
        compiler_params=_cparams(("arbitrary", "arbitrary")),
        name="mix_in",
    )(x, mod, g, pos, wa, wb, wt, ones, gmat, gains, freq, fbias)


def _fox_kernel(q_ref, k_ref, faug_ref, vt_ref, hm_ref, fsel_ref, bound_ref, o_ref, qf_ref, m_ref, acc_ref):
    i = pl.program_id(2)
    tq = q_ref.shape[1]
    tk = tq
    q = q_ref[0]
    for a in range(2):
        sel_row = fsel_ref[0, a:a + 1, :]
        qf_ref[a * tq:(a + 1) * tq, 0:LANES] = q * hm_ref[a:a + 1, :]
        qf_ref[a * tq:(a + 1) * tq, LANES:2 * LANES] = jnp.broadcast_to(sel_row, (tq, LANES))
    m_ref[...] = jnp.full_like(m_ref, -jnp.inf)
    acc_ref[...] = jnp.zeros_like(acc_ref)

    def block(j, masked):
        start = pl.multiple_of(j * tk, tk)
        kf = jnp.concatenate([k_ref[0, pl.ds(start, tk), :], faug_ref[0, pl.ds(start, tk), :]], axis=1)
        grp = min(FOX_GROUP, tq)
        n_groups = 2 * tq // grp

        def rows(g):
            return min(tk, (g * grp) % tq + grp) if masked else tk

        def scores(g):
            s = lax.dot_general(kf[:rows(g)], qf_ref[g * grp:(g + 1) * grp, :], _NT,
                                preferred_element_type=F32)
            if masked:
                krow = lax.broadcasted_iota(jnp.int32, s.shape, 0)
                qcol = lax.broadcasted_iota(jnp.int32, s.shape, 1) + (g * grp) % tq
                s = jnp.where(krow <= qcol, s, -jnp.inf)
            return s

        ahead = [scores(g) for g in range(min(QK_AHEAD, n_groups))]
        for g in range(n_groups):
            s = ahead.pop(0)
            if g + QK_AHEAD < n_groups:
                ahead.append(scores(g + QK_AHEAD))
            sl = slice(g * grp, (g + 1) * grp)
            m_old = m_ref[:, sl]
            m_new = jnp.maximum(m_old, jnp.max(s, axis=0, keepdims=True))
            p = jnp.exp2((s - m_new).astype(BF16))
            alpha = jnp.exp2(m_old - m_new)
            m_ref[:, sl] = m_new
            a = (g * grp) // tq
            vt = vt_ref[0, a * V_ROWS:(a + 1) * V_ROWS, pl.ds(start, rows(g))]
            acc_ref[:, sl] = alpha * acc_ref[:, sl] + jnp.dot(vt, p, preferred_element_type=F32)

    block(i, True)
    bound = jnp.max(bound_ref[...])

    def older(jj, c):
        j = i - 1 - jj
        last = pl.multiple_of(j * tk + tk - 16, 16)
        tail_rows = faug_ref[0, pl.ds(last, 16), :].astype(F32)
        worst = -jnp.inf
        for a in range(2):
            nf_end = jnp.max(jnp.sum(tail_rows * fsel_ref[0, a:a + 1, :].astype(F32), axis=1, keepdims=True))
            worst = jnp.maximum(worst, nf_end + bound - jnp.min(m_ref[:, a * tq:(a + 1) * tq]))

        @pl.when(worst > -FOX_ZERO_EXP)
        def _():
            block(j, False)
        return c

    lax.fori_loop(0, i, older, 0)
    out = acc_ref[0:HEAD_DIM, :] / acc_ref[HEAD_DIM:HEAD_DIM + 1, :]
    for a in range(2):
        o_ref[0, a * HEAD_DIM:(a + 1) * HEAD_DIM, :] = out[:, a * tq:(a + 1) * tq].astype(BF16)


def _fox_call(fq, fk, faug, fvt, hm2, fsel, bound):
    bsz, s, _ = fq.shape
    tq = min(FOX_TILE, s)
    pairs = FOX_HEADS // 2
    return pl.pallas_call(
        _fox_kernel,
        grid=(bsz, pairs, s // tq),
        in_specs=[pl.BlockSpec((1, tq, LANES), lambda b, h, i: (b, i, h)),
                  pl.BlockSpec((1, s, LANES), lambda b, h, i: (b, 0, h)),
                  pl.BlockSpec((1, s, LANES), lambda b, h, i: (b, 0, 0)),
                  pl.BlockSpec((1, 2 * V_ROWS, s), lambda b, h, i: (b, h, 0)),
                  pl.BlockSpec((2, LANES), lambda b, h, i: (0, 0)),
                  pl.BlockSpec((1, 2, LANES), lambda b, h, i: (h, 0, 0)),
                  pl.BlockSpec((1, LANES), lambda b, h, i: (0, 0))],
        out_specs=pl.BlockSpec((1, LANES, tq), lambda b, h, i: (b, h, i)),
        out_shape=jax.ShapeDtypeStruct((bsz, FOX_WIDTH, s), BF16),
        scratch_shapes=[pltpu.VMEM((2 * tq, 2 * LANES), BF16),
                        pltpu.VMEM((1, 2 * tq), F32),
                        pltpu.VMEM((V_ROWS, 2 * tq), F32)],
        compiler_params=_cparams(("parallel", "parallel", "arbitrary")),
        name="fox_attn",
    )(fq, fk, faug, fvt, hm2, fsel, bound)


def _dsa_kernel(dq_ref, iq_ref, wt_ref, dk4_ref, dvt_ref, ik8_ref, hm8_ref, hm4_ref, bound_ref, o_ref,
                iq8_ref, q8_ref, keys_ref, hi_ref, m_ref, acc_ref, *, top_k):
    i = pl.program_id(1)
    tq = dq_ref.shape[1]
    s_len = dk4_ref.shape[1]
    tk = min(DSA_K_TILE, s_len)
    half_tk = tk // 2
    q_start = i * tq
    rem = (q_start + tq) % tk
    n_full = (q_start + tq) // tk + jnp.where(rem > half_tk, 1, 0)
    has_tail = (rem > 0) & (rem <= half_tk)

    def sweep(fn, init):
        carry = lax.fori_loop(0, n_full, lambda j, c: fn(pl.multiple_of(j * tk, tk), tk, c), init)
        return lax.cond(has_tail, lambda c: fn(pl.multiple_of(n_full * tk, tk), half_tk, c), lambda c: c, carry)

    iq = iq_ref[0]
    dq = dq_ref[0]
    half = dq.shape[1] // 2
    for h in range(IDX_HEADS):
        iq8_ref[h * tq:(h + 1) * tq, :] = iq * hm8_ref[h:h + 1, :]
        q8_ref[h * tq:(h + 1) * tq, :] = dq[:, (h // 4) * half:(h // 4 + 1) * half] * hm4_ref[h % 4:h % 4 + 1, :]

    adm_end = q_start + (lax.broadcasted_iota(jnp.int32, (1, tq), 1) // CHUNK + 1) * CHUNK
    wt = wt_ref[0, :, pl.ds(pl.multiple_of(q_start, tq), tq)]

    def score_block(start, size, c):
        d = lax.dot_general(ik8_ref[0, pl.ds(start, size), :], iq8_ref[...], _NT, preferred_element_type=F32)
        sc = jnp.zeros((size, tq), F32)
        for h in range(IDX_HEADS):
            sc = sc + jnp.maximum(d[:, h * tq:(h + 1) * tq], 0.0) * wt[IDX_HEADS + h:IDX_HEADS + h + 1, :]
        sc = jnp.where(jnp.abs(sc) < F32_MIN_NORMAL, 0.0, sc)
        bits = lax.bitcast_convert_type(sc, jnp.int32)
        key = bits ^ ((bits >> 31) & F32_MAGNITUDE_BITS)
        adm = lax.broadcasted_iota(jnp.int32, (size, tq), 0) + start < adm_end
        keys_ref[pl.ds(start, size), :] = jnp.where(adm, key, INT_MIN)
        top = lax.bitcast_convert_type(bits & TOP16_MASK, F32)
        hi_ref[pl.ds(start, size), :] = jnp.where(adm, top, jnp.nan).astype(BF16)
        return c

    sweep(score_block, 0)

    def count(pred):
        def body(start, size, acc):
            hit = jnp.where(pred(keys_ref[pl.ds(start, size), :]), 1, 0)
            return acc + jnp.sum(hit.reshape(size // COUNT_ROWS, COUNT_ROWS, tq), axis=0)
        return jnp.sum(sweep(body, jnp.zeros((COUNT_ROWS, tq), jnp.int32)), axis=0, keepdims=True)

    def count_hi(cand):
        def body(start, size, acc):
            hit = jnp.where(hi_ref[pl.ds(start, size), :] >= cand, jnp.ones((), BF16), jnp.zeros((), BF16))
            part = hit[0:COUNT_ROWS]
            for r in range(1, size // COUNT_ROWS):
                part = part + hit[r * COUNT_ROWS:(r + 1) * COUNT_ROWS]
            return acc + part.astype(F32)
        return jnp.sum(sweep(body, jnp.zeros((COUNT_ROWS, tq), F32)), axis=0, keepdims=True)

    def step(cand, cnt, state):
        thr, done, n_ge, n_gt = state
        live = done == 0
        ok = (cnt >= top_k) & live
        return (jnp.where(ok, cand, thr), jnp.where(ok & (cnt == top_k), 1, done),
                jnp.where(ok, cnt, n_ge), jnp.where((cnt < top_k) & live, cnt, n_gt))

    def bisect_hi(it, state):
        cand = state[0] + jnp.left_shift(jnp.int32(1), 15 - it)
        pattern = jnp.where(cand >= 0, cand, cand ^ BF16_MAGNITUDE_BITS)
        pattern = jnp.where((pattern & BF16_EXPONENT_BITS) == 0, jnp.where(cand > 0, BF16_MIN_NORMAL_BITS, 0), pattern)
        cand_f = lax.bitcast_convert_type(jnp.left_shift(pattern, 16), F32).astype(BF16)
        return step(cand, count_hi(cand_f).astype(jnp.int32), state)

    def bisect_lo(carry):
        g, state = carry
        for u in range(LOW_PASSES_PER_CHECK):
            cand = state[0] + jnp.left_shift(jnp.int32(1), 15 - (g * LOW_PASSES_PER_CHECK + u))
            state = step(cand, count(lambda kblk: kblk >= cand), state)
        return g + 1, state

    state = (jnp.full((1, tq), -2 ** 15, jnp.int32), jnp.where(adm_end <= top_k, 1, 0), adm_end,
             jnp.zeros((1, tq), jnp.int32))
    t16, done, n_ge, n_gt = lax.fori_loop(0, 16, bisect_hi, state)
    base = jnp.left_shift(t16, 16)
    above = count(lambda kblk: kblk >= base + 1)
    exact = (done == 0) & (above == top_k)
    short = (done == 0) & (above < top_k)
    state = (jnp.where(exact, base + 1, base), jnp.where(above <= top_k, 1, done),
             jnp.where(exact, above, n_ge), jnp.where(short, above, n_gt))
    _, (thr, _, n_ge, n_gt) = lax.while_loop(
        lambda c: (c[0] < 16 // LOW_PASSES_PER_CHECK) & (jnp.min(c[1][1]) == 0), bisect_lo, (jnp.int32(0), state))
    thr = jnp.maximum(thr, INT_MIN + 1)

    @pl.when(jnp.max(n_ge) > top_k)
    def _():
        need = (top_k - n_gt).astype(F32)

        r2 = lax.broadcasted_iota(jnp.int32, (DEMOTE_ROWS, DEMOTE_ROWS), 0)
        c2 = lax.broadcasted_iota(jnp.int32, (DEMOTE_ROWS, DEMOTE_ROWS), 1)
        lower = jnp.where(c2 < r2, 1.0, 0.0).astype(BF16)

        def demote(start, size, seen):
            chunks = []
            for c0 in range(0, size, DEMOTE_ROWS):
                kblk = keys_ref[pl.ds(start + c0, DEMOTE_ROWS), :]
                eqf = jnp.where(kblk == thr, 1.0, 0.0)
                chunks.append((c0, kblk, eqf, jnp.dot(lower, eqf.astype(BF16), preferred_element_type=F32),
                               jnp.sum(eqf, axis=0, keepdims=True)))
            for c0, kblk, eqf, within, total in chunks:
                drop = (eqf > 0.0) & (within + seen >= need)
                keys_ref[pl.ds(start + c0, DEMOTE_ROWS), :] = jnp.where(drop, thr - 1, kblk)
                seen = seen + total
            return seen

        sweep(demote, jnp.zeros((1, tq), F32))

    acc_ref[...] = jnp.zeros_like(acc_ref)
    grp = min(DSA_GROUP, DSA_HEADS * tq)
    n_groups = DSA_HEADS * tq // grp
    bound = jnp.max(bound_ref[...])

    def attend_bounded(start, size, c):
        bias = jnp.where(keys_ref[pl.ds(start, size), :] >= thr, -bound, -jnp.inf)
        bias = jnp.concatenate([bias] * (grp // tq), axis=1)
        kb = dk4_ref[0, pl.ds(start, size), :]
        vt = dvt_ref[0, :, pl.ds(start, size)]
        for g in range(n_groups):
            sl = slice(g * grp, (g + 1) * grp)
            s = lax.dot_general(kb, q8_ref[sl, :], _NT, preferred_element_type=F32) + bias
            acc_ref[:, sl] += jnp.dot(vt, jnp.exp2(s).astype(BF16), preferred_element_type=F32)
        return c

    def attend(start, size, c):
        bias = jnp.where(keys_ref[pl.ds(start, size), :] >= thr, 0.0, -jnp.inf)
        bias = jnp.concatenate([bias] * (grp // tq), axis=1)
        kb = dk4_ref[0, pl.ds(start, size), :]
        vt = dvt_ref[0, :, pl.ds(start, size)]

        def scores(g):
            return lax.dot_general(kb, q8_ref[g * grp:(g + 1) * grp, :], _NT,
                                   preferred_element_type=F32) + bias

        ahead = [scores(g) for g in range(min(QK_AHEAD, n_groups))]
        for g in range(n_groups):
            s = ahead.pop(0)
            if g + QK_AHEAD < n_groups:
                ahead.append(scores(g + QK_AHEAD))
            sl = slice(g * grp, (g + 1) * grp)
            m_old = m_ref[:, sl]
            m_new = jnp.maximum(m_old, jnp.max(s, axis=0, keepdims=True))
            m_safe = jnp.where(m_new == -jnp.inf, 0.0, m_new)
            p = jnp.exp2((s - m_safe).astype(BF16))
            alpha = jnp.exp2(m_old - m_safe)
            m_ref[:, sl] = m_new
            acc_ref[:, sl] = alpha * acc_ref[:, sl] + jnp.dot(vt, p, preferred_element_type=F32)
        return c

    @pl.when(bound <= DSA_MAX_FIXED_SHIFT)
    def _():
        sweep(attend_bounded, 0)

    @pl.when(bound > DSA_MAX_FIXED_SHIFT)
    def _():
        m_ref[...] = jnp.full_like(m_ref, -jnp.inf)
        sweep(attend, 0)

    out = acc_ref[0:HEAD_DIM, :] / acc_ref[HEAD_DIM:HEAD_DIM + 1, :]
    for h in range(DSA_HEADS):
        o_ref[0, h * HEAD_DIM:(h + 1) * HEAD_DIM, :] = out[:, h * tq:(h + 1) * tq].astype(BF16)


def _dsa_call(dq, iq, wt, dk4, dvt, ik8, hm8, hm4, bound, top_k):
    bsz, s, _ = dq.shape
    tq = min(DSA_Q_TILE, s)

    def tok(w):
        return pl.BlockSpec((1, tq, w), lambda b, i: (b, i, 0))

    def seq(w):
        return pl.BlockSpec((1, s, w), lambda b, i: (b, 0, 0))

    def seqt(r):
        return pl.BlockSpec((1, r, s), lambda b, i: (b, 0, 0))

    return pl.pallas_call(
        functools.partial(_dsa_kernel, top_k=top_k),
        grid=(bsz, s // tq),
        in_specs=[tok(DSA_WIDTH), tok(IDX_WIDTH), seqt(2 * IDX_HEADS), seq(DK_TILED), seqt(V_ROWS), seq(IDX_WIDTH),
                  pl.BlockSpec(hm8.shape, lambda b, i: (0, 0)), pl.BlockSpec(hm4.shape, lambda b, i: (0, 0)),
                  pl.BlockSpec((1, LANES), lambda b, i: (0, 0))],
        out_specs=pl.BlockSpec((1, DSA_WIDTH, tq), lambda b, i: (b, 0, i)),
        out_shape=jax.ShapeDtypeStruct((bsz, DSA_WIDTH, s), BF16),
        scratch_shapes=[pltpu.VMEM((IDX_HEADS * tq, IDX_WIDTH), BF16), pltpu.VMEM((DSA_HEADS * tq, DK_TILED), BF16),
                        pltpu.VMEM((s, tq), jnp.int32), pltpu.VMEM((s, tq), BF16),
                        pltpu.VMEM((1, DSA_HEADS * tq), F32),
                        pltpu.VMEM((V_ROWS, DSA_HEADS * tq), F32)],
        compiler_params=_cparams(("parallel", "arbitrary")),
        name="dsa_attn",
    )(dq, iq, wt, dk4, dvt, ik8, hm8, hm4, bound)


def _merge_kernel(x_ref, mod_ref, g_ref, ya_ref, yb_ref, wg_ref, wfa_ref, wfb_ref, wo_ref, o_ref):
    x = x_ref[0]
    d = x.shape[1]
    h = _modulated(x, mod_ref, g_ref, 1).astype(BF16)
    zg = jnp.dot(h, wg_ref[...], preferred_element_type=F32)
    pa = lax.dot_general(ya_ref[0], wfa_ref[...], _TN, preferred_element_type=F32)
    pb = lax.dot_general(yb_ref[0], wfb_ref[...], _TN, preferred_element_type=F32)
    merged = jax.nn.sigmoid(zg[:, :d]) * pa + jax.nn.sigmoid(zg[:, d:]) * pb
    y = jnp.dot(merged.astype(BF16), wo_ref[...], preferred_element_type=F32)
    o_ref[0] = x + mod_ref[0, 5:6, :] * y


def _merge_call(x, mod, g, ya, yb, wg, wfa, wfb, wo):
    bsz, s, d = x.shape
    tm = min(TOKEN_TILE, s)

    def tok(w):
        return pl.BlockSpec((1, tm, w), lambda b, i: (b, i, 0))

    def tokt(w):
        return pl.BlockSpec((1, w, tm), lambda b, i: (b, 0, i))

    return pl.pallas_call(
        _merge_kernel,
        grid=(bsz, s // tm),
        in_specs=[tok(d),
                  pl.BlockSpec((1, 3 * N_SUBLAYERS, d), lambda b, i: (b, 0, 0)),
                  _const_spec((N_SUBLAYERS, d)),
                  tokt(FOX_WIDTH), tokt(DSA_WIDTH),
                  _const_spec(wg.shape), _const_spec(wfa.shape), _const_spec(wfb.shape), _const_spec(wo.shape)],
        out_specs=tok(d),
        out_shape=jax.ShapeDtypeStruct(x.shape, F32),
        compiler_params=_cparams(("parallel", "parallel")),
        name="merge_out",
    )(x, mod, g, ya, yb, wg, wfa, wfb, wo)


def _head_mask(n_heads, width):
    return jnp.asarray(np.kron(np.eye(n_heads), np.ones((1, width))), BF16)


def _rope_freq_row(rot_dim, period):
    inv_freq = ROPE_THETA ** (-jnp.arange(0, rot_dim, 2, dtype=F32) / rot_dim)
    half = rot_dim // 2
    head = jnp.concatenate([-inv_freq, inv_freq, jnp.zeros((period - 2 * half,), F32)])
    return jnp.tile(head, LANES // period)


def kernel(x, c, positions, ada_w, ada_b, norm_g, ffn1_w1, ffn1_w3, ffn1_w2, w_in, fox_f_bias, fox_qk_g, dsa_qk_g,
           w_br_fox, w_br_dsa, w_out, ffn2_w1, ffn2_w3, ffn2_w2):
    bsz, s, d = x.shape
    top_k = min(TOPK_MAX, s // 4)
    depth = ada_w.shape[0]
    pos = jnp.broadcast_to(positions.astype(F32)[:, :, None], (bsz, s, LANES))
    freq = jnp.stack([_rope_freq_row(HEAD_DIM // ROPE_FRACTION, HEAD_DIM),
                      _rope_freq_row(IDX_DIM // ROPE_FRACTION, IDX_DIM)])
    gmat = jnp.asarray(np.kron(np.eye(FOX_HEADS), np.ones((HEAD_DIM, HEAD_DIM))), BF16)
    hm2 = _head_mask(2, HEAD_DIM)
    hm4 = _head_mask(4, HEAD_DIM)
    hm8 = _head_mask(IDX_HEADS, IDX_DIM)
    fsel = jnp.asarray(np.tile(np.eye(FOX_HEADS), (1, LANES // FOX_HEADS))
                       * (np.arange(LANES) < 3 * FOX_HEADS), BF16).reshape(FOX_HEADS // 2, 2, LANES)
    o_fq, o_fk, o_fv = 0, FOX_WIDTH, 2 * FOX_WIDTH
    o_ff = 3 * FOX_WIDTH
    o_dq = o_ff + FOX_HEADS
    o_dk = o_dq + DSA_WIDTH
    o_dv = o_dk + HEAD_DIM
    o_iq = o_dv + HEAD_DIM
    o_ik = o_iq + IDX_WIDTH
    o_iw = o_ik + IDX_DIM
    o_ga = o_iw + IDX_HEADS

    for l in range(depth):
        mod = _ada_call(c, ada_w[l], ada_b[l]).reshape(bsz, 3 * N_SUBLAYERS, d)
        g = norm_g[l]
        x = _ffn_call(x, mod, g, ffn1_w1[l].astype(BF16), ffn1_w3[l].astype(BF16), ffn1_w2[l].astype(BF16), 0)

        w = w_in[l]
        wa = jnp.concatenate([w[:, o_fq:o_fq + 2 * FOX_WIDTH], w[:, o_dq:o_dq + DSA_WIDTH]], axis=1).astype(BF16)
        pad = jnp.zeros((d, LANES - FOX_HEADS), F32)
        wb = jnp.concatenate([jnp.tile(w[:, o_dk:o_dk + HEAD_DIM], (1, 4)),
                              jnp.tile(w[:, o_ik:o_ik + IDX_DIM], (1, IDX_HEADS)),
                              w[:, o_iq:o_iq + IDX_WIDTH],
                              w[:, o_ff:o_ff + FOX_HEADS], pad], axis=1).astype(BF16)
        zrows = jnp.zeros((FOX_HEADS, V_ROWS - HEAD_DIM, d), F32)
        wfv = jnp.transpose(w[:, o_fv:o_fv + FOX_WIDTH]).reshape(FOX_HEADS, HEAD_DIM, d)
        wt = jnp.concatenate([jnp.concatenate([wfv, zrows], axis=1).reshape(FOX_HEADS * V_ROWS, d),
                              jnp.transpose(w[:, o_dv:o_dv + HEAD_DIM]), zrows[0],
                              jnp.zeros((IDX_HEADS, d), F32), jnp.transpose(w[:, o_iw:o_iw + IDX_HEADS])],
                             axis=0).astype(BF16)
        ones = jnp.asarray(np.tile((np.arange(FOX_HEADS * V_ROWS) % V_ROWS >= HEAD_DIM)[:, None], (1, LANES)), F32)
        wg = w[:, o_ga:o_ga + 2 * d].astype(BF16)
        gains = jnp.stack([jnp.tile(fox_qk_g[l, 0], FOX_HEADS), jnp.tile(fox_qk_g[l, 1], FOX_HEADS),
                           jnp.tile(dsa_qk_g[l, 0], DSA_HEADS), jnp.tile(dsa_qk_g[l, 1], DSA_HEADS)]).astype(F32)
        fbias = jnp.concatenate([fox_f_bias[l].astype(F32), jnp.zeros((LANES - FOX_HEADS,), F32)]).reshape(1, LANES)

        fq, fk, dq, dk4, ik8, iq, faug, fvt, dvt, iwt = _mix_in_call(x, mod, g, pos, wa, wb, wt, ones, gmat, gains,
                                                                       freq, fbias)
        unit = jnp.full((1, LANES), HEAD_DIM ** 0.5 * LOG2E * NORM_SLACK, F32)
        fox_bound = unit * (jnp.max(jnp.abs(fox_qk_g[l, 0])) * jnp.max(jnp.abs(fox_qk_g[l, 1])))
        dsa_bound = unit * (jnp.max(jnp.abs(dsa_qk_g[l, 0])) * jnp.max(jnp.abs(dsa_qk_g[l, 1])))
        yat = _fox_call(fq, fk, faug, fvt, hm2, fsel, fox_bound)
        ybt = _dsa_call(dq, iq, iwt, dk4, dvt, ik8, hm8, hm4, dsa_bound, top_k)
        x = _merge_call(x, mod, g, yat, ybt, wg, w_br_fox[l].astype(BF16), w_br_dsa[l].astype(BF16),
                        w_out[l].astype(BF16))
        x = _ffn_call(x, mod, g, ffn2_w1[l].astype(BF16), ffn2_w3[l].astype(BF16), ffn2_w2[l].astype(BF16), 2)
    return x
```

```python
import functools

import numpy as np
import jax
import jax.numpy as jnp
from jax import lax
from jax.experimental import pallas as pl
from jax.experimental.pallas import tpu as pltpu

CHUNK = 64
HEAD_DIM = 64
FOX_HEADS = 8
DSA_HEADS = 8
IDX_HEADS = 8
IDX_DIM = 32
TOPK_MAX = 256
ROPE_THETA = 500000.0
ROPE_FRACTION = 4
N_SUBLAYERS = 3
NORM_EPS = 1e-6
FOX_WIDTH = FOX_HEADS * HEAD_DIM
DSA_WIDTH = DSA_HEADS * HEAD_DIM
IDX_WIDTH = IDX_HEADS * IDX_DIM
DK_TILED = 4 * HEAD_DIM

LANES = 128
VMEM_LIMIT_BYTES = 56 * 1024 * 1024

ADA_COLS = 1152
TOKEN_TILE = 512
FOX_TILE = 1024
DSA_Q_TILE = 256
DSA_K_TILE = 1024
FOX_GROUP = 512
DSA_GROUP = 1024
QK_AHEAD = 3
LOW_PASSES_PER_CHECK = 2
DEMOTE_ROWS = 128
COUNT_ROWS = 32

MAX_FIXED_SHIFT = 60.0
FOX_ZERO_EXP = 160.0
V_ROWS = HEAD_DIM + 16
LOG2E = 1.4426950408889634
INT_MIN = -2 ** 31
F32_MAGNITUDE_BITS = 0x7FFFFFFF
TOP16_MASK = -65536
BF16_MAGNITUDE_BITS = 0x7FFF
BF16_EXPONENT_BITS = 0x7F80
BF16_MIN_NORMAL_BITS = 0x0080
NORM_SLACK = 1.02
F32_MIN_NORMAL = 1.1754943508222875e-38
F32 = jnp.float32
BF16 = jnp.bfloat16
_NT = (((1,), (1,)), ((), ()))
_TN = (((0,), (0,)), ((), ()))


def _cparams(sem):
    return pltpu.CompilerParams(dimension_semantics=sem, vmem_limit_bytes=VMEM_LIMIT_BYTES)


def _const_spec(shape):
    nd = len(shape)
    return pl.BlockSpec(shape, lambda *_: (0,) * nd, pipeline_mode=pl.Buffered(1))


def _modulated(x, mod_ref, g_ref, sub):
    ms = jnp.mean(x * x, axis=-1, keepdims=True)
    y = x * lax.rsqrt(ms + NORM_EPS) * g_ref[sub:sub + 1, :]
    shift = mod_ref[0, 3 * sub:3 * sub + 1, :]
    scale = mod_ref[0, 3 * sub + 1:3 * sub + 2, :]
    return y * (1.0 + scale) + shift


def _ada_kernel(c_ref, w_ref, b_ref, o_ref):
    c = c_ref[...]
    cond = c * jax.nn.sigmoid(c)
    o_ref[...] = jnp.dot(cond, w_ref[...], preferred_element_type=F32,
                         precision=lax.Precision.HIGHEST) + b_ref[...]


def _ada_call(c, w, b):
    bsz, d = c.shape
    n = w.shape[1]
    tn = ADA_COLS
    return pl.pallas_call(
        _ada_kernel,
        grid=(n // tn,),
        in_specs=[pl.BlockSpec((bsz, d), lambda j: (0, 0)),
                  pl.BlockSpec((d, tn), lambda j: (0, j)),
                  pl.BlockSpec((1, tn), lambda j: (0, j))],
        out_specs=pl.BlockSpec((bsz, tn), lambda j: (0, j)),
        out_shape=jax.ShapeDtypeStruct((bsz, n), F32),
        compiler_params=_cparams(("arbitrary",)),
        name="ada_ln",
    )(c, w, b.reshape(1, n))


def _ffn_kernel(x_ref, mod_ref, g_ref, w1_ref, w3_ref, w2_ref, o_ref, *, sub):
    x = x_ref[0]
    h = _modulated(x, mod_ref, g_ref, sub).astype(BF16)
    a = jnp.dot(h, w1_ref[...], preferred_element_type=F32)
    b = jnp.dot(h, w3_ref[...], preferred_element_type=F32)
    act = (a * jax.nn.sigmoid(a) * b).astype(BF16)
    y = jnp.dot(act, w2_ref[...], preferred_element_type=F32)
    gate = mod_ref[0, 3 * sub + 2:3 * sub + 3, :]
    o_ref[0] = x + (0.5 * gate) * y


def _ffn_call(x, mod, g, w1, w3, w2, sub):
    bsz, s, d = x.shape
    tm = min(TOKEN_TILE, s)
    f = w1.shape[1]
    tok = pl.BlockSpec((1, tm, d), lambda b, i: (b, i, 0))
    return pl.pallas_call(
        functools.partial(_ffn_kernel, sub=sub),
        grid=(bsz, s // tm),
        in_specs=[tok,
                  pl.BlockSpec((1, 3 * N_SUBLAYERS, d), lambda b, i: (b, 0, 0)),
                  _const_spec((N_SUBLAYERS, d)),
                  _const_spec((d, f)), _const_spec((d, f)), _const_spec((f, d))],
        out_specs=tok,
        out_shape=jax.ShapeDtypeStruct(x.shape, F32),
        compiler_params=_cparams(("parallel", "parallel")),
        name=f"ffn{sub}",
    )(x, mod, g, w1, w3, w2)


def _split3(v):
    hi = v.astype(BF16)
    r = v - hi.astype(F32)
    mid = r.astype(BF16)
    lo = (r - mid.astype(F32)).astype(BF16)
    return hi, mid, lo


def _group_norm(z, gmat, gain, group):
    ssq = jnp.dot((z * z).astype(BF16), gmat, preferred_element_type=F32)
    return z * lax.rsqrt(ssq * (1.0 / group) + NORM_EPS) * gain


def _rope(z, cos, sin, half, period):
    width = z.shape[1]
    lane = lax.broadcasted_iota(jnp.int32, z.shape, 1) % period
    up = pltpu.roll(z, width - half, axis=1)
    dn = pltpu.roll(z, half, axis=1)
    return z * cos + jnp.where(lane < half, up, dn) * sin


def _mix_in_kernel(x_ref, mod_ref, g_ref, pos_ref, wa_ref, wb_ref, wt_ref, ones_ref, gmat_ref, gains_ref, freq_ref,
                   fbias_ref, fq_ref, fk_ref, dq_ref, dk4_ref, ik8_ref, iq_ref, faug_ref, fvt_ref, dvt_ref, iwt_ref,
                   carry_ref):
    @pl.when(pl.program_id(1) == 0)
    def _():
        carry_ref[...] = jnp.zeros_like(carry_ref)

    x = x_ref[0]
    tm = x.shape[0]
    h = _modulated(x, mod_ref, g_ref, 1).astype(BF16)
    za = jnp.dot(h, wa_ref[...], preferred_element_type=F32)
    zb = jnp.dot(h, wb_ref[...], preferred_element_type=F32)
    zt = lax.dot_general(wt_ref[...], h, _NT, preferred_element_type=F32)
    n_v = FOX_HEADS * V_ROWS
    fvt_ref[0] = (zt[0:n_v] + jnp.concatenate([ones_ref[...]] * (tm // LANES), axis=1)).astype(BF16)
    dvt_ref[0] = (zt[n_v:n_v + V_ROWS]
                  + jnp.concatenate([ones_ref[0:V_ROWS, :]] * (tm // LANES), axis=1)).astype(BF16)
    iwt_ref[0] = zt[n_v + V_ROWS:, :] * ((IDX_HEADS * IDX_DIM) ** -0.5)

    gmat = gmat_ref[...]
    scale = HEAD_DIM ** -0.5 * LOG2E
    fq = _group_norm(za[:, 0:FOX_WIDTH], gmat, gains_ref[0:1, :], HEAD_DIM) * scale
    fk = _group_norm(za[:, FOX_WIDTH:2 * FOX_WIDTH], gmat, gains_ref[1:2, :], HEAD_DIM)
    fq_ref[0] = fq.astype(BF16)
    fk_ref[0] = fk.astype(BF16)

    pos = pos_ref[0]
    ang_a = pos * freq_ref[0:1, :]
    ang_i = pos * freq_ref[1:2, :]
    cos_a, sin_a = jnp.cos(ang_a), jnp.sin(ang_a)
    cos_i, sin_i = jnp.cos(ang_i), jnp.sin(ang_i)
    rot_a = HEAD_DIM // ROPE_FRACTION // 2
    rot_i = IDX_DIM // ROPE_FRACTION // 2

    dq = _group_norm(za[:, 2 * FOX_WIDTH:2 * FOX_WIDTH + DSA_WIDTH], gmat, gains_ref[2:3, :], HEAD_DIM)
    dq = _rope(dq, jnp.concatenate([cos_a] * 4, axis=1), jnp.concatenate([sin_a] * 4, axis=1), rot_a, HEAD_DIM)
    dq_ref[0] = (dq * scale).astype(BF16)

    dk4 = _group_norm(zb[:, 0:DK_TILED], gmat[0:DK_TILED, 0:DK_TILED], gains_ref[3:4, 0:DK_TILED], HEAD_DIM)
    dk4 = _rope(dk4, jnp.concatenate([cos_a] * 2, axis=1), jnp.concatenate([sin_a] * 2, axis=1), rot_a, HEAD_DIM)
    dk4_ref[0] = dk4.astype(BF16)

    cos_i2 = jnp.concatenate([cos_i] * 2, axis=1)
    sin_i2 = jnp.concatenate([sin_i] * 2, axis=1)
    o_ik, o_iq, o_sm = DK_TILED, DK_TILED + IDX_WIDTH, DK_TILED + 2 * IDX_WIDTH
    ik8_ref[0] = _rope(zb[:, o_ik:o_iq], cos_i2, sin_i2, rot_i, IDX_DIM).astype(BF16)
    iq_ref[0] = _rope(zb[:, o_iq:o_sm], cos_i2, sin_i2, rot_i, IDX_DIM).astype(BF16)

    sm = zb[:, o_sm:o_sm + LANES]
    v = sm + fbias_ref[...]
    logf = jnp.minimum(v, 0.0) - jnp.log1p(jnp.exp(-jnp.abs(v)))
    row = lax.broadcasted_iota(jnp.int32, (tm, tm), 0)
    col = lax.broadcasted_iota(jnp.int32, (tm, tm), 1)
    tri = jnp.where(row >= col, 1.0, 0.0).astype(BF16)
    hi, mid, lo = _split3(logf)
    csum = (jnp.dot(tri, hi, preferred_element_type=F32) + jnp.dot(tri, mid, preferred_element_type=F32)
            + jnp.dot(tri, lo, preferred_element_type=F32)) + carry_ref[...]
    carry_ref[...] = csum[tm - 1:tm, :]
    lane = lax.broadcasted_iota(jnp.int32, sm.shape, 1)

    nf = -csum * LOG2E
    t0 = nf.astype(BF16).astype(F32)
    r1 = nf - t0
    t1 = r1.astype(BF16).astype(F32)
    t2 = r1 - t1
    aug = jnp.where(lane < FOX_HEADS, t0,
                    jnp.where(lane < 2 * FOX_HEADS, pltpu.roll(t1, FOX_HEADS, axis=1),
                              jnp.where(lane < 3 * FOX_HEADS, pltpu.roll(t2, 2 * FOX_HEADS, axis=1),
                                        jnp.where(lane < 6 * FOX_HEADS, 1.0, 0.0))))
    faug_ref[0] = aug.astype(BF16)


def _mix_in_call(x, mod, g, pos, wa, wb, wt, ones, gmat, gains, freq, fbias):
    bsz, s, d = x.shape
    tm = min(TOKEN_TILE, s)

    def tok(w):
        return pl.BlockSpec((1, tm, w), lambda b, i: (b, i, 0))

    def tokt(r):
        return pl.BlockSpec((1, r, tm), lambda b, i: (b, 0, i))

    widths = (FOX_WIDTH, FOX_WIDTH, DSA_WIDTH, DK_TILED, IDX_WIDTH, IDX_WIDTH, LANES)
    rows = ((FOX_HEADS * V_ROWS, BF16), (V_ROWS, BF16), (2 * IDX_HEADS, F32))
    out_shapes = [jax.ShapeDtypeStruct((bsz, s, w), BF16) for w in widths]
    out_shapes += [jax.ShapeDtypeStruct((bsz, r, s), dt) for r, dt in rows]
    return pl.pallas_call(
        _mix_in_kernel,
        grid=(bsz, s // tm),
        in_specs=[tok(d),
                  pl.BlockSpec((1, 3 * N_SUBLAYERS, d), lambda b, i: (b, 0, 0)),
                  _const_spec((N_SUBLAYERS, d)),
                  tok(LANES),
                  _const_spec(wa.shape), _const_spec(wb.shape), _const_spec(wt.shape), _const_spec(ones.shape),
                  _const_spec(gmat.shape), _const_spec(gains.shape), _const_spec(freq.shape), _const_spec(fbias.shape)],
        out_specs=[tok(w) for w in widths] + [tokt(r) for r, _ in rows],
        out_shape=out_shapes,
        scratch_shapes=[pltpu.VMEM((1, LANES), F32)],
        compiler_params=_cparams(("arbitrary", "arbitrary")),
        name="mix_in",
    )(x, mod, g, pos, wa, wb, wt, ones, gmat, gains, freq, fbias)


def _fox_kernel(q_ref, qaug_ref, k_ref, faug_ref, vt_ref, hm_ref, fsel_ref, bound_ref, o_ref, qf_ref, m_ref, acc_ref):
    i = pl.program_id(2)
    tq = q_ref.shape[1]
    tk = tq
    q = q_ref[0]
    own = qaug_ref[0].astype(F32)
    for a in range(2):
        sel_row = fsel_ref[0, a:a + 1, :].astype(F32)
        plus_f = pltpu.roll(-(own * sel_row), 3 * FOX_HEADS, axis=1)
        qf_ref[a * tq:(a + 1) * tq, 0:LANES] = q * hm_ref[a:a + 1, :]
        qf_ref[a * tq:(a + 1) * tq, LANES:2 * LANES] = (sel_row + plus_f).astype(BF16)
    acc_ref[...] = jnp.zeros_like(acc_ref)
    bound = jnp.max(bound_ref[...])
    bounded = bound <= MAX_FIXED_SHIFT

    def block_bounded(j, masked):
        start = pl.multiple_of(j * tk, tk)
        kf = jnp.concatenate([k_ref[0, pl.ds(start, tk), :], faug_ref[0, pl.ds(start, tk), :]], axis=1)
        grp = min(FOX_GROUP, tq)
        for g in range(2 * tq // grp):
            rows = min(tk, (g * grp) % tq + grp) if masked else tk
            sl = slice(g * grp, (g + 1) * grp)
            s = lax.dot_general(kf[:rows], qf_ref[sl, :], _NT, preferred_element_type=F32) - bound
            if masked:
                krow = lax.broadcasted_iota(jnp.int32, s.shape, 0)
                qcol = lax.broadcasted_iota(jnp.int32, s.shape, 1) + (g * grp) % tq
                s = jnp.where(krow <= qcol, s, -jnp.inf)
            a = (g * grp) // tq
            vt = vt_ref[0, a * V_ROWS:(a + 1) * V_ROWS, pl.ds(start, rows)]
            acc_ref[:, sl] += jnp.dot(vt, jnp.exp2(s).astype(BF16), preferred_element_type=F32)

    def block(j, masked):
        start = pl.multiple_of(j * tk, tk)
        kf = jnp.concatenate([k_ref[0, pl.ds(start, tk), :], faug_ref[0, pl.ds(start, tk), :]], axis=1)
        grp = min(FOX_GROUP, tq)
        n_groups = 2 * tq // grp

        def rows(g):
            return min(tk, (g * grp) % tq + grp) if masked else tk

        def scores(g):
            s = lax.dot_general(kf[:rows(g)], qf_ref[g * grp:(g + 1) * grp, :], _NT,
                                preferred_element_type=F32)
            if masked:
                krow = lax.broadcasted_iota(jnp.int32, s.shape, 0)
                qcol = lax.broadcasted_iota(jnp.int32, s.shape, 1) + (g * grp) % tq
                s = jnp.where(krow <= qcol, s, -jnp.inf)
            return s

        ahead = [scores(g) for g in range(min(QK_AHEAD, n_groups))]
        for g in range(n_groups):
            s = ahead.pop(0)
            if g + QK_AHEAD < n_groups:
                ahead.append(scores(g + QK_AHEAD))
            sl = slice(g * grp, (g + 1) * grp)
            m_old = m_ref[:, sl]
            m_new = jnp.maximum(m_old, jnp.max(s, axis=0, keepdims=True))
            p = jnp.exp2((s - m_new).astype(BF16))
            alpha = jnp.exp2(m_old - m_new)
            m_ref[:, sl] = m_new
            a = (g * grp) // tq
            vt = vt_ref[0, a * V_ROWS:(a + 1) * V_ROWS, pl.ds(start, rows(g))]
            acc_ref[:, sl] = alpha * acc_ref[:, sl] + jnp.dot(vt, p, preferred_element_type=F32)

    def nf_of(rows16, a):
        return jnp.sum(rows16 * fsel_ref[0, a:a + 1, :].astype(F32), axis=1, keepdims=True)

    def sweep(block_fn, slack):
        block_fn(i, True)

        def older(jj, c):
            j = i - 1 - jj
            last = pl.multiple_of(j * tk + tk - 16, 16)
            tail_rows = faug_ref[0, pl.ds(last, 16), :].astype(F32)
            worst = -jnp.inf
            for a in range(2):
                worst = jnp.maximum(worst, jnp.max(nf_of(tail_rows, a)) - jnp.min(nf_of(own[0:16], a)))

            @pl.when(worst + slack > -FOX_ZERO_EXP)
            def _():
                block_fn(j, False)
            return c

        lax.fori_loop(0, i, older, 0)

    @pl.when(bounded)
    def _():
        sweep(block_bounded, 0.0)

    @pl.when(jnp.logical_not(bounded))
    def _():
        m_ref[...] = jnp.full_like(m_ref, -jnp.inf)
        sweep(block, 2.0 * bound)

    out = acc_ref[0:HEAD_DIM, :] / acc_ref[HEAD_DIM:HEAD_DIM + 1, :]
    for a in range(2):
        o_ref[0, a * HEAD_DIM:(a + 1) * HEAD_DIM, :] = out[:, a * tq:(a + 1) * tq].astype(BF16)


def _fox_call(fq, fk, faug, fvt, hm2, fsel, bound):
    bsz, s, _ = fq.shape
    tq = min(FOX_TILE, s)
    pairs = FOX_HEADS // 2
    return pl.pallas_call(
        _fox_kernel,
        grid=(bsz, pairs, s // tq),
        in_specs=[pl.BlockSpec((1, tq, LANES), lambda b, h, i: (b, i, h)),
                  pl.BlockSpec((1, tq, LANES), lambda b, h, i: (b, i, 0)),
                  pl.BlockSpec((1, s, LANES), lambda b, h, i: (b, 0, h)),
                  pl.BlockSpec((1, s, LANES), lambda b, h, i: (b, 0, 0)),
                  pl.BlockSpec((1, 2 * V_ROWS, s), lambda b, h, i: (b, h, 0)),
                  pl.BlockSpec((2, LANES), lambda b, h, i: (0, 0)),
                  pl.BlockSpec((1, 2, LANES), lambda b, h, i: (h, 0, 0)),
                  pl.BlockSpec((1, LANES), lambda b, h, i: (0, 0))],
        out_specs=pl.BlockSpec((1, LANES, tq), lambda b, h, i: (b, h, i)),
        out_shape=jax.ShapeDtypeStruct((bsz, FOX_WIDTH, s), BF16),
        scratch_shapes=[pltpu.VMEM((2 * tq, 2 * LANES), BF16),
                        pltpu.VMEM((1, 2 * tq), F32),
                        pltpu.VMEM((V_ROWS, 2 * tq), F32)],
        compiler_params=_cparams(("parallel", "parallel", "arbitrary")),
        name="fox_attn",
    )(fq, faug, fk, faug, fvt, hm2, fsel, bound)


def _dsa_kernel(dq_ref, iq_ref, wt_ref, dk4_ref, dvt_ref, ik8_ref, hm8_ref, hm4_ref, bound_ref, o_ref,
                iq8_ref, q8_ref, keys_ref, hi_ref, m_ref, acc_ref, *, top_k):
    i = pl.program_id(1)
    tq = dq_ref.shape[1]
    s_len = dk4_ref.shape[1]
    tk = min(DSA_K_TILE, s_len)
    half_tk = tk // 2
    q_start = i * tq
    rem = (q_start + tq) % tk
    n_full = (q_start + tq) // tk + jnp.where(rem > half_tk, 1, 0)
    has_tail = (rem > 0) & (rem <= half_tk)

    def sweep(fn, init):
        carry = lax.fori_loop(0, n_full, lambda j, c: fn(pl.multiple_of(j * tk, tk), tk, c), init)
        return lax.cond(has_tail, lambda c: fn(pl.multiple_of(n_full * tk, tk), half_tk, c), lambda c: c, carry)

    iq = iq_ref[0]
    dq = dq_ref[0]
    half = dq.shape[1] // 2
    for h in range(IDX_HEADS):
        iq8_ref[h * tq:(h + 1) * tq, :] = iq * hm8_ref[h:h + 1, :]
        q8_ref[h * tq:(h + 1) * tq, :] = dq[:, (h // 4) * half:(h // 4 + 1) * half] * hm4_ref[h % 4:h % 4 + 1, :]

    adm_end = q_start + (lax.broadcasted_iota(jnp.int32, (1, tq), 1) // CHUNK + 1) * CHUNK
    wt = wt_ref[0, :, pl.ds(pl.multiple_of(q_start, tq), tq)]

    def score_block(start, size, c):
        d = lax.dot_general(ik8_ref[0, pl.ds(start, size), :], iq8_ref[...], _NT, preferred_element_type=F32)
        sc = jnp.zeros((size, tq), F32)
        for h in range(IDX_HEADS):
            sc = sc + jnp.maximum(d[:, h * tq:(h + 1) * tq], 0.0) * wt[IDX_HEADS + h:IDX_HEADS + h + 1, :]
        sc = jnp.where(jnp.abs(sc) < F32_MIN_NORMAL, 0.0, sc)
        bits = lax.bitcast_convert_type(sc, jnp.int32)
        key = bits ^ ((bits >> 31) & F32_MAGNITUDE_BITS)
        adm = lax.broadcasted_iota(jnp.int32, (size, tq), 0) + start < adm_end
        keys_ref[pl.ds(start, size), :] = jnp.where(adm, key, INT_MIN)
        top = lax.bitcast_convert_type(bits & TOP16_MASK, F32)
        hi_ref[pl.ds(start, size), :] = jnp.where(adm, top, jnp.nan).astype(BF16)
        return c

    sweep(score_block, 0)

    def count(pred):
        def body(start, size, acc):
            hit = jnp.where(pred(keys_ref[pl.ds(start, size), :]), 1, 0)
            return acc + jnp.sum(hit.reshape(size // COUNT_ROWS, COUNT_ROWS, tq), axis=0)
        return jnp.sum(sweep(body, jnp.zeros((COUNT_ROWS, tq), jnp.int32)), axis=0, keepdims=True)

    def count_hi(cand):
        def body(start, size, acc):
            hit = jnp.where(hi_ref[pl.ds(start, size), :] >= cand, jnp.ones((), BF16), jnp.zeros((), BF16))
            part = hit[0:COUNT_ROWS]
            for r in range(1, size // COUNT_ROWS):
                part = part + hit[r * COUNT_ROWS:(r + 1) * COUNT_ROWS]
            return acc + part.astype(F32)
        return jnp.sum(sweep(body, jnp.zeros((COUNT_ROWS, tq), F32)), axis=0, keepdims=True)

    def step(cand, cnt, state):
        thr, done, n_ge, n_gt = state
        live = done == 0
        ok = (cnt >= top_k) & live
        return (jnp.where(ok, cand, thr), jnp.where(ok & (cnt == top_k), 1, done),
                jnp.where(ok, cnt, n_ge), jnp.where((cnt < top_k) & live, cnt, n_gt))

    def bisect_hi(it, state):
        cand = state[0] + jnp.left_shift(jnp.int32(1), 15 - it)
        pattern = jnp.where(cand >= 0, cand, cand ^ BF16_MAGNITUDE_BITS)
        pattern = jnp.where((pattern & BF16_EXPONENT_BITS) == 0, jnp.where(cand > 0, BF16_MIN_NORMAL_BITS, 0), pattern)
        cand_f = lax.bitcast_convert_type(jnp.left_shift(pattern, 16), F32).astype(BF16)
        return step(cand, count_hi(cand_f).astype(jnp.int32), state)

    def bisect_lo(carry):
        g, state = carry
        for u in range(LOW_PASSES_PER_CHECK):
            cand = state[0] + jnp.left_shift(jnp.int32(1), 15 - (g * LOW_PASSES_PER_CHECK + u))
            state = step(cand, count(lambda kblk: kblk >= cand), state)
        return g + 1, state

    state = (jnp.full((1, tq), -2 ** 15, jnp.int32), jnp.where(adm_end <= top_k, 1, 0), adm_end,
             jnp.zeros((1, tq), jnp.int32))
    t16, done, n_ge, n_gt = lax.fori_loop(0, 16, bisect_hi, state)
    base = jnp.left_shift(t16, 16)
    above = count(lambda kblk: kblk >= base + 1)
    exact = (done == 0) & (above == top_k)
    short = (done == 0) & (above < top_k)
    state = (jnp.where(exact, base + 1, base), jnp.where(above <= top_k, 1, done),
             jnp.where(exact, above, n_ge), jnp.where(short, above, n_gt))
    _, (thr, _, n_ge, n_gt) = lax.while_loop(
        lambda c: (c[0] < 16 // LOW_PASSES_PER_CHECK) & (jnp.min(c[1][1]) == 0), bisect_lo, (jnp.int32(0), state))
    thr = jnp.maximum(thr, INT_MIN + 1)

    @pl.when(jnp.max(n_ge) > top_k)
    def _():
        need = (top_k - n_gt).astype(F32)

        r2 = lax.broadcasted_iota(jnp.int32, (DEMOTE_ROWS, DEMOTE_ROWS), 0)
        c2 = lax.broadcasted_iota(jnp.int32, (DEMOTE_ROWS, DEMOTE_ROWS), 1)
        lower = jnp.where(c2 < r2, 1.0, 0.0).astype(BF16)

        def demote(start, size, seen):
            chunks = []
            for c0 in range(0, size, DEMOTE_ROWS):
                kblk = keys_ref[pl.ds(start + c0, DEMOTE_ROWS), :]
                eqf = jnp.where(kblk == thr, 1.0, 0.0)
                chunks.append((c0, kblk, eqf, jnp.dot(lower, eqf.astype(BF16), preferred_element_type=F32),
                               jnp.sum(eqf, axis=0, keepdims=True)))
            for c0, kblk, eqf, within, total in chunks:
                drop = (eqf > 0.0) & (within + seen >= need)
                keys_ref[pl.ds(start + c0, DEMOTE_ROWS), :] = jnp.where(drop, thr - 1, kblk)
                seen = seen + total
            return seen

        sweep(demote, jnp.zeros((1, tq), F32))

    acc_ref[...] = jnp.zeros_like(acc_ref)
    grp = min(DSA_GROUP, DSA_HEADS * tq)
    n_groups = DSA_HEADS * tq // grp
    bound = jnp.max(bound_ref[...])

    def attend_bounded(start, size, c):
        bias = jnp.where(keys_ref[pl.ds(start, size), :] >= thr, -bound, -jnp.inf)
        bias = jnp.concatenate([bias] * (grp // tq), axis=1)
        kb = dk4_ref[0, pl.ds(start, size), :]
        vt = dvt_ref[0, :, pl.ds(start, size)]
        for g in range(n_groups):
            sl = slice(g * grp, (g + 1) * grp)
            s = lax.dot_general(kb, q8_ref[sl, :], _NT, preferred_element_type=F32) + bias
            acc_ref[:, sl] += jnp.dot(vt, jnp.exp2(s).astype(BF16), preferred_element_type=F32)
        return c

    def attend(start, size, c):
        bias = jnp.where(keys_ref[pl.ds(start, size), :] >= thr, 0.0, -jnp.inf)
        bias = jnp.concatenate([bias] * (grp // tq), axis=1)
        kb = dk4_ref[0, pl.ds(start, size), :]
        vt = dvt_ref[0, :, pl.ds(start, size)]

        def scores(g):
            return lax.dot_general(kb, q8_ref[g * grp:(g + 1) * grp, :], _NT,
                                   preferred_element_type=F32) + bias

        ahead = [scores(g) for g in range(min(QK_AHEAD, n_groups))]
        for g in range(n_groups):
            s = ahead.pop(0)
            if g + QK_AHEAD < n_groups:
                ahead.append(scores(g + QK_AHEAD))
            sl = slice(g * grp, (g + 1) * grp)
            m_old = m_ref[:, sl]
            m_new = jnp.maximum(m_old, jnp.max(s, axis=0, keepdims=True))
            m_safe = jnp.where(m_new == -jnp.inf, 0.0, m_new)
            p = jnp.exp2((s - m_safe).astype(BF16))
            alpha = jnp.exp2(m_old - m_safe)
            m_ref[:, sl] = m_new
            acc_ref[:, sl] = alpha * acc_ref[:, sl] + jnp.dot(vt, p, preferred_element_type=F32)
        return c

    @pl.when(bound <= MAX_FIXED_SHIFT)
    def _():
        sweep(attend_bounded, 0)

    @pl.when(bound > MAX_FIXED_SHIFT)
    def _():
        m_ref[...] = jnp.full_like(m_ref, -jnp.inf)
        sweep(attend, 0)

    out = acc_ref[0:HEAD_DIM, :] / acc_ref[HEAD_DIM:HEAD_DIM + 1, :]
    for h in range(DSA_HEADS):
        o_ref[0, h * HEAD_DIM:(h + 1) * HEAD_DIM, :] = out[:, h * tq:(h + 1) * tq].astype(BF16)


def _dsa_call(dq, iq, wt, dk4, dvt, ik8, hm8, hm4, bound, top_k):
    bsz, s, _ = dq.shape
    tq = min(DSA_Q_TILE, s)

    def tok(w):
        return pl.BlockSpec((1, tq, w), lambda b, i: (b, i, 0))

    def seq(w):
        return pl.BlockSpec((1, s, w), lambda b, i: (b, 0, 0))

    def seqt(r):
        return pl.BlockSpec((1, r, s), lambda b, i: (b, 0, 0))

    return pl.pallas_call(
        functools.partial(_dsa_kernel, top_k=top_k),
        grid=(bsz, s // tq),
        in_specs=[tok(DSA_WIDTH), tok(IDX_WIDTH), seqt(2 * IDX_HEADS), seq(DK_TILED), seqt(V_ROWS), seq(IDX_WIDTH),
                  pl.BlockSpec(hm8.shape, lambda b, i: (0, 0)), pl.BlockSpec(hm4.shape, lambda b, i: (0, 0)),
                  pl.BlockSpec((1, LANES), lambda b, i: (0, 0))],
        out_specs=pl.BlockSpec((1, DSA_WIDTH, tq), lambda b, i: (b, 0, i)),
        out_shape=jax.ShapeDtypeStruct((bsz, DSA_WIDTH, s), BF16),
        scratch_shapes=[pltpu.VMEM((IDX_HEADS * tq, IDX_WIDTH), BF16), pltpu.VMEM((DSA_HEADS * tq, DK_TILED), BF16),
                        pltpu.VMEM((s, tq), jnp.int32), pltpu.VMEM((s, tq), BF16),
                        pltpu.VMEM((1, DSA_HEADS * tq), F32),
                        pltpu.VMEM((V_ROWS, DSA_HEADS * tq), F32)],
        compiler_params=_cparams(("parallel", "arbitrary")),
        name="dsa_attn",
    )(dq, iq, wt, dk4, dvt, ik8, hm8, hm4, bound)


def _merge_kernel(x_ref, mod_ref, g_ref, ya_ref, yb_ref, wg_ref, wfa_ref, wfb_ref, wo_ref, o_ref):
    x = x_ref[0]
    d = x.shape[1]
    h = _modulated(x, mod_ref, g_ref, 1).astype(BF16)
    zg = jnp.dot(h, wg_ref[...], preferred_element_type=F32)
    pa = lax.dot_general(ya_ref[0], wfa_ref[...], _TN, preferred_element_type=F32)
    pb = lax.dot_general(yb_ref[0], wfb_ref[...], _TN, preferred_element_type=F32)
    merged = jax.nn.sigmoid(zg[:, :d]) * pa + jax.nn.sigmoid(zg[:, d:]) * pb
    y = jnp.dot(merged.astype(BF16), wo_ref[...], preferred_element_type=F32)
    o_ref[0] = x + mod_ref[0, 5:6, :] * y


def _merge_call(x, mod, g, ya, yb, wg, wfa, wfb, wo):
    bsz, s, d = x.shape
    tm = min(TOKEN_TILE, s)

    def tok(w):
        return pl.BlockSpec((1, tm, w), lambda b, i: (b, i, 0))

    def tokt(w):
        return pl.BlockSpec((1, w, tm), lambda b, i: (b, 0, i))

    return pl.pallas_call(
        _merge_kernel,
        grid=(bsz, s // tm),
        in_specs=[tok(d),
                  pl.BlockSpec((1, 3 * N_SUBLAYERS, d), lambda b, i: (b, 0, 0)),
                  _const_spec((N_SUBLAYERS, d)),
                  tokt(FOX_WIDTH), tokt(DSA_WIDTH),
                  _const_spec(wg.shape), _const_spec(wfa.shape), _const_spec(wfb.shape), _const_spec(wo.shape)],
        out_specs=tok(d),
        out_shape=jax.ShapeDtypeStruct(x.shape, F32),
        compiler_params=_cparams(("parallel", "parallel")),
        name="merge_out",
    )(x, mod, g, ya, yb, wg, wfa, wfb, wo)


def _head_mask(n_heads, width):
    return jnp.asarray(np.kron(np.eye(n_heads), np.ones((1, width))), BF16)


def _rope_freq_row(rot_dim, period):
    inv_freq = ROPE_THETA ** (-jnp.arange(0, rot_dim, 2, dtype=F32) / rot_dim)
    half = rot_dim // 2
    head = jnp.concatenate([-inv_freq, inv_freq, jnp.zeros((period - 2 * half,), F32)])
    return jnp.tile(head, LANES // period)


def kernel(x, c, positions, ada_w, ada_b, norm_g, ffn1_w1, ffn1_w3, ffn1_w2, w_in, fox_f_bias, fox_qk_g, dsa_qk_g,
           w_br_fox, w_br_dsa, w_out, ffn2_w1, ffn2_w3, ffn2_w2):
    bsz, s, d = x.shape
    top_k = min(TOPK_MAX, s // 4)
    depth = ada_w.shape[0]
    pos = jnp.broadcast_to(positions.astype(F32)[:, :, None], (bsz, s, LANES))
    freq = jnp.stack([_rope_freq_row(HEAD_DIM // ROPE_FRACTION, HEAD_DIM),
                      _rope_freq_row(IDX_DIM // ROPE_FRACTION, IDX_DIM)])
    gmat = jnp.asarray(np.kron(np.eye(FOX_HEADS), np.ones((HEAD_DIM, HEAD_DIM))), BF16)
    hm2 = _head_mask(2, HEAD_DIM)
    hm4 = _head_mask(4, HEAD_DIM)
    hm8 = _head_mask(IDX_HEADS, IDX_DIM)
    fsel = jnp.asarray(np.tile(np.eye(FOX_HEADS), (1, LANES // FOX_HEADS))
                       * (np.arange(LANES) < 3 * FOX_HEADS), BF16).reshape(FOX_HEADS // 2, 2, LANES)
    o_fq, o_fk, o_fv = 0, FOX_WIDTH, 2 * FOX_WIDTH
    o_ff = 3 * FOX_WIDTH
    o_dq = o_ff + FOX_HEADS
    o_dk = o_dq + DSA_WIDTH
    o_dv = o_dk + HEAD_DIM
    o_iq = o_dv + HEAD_DIM
    o_ik = o_iq + IDX_WIDTH
    o_iw = o_ik + IDX_DIM
    o_ga = o_iw + IDX_HEADS

    for l in range(depth):
        mod = _ada_call(c, ada_w[l], ada_b[l]).reshape(bsz, 3 * N_SUBLAYERS, d)
        g = norm_g[l]
        x = _ffn_call(x, mod, g, ffn1_w1[l].astype(BF16), ffn1_w3[l].astype(BF16), ffn1_w2[l].astype(BF16), 0)

        w = w_in[l]
        wa = jnp.concatenate([w[:, o_fq:o_fq + 2 * FOX_WIDTH], w[:, o_dq:o_dq + DSA_WIDTH]], axis=1).astype(BF16)
        pad = jnp.zeros((d, LANES - FOX_HEADS), F32)
        wb = jnp.concatenate([jnp.tile(w[:, o_dk:o_dk + HEAD_DIM], (1, 4)),
                              jnp.tile(w[:, o_ik:o_ik + IDX_DIM], (1, IDX_HEADS)),
                              w[:, o_iq:o_iq + IDX_WIDTH],
                              w[:, o_ff:o_ff + FOX_HEADS], pad], axis=1).astype(BF16)
        zrows = jnp.zeros((FOX_HEADS, V_ROWS - HEAD_DIM, d), F32)
        wfv = jnp.transpose(w[:, o_fv:o_fv + FOX_WIDTH]).reshape(FOX_HEADS, HEAD_DIM, d)
        wt = jnp.concatenate([jnp.concatenate([wfv, zrows], axis=1).reshape(FOX_HEADS * V_ROWS, d),
                              jnp.transpose(w[:, o_dv:o_dv + HEAD_DIM]), zrows[0],
                              jnp.zeros((IDX_HEADS, d), F32), jnp.transpose(w[:, o_iw:o_iw + IDX_HEADS])],
                             axis=0).astype(BF16)
        ones = jnp.asarray(np.tile((np.arange(FOX_HEADS * V_ROWS) % V_ROWS >= HEAD_DIM)[:, None], (1, LANES)), F32)
        wg = w[:, o_ga:o_ga + 2 * d].astype(BF16)
        gains = jnp.stack([jnp.tile(fox_qk_g[l, 0], FOX_HEADS), jnp.tile(fox_qk_g[l, 1], FOX_HEADS),
                           jnp.tile(dsa_qk_g[l, 0], DSA_HEADS), jnp.tile(dsa_qk_g[l, 1], DSA_HEADS)]).astype(F32)
        fbias = jnp.concatenate([fox_f_bias[l].astype(F32), jnp.zeros((LANES - FOX_HEADS,), F32)]).reshape(1, LANES)

        fq, fk, dq, dk4, ik8, iq, faug, fvt, dvt, iwt = _mix_in_call(x, mod, g, pos, wa, wb, wt, ones, gmat, gains,
                                                                       freq, fbias)
        unit = jnp.full((1, LANES), HEAD_DIM ** 0.5 * LOG2E * NORM_SLACK, F32)
        fox_bound = unit * (jnp.max(jnp.abs(fox_qk_g[l, 0])) * jnp.max(jnp.abs(fox_qk_g[l, 1])))
        dsa_bound = unit * (jnp.max(jnp.abs(dsa_qk_g[l, 0])) * jnp.max(jnp.abs(dsa_qk_g[l, 1])))
        yat = _fox_call(fq, fk, faug, fvt, hm2, fsel, fox_bound)
        ybt = _dsa_call(dq, iq, iwt, dk4, dvt, ik8, hm8, hm4, dsa_bound, top_k)
        x = _merge_call(x, mod, g, yat, ybt, wg, w_br_fox[l].astype(BF16), w_br_dsa[l].astype(BF16),
                        w_out[l].astype(BF16))
        x = _ffn_call(x, mod, g, ffn2_w1[l].astype(BF16), ffn2_w3[l].astype(BF16), ffn2_w2[l].astype(BF16), 2)
    return x
```

```python
import functools

import numpy as np
import jax
import jax.numpy as jnp
from jax import lax
from jax.experimental import pallas as pl
from jax.experimental.pallas import tpu as pltpu

CHUNK = 64
HEAD_DIM = 64
FOX_HEADS = 8
DSA_HEADS = 8
IDX_HEADS = 8
IDX_DIM = 32
TOPK_MAX = 256
ROPE_THETA = 500000.0
ROPE_FRACTION = 4
N_SUBLAYERS = 3
NORM_EPS = 1e-6
FOX_WIDTH = FOX_HEADS * HEAD_DIM
DSA_WIDTH = DSA_HEADS * HEAD_DIM
IDX_WIDTH = IDX_HEADS * IDX_DIM
DK_TILED = 4 * HEAD_DIM

LANES = 128
VMEM_LIMIT_BYTES = 56 * 1024 * 1024

ADA_COLS = 1152
TOKEN_TILE = 512
FOX_TILE = 1024
DSA_Q_TILE = 256
DSA_K_TILE = 1024
FOX_GROUP = 512
DSA_GROUP = 1024
QK_AHEAD = 3
LOW_PASSES_PER_CHECK = 2
DEMOTE_ROWS = 128
COUNT_ROWS = 32

MAX_FIXED_SHIFT = 60.0
FOX_ZERO_EXP = 160.0
V_ROWS = HEAD_DIM + 16
LOG2E = 1.4426950408889634
INT_MIN = -2 ** 31
F32_MAGNITUDE_BITS = 0x7FFFFFFF
LOW16_MASK = 0xFFFF
LOW16_OFFSET = 32768
TOP16_MASK = -65536
BF16_MAGNITUDE_BITS = 0x7FFF
BF16_EXPONENT_BITS = 0x7F80
BF16_MIN_NORMAL_BITS = 0x0080
NORM_SLACK = 1.02
F32_MIN_NORMAL = 1.1754943508222875e-38
F32 = jnp.float32
BF16 = jnp.bfloat16
_NT = (((1,), (1,)), ((), ()))
_TN = (((0,), (0,)), ((), ()))


def _cparams(sem):
    return pltpu.CompilerParams(dimension_semantics=sem, vmem_limit_bytes=VMEM_LIMIT_BYTES)


def _const_spec(shape):
    nd = len(shape)
    return pl.BlockSpec(shape, lambda *_: (0,) * nd, pipeline_mode=pl.Buffered(1))


def _modulated(x, mod_ref, g_ref, sub):
    ms = jnp.mean(x * x, axis=-1, keepdims=True)
    y = x * lax.rsqrt(ms + NORM_EPS) * g_ref[sub:sub + 1, :]
    shift = mod_ref[0, 3 * sub:3 * sub + 1, :]
    scale = mod_ref[0, 3 * sub + 1:3 * sub + 2, :]
    return y * (1.0 + scale) + shift


def _ada_kernel(c_ref, w_ref, b_ref, o_ref):
    c = c_ref[...]
    cond = c * jax.nn.sigmoid(c)
    o_ref[...] = jnp.dot(cond, w_ref[...], preferred_element_type=F32,
                         precision=lax.Precision.HIGHEST) + b_ref[...]


def _ada_call(c, w, b):
    bsz, d = c.shape
    n = w.shape[1]
    tn = ADA_COLS
    return pl.pallas_call(
        _ada_kernel,
        grid=(n // tn,),
        in_specs=[pl.BlockSpec((bsz, d), lambda j: (0, 0)),
                  pl.BlockSpec((d, tn), lambda j: (0, j)),
                  pl.BlockSpec((1, tn), lambda j: (0, j))],
        out_specs=pl.BlockSpec((bsz, tn), lambda j: (0, j)),
        out_shape=jax.ShapeDtypeStruct((bsz, n), F32),
        compiler_params=_cparams(("arbitrary",)),
        name="ada_ln",
    )(c, w, b.reshape(1, n))


def _ffn_kernel(x_ref, mod_ref, g_ref, w1_ref, w3_ref, w2_ref, o_ref, *, sub):
    x = x_ref[0]
    h = _modulated(x, mod_ref, g_ref, sub).astype(BF16)
    a = jnp.dot(h, w1_ref[...], preferred_element_type=F32)
    b = jnp.dot(h, w3_ref[...], preferred_element_type=F32)
    act = (a * jax.nn.sigmoid(a) * b).astype(BF16)
    y = jnp.dot(act, w2_ref[...], preferred_element_type=F32)
    gate = mod_ref[0, 3 * sub + 2:3 * sub + 3, :]
    o_ref[0] = x + (0.5 * gate) * y


def _ffn_call(x, mod, g, w1, w3, w2, sub):
    bsz, s, d = x.shape
    tm = min(TOKEN_TILE, s)
    f = w1.shape[1]
    tok = pl.BlockSpec((1, tm, d), lambda b, i: (b, i, 0))
    return pl.pallas_call(
        functools.partial(_ffn_kernel, sub=sub),
        grid=(bsz, s // tm),
        in_specs=[tok,
                  pl.BlockSpec((1, 3 * N_SUBLAYERS, d), lambda b, i: (b, 0, 0)),
                  _const_spec((N_SUBLAYERS, d)),
                  _const_spec((d, f)), _const_spec((d, f)), _const_spec((f, d))],
        out_specs=tok,
        out_shape=jax.ShapeDtypeStruct(x.shape, F32),
        compiler_params=_cparams(("parallel", "parallel")),
        name=f"ffn{sub}",
    )(x, mod, g, w1, w3, w2)


def _split3(v):
    hi = v.astype(BF16)
    r = v - hi.astype(F32)
    mid = r.astype(BF16)
    lo = (r - mid.astype(F32)).astype(BF16)
    return hi, mid, lo


def _group_norm(z, gmat, gain, group):
    ssq = jnp.dot((z * z).astype(BF16), gmat, preferred_element_type=F32)
    return z * lax.rsqrt(ssq * (1.0 / group) + NORM_EPS) * gain


def _rope(z, cos, sin, half, period):
    width = z.shape[1]
    lane = lax.broadcasted_iota(jnp.int32, z.shape, 1) % period
    up = pltpu.roll(z, width - half, axis=1)
    dn = pltpu.roll(z, half, axis=1)
    return z * cos + jnp.where(lane < half, up, dn) * sin


def _mix_in_kernel(x_ref, mod_ref, g_ref, pos_ref, wa_ref, wb_ref, wt_ref, ones_ref, gmat_ref, gains_ref, freq_ref,
                   fbias_ref, fq_ref, fk_ref, dq_ref, dk4_ref, ik8_ref, iq_ref, faug_ref, fvt_ref, dvt_ref, iwt_ref,
                   carry_ref):
    @pl.when(pl.program_id(1) == 0)
    def _():
        carry_ref[...] = jnp.zeros_like(carry_ref)

    x = x_ref[0]
    tm = x.shape[0]
    h = _modulated(x, mod_ref, g_ref, 1).astype(BF16)
    za = jnp.dot(h, wa_ref[...], preferred_element_type=F32)
    zb = jnp.dot(h, wb_ref[...], preferred_element_type=F32)
    zt = lax.dot_general(wt_ref[...], h, _NT, preferred_element_type=F32)
    n_v = FOX_HEADS * V_ROWS
    fvt_ref[0] = (zt[0:n_v] + jnp.concatenate([ones_ref[...]] * (tm // LANES), axis=1)).astype(BF16)
    dvt_ref[0] = (zt[n_v:n_v + V_ROWS]
                  + jnp.concatenate([ones_ref[0:V_ROWS, :]] * (tm // LANES), axis=1)).astype(BF16)
    iwt_ref[0] = zt[n_v + V_ROWS:, :] * ((IDX_HEADS * IDX_DIM) ** -0.5)

    gmat = gmat_ref[...]
    scale = HEAD_DIM ** -0.5 * LOG2E
    fq = _group_norm(za[:, 0:FOX_WIDTH], gmat, gains_ref[0:1, :], HEAD_DIM) * scale
    fk = _group_norm(za[:, FOX_WIDTH:2 * FOX_WIDTH], gmat, gains_ref[1:2, :], HEAD_DIM)
    fq_ref[0] = fq.astype(BF16)
    fk_ref[0] = fk.astype(BF16)

    pos = pos_ref[0]
    ang_a = pos * freq_ref[0:1, :]
    ang_i = pos * freq_ref[1:2, :]
    cos_a, sin_a = jnp.cos(ang_a), jnp.sin(ang_a)
    cos_i, sin_i = jnp.cos(ang_i), jnp.sin(ang_i)
    rot_a = HEAD_DIM // ROPE_FRACTION // 2
    rot_i = IDX_DIM // ROPE_FRACTION // 2

    dq = _group_norm(za[:, 2 * FOX_WIDTH:2 * FOX_WIDTH + DSA_WIDTH], gmat, gains_ref[2:3, :], HEAD_DIM)
    dq = _rope(dq, jnp.concatenate([cos_a] * 4, axis=1), jnp.concatenate([sin_a] * 4, axis=1), rot_a, HEAD_DIM)
    dq_ref[0] = (dq * scale).astype(BF16)

    dk4 = _group_norm(zb[:, 0:DK_TILED], gmat[0:DK_TILED, 0:DK_TILED], gains_ref[3:4, 0:DK_TILED], HEAD_DIM)
    dk4 = _rope(dk4, jnp.concatenate([cos_a] * 2, axis=1), jnp.concatenate([sin_a] * 2, axis=1), rot_a, HEAD_DIM)
    dk4_ref[0] = dk4.astype(BF16)

    cos_i2 = jnp.concatenate([cos_i] * 2, axis=1)
    sin_i2 = jnp.concatenate([sin_i] * 2, axis=1)
    o_ik, o_iq, o_sm = DK_TILED, DK_TILED + IDX_WIDTH, DK_TILED + 2 * IDX_WIDTH
    ik8_ref[0] = _rope(zb[:, o_ik:o_iq], cos_i2, sin_i2, rot_i, IDX_DIM).astype(BF16)
    iq_ref[0] = _rope(zb[:, o_iq:o_sm], cos_i2, sin_i2, rot_i, IDX_DIM).astype(BF16)

    sm = zb[:, o_sm:o_sm + LANES]
    v = sm + fbias_ref[...]
    logf = jnp.minimum(v, 0.0) - jnp.log1p(jnp.exp(-jnp.abs(v)))
    row = lax.broadcasted_iota(jnp.int32, (tm, tm), 0)
    col = lax.broadcasted_iota(jnp.int32, (tm, tm), 1)
    tri = jnp.where(row >= col, 1.0, 0.0).astype(BF16)
    hi, mid, lo = _split3(logf)
    csum = (jnp.dot(tri, hi, preferred_element_type=F32) + jnp.dot(tri, mid, preferred_element_type=F32)
            + jnp.dot(tri, lo, preferred_element_type=F32)) + carry_ref[...]
    carry_ref[...] = csum[tm - 1:tm, :]
    lane = lax.broadcasted_iota(jnp.int32, sm.shape, 1)

    nf = -csum * LOG2E
    t0 = nf.astype(BF16).astype(F32)
    r1 = nf - t0
    t1 = r1.astype(BF16).astype(F32)
    t2 = r1 - t1
    aug = jnp.where(lane < FOX_HEADS, t0,
                    jnp.where(lane < 2 * FOX_HEADS, pltpu.roll(t1, FOX_HEADS, axis=1),
                              jnp.where(lane < 3 * FOX_HEADS, pltpu.roll(t2, 2 * FOX_HEADS, axis=1),
                                        jnp.where(lane < 6 * FOX_HEADS, 1.0, 0.0))))
    faug_ref[0] = aug.astype(BF16)


def _mix_in_call(x, mod, g, pos, wa, wb, wt, ones, gmat, gains, freq, fbias):
    bsz, s, d = x.shape
    tm = min(TOKEN_TILE, s)

    def tok(w):
        return pl.BlockSpec((1, tm, w), lambda b, i: (b, i, 0))

    def tokt(r):
        return pl.BlockSpec((1, r, tm), lambda b, i: (b, 0, i))

    widths = (FOX_WIDTH, FOX_WIDTH, DSA_WIDTH, DK_TILED, IDX_WIDTH, IDX_WIDTH, LANES)
    rows = ((FOX_HEADS * V_ROWS, BF16), (V_ROWS, BF16), (2 * IDX_HEADS, F32))
    out_shapes = [jax.ShapeDtypeStruct((bsz, s, w), BF16) for w in widths]
    out_shapes += [jax.ShapeDtypeStruct((bsz, r, s), dt) for r, dt in rows]
    return pl.pallas_call(
        _mix_in_kernel,
        grid=(bsz, s // tm),
        in_specs=[tok(d),
                  pl.BlockSpec((1, 3 * N_SUBLAYERS, d), lambda b, i: (b, 0, 0)),
                  _const_spec((N_SUBLAYERS, d)),
                  tok(LANES),
                  _const_spec(wa.shape), _const_spec(wb.shape), _const_spec(wt.shape), _const_spec(ones.shape),
                  _const_spec(gmat.shape), _const_spec(gains.shape), _const_spec(freq.shape), _const_spec(fbias.shape)],
        out_specs=[tok(w) for w in widths] + [tokt(r) for r, _ in rows],
        out_shape=out_shapes,
        scratch_shapes=[pltpu.VMEM((1, LANES), F32)],
        compiler_params=_cparams(("arbitrary", "arbitrary")),
        name="mix_in",
    )(x, mod, g, pos, wa, wb, wt, ones, gmat, gains, freq, fbias)


def _fox_kernel(q_ref, qaug_ref, k_ref, faug_ref, vt_ref, hm_ref, fsel_ref, bound_ref, o_ref, qf_ref, m_ref, acc_ref):
    i = pl.program_id(2)
    tq = q_ref.shape[1]
    tk = tq
    q = q_ref[0]
    own = qaug_ref[0].astype(F32)
    for a in range(2):
        sel_row = fsel_ref[0, a:a + 1, :].astype(F32)
        plus_f = pltpu.roll(-(own * sel_row), 3 * FOX_HEADS, axis=1)
        qf_ref[a * tq:(a + 1) * tq, 0:LANES] = q * hm_ref[a:a + 1, :]
        qf_ref[a * tq:(a + 1) * tq, LANES:2 * LANES] = (sel_row + plus_f).astype(BF16)
    acc_ref[...] = jnp.zeros_like(acc_ref)
    bound = jnp.max(bound_ref[...])
    bounded = bound <= MAX_FIXED_SHIFT

    def block_bounded(j, masked):
        start = pl.multiple_of(j * tk, tk)
        kf = jnp.concatenate([k_ref[0, pl.ds(start, tk), :], faug_ref[0, pl.ds(start, tk), :]], axis=1)
        grp = min(FOX_GROUP, tq)
        for g in range(2 * tq // grp):
            rows = min(tk, (g * grp) % tq + grp) if masked else tk
            sl = slice(g * grp, (g + 1) * grp)
            s = lax.dot_general(kf[:rows], qf_ref[sl, :], _NT, preferred_element_type=F32) - bound
            if masked:
                krow = lax.broadcasted_iota(jnp.int32, s.shape, 0)
                qcol = lax.broadcasted_iota(jnp.int32, s.shape, 1) + (g * grp) % tq
                s = jnp.where(krow <= qcol, s, -jnp.inf)
            a = (g * grp) // tq
            vt = vt_ref[0, a * V_ROWS:(a + 1) * V_ROWS, pl.ds(start, rows)]
            acc_ref[:, sl] += jnp.dot(vt, jnp.exp2(s).astype(BF16), preferred_element_type=F32)

    def block(j, masked):
        start = pl.multiple_of(j * tk, tk)
        kf = jnp.concatenate([k_ref[0, pl.ds(start, tk), :], faug_ref[0, pl.ds(start, tk), :]], axis=1)
        grp = min(FOX_GROUP, tq)
        n_groups = 2 * tq // grp

        def rows(g):
            return min(tk, (g * grp) % tq + grp) if masked else tk

        def scores(g):
            s = lax.dot_general(kf[:rows(g)], qf_ref[g * grp:(g + 1) * grp, :], _NT,
                                preferred_element_type=F32)
            if masked:
                krow = lax.broadcasted_iota(jnp.int32, s.shape, 0)
                qcol = lax.broadcasted_iota(jnp.int32, s.shape, 1) + (g * grp) % tq
                s = jnp.where(krow <= qcol, s, -jnp.inf)
            return s

        ahead = [scores(g) for g in range(min(QK_AHEAD, n_groups))]
        for g in range(n_groups):
            s = ahead.pop(0)
            if g + QK_AHEAD < n_groups:
                ahead.append(scores(g + QK_AHEAD))
            sl = slice(g * grp, (g + 1) * grp)
            m_old = m_ref[:, sl]
            m_new = jnp.maximum(m_old, jnp.max(s, axis=0, keepdims=True))
            p = jnp.exp2((s - m_new).astype(BF16))
            alpha = jnp.exp2(m_old - m_new)
            m_ref[:, sl] = m_new
            a = (g * grp) // tq
            vt = vt_ref[0, a * V_ROWS:(a + 1) * V_ROWS, pl.ds(start, rows(g))]
            acc_ref[:, sl] = alpha * acc_ref[:, sl] + jnp.dot(vt, p, preferred_element_type=F32)

    def nf_of(rows16, a):
        return jnp.sum(rows16 * fsel_ref[0, a:a + 1, :].astype(F32), axis=1, keepdims=True)

    def sweep(block_fn, slack):
        block_fn(i, True)

        def older(jj, c):
            j = i - 1 - jj
            last = pl.multiple_of(j * tk + tk - 16, 16)
            tail_rows = faug_ref[0, pl.ds(last, 16), :].astype(F32)
            worst = -jnp.inf
            for a in range(2):
                worst = jnp.maximum(worst, jnp.max(nf_of(tail_rows, a)) - jnp.min(nf_of(own[0:16], a)))

            @pl.when(worst + slack > -FOX_ZERO_EXP)
            def _():
                block_fn(j, False)
            return c

        lax.fori_loop(0, i, older, 0)

    @pl.when(bounded)
    def _():
        sweep(block_bounded, 0.0)

    @pl.when(jnp.logical_not(bounded))
    def _():
        m_ref[...] = jnp.full_like(m_ref, -jnp.inf)
        sweep(block, 2.0 * bound)

    out = acc_ref[0:HEAD_DIM, :] / acc_ref[HEAD_DIM:HEAD_DIM + 1, :]
    for a in range(2):
        o_ref[0, a * HEAD_DIM:(a + 1) * HEAD_DIM, :] = out[:, a * tq:(a + 1) * tq].astype(BF16)


def _fox_call(fq, fk, faug, fvt, hm2, fsel, bound):
    bsz, s, _ = fq.shape
    tq = min(FOX_TILE, s)
    pairs = FOX_HEADS // 2
    return pl.pallas_call(
        _fox_kernel,
        grid=(bsz, pairs, s // tq),
        in_specs=[pl.BlockSpec((1, tq, LANES), lambda b, h, i: (b, i, h)),
                  pl.BlockSpec((1, tq, LANES), lambda b, h, i: (b, i, 0)),
                  pl.BlockSpec((1, s, LANES), lambda b, h, i: (b, 0, h)),
                  pl.BlockSpec((1, s, LANES), lambda b, h, i: (b, 0, 0)),
                  pl.BlockSpec((1, 2 * V_ROWS, s), lambda b, h, i: (b, h, 0)),
                  pl.BlockSpec((2, LANES), lambda b, h, i: (0, 0)),
                  pl.BlockSpec((1, 2, LANES), lambda b, h, i: (h, 0, 0)),
                  pl.BlockSpec((1, LANES), lambda b, h, i: (0, 0))],
        out_specs=pl.BlockSpec((1, LANES, tq), lambda b, h, i: (b, h, i)),
        out_shape=jax.ShapeDtypeStruct((bsz, FOX_WIDTH, s), BF16),
        scratch_shapes=[pltpu.VMEM((2 * tq, 2 * LANES), BF16),
                        pltpu.VMEM((1, 2 * tq), F32),
                        pltpu.VMEM((V_ROWS, 2 * tq), F32)],
        compiler_params=_cparams(("parallel", "parallel", "arbitrary")),
        name="fox_attn",
    )(fq, faug, fk, faug, fvt, hm2, fsel, bound)


def _dsa_kernel(dq_ref, iq_ref, wt_ref, dk4_ref, dvt_ref, ik8_ref, hm8_ref, hm4_ref, bound_ref, o_ref,
                iq8_ref, q8_ref, keys_ref, hi_ref, lo_ref, m_ref, acc_ref, *, top_k):
    i = pl.program_id(1)
    tq = dq_ref.shape[1]
    s_len = dk4_ref.shape[1]
    tk = min(DSA_K_TILE, s_len)
    half_tk = tk // 2
    q_start = i * tq
    rem = (q_start + tq) % tk
    n_full = (q_start + tq) // tk + jnp.where(rem > half_tk, 1, 0)
    has_tail = (rem > 0) & (rem <= half_tk)

    def sweep(fn, init):
        carry = lax.fori_loop(0, n_full, lambda j, c: fn(pl.multiple_of(j * tk, tk), tk, c), init)
        return lax.cond(has_tail, lambda c: fn(pl.multiple_of(n_full * tk, tk), half_tk, c), lambda c: c, carry)

    iq = iq_ref[0]
    dq = dq_ref[0]
    half = dq.shape[1] // 2
    for h in range(IDX_HEADS):
        iq8_ref[h * tq:(h + 1) * tq, :] = iq * hm8_ref[h:h + 1, :]
        q8_ref[h * tq:(h + 1) * tq, :] = dq[:, (h // 4) * half:(h // 4 + 1) * half] * hm4_ref[h % 4:h % 4 + 1, :]

    adm_end = q_start + (lax.broadcasted_iota(jnp.int32, (1, tq), 1) // CHUNK + 1) * CHUNK
    wt = wt_ref[0, :, pl.ds(pl.multiple_of(q_start, tq), tq)]

    def score_block(start, size, c):
        d = lax.dot_general(ik8_ref[0, pl.ds(start, size), :], iq8_ref[...], _NT, preferred_element_type=F32)
        sc = jnp.zeros((size, tq), F32)
        for h in range(IDX_HEADS):
            sc = sc + jnp.maximum(d[:, h * tq:(h + 1) * tq], 0.0) * wt[IDX_HEADS + h:IDX_HEADS + h + 1, :]
        sc = jnp.where(jnp.abs(sc) < F32_MIN_NORMAL, 0.0, sc)
        bits = lax.bitcast_convert_type(sc, jnp.int32)
        key = bits ^ ((bits >> 31) & F32_MAGNITUDE_BITS)
        adm = lax.broadcasted_iota(jnp.int32, (size, tq), 0) + start < adm_end
        keys_ref[pl.ds(start, size), :] = jnp.where(adm, key, INT_MIN)
        top = lax.bitcast_convert_type(bits & TOP16_MASK, F32)
        hi_ref[pl.ds(start, size), :] = jnp.where(adm, top, jnp.nan).astype(BF16)
        return c

    sweep(score_block, 0)

    def count(pred):
        def body(start, size, acc):
            hit = jnp.where(pred(keys_ref[pl.ds(start, size), :]), 1, 0)
            return acc + jnp.sum(hit.reshape(size // COUNT_ROWS, COUNT_ROWS, tq), axis=0)
        return jnp.sum(sweep(body, jnp.zeros((COUNT_ROWS, tq), jnp.int32)), axis=0, keepdims=True)

    def count_hi(cand):
        def body(start, size, acc):
            hit = jnp.where(hi_ref[pl.ds(start, size), :] >= cand, jnp.ones((), BF16), jnp.zeros((), BF16))
            part = hit[0:COUNT_ROWS]
            for r in range(1, size // COUNT_ROWS):
                part = part + hit[r * COUNT_ROWS:(r + 1) * COUNT_ROWS]
            return acc + part.astype(F32)
        return jnp.sum(sweep(body, jnp.zeros((COUNT_ROWS, tq), F32)), axis=0, keepdims=True)

    def count_lo(cand, n_above_bucket):
        c16 = ((cand & LOW16_MASK) - LOW16_OFFSET).astype(jnp.int16)

        def body(start, size, acc):
            hit = jnp.where(lo_ref[pl.ds(start, size), :] >= c16, jnp.ones((), jnp.int16), jnp.zeros((), jnp.int16))
            part = hit[0:COUNT_ROWS]
            for r in range(1, size // COUNT_ROWS):
                part = part + hit[r * COUNT_ROWS:(r + 1) * COUNT_ROWS]
            return acc + part.astype(jnp.int32)
        hits = jnp.sum(sweep(body, jnp.zeros((COUNT_ROWS, tq), jnp.int32)), axis=0, keepdims=True)
        return n_above_bucket + hits

    def step(cand, cnt, state):
        thr, done, n_ge, n_gt = state
        live = done == 0
        ok = (cnt >= top_k) & live
        return (jnp.where(ok, cand, thr), jnp.where(ok & (cnt == top_k), 1, done),
                jnp.where(ok, cnt, n_ge), jnp.where((cnt < top_k) & live, cnt, n_gt))

    def bisect_hi(it, state):
        cand = state[0] + jnp.left_shift(jnp.int32(1), 15 - it)
        pattern = jnp.where(cand >= 0, cand, cand ^ BF16_MAGNITUDE_BITS)
        pattern = jnp.where((pattern & BF16_EXPONENT_BITS) == 0, jnp.where(cand > 0, BF16_MIN_NORMAL_BITS, 0), pattern)
        cand_f = lax.bitcast_convert_type(jnp.left_shift(pattern, 16), F32).astype(BF16)
        return step(cand, count_hi(cand_f).astype(jnp.int32), state)

    def bisect_lo(carry):
        g, state = carry
        for u in range(LOW_PASSES_PER_CHECK):
            cand = state[0] + jnp.left_shift(jnp.int32(1), 15 - (g * LOW_PASSES_PER_CHECK + u))
            state = step(cand, count_lo(cand, n_bucket), state)
        return g + 1, state

    state = (jnp.full((1, tq), -2 ** 15, jnp.int32), jnp.where(adm_end <= top_k, 1, 0), adm_end,
             jnp.zeros((1, tq), jnp.int32))
    t16, done, n_ge, n_gt = lax.fori_loop(0, 16, bisect_hi, state)
    base = jnp.left_shift(t16, 16)
    above = count(lambda kblk: kblk >= base + 1)
    exact = (done == 0) & (above == top_k)
    short = (done == 0) & (above < top_k)
    state = (jnp.where(exact, base + 1, base), jnp.where(above <= top_k, 1, done),
             jnp.where(exact, above, n_ge), jnp.where(short, above, n_gt))
    n_bucket = n_gt

    def mark_bucket(start, size, c):
        off = keys_ref[pl.ds(start, size), :] - base
        inside = lax.bitcast_convert_type(off, jnp.uint32) < jnp.uint32(LOW16_MASK + 1)
        lo_ref[pl.ds(start, size), :] = jnp.where(inside, off - LOW16_OFFSET, -LOW16_OFFSET).astype(jnp.int16)
        return c

    sweep(mark_bucket, 0)
    _, (thr, _, n_ge, n_gt) = lax.while_loop(
        lambda c: (c[0] < 16 // LOW_PASSES_PER_CHECK) & (jnp.min(c[1][1]) == 0), bisect_lo, (jnp.int32(0), state))
    thr = jnp.maximum(thr, INT_MIN + 1)

    @pl.when(jnp.max(n_ge) > top_k)
    def _():
        need = (top_k - n_gt).astype(F32)

        r2 = lax.broadcasted_iota(jnp.int32, (DEMOTE_ROWS, DEMOTE_ROWS), 0)
        c2 = lax.broadcasted_iota(jnp.int32, (DEMOTE_ROWS, DEMOTE_ROWS), 1)
        lower = jnp.where(c2 < r2, 1.0, 0.0).astype(BF16)

        def demote(start, size, seen):
            chunks = []
            for c0 in range(0, size, DEMOTE_ROWS):
                kblk = keys_ref[pl.ds(start + c0, DEMOTE_ROWS), :]
                eqf = jnp.where(kblk == thr, 1.0, 0.0)
                chunks.append((c0, kblk, eqf, jnp.dot(lower, eqf.astype(BF16), preferred_element_type=F32),
                               jnp.sum(eqf, axis=0, keepdims=True)))
            for c0, kblk, eqf, within, total in chunks:
                drop = (eqf > 0.0) & (within + seen >= need)
                keys_ref[pl.ds(start + c0, DEMOTE_ROWS), :] = jnp.where(drop, thr - 1, kblk)
                seen = seen + total
            return seen

        sweep(demote, jnp.zeros((1, tq), F32))

    acc_ref[...] = jnp.zeros_like(acc_ref)
    grp = min(DSA_GROUP, DSA_HEADS * tq)
    n_groups = DSA_HEADS * tq // grp
    bound = jnp.max(bound_ref[...])

    def attend_bounded(start, size, c):
        bias = jnp.where(keys_ref[pl.ds(start, size), :] >= thr, -bound, -jnp.inf)
        bias = jnp.concatenate([bias] * (grp // tq), axis=1)
        kb = dk4_ref[0, pl.ds(start, size), :]
        vt = dvt_ref[0, :, pl.ds(start, size)]
        for g in range(n_groups):
            sl = slice(g * grp, (g + 1) * grp)
            s = lax.dot_general(kb, q8_ref[sl, :], _NT, preferred_element_type=F32) + bias
            acc_ref[:, sl] += jnp.dot(vt, jnp.exp2(s).astype(BF16), preferred_element_type=F32)
        return c

    def attend(start, size, c):
        bias = jnp.where(keys_ref[pl.ds(start, size), :] >= thr, 0.0, -jnp.inf)
        bias = jnp.concatenate([bias] * (grp // tq), axis=1)
        kb = dk4_ref[0, pl.ds(start, size), :]
        vt = dvt_ref[0, :, pl.ds(start, size)]

        def scores(g):
            return lax.dot_general(kb, q8_ref[g * grp:(g + 1) * grp, :], _NT,
                                   preferred_element_type=F32) + bias

        ahead = [scores(g) for g in range(min(QK_AHEAD, n_groups))]
        for g in range(n_groups):
            s = ahead.pop(0)
            if g + QK_AHEAD < n_groups:
                ahead.append(scores(g + QK_AHEAD))
            sl = slice(g * grp, (g + 1) * grp)
            m_old = m_ref[:, sl]
            m_new = jnp.maximum(m_old, jnp.max(s, axis=0, keepdims=True))
            m_safe = jnp.where(m_new == -jnp.inf, 0.0, m_new)
            p = jnp.exp2((s - m_safe).astype(BF16))
            alpha = jnp.exp2(m_old - m_safe)
            m_ref[:, sl] = m_new
            acc_ref[:, sl] = alpha * acc_ref[:, sl] + jnp.dot(vt, p, preferred_element_type=F32)
        return c

    @pl.when(bound <= MAX_FIXED_SHIFT)
    def _():
        sweep(attend_bounded, 0)

    @pl.when(bound > MAX_FIXED_SHIFT)
    def _():
        m_ref[...] = jnp.full_like(m_ref, -jnp.inf)
        sweep(attend, 0)

    out = acc_ref[0:HEAD_DIM, :] / acc_ref[HEAD_DIM:HEAD_DIM + 1, :]
    for h in range(DSA_HEADS):
        o_ref[0, h * HEAD_DIM:(h + 1) * HEAD_DIM, :] = out[:, h * tq:(h + 1) * tq].astype(BF16)


def _dsa_call(dq, iq, wt, dk4, dvt, ik8, hm8, hm4, bound, top_k):
    bsz, s, _ = dq.shape
    tq = min(DSA_Q_TILE, s)

    def tok(w):
        return pl.BlockSpec((1, tq, w), lambda b, i: (b, i, 0))

    def seq(w):
        return pl.BlockSpec((1, s, w), lambda b, i: (b, 0, 0))

    def seqt(r):
        return pl.BlockSpec((1, r, s), lambda b, i: (b, 0, 0))

    return pl.pallas_call(
        functools.partial(_dsa_kernel, top_k=top_k),
        grid=(bsz, s // tq),
        in_specs=[tok(DSA_WIDTH), tok(IDX_WIDTH), seqt(2 * IDX_HEADS), seq(DK_TILED), seqt(V_ROWS), seq(IDX_WIDTH),
                  pl.BlockSpec(hm8.shape, lambda b, i: (0, 0)), pl.BlockSpec(hm4.shape, lambda b, i: (0, 0)),
                  pl.BlockSpec((1, LANES), lambda b, i: (0, 0))],
        out_specs=pl.BlockSpec((1, DSA_WIDTH, tq), lambda b, i: (b, 0, i)),
        out_shape=jax.ShapeDtypeStruct((bsz, DSA_WIDTH, s), BF16),
        scratch_shapes=[pltpu.VMEM((IDX_HEADS * tq, IDX_WIDTH), BF16), pltpu.VMEM((DSA_HEADS * tq, DK_TILED), BF16),
                        pltpu.VMEM((s, tq), jnp.int32), pltpu.VMEM((s, tq), BF16),
                        pltpu.VMEM((s, tq), jnp.int16),
                        pltpu.VMEM((1, DSA_HEADS * tq), F32),
                        pltpu.VMEM((V_ROWS, DSA_HEADS * tq), F32)],
        compiler_params=_cparams(("parallel", "arbitrary")),
        name="dsa_attn",
    )(dq, iq, wt, dk4, dvt, ik8, hm8, hm4, bound)


def _merge_kernel(x_ref, mod_ref, g_ref, ya_ref, yb_ref, wg_ref, wfa_ref, wfb_ref, wo_ref, o_ref):
    x = x_ref[0]
    d = x.shape[1]
    h = _modulated(x, mod_ref, g_ref, 1).astype(BF16)
    zg = jnp.dot(h, wg_ref[...], preferred_element_type=F32)
    pa = lax.dot_general(ya_ref[0], wfa_ref[...], _TN, preferred_element_type=F32)
    pb = lax.dot_general(yb_ref[0], wfb_ref[...], _TN, preferred_element_type=F32)
    merged = jax.nn.sigmoid(zg[:, :d]) * pa + jax.nn.sigmoid(zg[:, d:]) * pb
    y = jnp.dot(merged.astype(BF16), wo_ref[...], preferred_element_type=F32)
    o_ref[0] = x + mod_ref[0, 5:6, :] * y


def _merge_call(x, mod, g, ya, yb, wg, wfa, wfb, wo):
    bsz, s, d = x.shape
    tm = min(TOKEN_TILE, s)

    def tok(w):
        return pl.BlockSpec((1, tm, w), lambda b, i: (b, i, 0))

    def tokt(w):
        return pl.BlockSpec((1, w, tm), lambda b, i: (b, 0, i))

    return pl.pallas_call(
        _merge_kernel,
        grid=(bsz, s // tm),
        in_specs=[tok(d),
                  pl.BlockSpec((1, 3 * N_SUBLAYERS, d), lambda b, i: (b, 0, 0)),
                  _const_spec((N_SUBLAYERS, d)),
                  tokt(FOX_WIDTH), tokt(DSA_WIDTH),
                  _const_spec(wg.shape), _const_spec(wfa.shape), _const_spec(wfb.shape), _const_spec(wo.shape)],
        out_specs=tok(d),
        out_shape=jax.ShapeDtypeStruct(x.shape, F32),
        compiler_params=_cparams(("parallel", "parallel")),
        name="merge_out",
    )(x, mod, g, ya, yb, wg, wfa, wfb, wo)


def _head_mask(n_heads, width):
    return jnp.asarray(np.kron(np.eye(n_heads), np.ones((1, width))), BF16)


def _rope_freq_row(rot_dim, period):
    inv_freq = ROPE_THETA ** (-jnp.arange(0, rot_dim, 2, dtype=F32) / rot_dim)
    half = rot_dim // 2
    head = jnp.concatenate([-inv_freq, inv_freq, jnp.zeros((period - 2 * half,), F32)])
    return jnp.tile(head, LANES // period)


def kernel(x, c, positions, ada_w, ada_b, norm_g, ffn1_w1, ffn1_w3, ffn1_w2, w_in, fox_f_bias, fox_qk_g, dsa_qk_g,
           w_br_fox, w_br_dsa, w_out, ffn2_w1, ffn2_w3, ffn2_w2):
    bsz, s, d = x.shape
    top_k = min(TOPK_MAX, s // 4)
    depth = ada_w.shape[0]
    pos = jnp.broadcast_to(positions.astype(F32)[:, :, None], (bsz, s, LANES))
    freq = jnp.stack([_rope_freq_row(HEAD_DIM // ROPE_FRACTION, HEAD_DIM),
                      _rope_freq_row(IDX_DIM // ROPE_FRACTION, IDX_DIM)])
    gmat = jnp.asarray(np.kron(np.eye(FOX_HEADS), np.ones((HEAD_DIM, HEAD_DIM))), BF16)
    hm2 = _head_mask(2, HEAD_DIM)
    hm4 = _head_mask(4, HEAD_DIM)
    hm8 = _head_mask(IDX_HEADS, IDX_DIM)
    fsel = jnp.asarray(np.tile(np.eye(FOX_HEADS), (1, LANES // FOX_HEADS))
                       * (np.arange(LANES) < 3 * FOX_HEADS), BF16).reshape(FOX_HEADS // 2, 2, LANES)
    o_fq, o_fk, o_fv = 0, FOX_WIDTH, 2 * FOX_WIDTH
    o_ff = 3 * FOX_WIDTH
    o_dq = o_ff + FOX_HEADS
    o_dk = o_dq + DSA_WIDTH
    o_dv = o_dk + HEAD_DIM
    o_iq = o_dv + HEAD_DIM
    o_ik = o_iq + IDX_WIDTH
    o_iw = o_ik + IDX_DIM
    o_ga = o_iw + IDX_HEADS

    for l in range(depth):
        mod = _ada_call(c, ada_w[l], ada_b[l]).reshape(bsz, 3 * N_SUBLAYERS, d)
        g = norm_g[l]
        x = _ffn_call(x, mod, g, ffn1_w1[l].astype(BF16), ffn1_w3[l].astype(BF16), ffn1_w2[l].astype(BF16), 0)

        w = w_in[l]
        wa = jnp.concatenate([w[:, o_fq:o_fq + 2 * FOX_WIDTH], w[:, o_dq:o_dq + DSA_WIDTH]], axis=1).astype(BF16)
        pad = jnp.zeros((d, LANES - FOX_HEADS), F32)
        wb = jnp.concatenate([jnp.tile(w[:, o_dk:o_dk + HEAD_DIM], (1, 4)),
                              jnp.tile(w[:, o_ik:o_ik + IDX_DIM], (1, IDX_HEADS)),
                              w[:, o_iq:o_iq + IDX_WIDTH],
                              w[:, o_ff:o_ff + FOX_HEADS], pad], axis=1).astype(BF16)
        zrows = jnp.zeros((FOX_HEADS, V_ROWS - HEAD_DIM, d), F32)
        wfv = jnp.transpose(w[:, o_fv:o_fv + FOX_WIDTH]).reshape(FOX_HEADS, HEAD_DIM, d)
        wt = jnp.concatenate([jnp.concatenate([wfv, zrows], axis=1).reshape(FOX_HEADS * V_ROWS, d),
                              jnp.transpose(w[:, o_dv:o_dv + HEAD_DIM]), zrows[0],
                              jnp.zeros((IDX_HEADS, d), F32), jnp.transpose(w[:, o_iw:o_iw + IDX_HEADS])],
                             axis=0).astype(BF16)
        ones = jnp.asarray(np.tile((np.arange(FOX_HEADS * V_ROWS) % V_ROWS >= HEAD_DIM)[:, None], (1, LANES)), F32)
        wg = w[:, o_ga:o_ga + 2 * d].astype(BF16)
        gains = jnp.stack([jnp.tile(fox_qk_g[l, 0], FOX_HEADS), jnp.tile(fox_qk_g[l, 1], FOX_HEADS),
                           jnp.tile(dsa_qk_g[l, 0], DSA_HEADS), jnp.tile(dsa_qk_g[l, 1], DSA_HEADS)]).astype(F32)
        fbias = jnp.concatenate([fox_f_bias[l].astype(F32), jnp.zeros((LANES - FOX_HEADS,), F32)]).reshape(1, LANES)

        fq, fk, dq, dk4, ik8, iq, faug, fvt, dvt, iwt = _mix_in_call(x, mod, g, pos, wa, wb, wt, ones, gmat, gains,
                                                                       freq, fbias)
        unit = jnp.full((1, LANES), HEAD_DIM ** 0.5 * LOG2E * NORM_SLACK, F32)
        fox_bound = unit * (jnp.max(jnp.abs(fox_qk_g[l, 0])) * jnp.max(jnp.abs(fox_qk_g[l, 1])))
        dsa_bound = unit * (jnp.max(jnp.abs(dsa_qk_g[l, 0])) * jnp.max(jnp.abs(dsa_qk_g[l, 1])))
        yat = _fox_call(fq, fk, faug, fvt, hm2, fsel, fox_bound)
        ybt = _dsa_call(dq, iq, iwt, dk4, dvt, ik8, hm8, hm4, dsa_bound, top_k)
        x = _merge_call(x, mod, g, yat, ybt, wg, w_br_fox[l].astype(BF16), w_br_dsa[l].astype(BF16),
                        w_out[l].astype(BF16))
        x = _ffn_call(x, mod, g, ffn2_w1[l].astype(BF16), ffn2_w3[l].astype(BF16), ffn2_w2[l].astype(BF16), 2)
    return x
```

```python
import functools

import numpy as np
import jax
import jax.numpy as jnp
from jax import lax
from jax.experimental import pallas as pl
from jax.experimental.pallas import tpu as pltpu

CHUNK = 64
HEAD_DIM = 64
FOX_HEADS = 8
DSA_HEADS = 8
IDX_HEADS = 8
IDX_DIM = 32
TOPK_MAX = 256
ROPE_THETA = 500000.0
ROPE_FRACTION = 4
N_SUBLAYERS = 3
NORM_EPS = 1e-6
FOX_WIDTH = FOX_HEADS * HEAD_DIM
DSA_WIDTH = DSA_HEADS * HEAD_DIM
IDX_WIDTH = IDX_HEADS * IDX_DIM
DK_TILED = 4 * HEAD_DIM

LANES = 128
VMEM_LIMIT_BYTES = 56 * 1024 * 1024

ADA_COLS = 1152
TOKEN_TILE = 512
FOX_TILE = 1024
DSA_Q_TILE = 256
DSA_K_TILE = 1024
FOX_GROUP = 512
DSA_GROUP = 1024
QK_AHEAD = 3
LOW_PASSES_PER_CHECK = 2
DEMOTE_ROWS = 128
COUNT_ROWS = 32

MAX_FIXED_SHIFT = 60.0
FOX_ZERO_EXP = 160.0
V_ROWS = HEAD_DIM + 16
LOG2E = 1.4426950408889634
INT_MIN = -2 ** 31
F32_MAGNITUDE_BITS = 0x7FFFFFFF
LOW16_MASK = 0xFFFF
LOW16_OFFSET = 32768
TOP16_MASK = -65536
BF16_MAGNITUDE_BITS = 0x7FFF
BF16_EXPONENT_BITS = 0x7F80
BF16_MIN_NORMAL_BITS = 0x0080
NORM_SLACK = 1.02
F32_MIN_NORMAL = 1.1754943508222875e-38
F32 = jnp.float32
BF16 = jnp.bfloat16
_NT = (((1,), (1,)), ((), ()))
_TN = (((0,), (0,)), ((), ()))


def _cparams(sem):
    return pltpu.CompilerParams(dimension_semantics=sem, vmem_limit_bytes=VMEM_LIMIT_BYTES)


def _const_spec(shape):
    nd = len(shape)
    return pl.BlockSpec(shape, lambda *_: (0,) * nd, pipeline_mode=pl.Buffered(1))


def _modulated(x, mod_ref, g_ref, sub):
    ms = jnp.mean(x * x, axis=-1, keepdims=True)
    y = x * lax.rsqrt(ms + NORM_EPS) * g_ref[sub:sub + 1, :]
    shift = mod_ref[0, 3 * sub:3 * sub + 1, :]
    scale = mod_ref[0, 3 * sub + 1:3 * sub + 2, :]
    return y * (1.0 + scale) + shift


def _ada_kernel(c_ref, w_ref, b_ref, o_ref):
    c = c_ref[...]
    cond = c * jax.nn.sigmoid(c)
    o_ref[...] = jnp.dot(cond, w_ref[...], preferred_element_type=F32,
                         precision=lax.Precision.HIGHEST) + b_ref[...]


def _ada_call(c, w, b):
    bsz, d = c.shape
    n = w.shape[1]
    tn = ADA_COLS
    return pl.pallas_call(
        _ada_kernel,
        grid=(n // tn,),
        in_specs=[pl.BlockSpec((bsz, d), lambda j: (0, 0)),
                  pl.BlockSpec((d, tn), lambda j: (0, j)),
                  pl.BlockSpec((1, tn), lambda j: (0, j))],
        out_specs=pl.BlockSpec((bsz, tn), lambda j: (0, j)),
        out_shape=jax.ShapeDtypeStruct((bsz, n), F32),
        compiler_params=_cparams(("arbitrary",)),
        name="ada_ln",
    )(c, w, b.reshape(1, n))


def _ffn_kernel(x_ref, mod_ref, g_ref, w1_ref, w3_ref, w2_ref, o_ref, *, sub):
    x = x_ref[0]
    h = _modulated(x, mod_ref, g_ref, sub).astype(BF16)
    a = jnp.dot(h, w1_ref[...], preferred_element_type=F32)
    b = jnp.dot(h, w3_ref[...], preferred_element_type=F32)
    act = (a * jax.nn.sigmoid(a) * b).astype(BF16)
    y = jnp.dot(act, w2_ref[...], preferred_element_type=F32)
    gate = mod_ref[0, 3 * sub + 2:3 * sub + 3, :]
    o_ref[0] = x + (0.5 * gate) * y


def _ffn_call(x, mod, g, w1, w3, w2, sub):
    bsz, s, d = x.shape
    tm = min(TOKEN_TILE, s)
    f = w1.shape[1]
    tok = pl.BlockSpec((1, tm, d), lambda b, i: (b, i, 0))
    return pl.pallas_call(
        functools.partial(_ffn_kernel, sub=sub),
        grid=(bsz, s // tm),
        in_specs=[tok,
                  pl.BlockSpec((1, 3 * N_SUBLAYERS, d), lambda b, i: (b, 0, 0)),
                  _const_spec((N_SUBLAYERS, d)),
                  _const_spec((d, f)), _const_spec((d, f)), _const_spec((f, d))],
        out_specs=tok,
        out_shape=jax.ShapeDtypeStruct(x.shape, F32),
        compiler_params=_cparams(("parallel", "parallel")),
        name=f"ffn{sub}",
    )(x, mod, g, w1, w3, w2)


def _split3(v):
    hi = v.astype(BF16)
    r = v - hi.astype(F32)
    mid = r.astype(BF16)
    lo = (r - mid.astype(F32)).astype(BF16)
    return hi, mid, lo


def _group_norm(z, gmat, gain, group):
    ssq = jnp.dot((z * z).astype(BF16), gmat, preferred_element_type=F32)
    return z * lax.rsqrt(ssq * (1.0 / group) + NORM_EPS) * gain


def _rope(z, cos, sin, half, period):
    width = z.shape[1]
    lane = lax.broadcasted_iota(jnp.int32, z.shape, 1) % period
    up = pltpu.roll(z, width - half, axis=1)
    dn = pltpu.roll(z, half, axis=1)
    return z * cos + jnp.where(lane < half, up, dn) * sin


def _mix_in_kernel(x_ref, mod_ref, g_ref, pos_ref, wa_ref, wb_ref, wt_ref, ones_ref, gmat_ref, gains_ref, freq_ref,
                   fbias_ref, fq_ref, fk_ref, dq_ref, dk4_ref, ik8_ref, iq_ref, faug_ref, fvt_ref, dvt_ref, iwt_ref,
                   carry_ref):
    @pl.when(pl.program_id(1) == 0)
    def _():
        carry_ref[...] = jnp.zeros_like(carry_ref)

    x = x_ref[0]
    tm = x.shape[0]
    h = _modulated(x, mod_ref, g_ref, 1).astype(BF16)
    za = jnp.dot(h, wa_ref[...], preferred_element_type=F32)
    zb = jnp.dot(h, wb_ref[...], preferred_element_type=F32)
    zt = lax.dot_general(wt_ref[...], h, _NT, preferred_element_type=F32)
    n_v = FOX_HEADS * V_ROWS
    fvt_ref[0] = (zt[0:n_v] + jnp.concatenate([ones_ref[...]] * (tm // LANES), axis=1)).astype(BF16)
    dvt_ref[0] = (zt[n_v:n_v + V_ROWS]
                  + jnp.concatenate([ones_ref[0:V_ROWS, :]] * (tm // LANES), axis=1)).astype(BF16)
    iwt_ref[0] = zt[n_v + V_ROWS:, :] * ((IDX_HEADS * IDX_DIM) ** -0.5)

    gmat = gmat_ref[...]
    scale = HEAD_DIM ** -0.5 * LOG2E
    fq = _group_norm(za[:, 0:FOX_WIDTH], gmat, gains_ref[0:1, :], HEAD_DIM) * scale
    fk = _group_norm(za[:, FOX_WIDTH:2 * FOX_WIDTH], gmat, gains_ref[1:2, :], HEAD_DIM)
    fq_ref[0] = fq.astype(BF16)
    fk_ref[0] = fk.astype(BF16)

    pos = pos_ref[0]
    ang_a = pos * freq_ref[0:1, :]
    ang_i = pos * freq_ref[1:2, :]
    cos_a, sin_a = jnp.cos(ang_a), jnp.sin(ang_a)
    cos_i, sin_i = jnp.cos(ang_i), jnp.sin(ang_i)
    rot_a = HEAD_DIM // ROPE_FRACTION // 2
    rot_i = IDX_DIM // ROPE_FRACTION // 2

    dq = _group_norm(za[:, 2 * FOX_WIDTH:2 * FOX_WIDTH + DSA_WIDTH], gmat, gains_ref[2:3, :], HEAD_DIM)
    dq = _rope(dq, jnp.concatenate([cos_a] * 4, axis=1), jnp.concatenate([sin_a] * 4, axis=1), rot_a, HEAD_DIM)
    dq_ref[0] = (dq * scale).astype(BF16)

    dk4 = _group_norm(zb[:, 0:DK_TILED], gmat[0:DK_TILED, 0:DK_TILED], gains_ref[3:4, 0:DK_TILED], HEAD_DIM)
    dk4 = _rope(dk4, jnp.concatenate([cos_a] * 2, axis=1), jnp.concatenate([sin_a] * 2, axis=1), rot_a, HEAD_DIM)
    dk4_ref[0] = dk4.astype(BF16)

    cos_i2 = jnp.concatenate([cos_i] * 2, axis=1)
    sin_i2 = jnp.concatenate([sin_i] * 2, axis=1)
    o_ik, o_iq, o_sm = DK_TILED, DK_TILED + IDX_WIDTH, DK_TILED + 2 * IDX_WIDTH
    ik8_ref[0] = _rope(zb[:, o_ik:o_iq], cos_i2, sin_i2, rot_i, IDX_DIM).astype(BF16)
    iq_ref[0] = _rope(zb[:, o_iq:o_sm], cos_i2, sin_i2, rot_i, IDX_DIM).astype(BF16)

    sm = zb[:, o_sm:o_sm + LANES]
    v = sm + fbias_ref[...]
    logf = jnp.minimum(v, 0.0) - jnp.log1p(jnp.exp(-jnp.abs(v)))
    row = lax.broadcasted_iota(jnp.int32, (tm, tm), 0)
    col = lax.broadcasted_iota(jnp.int32, (tm, tm), 1)
    tri = jnp.where(row >= col, 1.0, 0.0).astype(BF16)
    hi, mid, lo = _split3(logf)
    csum = (jnp.dot(tri, hi, preferred_element_type=F32) + jnp.dot(tri, mid, preferred_element_type=F32)
            + jnp.dot(tri, lo, preferred_element_type=F32)) + carry_ref[...]
    carry_ref[...] = csum[tm - 1:tm, :]
    lane = lax.broadcasted_iota(jnp.int32, sm.shape, 1)

    nf = -csum * LOG2E
    t0 = nf.astype(BF16).astype(F32)
    r1 = nf - t0
    t1 = r1.astype(BF16).astype(F32)
    t2 = r1 - t1
    aug = jnp.where(lane < FOX_HEADS, t0,
                    jnp.where(lane < 2 * FOX_HEADS, pltpu.roll(t1, FOX_HEADS, axis=1),
                              jnp.where(lane < 3 * FOX_HEADS, pltpu.roll(t2, 2 * FOX_HEADS, axis=1),
                                        jnp.where(lane < 6 * FOX_HEADS, 1.0, 0.0))))
    faug_ref[0] = aug.astype(BF16)


def _mix_in_call(x, mod, g, pos, wa, wb, wt, ones, gmat, gains, freq, fbias):
    bsz, s, d = x.shape
    tm = min(TOKEN_TILE, s)

    def tok(w):
        return pl.BlockSpec((1, tm, w), lambda b, i: (b, i, 0))

    def tokt(r):
        return pl.BlockSpec((1, r, tm), lambda b, i: (b, 0, i))

    widths = (FOX_WIDTH, FOX_WIDTH, DSA_WIDTH, DK_TILED, IDX_WIDTH, IDX_WIDTH, LANES)
    rows = ((FOX_HEADS * V_ROWS, BF16), (V_ROWS, BF16), (2 * IDX_HEADS, F32))
    out_shapes = [jax.ShapeDtypeStruct((bsz, s, w), BF16) for w in widths]
    out_shapes += [jax.ShapeDtypeStruct((bsz, r, s), dt) for r, dt in rows]
    return pl.pallas_call(
        _mix_in_kernel,
        grid=(bsz, s // tm),
        in_specs=[tok(d),
                  pl.BlockSpec((1, 3 * N_SUBLAYERS, d), lambda b, i: (b, 0, 0)),
                  _const_spec((N_SUBLAYERS, d)),
                  tok(LANES),
                  _const_spec(wa.shape), _const_spec(wb.shape), _const_spec(wt.shape), _const_spec(ones.shape),
                  _const_spec(gmat.shape), _const_spec(gains.shape), _const_spec(freq.shape), _const_spec(fbias.shape)],
        out_specs=[tok(w) for w in widths] + [tokt(r) for r, _ in rows],
        out_shape=out_shapes,
        scratch_shapes=[pltpu.VMEM((1, LANES), F32)],
        compiler_params=_cparams(("arbitrary", "arbitrary")),
        name="mix_in",
    )(x, mod, g, pos, wa, wb, wt, ones, gmat, gains, freq, fbias)


def _fox_kernel(q_ref, qaug_ref, k_ref, faug_ref, vt_ref, hm_ref, fsel_ref, bound_ref, o_ref, qf_ref, m_ref, acc_ref):
    i = pl.program_id(2)
    tq = q_ref.shape[1]
    tk = tq
    q = q_ref[0]
    own = qaug_ref[0].astype(F32)
    for a in range(2):
        sel_row = fsel_ref[0, a:a + 1, :].astype(F32)
        plus_f = pltpu.roll(-(own * sel_row), 3 * FOX_HEADS, axis=1)
        qf_ref[a * tq:(a + 1) * tq, 0:LANES] = q * hm_ref[a:a + 1, :]
        qf_ref[a * tq:(a + 1) * tq, LANES:2 * LANES] = (sel_row + plus_f).astype(BF16)
    acc_ref[...] = jnp.zeros_like(acc_ref)
    bound = jnp.max(bound_ref[...])
    bounded = bound <= MAX_FIXED_SHIFT

    def block_bounded(j, masked):
        start = pl.multiple_of(j * tk, tk)
        kf = jnp.concatenate([k_ref[0, pl.ds(start, tk), :], faug_ref[0, pl.ds(start, tk), :]], axis=1)
        grp = min(FOX_GROUP, tq)
        for g in range(2 * tq // grp):
            rows = min(tk, (g * grp) % tq + grp) if masked else tk
            sl = slice(g * grp, (g + 1) * grp)
            s = lax.dot_general(kf[:rows], qf_ref[sl, :], _NT, preferred_element_type=F32) - bound
            if masked:
                krow = lax.broadcasted_iota(jnp.int32, s.shape, 0)
                qcol = lax.broadcasted_iota(jnp.int32, s.shape, 1) + (g * grp) % tq
                s = jnp.where(krow <= qcol, s, -jnp.inf)
            a = (g * grp) // tq
            vt = vt_ref[0, a * V_ROWS:(a + 1) * V_ROWS, pl.ds(start, rows)]
            acc_ref[:, sl] += jnp.dot(vt, jnp.exp2(s).astype(BF16), preferred_element_type=F32)

    def block(j, masked):
        start = pl.multiple_of(j * tk, tk)
        kf = jnp.concatenate([k_ref[0, pl.ds(start, tk), :], faug_ref[0, pl.ds(start, tk), :]], axis=1)
        grp = min(FOX_GROUP, tq)
        n_groups = 2 * tq // grp

        def rows(g):
            return min(tk, (g * grp) % tq + grp) if masked else tk

        def scores(g):
            s = lax.dot_general(kf[:rows(g)], qf_ref[g * grp:(g + 1) * grp, :], _NT,
                                preferred_element_type=F32)
            if masked:
                krow = lax.broadcasted_iota(jnp.int32, s.shape, 0)
                qcol = lax.broadcasted_iota(jnp.int32, s.shape, 1) + (g * grp) % tq
                s = jnp.where(krow <= qcol, s, -jnp.inf)
            return s

        ahead = [scores(g) for g in range(min(QK_AHEAD, n_groups))]
        for g in range(n_groups):
            s = ahead.pop(0)
            if g + QK_AHEAD < n_groups:
                ahead.append(scores(g + QK_AHEAD))
            sl = slice(g * grp, (g + 1) * grp)
            m_old = m_ref[:, sl]
            m_new = jnp.maximum(m_old, jnp.max(s, axis=0, keepdims=True))
            p = jnp.exp2((s - m_new).astype(BF16))
            alpha = jnp.exp2(m_old - m_new)
            m_ref[:, sl] = m_new
            a = (g * grp) // tq
            vt = vt_ref[0, a * V_ROWS:(a + 1) * V_ROWS, pl.ds(start, rows(g))]
            acc_ref[:, sl] = alpha * acc_ref[:, sl] + jnp.dot(vt, p, preferred_element_type=F32)

    def nf_of(rows16, a):
        return jnp.sum(rows16 * fsel_ref[0, a:a + 1, :].astype(F32), axis=1, keepdims=True)

    def sweep(block_fn, slack):
        block_fn(i, True)

        def older(jj, c):
            j = i - 1 - jj
            last = pl.multiple_of(j * tk + tk - 16, 16)
            tail_rows = faug_ref[0, pl.ds(last, 16), :].astype(F32)
            worst = -jnp.inf
            for a in range(2):
                worst = jnp.maximum(worst, jnp.max(nf_of(tail_rows, a)) - jnp.min(nf_of(own[0:16], a)))

            @pl.when(worst + slack > -FOX_ZERO_EXP)
            def _():
                block_fn(j, False)
            return c

        lax.fori_loop(0, i, older, 0)

    @pl.when(bounded)
    def _():
        sweep(block_bounded, 0.0)

    @pl.when(jnp.logical_not(bounded))
    def _():
        m_ref[...] = jnp.full_like(m_ref, -jnp.inf)
        sweep(block, 2.0 * bound)

    out = acc_ref[0:HEAD_DIM, :] / acc_ref[HEAD_DIM:HEAD_DIM + 1, :]
    for a in range(2):
        o_ref[0, a * HEAD_DIM:(a + 1) * HEAD_DIM, :] = out[:, a * tq:(a + 1) * tq].astype(BF16)


def _fox_call(fq, fk, faug, fvt, hm2, fsel, bound):
    bsz, s, _ = fq.shape
    tq = min(FOX_TILE, s)
    pairs = FOX_HEADS // 2
    return pl.pallas_call(
        _fox_kernel,
        grid=(bsz, pairs, s // tq),
        in_specs=[pl.BlockSpec((1, tq, LANES), lambda b, h, i: (b, i, h)),
                  pl.BlockSpec((1, tq, LANES), lambda b, h, i: (b, i, 0)),
                  pl.BlockSpec((1, s, LANES), lambda b, h, i: (b, 0, h)),
                  pl.BlockSpec((1, s, LANES), lambda b, h, i: (b, 0, 0)),
                  pl.BlockSpec((1, 2 * V_ROWS, s), lambda b, h, i: (b, h, 0)),
                  pl.BlockSpec((2, LANES), lambda b, h, i: (0, 0)),
                  pl.BlockSpec((1, 2, LANES), lambda b, h, i: (h, 0, 0)),
                  pl.BlockSpec((1, LANES), lambda b, h, i: (0, 0))],
        out_specs=pl.BlockSpec((1, LANES, tq), lambda b, h, i: (b, h, i)),
        out_shape=jax.ShapeDtypeStruct((bsz, FOX_WIDTH, s), BF16),
        scratch_shapes=[pltpu.VMEM((2 * tq, 2 * LANES), BF16),
                        pltpu.VMEM((1, 2 * tq), F32),
                        pltpu.VMEM((V_ROWS, 2 * tq), F32)],
        compiler_params=_cparams(("parallel", "parallel", "arbitrary")),
        name="fox_attn",
    )(fq, faug, fk, faug, fvt, hm2, fsel, bound)


def _dsa_kernel(dq_ref, iq_ref, wt_ref, dk4_ref, dvt_ref, ik8_ref, hm8_ref, hm4_ref, bound_ref, o_ref,
                iq8_ref, q8_ref, keys_ref, hi_ref, lo_ref, m_ref, acc_ref, *, top_k):
    i = pl.program_id(1)
    tq = dq_ref.shape[1]
    s_len = dk4_ref.shape[1]
    tk = min(DSA_K_TILE, s_len)
    half_tk = tk // 2
    q_start = i * tq
    rem = (q_start + tq) % tk
    n_full = (q_start + tq) // tk + jnp.where(rem > half_tk, 1, 0)
    has_tail = (rem > 0) & (rem <= half_tk)

    def sweep(fn, init):
        carry = lax.fori_loop(0, n_full, lambda j, c: fn(pl.multiple_of(j * tk, tk), tk, c), init)
        return lax.cond(has_tail, lambda c: fn(pl.multiple_of(n_full * tk, tk), half_tk, c), lambda c: c, carry)

    iq = iq_ref[0]
    dq = dq_ref[0]
    half = dq.shape[1] // 2
    for h in range(IDX_HEADS):
        iq8_ref[h * tq:(h + 1) * tq, :] = iq * hm8_ref[h:h + 1, :]
        q8_ref[h * tq:(h + 1) * tq, :] = dq[:, (h // 4) * half:(h // 4 + 1) * half] * hm4_ref[h % 4:h % 4 + 1, :]

    adm_end = q_start + (lax.broadcasted_iota(jnp.int32, (1, tq), 1) // CHUNK + 1) * CHUNK
    wt = wt_ref[0, :, pl.ds(pl.multiple_of(q_start, tq), tq)]

    def score_block(start, size, c):
        d = lax.dot_general(ik8_ref[0, pl.ds(start, size), :], iq8_ref[...], _NT, preferred_element_type=F32)
        sc = jnp.zeros((size, tq), F32)
        for h in range(IDX_HEADS):
            sc = sc + jnp.maximum(d[:, h * tq:(h + 1) * tq], 0.0) * wt[IDX_HEADS + h:IDX_HEADS + h + 1, :]
        sc = jnp.where(jnp.abs(sc) < F32_MIN_NORMAL, 0.0, sc)
        bits = lax.bitcast_convert_type(sc, jnp.int32)
        key = bits ^ ((bits >> 31) & F32_MAGNITUDE_BITS)
        adm = lax.broadcasted_iota(jnp.int32, (size, tq), 0) + start < adm_end
        keys_ref[pl.ds(start, size), :] = jnp.where(adm, key, INT_MIN)
        top = lax.bitcast_convert_type(bits & TOP16_MASK, F32)
        hi_ref[pl.ds(start, size), :] = jnp.where(adm, top, jnp.nan).astype(BF16)
        return c

    sweep(score_block, 0)

    def count_hi(cand):
        def body(start, size, acc):
            hit = jnp.where(hi_ref[pl.ds(start, size), :] >= cand, jnp.ones((), BF16), jnp.zeros((), BF16))
            part = hit[0:COUNT_ROWS]
            for r in range(1, size // COUNT_ROWS):
                part = part + hit[r * COUNT_ROWS:(r + 1) * COUNT_ROWS]
            return acc + part.astype(F32)
        return jnp.sum(sweep(body, jnp.zeros((COUNT_ROWS, tq), F32)), axis=0, keepdims=True)

    def count_lo(cand, n_above_bucket):
        c16 = ((cand & LOW16_MASK) - LOW16_OFFSET).astype(jnp.int16)

        def body(start, size, acc):
            hit = jnp.where(lo_ref[pl.ds(start, size), :] >= c16, jnp.ones((), jnp.int16), jnp.zeros((), jnp.int16))
            part = hit[0:COUNT_ROWS]
            for r in range(1, size // COUNT_ROWS):
                part = part + hit[r * COUNT_ROWS:(r + 1) * COUNT_ROWS]
            return acc + part.astype(jnp.int32)
        hits = jnp.sum(sweep(body, jnp.zeros((COUNT_ROWS, tq), jnp.int32)), axis=0, keepdims=True)
        return n_above_bucket + hits

    def step(cand, cnt, state):
        thr, done, n_ge, n_gt = state
        live = done == 0
        ok = (cnt >= top_k) & live
        return (jnp.where(ok, cand, thr), jnp.where(ok & (cnt == top_k), 1, done),
                jnp.where(ok, cnt, n_ge), jnp.where((cnt < top_k) & live, cnt, n_gt))

    def bisect_hi(it, state):
        cand = state[0] + jnp.left_shift(jnp.int32(1), 15 - it)
        pattern = jnp.where(cand >= 0, cand, cand ^ BF16_MAGNITUDE_BITS)
        pattern = jnp.where((pattern & BF16_EXPONENT_BITS) == 0, jnp.where(cand > 0, BF16_MIN_NORMAL_BITS, 0), pattern)
        cand_f = lax.bitcast_convert_type(jnp.left_shift(pattern, 16), F32).astype(BF16)
        return step(cand, count_hi(cand_f).astype(jnp.int32), state)

    def bisect_lo(carry):
        g, state = carry
        for u in range(LOW_PASSES_PER_CHECK):
            cand = state[0] + jnp.left_shift(jnp.int32(1), 15 - (g * LOW_PASSES_PER_CHECK + u))
            state = step(cand, count_lo(cand, n_bucket), state)
        return g + 1, state

    state = (jnp.full((1, tq), -2 ** 15, jnp.int32), jnp.where(adm_end <= top_k, 1, 0), adm_end,
             jnp.zeros((1, tq), jnp.int32))
    t16, done, n_ge, n_gt = lax.fori_loop(0, 16, bisect_hi, state)
    base = jnp.left_shift(t16, 16)
    n_bucket = n_gt

    def mark_bucket(start, size, c):
        off = keys_ref[pl.ds(start, size), :] - base
        inside = lax.bitcast_convert_type(off, jnp.uint32) < jnp.uint32(LOW16_MASK + 1)
        lo_ref[pl.ds(start, size), :] = jnp.where(inside, off - LOW16_OFFSET, -LOW16_OFFSET).astype(jnp.int16)
        return c

    sweep(mark_bucket, 0)
    above = count_lo(base + 1, n_bucket)
    exact = (done == 0) & (above == top_k)
    short = (done == 0) & (above < top_k)
    state = (jnp.where(exact, base + 1, base), jnp.where(above <= top_k, 1, done),
             jnp.where(exact, above, n_ge), jnp.where(short, above, n_gt))
    _, (thr, _, n_ge, n_gt) = lax.while_loop(
        lambda c: (c[0] < 16 // LOW_PASSES_PER_CHECK) & (jnp.min(c[1][1]) == 0), bisect_lo, (jnp.int32(0), state))
    thr = jnp.maximum(thr, INT_MIN + 1)

    @pl.when(jnp.max(n_ge) > top_k)
    def _():
        need = (top_k - n_gt).astype(F32)

        r2 = lax.broadcasted_iota(jnp.int32, (DEMOTE_ROWS, DEMOTE_ROWS), 0)
        c2 = lax.broadcasted_iota(jnp.int32, (DEMOTE_ROWS, DEMOTE_ROWS), 1)
        lower = jnp.where(c2 < r2, 1.0, 0.0).astype(BF16)

        def demote(start, size, seen):
            chunks = []
            for c0 in range(0, size, DEMOTE_ROWS):
                kblk = keys_ref[pl.ds(start + c0, DEMOTE_ROWS), :]
                eqf = jnp.where(kblk == thr, 1.0, 0.0)
                chunks.append((c0, kblk, eqf, jnp.dot(lower, eqf.astype(BF16), preferred_element_type=F32),
                               jnp.sum(eqf, axis=0, keepdims=True)))
            for c0, kblk, eqf, within, total in chunks:
                drop = (eqf > 0.0) & (within + seen >= need)
                keys_ref[pl.ds(start + c0, DEMOTE_ROWS), :] = jnp.where(drop, thr - 1, kblk)
                seen = seen + total
            return seen

        sweep(demote, jnp.zeros((1, tq), F32))

    acc_ref[...] = jnp.zeros_like(acc_ref)
    grp = min(DSA_GROUP, DSA_HEADS * tq)
    n_groups = DSA_HEADS * tq // grp
    bound = jnp.max(bound_ref[...])

    def attend_bounded(start, size, c):
        bias = jnp.where(keys_ref[pl.ds(start, size), :] >= thr, -bound, -jnp.inf)
        bias = jnp.concatenate([bias] * (grp // tq), axis=1)
        kb = dk4_ref[0, pl.ds(start, size), :]
        vt = dvt_ref[0, :, pl.ds(start, size)]
        for g in range(n_groups):
            sl = slice(g * grp, (g + 1) * grp)
            s = lax.dot_general(kb, q8_ref[sl, :], _NT, preferred_element_type=F32) + bias
            acc_ref[:, sl] += jnp.dot(vt, jnp.exp2(s).astype(BF16), preferred_element_type=F32)
        return c

    def attend(start, size, c):
        bias = jnp.where(keys_ref[pl.ds(start, size), :] >= thr, 0.0, -jnp.inf)
        bias = jnp.concatenate([bias] * (grp // tq), axis=1)
        kb = dk4_ref[0, pl.ds(start, size), :]
        vt = dvt_ref[0, :, pl.ds(start, size)]

        def scores(g):
            return lax.dot_general(kb, q8_ref[g * grp:(g + 1) * grp, :], _NT,
                                   preferred_element_type=F32) + bias

        ahead = [scores(g) for g in range(min(QK_AHEAD, n_groups))]
        for g in range(n_groups):
            s = ahead.pop(0)
            if g + QK_AHEAD < n_groups:
                ahead.append(scores(g + QK_AHEAD))
            sl = slice(g * grp, (g + 1) * grp)
            m_old = m_ref[:, sl]
            m_new = jnp.maximum(m_old, jnp.max(s, axis=0, keepdims=True))
            m_safe = jnp.where(m_new == -jnp.inf, 0.0, m_new)
            p = jnp.exp2((s - m_safe).astype(BF16))
            alpha = jnp.exp2(m_old - m_safe)
            m_ref[:, sl] = m_new
            acc_ref[:, sl] = alpha * acc_ref[:, sl] + jnp.dot(vt, p, preferred_element_type=F32)
        return c

    @pl.when(bound <= MAX_FIXED_SHIFT)
    def _():
        sweep(attend_bounded, 0)

    @pl.when(bound > MAX_FIXED_SHIFT)
    def _():
        m_ref[...] = jnp.full_like(m_ref, -jnp.inf)
        sweep(attend, 0)

    out = acc_ref[0:HEAD_DIM, :] / acc_ref[HEAD_DIM:HEAD_DIM + 1, :]
    for h in range(DSA_HEADS):
        o_ref[0, h * HEAD_DIM:(h + 1) * HEAD_DIM, :] = out[:, h * tq:(h + 1) * tq].astype(BF16)


def _dsa_call(dq, iq, wt, dk4, dvt, ik8, hm8, hm4, bound, top_k):
    bsz, s, _ = dq.shape
    tq = min(DSA_Q_TILE, s)

    def tok(w):
        return pl.BlockSpec((1, tq, w), lambda b, i: (b, i, 0))

    def seq(w):
        return pl.BlockSpec((1, s, w), lambda b, i: (b, 0, 0))

    def seqt(r):
        return pl.BlockSpec((1, r, s), lambda b, i: (b, 0, 0))

    return pl.pallas_call(
        functools.partial(_dsa_kernel, top_k=top_k),
        grid=(bsz, s // tq),
        in_specs=[tok(DSA_WIDTH), tok(IDX_WIDTH), seqt(2 * IDX_HEADS), seq(DK_TILED), seqt(V_ROWS), seq(IDX_WIDTH),
                  pl.BlockSpec(hm8.shape, lambda b, i: (0, 0)), pl.BlockSpec(hm4.shape, lambda b, i: (0, 0)),
                  pl.BlockSpec((1, LANES), lambda b, i: (0, 0))],
        out_specs=pl.BlockSpec((1, DSA_WIDTH, tq), lambda b, i: (b, 0, i)),
        out_shape=jax.ShapeDtypeStruct((bsz, DSA_WIDTH, s), BF16),
        scratch_shapes=[pltpu.VMEM((IDX_HEADS * tq, IDX_WIDTH), BF16), pltpu.VMEM((DSA_HEADS * tq, DK_TILED), BF16),
                        pltpu.VMEM((s, tq), jnp.int32), pltpu.VMEM((s, tq), BF16),
                        pltpu.VMEM((s, tq), jnp.int16),
                        pltpu.VMEM((1, DSA_HEADS * tq), F32),
                        pltpu.VMEM((V_ROWS, DSA_HEADS * tq), F32)],
        compiler_params=_cparams(("parallel", "arbitrary")),
        name="dsa_attn",
    )(dq, iq, wt, dk4, dvt, ik8, hm8, hm4, bound)


def _merge_kernel(x_ref, mod_ref, g_ref, ya_ref, yb_ref, wg_ref, wfa_ref, wfb_ref, wo_ref, o_ref):
    x = x_ref[0]
    d = x.shape[1]
    h = _modulated(x, mod_ref, g_ref, 1).astype(BF16)
    zg = jnp.dot(h, wg_ref[...], preferred_element_type=F32)
    pa = lax.dot_general(ya_ref[0], wfa_ref[...], _TN, preferred_element_type=F32)
    pb = lax.dot_general(yb_ref[0], wfb_ref[...], _TN, preferred_element_type=F32)
    merged = jax.nn.sigmoid(zg[:, :d]) * pa + jax.nn.sigmoid(zg[:, d:]) * pb
    y = jnp.dot(merged.astype(BF16), wo_ref[...], preferred_element_type=F32)
    o_ref[0] = x + mod_ref[0, 5:6, :] * y


def _merge_call(x, mod, g, ya, yb, wg, wfa, wfb, wo):
    bsz, s, d = x.shape
    tm = min(TOKEN_TILE, s)

    def tok(w):
        return pl.BlockSpec((1, tm, w), lambda b, i: (b, i, 0))

    def tokt(w):
        return pl.BlockSpec((1, w, tm), lambda b, i: (b, 0, i))

    return pl.pallas_call(
        _merge_kernel,
        grid=(bsz, s // tm),
        in_specs=[tok(d),
                  pl.BlockSpec((1, 3 * N_SUBLAYERS, d), lambda b, i: (b, 0, 0)),
                  _const_spec((N_SUBLAYERS, d)),
                  tokt(FOX_WIDTH), tokt(DSA_WIDTH),
                  _const_spec(wg.shape), _const_spec(wfa.shape), _const_spec(wfb.shape), _const_spec(wo.shape)],
        out_specs=tok(d),
        out_shape=jax.ShapeDtypeStruct(x.shape, F32),
        compiler_params=_cparams(("parallel", "parallel")),
        name="merge_out",
    )(x, mod, g, ya, yb, wg, wfa, wfb, wo)


def _head_mask(n_heads, width):
    return jnp.asarray(np.kron(np.eye(n_heads), np.ones((1, width))), BF16)


def _rope_freq_row(rot_dim, period):
    inv_freq = ROPE_THETA ** (-jnp.arange(0, rot_dim, 2, dtype=F32) / rot_dim)
    half = rot_dim // 2
    head = jnp.concatenate([-inv_freq, inv_freq, jnp.zeros((period - 2 * half,), F32)])
    return jnp.tile(head, LANES // period)


def kernel(x, c, positions, ada_w, ada_b, norm_g, ffn1_w1, ffn1_w3, ffn1_w2, w_in, fox_f_bias, fox_qk_g, dsa_qk_g,
           w_br_fox, w_br_dsa, w_out, ffn2_w1, ffn2_w3, ffn2_w2):
    bsz, s, d = x.shape
    top_k = min(TOPK_MAX, s // 4)
    depth = ada_w.shape[0]
    pos = jnp.broadcast_to(positions.astype(F32)[:, :, None], (bsz, s, LANES))
    freq = jnp.stack([_rope_freq_row(HEAD_DIM // ROPE_FRACTION, HEAD_DIM),
                      _rope_freq_row(IDX_DIM // ROPE_FRACTION, IDX_DIM)])
    gmat = jnp.asarray(np.kron(np.eye(FOX_HEADS), np.ones((HEAD_DIM, HEAD_DIM))), BF16)
    hm2 = _head_mask(2, HEAD_DIM)
    hm4 = _head_mask(4, HEAD_DIM)
    hm8 = _head_mask(IDX_HEADS, IDX_DIM)
    fsel = jnp.asarray(np.tile(np.eye(FOX_HEADS), (1, LANES // FOX_HEADS))
                       * (np.arange(LANES) < 3 * FOX_HEADS), BF16).reshape(FOX_HEADS // 2, 2, LANES)
    o_fq, o_fk, o_fv = 0, FOX_WIDTH, 2 * FOX_WIDTH
    o_ff = 3 * FOX_WIDTH
    o_dq = o_ff + FOX_HEADS
    o_dk = o_dq + DSA_WIDTH
    o_dv = o_dk + HEAD_DIM
    o_iq = o_dv + HEAD_DIM
    o_ik = o_iq + IDX_WIDTH
    o_iw = o_ik + IDX_DIM
    o_ga = o_iw + IDX_HEADS

    for l in range(depth):
        mod = _ada_call(c, ada_w[l], ada_b[l]).reshape(bsz, 3 * N_SUBLAYERS, d)
        g = norm_g[l]
        x = _ffn_call(x, mod, g, ffn1_w1[l].astype(BF16), ffn1_w3[l].astype(BF16), ffn1_w2[l].astype(BF16), 0)

        w = w_in[l]
        wa = jnp.concatenate([w[:, o_fq:o_fq + 2 * FOX_WIDTH], w[:, o_dq:o_dq + DSA_WIDTH]], axis=1).astype(BF16)
        pad = jnp.zeros((d, LANES - FOX_HEADS), F32)
        wb = jnp.concatenate([jnp.tile(w[:, o_dk:o_dk + HEAD_DIM], (1, 4)),
                              jnp.tile(w[:, o_ik:o_ik + IDX_DIM], (1, IDX_HEADS)),
                              w[:, o_iq:o_iq + IDX_WIDTH],
                              w[:, o_ff:o_ff + FOX_HEADS], pad], axis=1).astype(BF16)
        zrows = jnp.zeros((FOX_HEADS, V_ROWS - HEAD_DIM, d), F32)
        wfv = jnp.transpose(w[:, o_fv:o_fv + FOX_WIDTH]).reshape(FOX_HEADS, HEAD_DIM, d)
        wt = jnp.concatenate([jnp.concatenate([wfv, zrows], axis=1).reshape(FOX_HEADS * V_ROWS, d),
                              jnp.transpose(w[:, o_dv:o_dv + HEAD_DIM]), zrows[0],
                              jnp.zeros((IDX_HEADS, d), F32), jnp.transpose(w[:, o_iw:o_iw + IDX_HEADS])],
                             axis=0).astype(BF16)
        ones = jnp.asarray(np.tile((np.arange(FOX_HEADS * V_ROWS) % V_ROWS >= HEAD_DIM)[:, None], (1, LANES)), F32)
        wg = w[:, o_ga:o_ga + 2 * d].astype(BF16)
        gains = jnp.stack([jnp.tile(fox_qk_g[l, 0], FOX_HEADS), jnp.tile(fox_qk_g[l, 1], FOX_HEADS),
                           jnp.tile(dsa_qk_g[l, 0], DSA_HEADS), jnp.tile(dsa_qk_g[l, 1], DSA_HEADS)]).astype(F32)
        fbias = jnp.concatenate([fox_f_bias[l].astype(F32), jnp.zeros((LANES - FOX_HEADS,), F32)]).reshape(1, LANES)

        fq, fk, dq, dk4, ik8, iq, faug, fvt, dvt, iwt = _mix_in_call(x, mod, g, pos, wa, wb, wt, ones, gmat, gains,
                                                                       freq, fbias)
        unit = jnp.full((1, LANES), HEAD_DIM ** 0.5 * LOG2E * NORM_SLACK, F32)
        fox_bound = unit * (jnp.max(jnp.abs(fox_qk_g[l, 0])) * jnp.max(jnp.abs(fox_qk_g[l, 1])))
        dsa_bound = unit * (jnp.max(jnp.abs(dsa_qk_g[l, 0])) * jnp.max(jnp.abs(dsa_qk_g[l, 1])))
        yat = _fox_call(fq, fk, faug, fvt, hm2, fsel, fox_bound)
        ybt = _dsa_call(dq, iq, iwt, dk4, dvt, ik8, hm8, hm4, dsa_bound, top_k)
        x = _merge_call(x, mod, g, yat, ybt, wg, w_br_fox[l].astype(BF16), w_br_dsa[l].astype(BF16),
                        w_out[l].astype(BF16))
        x = _ffn_call(x, mod, g, ffn2_w1[l].astype(BF16), ffn2_w3[l].astype(BF16), ffn2_w2[l].astype(BF16), 2)
    return x
```

```python
import functools

import numpy as np
import jax
import jax.numpy as jnp
from jax import lax
from jax.experimental import pallas as pl
from jax.experimental.pallas import tpu as pltpu

CHUNK = 64
HEAD_DIM = 64
FOX_HEADS = 8
DSA_HEADS = 8
IDX_HEADS = 8
IDX_DIM = 32
TOPK_MAX = 256
ROPE_THETA = 500000.0
ROPE_FRACTION = 4
N_SUBLAYERS = 3
NORM_EPS = 1e-6
FOX_WIDTH = FOX_HEADS * HEAD_DIM
DSA_WIDTH = DSA_HEADS * HEAD_DIM
IDX_WIDTH = IDX_HEADS * IDX_DIM
DK_TILED = 4 * HEAD_DIM

LANES = 128
VMEM_LIMIT_BYTES = 56 * 1024 * 1024

ADA_COLS = 1152
TOKEN_TILE = 512
FOX_TILE = 1024
DSA_Q_TILE = 256
DSA_K_TILE = 1024
FOX_GROUP = 512
DSA_GROUP = 1024
QK_AHEAD = 3
LOW_PASSES_PER_CHECK = 2
DEMOTE_ROWS = 128
COUNT_ROWS = 32

MAX_FIXED_SHIFT = 60.0
FOX_ZERO_EXP = 160.0
V_ROWS = HEAD_DIM + 16
LOG2E = 1.4426950408889634
INT_MIN = -2 ** 31
F32_MAGNITUDE_BITS = 0x7FFFFFFF
LOW16_MASK = 0xFFFF
LOW16_OFFSET = 32768
TOP16_MASK = -65536
BF16_MAGNITUDE_BITS = 0x7FFF
BF16_EXPONENT_BITS = 0x7F80
BF16_MIN_NORMAL_BITS = 0x0080
NORM_SLACK = 1.02
F32_MIN_NORMAL = 1.1754943508222875e-38
F32 = jnp.float32
BF16 = jnp.bfloat16
_NT = (((1,), (1,)), ((), ()))
_TN = (((0,), (0,)), ((), ()))


def _cparams(sem):
    return pltpu.CompilerParams(dimension_semantics=sem, vmem_limit_bytes=VMEM_LIMIT_BYTES)


def _const_spec(shape):
    nd = len(shape)
    return pl.BlockSpec(shape, lambda *_: (0,) * nd, pipeline_mode=pl.Buffered(1))


def _modulated(x, mod_ref, g_ref, sub):
    ms = jnp.mean(x * x, axis=-1, keepdims=True)
    y = x * lax.rsqrt(ms + NORM_EPS) * g_ref[sub:sub + 1, :]
    shift = mod_ref[0, 3 * sub:3 * sub + 1, :]
    scale = mod_ref[0, 3 * sub + 1:3 * sub + 2, :]
    return y * (1.0 + scale) + shift


def _ada_kernel(c_ref, w_ref, b_ref, o_ref):
    c = c_ref[...]
    cond = c * jax.nn.sigmoid(c)
    o_ref[...] = jnp.dot(cond, w_ref[...], preferred_element_type=F32,
                         precision=lax.Precision.HIGHEST) + b_ref[...]


def _ada_call(c, w, b):
    bsz, d = c.shape
    n = w.shape[1]
    tn = ADA_COLS
    return pl.pallas_call(
        _ada_kernel,
        grid=(n // tn,),
        in_specs=[pl.BlockSpec((bsz, d), lambda j: (0, 0)),
                  pl.BlockSpec((d, tn), lambda j: (0, j)),
                  pl.BlockSpec((1, tn), lambda j: (0, j))],
        out_specs=pl.BlockSpec((bsz, tn), lambda j: (0, j)),
        out_shape=jax.ShapeDtypeStruct((bsz, n), F32),
        compiler_params=_cparams(("arbitrary",)),
        name="ada_ln",
    )(c, w, b.reshape(1, n))


def _ffn_kernel(x_ref, mod_ref, g_ref, w1_ref, w3_ref, w2_ref, o_ref, *, sub):
    x = x_ref[0]
    h = _modulated(x, mod_ref, g_ref, sub).astype(BF16)
    a = jnp.dot(h, w1_ref[...], preferred_element_type=F32)
    b = jnp.dot(h, w3_ref[...], preferred_element_type=F32)
    act = (a * jax.nn.sigmoid(a) * b).astype(BF16)
    y = jnp.dot(act, w2_ref[...], preferred_element_type=F32)
    gate = mod_ref[0, 3 * sub + 2:3 * sub + 3, :]
    o_ref[0] = x + (0.5 * gate) * y


def _ffn_call(x, mod, g, w1, w3, w2, sub):
    bsz, s, d = x.shape
    tm = min(TOKEN_TILE, s)
    f = w1.shape[1]
    tok = pl.BlockSpec((1, tm, d), lambda b, i: (b, i, 0))
    return pl.pallas_call(
        functools.partial(_ffn_kernel, sub=sub),
        grid=(bsz, s // tm),
        in_specs=[tok,
                  pl.BlockSpec((1, 3 * N_SUBLAYERS, d), lambda b, i: (b, 0, 0)),
                  _const_spec((N_SUBLAYERS, d)),
                  _const_spec((d, f)), _const_spec((d, f)), _const_spec((f, d))],
        out_specs=tok,
        out_shape=jax.ShapeDtypeStruct(x.shape, F32),
        compiler_params=_cparams(("parallel", "parallel")),
        name=f"ffn{sub}",
    )(x, mod, g, w1, w3, w2)


def _split3(v):
    hi = v.astype(BF16)
    r = v - hi.astype(F32)
    mid = r.astype(BF16)
    lo = (r - mid.astype(F32)).astype(BF16)
    return hi, mid, lo


def _group_norm(z, gmat, gain, group):
    ssq = jnp.dot((z * z).astype(BF16), gmat, preferred_element_type=F32)
    return z * lax.rsqrt(ssq * (1.0 / group) + NORM_EPS) * gain


def _rope(z, cos, sin, half, period):
    width = z.shape[1]
    lane = lax.broadcasted_iota(jnp.int32, z.shape, 1) % period
    up = pltpu.roll(z, width - half, axis=1)
    dn = pltpu.roll(z, half, axis=1)
    return z * cos + jnp.where(lane < half, up, dn) * sin


def _mix_in_kernel(x_ref, mod_ref, g_ref, pos_ref, wa_ref, wb_ref, wt_ref, ones_ref, gmat_ref, gains_ref, freq_ref,
                   fbias_ref, fq_ref, fk_ref, dq_ref, dk4_ref, ik8_ref, iq_ref, faug_ref, fvt_ref, dvt_ref, iwt_ref,
                   carry_ref):
    @pl.when(pl.program_id(1) == 0)
    def _():
        carry_ref[...] = jnp.zeros_like(carry_ref)

    x = x_ref[0]
    tm = x.shape[0]
    h = _modulated(x, mod_ref, g_ref, 1).astype(BF16)
    za = jnp.dot(h, wa_ref[...], preferred_element_type=F32)
    zb = jnp.dot(h, wb_ref[...], preferred_element_type=F32)
    zt = lax.dot_general(wt_ref[...], h, _NT, preferred_element_type=F32)
    n_v = FOX_HEADS * V_ROWS
    fvt_ref[0] = (zt[0:n_v] + jnp.concatenate([ones_ref[...]] * (tm // LANES), axis=1)).astype(BF16)
    dvt_ref[0] = (zt[n_v:n_v + V_ROWS]
                  + jnp.concatenate([ones_ref[0:V_ROWS, :]] * (tm // LANES), axis=1)).astype(BF16)
    iwt_ref[0] = zt[n_v + V_ROWS:, :] * ((IDX_HEADS * IDX_DIM) ** -0.5)

    gmat = gmat_ref[...]
    scale = HEAD_DIM ** -0.5 * LOG2E
    fq = _group_norm(za[:, 0:FOX_WIDTH], gmat, gains_ref[0:1, :], HEAD_DIM) * scale
    fk = _group_norm(za[:, FOX_WIDTH:2 * FOX_WIDTH], gmat, gains_ref[1:2, :], HEAD_DIM)
    fq_ref[0] = fq.astype(BF16)
    fk_ref[0] = fk.astype(BF16)

    pos = pos_ref[0]
    ang_a = pos * freq_ref[0:1, :]
    ang_i = pos * freq_ref[1:2, :]
    cos_a, sin_a = jnp.cos(ang_a), jnp.sin(ang_a)
    cos_i, sin_i = jnp.cos(ang_i), jnp.sin(ang_i)
    rot_a = HEAD_DIM // ROPE_FRACTION // 2
    rot_i = IDX_DIM // ROPE_FRACTION // 2

    dq = _group_norm(za[:, 2 * FOX_WIDTH:2 * FOX_WIDTH + DSA_WIDTH], gmat, gains_ref[2:3, :], HEAD_DIM)
    dq = _rope(dq, jnp.concatenate([cos_a] * 4, axis=1), jnp.concatenate([sin_a] * 4, axis=1), rot_a, HEAD_DIM)
    dq_ref[0] = (dq * scale).astype(BF16)

    dk4 = _group_norm(zb[:, 0:DK_TILED], gmat[0:DK_TILED, 0:DK_TILED], gains_ref[3:4, 0:DK_TILED], HEAD_DIM)
    dk4 = _rope(dk4, jnp.concatenate([cos_a] * 2, axis=1), jnp.concatenate([sin_a] * 2, axis=1), rot_a, HEAD_DIM)
    dk4_ref[0] = dk4.astype(BF16)

    cos_i2 = jnp.concatenate([cos_i] * 2, axis=1)
    sin_i2 = jnp.concatenate([sin_i] * 2, axis=1)
    o_ik, o_iq, o_sm = DK_TILED, DK_TILED + IDX_WIDTH, DK_TILED + 2 * IDX_WIDTH
    ik8_ref[0] = _rope(zb[:, o_ik:o_iq], cos_i2, sin_i2, rot_i, IDX_DIM).astype(BF16)
    iq_ref[0] = _rope(zb[:, o_iq:o_sm], cos_i2, sin_i2, rot_i, IDX_DIM).astype(BF16)

    sm = zb[:, o_sm:o_sm + LANES]
    v = sm + fbias_ref[...]
    logf = jnp.minimum(v, 0.0) - jnp.log1p(jnp.exp(-jnp.abs(v)))
    row = lax.broadcasted_iota(jnp.int32, (tm, tm), 0)
    col = lax.broadcasted_iota(jnp.int32, (tm, tm), 1)
    tri = jnp.where(row >= col, 1.0, 0.0).astype(BF16)
    hi, mid, lo = _split3(logf)
    csum = (jnp.dot(tri, hi, preferred_element_type=F32) + jnp.dot(tri, mid, preferred_element_type=F32)
            + jnp.dot(tri, lo, preferred_element_type=F32)) + carry_ref[...]
    carry_ref[...] = csum[tm - 1:tm, :]
    lane = lax.broadcasted_iota(jnp.int32, sm.shape, 1)

    nf = -csum * LOG2E
    t0 = nf.astype(BF16).astype(F32)
    r1 = nf - t0
    t1 = r1.astype(BF16).astype(F32)
    t2 = r1 - t1
    aug = jnp.where(lane < FOX_HEADS, t0,
                    jnp.where(lane < 2 * FOX_HEADS, pltpu.roll(t1, FOX_HEADS, axis=1),
                              jnp.where(lane < 3 * FOX_HEADS, pltpu.roll(t2, 2 * FOX_HEADS, axis=1),
                                        jnp.where(lane < 6 * FOX_HEADS, 1.0, 0.0))))
    faug_ref[0] = aug.astype(BF16)


def _mix_in_call(x, mod, g, pos, wa, wb, wt, ones, gmat, gains, freq, fbias):
    bsz, s, d = x.shape
    tm = min(TOKEN_TILE, s)

    def tok(w):
        return pl.BlockSpec((1, tm, w), lambda b, i: (b, i, 0))

    def tokt(r):
        return pl.BlockSpec((1, r, tm), lambda b, i: (b, 0, i))

    widths = (FOX_WIDTH, FOX_WIDTH, DSA_WIDTH, DK_TILED, IDX_WIDTH, IDX_WIDTH, LANES)
    rows = ((FOX_HEADS * V_ROWS, BF16), (V_ROWS, BF16), (2 * IDX_HEADS, F32))
    out_shapes = [jax.ShapeDtypeStruct((bsz, s, w), BF16) for w in widths]
    out_shapes += [jax.ShapeDtypeStruct((bsz, r, s), dt) for r, dt in rows]
    return pl.pallas_call(
        _mix_in_kernel,
        grid=(bsz, s // tm),
        in_specs=[tok(d),
                  pl.BlockSpec((1, 3 * N_SUBLAYERS, d), lambda b, i: (b, 0, 0)),
                  _const_spec((N_SUBLAYERS, d)),
                  tok(LANES),
                  _const_spec(wa.shape), _const_spec(wb.shape), _const_spec(wt.shape), _const_spec(ones.shape),
                  _const_spec(gmat.shape), _const_spec(gains.shape), _const_spec(freq.shape), _const_spec(fbias.shape)],
        out_specs=[tok(w) for w in widths] + [tokt(r) for r, _ in rows],
        out_shape=out_shapes,
        scratch_shapes=[pltpu.VMEM((1, LANES), F32)],
        compiler_params=_cparams(("arbitrary", "arbitrary")),
        name="mix_in",
    )(x, mod, g, pos, wa, wb, wt, ones, gmat, gains, freq, fbias)


def _fox_kernel(q_ref, qaug_ref, k_ref, faug_ref, vt_ref, hm_ref, fsel_ref, bound_ref, o_ref, qf_ref, m_ref, acc_ref):
    i = pl.program_id(2)
    tq = q_ref.shape[1]
    tk = tq
    q = q_ref[0]
    own = qaug_ref[0].astype(F32)
    for a in range(2):
        sel_row = fsel_ref[0, a:a + 1, :].astype(F32)
        plus_f = pltpu.roll(-(own * sel_row), 3 * FOX_HEADS, axis=1)
        qf_ref[a * tq:(a + 1) * tq, 0:LANES] = q * hm_ref[a:a + 1, :]
        qf_ref[a * tq:(a + 1) * tq, LANES:2 * LANES] = (sel_row + plus_f).astype(BF16)
    acc_ref[...] = jnp.zeros_like(acc_ref)
    bound = jnp.max(bound_ref[...])
    bounded = bound <= MAX_FIXED_SHIFT

    def block_bounded(j, masked):
        start = pl.multiple_of(j * tk, tk)
        kf = jnp.concatenate([k_ref[0, pl.ds(start, tk), :], faug_ref[0, pl.ds(start, tk), :]], axis=1)
        grp = min(FOX_GROUP, tq)
        for g in range(2 * tq // grp):
            rows = min(tk, (g * grp) % tq + grp) if masked else tk
            sl = slice(g * grp, (g + 1) * grp)
            s = lax.dot_general(kf[:rows], qf_ref[sl, :], _NT, preferred_element_type=F32) - bound
            if masked:
                krow = lax.broadcasted_iota(jnp.int32, s.shape, 0)
                qcol = lax.broadcasted_iota(jnp.int32, s.shape, 1) + (g * grp) % tq
                s = jnp.where(krow <= qcol, s, -jnp.inf)
            a = (g * grp) // tq
            vt = vt_ref[0, a * V_ROWS:(a + 1) * V_ROWS, pl.ds(start, rows)]
            acc_ref[:, sl] += jnp.dot(vt, jnp.exp2(s).astype(BF16), preferred_element_type=F32)

    def block(j, masked):
        start = pl.multiple_of(j * tk, tk)
        kf = jnp.concatenate([k_ref[0, pl.ds(start, tk), :], faug_ref[0, pl.ds(start, tk), :]], axis=1)
        grp = min(FOX_GROUP, tq)
        n_groups = 2 * tq // grp

        def rows(g):
            return min(tk, (g * grp) % tq + grp) if masked else tk

        def scores(g):
            s = lax.dot_general(kf[:rows(g)], qf_ref[g * grp:(g + 1) * grp, :], _NT,
                                preferred_element_type=F32)
            if masked:
                krow = lax.broadcasted_iota(jnp.int32, s.shape, 0)
                qcol = lax.broadcasted_iota(jnp.int32, s.shape, 1) + (g * grp) % tq
                s = jnp.where(krow <= qcol, s, -jnp.inf)
            return s

        ahead = [scores(g) for g in range(min(QK_AHEAD, n_groups))]
        for g in range(n_groups):
            s = ahead.pop(0)
            if g + QK_AHEAD < n_groups:
                ahead.append(scores(g + QK_AHEAD))
            sl = slice(g * grp, (g + 1) * grp)
            m_old = m_ref[:, sl]
            m_new = jnp.maximum(m_old, jnp.max(s, axis=0, keepdims=True))
            p = jnp.exp2((s - m_new).astype(BF16))
            alpha = jnp.exp2(m_old - m_new)
            m_ref[:, sl] = m_new
            a = (g * grp) // tq
            vt = vt_ref[0, a * V_ROWS:(a + 1) * V_ROWS, pl.ds(start, rows(g))]
            acc_ref[:, sl] = alpha * acc_ref[:, sl] + jnp.dot(vt, p, preferred_element_type=F32)

    def nf_of(rows16, a):
        return jnp.sum(rows16 * fsel_ref[0, a:a + 1, :].astype(F32), axis=1, keepdims=True)

    def sweep(block_fn, slack):
        block_fn(i, True)

        def older(jj, c):
            j = i - 1 - jj
            last = pl.multiple_of(j * tk + tk - 16, 16)
            tail_rows = faug_ref[0, pl.ds(last, 16), :].astype(F32)
            worst = -jnp.inf
            for a in range(2):
                worst = jnp.maximum(worst, jnp.max(nf_of(tail_rows, a)) - jnp.min(nf_of(own[0:16], a)))

            @pl.when(worst + slack > -FOX_ZERO_EXP)
            def _():
                block_fn(j, False)
            return c

        lax.fori_loop(0, i, older, 0)

    @pl.when(bounded)
    def _():
        sweep(block_bounded, 0.0)

    @pl.when(jnp.logical_not(bounded))
    def _():
        m_ref[...] = jnp.full_like(m_ref, -jnp.inf)
        sweep(block, 2.0 * bound)

    out = acc_ref[0:HEAD_DIM, :] / acc_ref[HEAD_DIM:HEAD_DIM + 1, :]
    for a in range(2):
        o_ref[0, a * HEAD_DIM:(a + 1) * HEAD_DIM, :] = out[:, a * tq:(a + 1) * tq].astype(BF16)


def _fox_call(fq, fk, faug, fvt, hm2, fsel, bound):
    bsz, s, _ = fq.shape
    tq = min(FOX_TILE, s)
    pairs = FOX_HEADS // 2
    return pl.pallas_call(
        _fox_kernel,
        grid=(bsz, pairs, s // tq),
        in_specs=[pl.BlockSpec((1, tq, LANES), lambda b, h, i: (b, i, h)),
                  pl.BlockSpec((1, tq, LANES), lambda b, h, i: (b, i, 0)),
                  pl.BlockSpec((1, s, LANES), lambda b, h, i: (b, 0, h)),
                  pl.BlockSpec((1, s, LANES), lambda b, h, i: (b, 0, 0)),
                  pl.BlockSpec((1, 2 * V_ROWS, s), lambda b, h, i: (b, h, 0)),
                  pl.BlockSpec((2, LANES), lambda b, h, i: (0, 0)),
                  pl.BlockSpec((1, 2, LANES), lambda b, h, i: (h, 0, 0)),
                  pl.BlockSpec((1, LANES), lambda b, h, i: (0, 0))],
        out_specs=pl.BlockSpec((1, LANES, tq), lambda b, h, i: (b, h, i)),
        out_shape=jax.ShapeDtypeStruct((bsz, FOX_WIDTH, s), BF16),
        scratch_shapes=[pltpu.VMEM((2 * tq, 2 * LANES), BF16),
                        pltpu.VMEM((1, 2 * tq), F32),
                        pltpu.VMEM((V_ROWS, 2 * tq), F32)],
        compiler_params=_cparams(("parallel", "parallel", "arbitrary")),
        name="fox_attn",
    )(fq, faug, fk, faug, fvt, hm2, fsel, bound)


def _dsa_kernel(dq_ref, iq_ref, wt_ref, dk4_ref, dvt_ref, ik8_ref, hm8_ref, hm4_ref, bound_ref, o_ref,
                iq8_ref, q8_ref, keys_ref, hi_ref, lo_ref, m_ref, acc_ref, *, top_k):
    i = pl.program_id(1)
    tq = dq_ref.shape[1]
    s_len = dk4_ref.shape[1]
    tk = min(DSA_K_TILE, s_len)
    half_tk = tk // 2
    q_start = i * tq
    rem = (q_start + tq) % tk
    n_full = (q_start + tq) // tk + jnp.where(rem > half_tk, 1, 0)
    has_tail = (rem > 0) & (rem <= half_tk)

    def sweep(fn, init):
        carry = lax.fori_loop(0, n_full, lambda j, c: fn(pl.multiple_of(j * tk, tk), tk, c), init)
        return lax.cond(has_tail, lambda c: fn(pl.multiple_of(n_full * tk, tk), half_tk, c), lambda c: c, carry)

    iq = iq_ref[0]
    dq = dq_ref[0]
    half = dq.shape[1] // 2
    for h in range(IDX_HEADS):
        iq8_ref[h * tq:(h + 1) * tq, :] = iq * hm8_ref[h:h + 1, :]
        q8_ref[h * tq:(h + 1) * tq, :] = dq[:, (h // 4) * half:(h // 4 + 1) * half] * hm4_ref[h % 4:h % 4 + 1, :]

    adm_end = q_start + (lax.broadcasted_iota(jnp.int32, (1, tq), 1) // CHUNK + 1) * CHUNK
    wt = wt_ref[0, :, pl.ds(pl.multiple_of(q_start, tq), tq)]

    def score_block(start, size, c):
        d = lax.dot_general(ik8_ref[0, pl.ds(start, size), :], iq8_ref[...], _NT, preferred_element_type=F32)
        sc = jnp.zeros((size, tq), F32)
        for h in range(IDX_HEADS):
            sc = sc + jnp.maximum(d[:, h * tq:(h + 1) * tq], 0.0) * wt[IDX_HEADS + h:IDX_HEADS + h + 1, :]
        sc = jnp.where(jnp.abs(sc) < F32_MIN_NORMAL, 0.0, sc)
        bits = lax.bitcast_convert_type(sc, jnp.int32)
        key = bits ^ ((bits >> 31) & F32_MAGNITUDE_BITS)
        adm = lax.broadcasted_iota(jnp.int32, (size, tq), 0) + start < adm_end
        keys_ref[pl.ds(start, size), :] = jnp.where(adm, key, INT_MIN)
        top = lax.bitcast_convert_type(bits & TOP16_MASK, F32)
        hi_ref[pl.ds(start, size), :] = jnp.where(adm, top, jnp.nan).astype(BF16)
        return c

    sweep(score_block, 0)

    def count_hi(cand):
        def body(start, size, acc):
            hit = jnp.where(hi_ref[pl.ds(start, size), :] >= cand, jnp.ones((), BF16), jnp.zeros((), BF16))
            part = hit[0:COUNT_ROWS]
            for r in range(1, size // COUNT_ROWS):
                part = part + hit[r * COUNT_ROWS:(r + 1) * COUNT_ROWS]
            return acc + part.astype(F32)
        return jnp.sum(sweep(body, jnp.zeros((COUNT_ROWS, tq), F32)), axis=0, keepdims=True)

    def count_lo(cand, n_above_bucket):
        c16 = ((cand & LOW16_MASK) - LOW16_OFFSET).astype(jnp.int16)

        def body(start, size, acc):
            hit = jnp.where(lo_ref[pl.ds(start, size), :] >= c16, jnp.ones((), jnp.int16), jnp.zeros((), jnp.int16))
            part = hit[0:COUNT_ROWS]
            for r in range(1, size // COUNT_ROWS):
                part = part + hit[r * COUNT_ROWS:(r + 1) * COUNT_ROWS]
            return acc + part.astype(jnp.int32)
        hits = jnp.sum(sweep(body, jnp.zeros((COUNT_ROWS, tq), jnp.int32)), axis=0, keepdims=True)
        return n_above_bucket + hits

    def step(cand, cnt, state):
        thr, done, n_ge, n_gt = state
        live = done == 0
        ok = (cnt >= top_k) & live
        return (jnp.where(ok, cand, thr), jnp.where(ok & (cnt == top_k), 1, done),
                jnp.where(ok, cnt, n_ge), jnp.where((cnt < top_k) & live, cnt, n_gt))

    def bisect_hi(it, state):
        cand = state[0] + jnp.left_shift(jnp.int32(1), 15 - it)
        pattern = jnp.where(cand >= 0, cand, cand ^ BF16_MAGNITUDE_BITS)
        pattern = jnp.where((pattern & BF16_EXPONENT_BITS) == 0, jnp.where(cand > 0, BF16_MIN_NORMAL_BITS, 0), pattern)
        cand_f = lax.bitcast_convert_type(jnp.left_shift(pattern, 16), F32).astype(BF16)
        return step(cand, count_hi(cand_f).astype(jnp.int32), state)

    def bisect_lo(carry):
        g, state = carry
        for u in range(LOW_PASSES_PER_CHECK):
            cand = state[0] + jnp.left_shift(jnp.int32(1), 15 - (g * LOW_PASSES_PER_CHECK + u))
            state = step(cand, count_lo(cand, n_bucket), state)
        return g + 1, state

    state = (jnp.full((1, tq), -2 ** 15, jnp.int32), jnp.where(adm_end <= top_k, 1, 0), adm_end,
             jnp.zeros((1, tq), jnp.int32))
    t16, done, n_ge, n_gt = lax.fori_loop(0, 16, bisect_hi, state)
    base = jnp.left_shift(t16, 16)
    n_bucket = n_gt

    def mark_bucket(start, size, c):
        off = keys_ref[pl.ds(start, size), :] - base
        inside = lax.bitcast_convert_type(off, jnp.uint32) < jnp.uint32(LOW16_MASK + 1)
        lo_ref[pl.ds(start, size), :] = jnp.where(inside, off - LOW16_OFFSET, -LOW16_OFFSET).astype(jnp.int16)
        return c

    sweep(mark_bucket, 0)
    above = count_lo(base + 1, n_bucket)
    exact = (done == 0) & (above == top_k)
    short = (done == 0) & (above < top_k)
    state = (jnp.where(exact, base + 1, base), jnp.where(above <= top_k, 1, done),
             jnp.where(exact, above, n_ge), jnp.where(short, above, n_gt))
    _, (thr, _, n_ge, n_gt) = lax.while_loop(
        lambda c: (c[0] < 16 // LOW_PASSES_PER_CHECK) & (jnp.min(c[1][1]) == 0), bisect_lo, (jnp.int32(0), state))
    thr = jnp.maximum(thr, INT_MIN + 1)

    @pl.when(jnp.max(n_ge) > top_k)
    def _():
        need = (top_k - n_gt).astype(F32)

        r2 = lax.broadcasted_iota(jnp.int32, (DEMOTE_ROWS, DEMOTE_ROWS), 0)
        c2 = lax.broadcasted_iota(jnp.int32, (DEMOTE_ROWS, DEMOTE_ROWS), 1)
        lower = jnp.where(c2 < r2, 1.0, 0.0).astype(BF16)

        def demote(start, size, seen):
            chunks = []
            for c0 in range(0, size, DEMOTE_ROWS):
                kblk = keys_ref[pl.ds(start + c0, DEMOTE_ROWS), :]
                eqf = jnp.where(kblk == thr, 1.0, 0.0)
                chunks.append((c0, kblk, eqf, jnp.dot(lower, eqf.astype(BF16), preferred_element_type=F32)))
            for c0, kblk, eqf, within in chunks:
                drop = (kblk == thr) & (within >= need - seen)
                keys_ref[pl.ds(start + c0, DEMOTE_ROWS), :] = jnp.where(drop, thr - 1, kblk)
                seen = seen + within[DEMOTE_ROWS - 1:DEMOTE_ROWS] + eqf[DEMOTE_ROWS - 1:DEMOTE_ROWS]
            return seen

        sweep(demote, jnp.zeros((1, tq), F32))

    acc_ref[...] = jnp.zeros_like(acc_ref)
    grp = min(DSA_GROUP, DSA_HEADS * tq)
    n_groups = DSA_HEADS * tq // grp
    bound = jnp.max(bound_ref[...])

    def attend_bounded(start, size, c):
        bias = jnp.where(keys_ref[pl.ds(start, size), :] >= thr, -bound, -jnp.inf)
        bias = jnp.concatenate([bias] * (grp // tq), axis=1)
        kb = dk4_ref[0, pl.ds(start, size), :]
        vt = dvt_ref[0, :, pl.ds(start, size)]
        for g in range(n_groups):
            sl = slice(g * grp, (g + 1) * grp)
            s = lax.dot_general(kb, q8_ref[sl, :], _NT, preferred_element_type=F32) + bias
            acc_ref[:, sl] += jnp.dot(vt, jnp.exp2(s).astype(BF16), preferred_element_type=F32)
        return c

    def attend(start, size, c):
        bias = jnp.where(keys_ref[pl.ds(start, size), :] >= thr, 0.0, -jnp.inf)
        bias = jnp.concatenate([bias] * (grp // tq), axis=1)
        kb = dk4_ref[0, pl.ds(start, size), :]
        vt = dvt_ref[0, :, pl.ds(start, size)]

        def scores(g):
            return lax.dot_general(kb, q8_ref[g * grp:(g + 1) * grp, :], _NT,
                                   preferred_element_type=F32) + bias

        ahead = [scores(g) for g in range(min(QK_AHEAD, n_groups))]
        for g in range(n_groups):
            s = ahead.pop(0)
            if g + QK_AHEAD < n_groups:
                ahead.append(scores(g + QK_AHEAD))
            sl = slice(g * grp, (g + 1) * grp)
            m_old = m_ref[:, sl]
            m_new = jnp.maximum(m_old, jnp.max(s, axis=0, keepdims=True))
            m_safe = jnp.where(m_new == -jnp.inf, 0.0, m_new)
            p = jnp.exp2((s - m_safe).astype(BF16))
            alpha = jnp.exp2(m_old - m_safe)
            m_ref[:, sl] = m_new
            acc_ref[:, sl] = alpha * acc_ref[:, sl] + jnp.dot(vt, p, preferred_element_type=F32)
        return c

    @pl.when(bound <= MAX_FIXED_SHIFT)
    def _():
        sweep(attend_bounded, 0)

    @pl.when(bound > MAX_FIXED_SHIFT)
    def _():
        m_ref[...] = jnp.full_like(m_ref, -jnp.inf)
        sweep(attend, 0)

    out = acc_ref[0:HEAD_DIM, :] / acc_ref[HEAD_DIM:HEAD_DIM + 1, :]
    for h in range(DSA_HEADS):
        o_ref[0, h * HEAD_DIM:(h + 1) * HEAD_DIM, :] = out[:, h * tq:(h + 1) * tq].astype(BF16)


def _dsa_call(dq, iq, wt, dk4, dvt, ik8, hm8, hm4, bound, top_k):
    bsz, s, _ = dq.shape
    tq = min(DSA_Q_TILE, s)

    def tok(w):
        return pl.BlockSpec((1, tq, w), lambda b, i: (b, i, 0))

    def seq(w):
        return pl.BlockSpec((1, s, w), lambda b, i: (b, 0, 0))

    def seqt(r):
        return pl.BlockSpec((1, r, s), lambda b, i: (b, 0, 0))

    return pl.pallas_call(
        functools.partial(_dsa_kernel, top_k=top_k),
        grid=(bsz, s // tq),
        in_specs=[tok(DSA_WIDTH), tok(IDX_WIDTH), seqt(2 * IDX_HEADS), seq(DK_TILED), seqt(V_ROWS), seq(IDX_WIDTH),
                  pl.BlockSpec(hm8.shape, lambda b, i: (0, 0)), pl.BlockSpec(hm4.shape, lambda b, i: (0, 0)),
                  pl.BlockSpec((1, LANES), lambda b, i: (0, 0))],
        out_specs=pl.BlockSpec((1, DSA_WIDTH, tq), lambda b, i: (b, 0, i)),
        out_shape=jax.ShapeDtypeStruct((bsz, DSA_WIDTH, s), BF16),
        scratch_shapes=[pltpu.VMEM((IDX_HEADS * tq, IDX_WIDTH), BF16), pltpu.VMEM((DSA_HEADS * tq, DK_TILED), BF16),
                        pltpu.VMEM((s, tq), jnp.int32), pltpu.VMEM((s, tq), BF16),
                        pltpu.VMEM((s, tq), jnp.int16),
                        pltpu.VMEM((1, DSA_HEADS * tq), F32),
                        pltpu.VMEM((V_ROWS, DSA_HEADS * tq), F32)],
        compiler_params=_cparams(("parallel", "arbitrary")),
        name="dsa_attn",
    )(dq, iq, wt, dk4, dvt, ik8, hm8, hm4, bound)


def _merge_kernel(x_ref, mod_ref, g_ref, ya_ref, yb_ref, wg_ref, wfa_ref, wfb_ref, wo_ref, o_ref):
    x = x_ref[0]
    d = x.shape[1]
    h = _modulated(x, mod_ref, g_ref, 1).astype(BF16)
    zg = jnp.dot(h, wg_ref[...], preferred_element_type=F32)
    pa = lax.dot_general(ya_ref[0], wfa_ref[...], _TN, preferred_element_type=F32)
    pb = lax.dot_general(yb_ref[0], wfb_ref[...], _TN, preferred_element_type=F32)
    merged = jax.nn.sigmoid(zg[:, :d]) * pa + jax.nn.sigmoid(zg[:, d:]) * pb
    y = jnp.dot(merged.astype(BF16), wo_ref[...], preferred_element_type=F32)
    o_ref[0] = x + mod_ref[0, 5:6, :] * y


def _merge_call(x, mod, g, ya, yb, wg, wfa, wfb, wo):
    bsz, s, d = x.shape
    tm = min(TOKEN_TILE, s)

    def tok(w):
        return pl.BlockSpec((1, tm, w), lambda b, i: (b, i, 0))

    def tokt(w):
        return pl.BlockSpec((1, w, tm), lambda b, i: (b, 0, i))

    return pl.pallas_call(
        _merge_kernel,
        grid=(bsz, s // tm),
        in_specs=[tok(d),
                  pl.BlockSpec((1, 3 * N_SUBLAYERS, d), lambda b, i: (b, 0, 0)),
                  _const_spec((N_SUBLAYERS, d)),
                  tokt(FOX_WIDTH), tokt(DSA_WIDTH),
                  _const_spec(wg.shape), _const_spec(wfa.shape), _const_spec(wfb.shape), _const_spec(wo.shape)],
        out_specs=tok(d),
        out_shape=jax.ShapeDtypeStruct(x.shape, F32),
        compiler_params=_cparams(("parallel", "parallel")),
        name="merge_out",
    )(x, mod, g, ya, yb, wg, wfa, wfb, wo)


def _head_mask(n_heads, width):
    return jnp.asarray(np.kron(np.eye(n_heads), np.ones((1, width))), BF16)


def _rope_freq_row(rot_dim, period):
    inv_freq = ROPE_THETA ** (-jnp.arange(0, rot_dim, 2, dtype=F32) / rot_dim)
    half = rot_dim // 2
    head = jnp.concatenate([-inv_freq, inv_freq, jnp.zeros((period - 2 * half,), F32)])
    return jnp.tile(head, LANES // period)


def kernel(x, c, positions, ada_w, ada_b, norm_g, ffn1_w1, ffn1_w3, ffn1_w2, w_in, fox_f_bias, fox_qk_g, dsa_qk_g,
           w_br_fox, w_br_dsa, w_out, ffn2_w1, ffn2_w3, ffn2_w2):
    bsz, s, d = x.shape
    top_k = min(TOPK_MAX, s // 4)
    depth = ada_w.shape[0]
    pos = jnp.broadcast_to(positions.astype(F32)[:, :, None], (bsz, s, LANES))
    freq = jnp.stack([_rope_freq_row(HEAD_DIM // ROPE_FRACTION, HEAD_DIM),
                      _rope_freq_row(IDX_DIM // ROPE_FRACTION, IDX_DIM)])
    gmat = jnp.asarray(np.kron(np.eye(FOX_HEADS), np.ones((HEAD_DIM, HEAD_DIM))), BF16)
    hm2 = _head_mask(2, HEAD_DIM)
    hm4 = _head_mask(4, HEAD_DIM)
    hm8 = _head_mask(IDX_HEADS, IDX_DIM)
    fsel = jnp.asarray(np.tile(np.eye(FOX_HEADS), (1, LANES // FOX_HEADS))
                       * (np.arange(LANES) < 3 * FOX_HEADS), BF16).reshape(FOX_HEADS // 2, 2, LANES)
    o_fq, o_fk, o_fv = 0, FOX_WIDTH, 2 * FOX_WIDTH
    o_ff = 3 * FOX_WIDTH
    o_dq = o_ff + FOX_HEADS
    o_dk = o_dq + DSA_WIDTH
    o_dv = o_dk + HEAD_DIM
    o_iq = o_dv + HEAD_DIM
    o_ik = o_iq + IDX_WIDTH
    o_iw = o_ik + IDX_DIM
    o_ga = o_iw + IDX_HEADS

    for l in range(depth):
        mod = _ada_call(c, ada_w[l], ada_b[l]).reshape(bsz, 3 * N_SUBLAYERS, d)
        g = norm_g[l]
        x = _ffn_call(x, mod, g, ffn1_w1[l].astype(BF16), ffn1_w3[l].astype(BF16), ffn1_w2[l].astype(BF16), 0)

        w = w_in[l]
        wa = jnp.concatenate([w[:, o_fq:o_fq + 2 * FOX_WIDTH], w[:, o_dq:o_dq + DSA_WIDTH]], axis=1).astype(BF16)
        pad = jnp.zeros((d, LANES - FOX_HEADS), F32)
        wb = jnp.concatenate([jnp.tile(w[:, o_dk:o_dk + HEAD_DIM], (1, 4)),
                              jnp.tile(w[:, o_ik:o_ik + IDX_DIM], (1, IDX_HEADS)),
                              w[:, o_iq:o_iq + IDX_WIDTH],
                              w[:, o_ff:o_ff + FOX_HEADS], pad], axis=1).astype(BF16)
        zrows = jnp.zeros((FOX_HEADS, V_ROWS - HEAD_DIM, d), F32)
        wfv = jnp.transpose(w[:, o_fv:o_fv + FOX_WIDTH]).reshape(FOX_HEADS, HEAD_DIM, d)
        wt = jnp.concatenate([jnp.concatenate([wfv, zrows], axis=1).reshape(FOX_HEADS * V_ROWS, d),
                              jnp.transpose(w[:, o_dv:o_dv + HEAD_DIM]), zrows[0],
                              jnp.zeros((IDX_HEADS, d), F32), jnp.transpose(w[:, o_iw:o_iw + IDX_HEADS])],
                             axis=0).astype(BF16)
        ones = jnp.asarray(np.tile((np.arange(FOX_HEADS * V_ROWS) % V_ROWS >= HEAD_DIM)[:, None], (1, LANES)), F32)
        wg = w[:, o_ga:o_ga + 2 * d].astype(BF16)
        gains = jnp.stack([jnp.tile(fox_qk_g[l, 0], FOX_HEADS), jnp.tile(fox_qk_g[l, 1], FOX_HEADS),
                           jnp.tile(dsa_qk_g[l, 0], DSA_HEADS), jnp.tile(dsa_qk_g[l, 1], DSA_HEADS)]).astype(F32)
        fbias = jnp.concatenate([fox_f_bias[l].astype(F32), jnp.zeros((LANES - FOX_HEADS,), F32)]).reshape(1, LANES)

        fq, fk, dq, dk4, ik8, iq, faug, fvt, dvt, iwt = _mix_in_call(x, mod, g, pos, wa, wb, wt, ones, gmat, gains,
                                                                       freq, fbias)
        unit = jnp.full((1, LANES), HEAD_DIM ** 0.5 * LOG2E * NORM_SLACK, F32)
        fox_bound = unit * (jnp.max(jnp.abs(fox_qk_g[l, 0])) * jnp.max(jnp.abs(fox_qk_g[l, 1])))
        dsa_bound = unit * (jnp.max(jnp.abs(dsa_qk_g[l, 0])) * jnp.max(jnp.abs(dsa_qk_g[l, 1])))
        yat = _fox_call(fq, fk, faug, fvt, hm2, fsel, fox_bound)
        ybt = _dsa_call(dq, iq, iwt, dk4, dvt, ik8, hm8, hm4, dsa_bound, top_k)
        x = _merge_call(x, mod, g, yat, ybt, wg, w_br_fox[l].astype(BF16), w_br_dsa[l].astype(BF16),
                        w_out[l].astype(BF16))
        x = _ffn_call(x, mod, g, ffn2_w1[l].astype(BF16), ffn2_w3[l].astype(BF16), ffn2_w2[l].astype(BF16), 2)
    return x
```

```python
import functools

import numpy as np
import jax
import jax.numpy as jnp
from jax import lax
from jax.experimental import pallas as pl
from jax.experimental.pallas import tpu as pltpu

CHUNK = 64
HEAD_DIM = 64
FOX_HEADS = 8
DSA_HEADS = 8
IDX_HEADS = 8
IDX_DIM = 32
TOPK_MAX = 256
ROPE_THETA = 500000.0
ROPE_FRACTION = 4
N_SUBLAYERS = 3
NORM_EPS = 1e-6
FOX_WIDTH = FOX_HEADS * HEAD_DIM
DSA_WIDTH = DSA_HEADS * HEAD_DIM
IDX_WIDTH = IDX_HEADS * IDX_DIM
DK_TILED = 4 * HEAD_DIM

LANES = 128
VMEM_LIMIT_BYTES = 56 * 1024 * 1024

ADA_COLS = 1152
TOKEN_TILE = 512
FOX_TILE = 1024
DSA_Q_TILE = 256
DSA_K_TILE = 1024
FOX_GROUP = 512
DSA_GROUP = 1024
QK_AHEAD = 3
LOW_PASSES_PER_CHECK = 2
DEMOTE_ROWS = 128
COUNT_ROWS = 32

MAX_FIXED_SHIFT = 60.0
FOX_ZERO_EXP = 160.0
V_ROWS = HEAD_DIM + 16
LOG2E = 1.4426950408889634
INT_MIN = -2 ** 31
F32_MAGNITUDE_BITS = 0x7FFFFFFF
LOW16_MASK = 0xFFFF
LOW16_OFFSET = 32768
TOP16_MASK = -65536
BF16_MAGNITUDE_BITS = 0x7FFF
BF16_EXPONENT_BITS = 0x7F80
BF16_MIN_NORMAL_BITS = 0x0080
NORM_SLACK = 1.02
F32_MIN_NORMAL = 1.1754943508222875e-38
F32 = jnp.float32
BF16 = jnp.bfloat16
_NT = (((1,), (1,)), ((), ()))
_TN = (((0,), (0,)), ((), ()))


def _cparams(sem):
    return pltpu.CompilerParams(dimension_semantics=sem, vmem_limit_bytes=VMEM_LIMIT_BYTES)


def _const_spec(shape):
    nd = len(shape)
    return pl.BlockSpec(shape, lambda *_: (0,) * nd, pipeline_mode=pl.Buffered(1))


def _modulated(x, mod_ref, g_ref, sub):
    ms = jnp.mean(x * x, axis=-1, keepdims=True)
    y = x * lax.rsqrt(ms + NORM_EPS) * g_ref[sub:sub + 1, :]
    shift = mod_ref[0, 3 * sub:3 * sub + 1, :]
    scale = mod_ref[0, 3 * sub + 1:3 * sub + 2, :]
    return y * (1.0 + scale) + shift


def _ada_kernel(c_ref, w_ref, b_ref, o_ref):
    c = c_ref[...]
    cond = c * jax.nn.sigmoid(c)
    o_ref[...] = jnp.dot(cond, w_ref[...], preferred_element_type=F32,
                         precision=lax.Precision.HIGHEST) + b_ref[...]


def _ada_call(c, w, b):
    bsz, d = c.shape
    n = w.shape[1]
    tn = ADA_COLS
    return pl.pallas_call(
        _ada_kernel,
        grid=(n // tn,),
        in_specs=[pl.BlockSpec((bsz, d), lambda j: (0, 0)),
                  pl.BlockSpec((d, tn), lambda j: (0, j)),
                  pl.BlockSpec((1, tn), lambda j: (0, j))],
        out_specs=pl.BlockSpec((bsz, tn), lambda j: (0, j)),
        out_shape=jax.ShapeDtypeStruct((bsz, n), F32),
        compiler_params=_cparams(("arbitrary",)),
        name="ada_ln",
    )(c, w, b.reshape(1, n))


def _ffn_kernel(x_ref, mod_ref, g_ref, w1_ref, w3_ref, w2_ref, o_ref, *, sub):
    x = x_ref[0]
    h = _modulated(x, mod_ref, g_ref, sub).astype(BF16)
    a = jnp.dot(h, w1_ref[...], preferred_element_type=F32)
    b = jnp.dot(h, w3_ref[...], preferred_element_type=F32)
    act = (a * jax.nn.sigmoid(a) * b).astype(BF16)
    y = jnp.dot(act, w2_ref[...], preferred_element_type=F32)
    gate = mod_ref[0, 3 * sub + 2:3 * sub + 3, :]
    o_ref[0] = x + (0.5 * gate) * y


def _ffn_call(x, mod, g, w1, w3, w2, sub):
    bsz, s, d = x.shape
    tm = min(TOKEN_TILE, s)
    f = w1.shape[1]
    tok = pl.BlockSpec((1, tm, d), lambda b, i: (b, i, 0))
    return pl.pallas_call(
        functools.partial(_ffn_kernel, sub=sub),
        grid=(bsz, s // tm),
        in_specs=[tok,
                  pl.BlockSpec((1, 3 * N_SUBLAYERS, d), lambda b, i: (b, 0, 0)),
                  _const_spec((N_SUBLAYERS, d)),
                  _const_spec((d, f)), _const_spec((d, f)), _const_spec((f, d))],
        out_specs=tok,
        out_shape=jax.ShapeDtypeStruct(x.shape, F32),
        compiler_params=_cparams(("parallel", "parallel")),
        name=f"ffn{sub}",
    )(x, mod, g, w1, w3, w2)


def _split3(v):
    hi = v.astype(BF16)
    r = v - hi.astype(F32)
    mid = r.astype(BF16)
    lo = (r - mid.astype(F32)).astype(BF16)
    return hi, mid, lo


def _group_norm(z, gmat, gain, group):
    ssq = jnp.dot((z * z).astype(BF16), gmat, preferred_element_type=F32)
    return z * lax.rsqrt(ssq * (1.0 / group) + NORM_EPS) * gain


def _rope(z, cos, sin, half, period):
    width = z.shape[1]
    lane = lax.broadcasted_iota(jnp.int32, z.shape, 1) % period
    up = pltpu.roll(z, width - half, axis=1)
    dn = pltpu.roll(z, half, axis=1)
    return z * cos + jnp.where(lane < half, up, dn) * sin


def _mix_in_kernel(x_ref, mod_ref, g_ref, pos_ref, wa_ref, wb_ref, wt_ref, ones_ref, gmat_ref, gains_ref, freq_ref,
                   fbias_ref, fq_ref, fk_ref, dq_ref, dk4_ref, ik8_ref, iq_ref, faug_ref, fvt_ref, dvt_ref, iwt_ref,
                   carry_ref):
    @pl.when(pl.program_id(1) == 0)
    def _():
        carry_ref[...] = jnp.zeros_like(carry_ref)

    x = x_ref[0]
    tm = x.shape[0]
    h = _modulated(x, mod_ref, g_ref, 1).astype(BF16)
    za = jnp.dot(h, wa_ref[...], preferred_element_type=F32)
    zb = jnp.dot(h, wb_ref[...], preferred_element_type=F32)
    zt = lax.dot_general(wt_ref[...], h, _NT, preferred_element_type=F32)
    n_v = FOX_HEADS * V_ROWS
    fvt_ref[0] = (zt[0:n_v] + jnp.concatenate([ones_ref[...]] * (tm // LANES), axis=1)).astype(BF16)
    dvt_ref[0] = (zt[n_v:n_v + V_ROWS]
                  + jnp.concatenate([ones_ref[0:V_ROWS, :]] * (tm // LANES), axis=1)).astype(BF16)
    iwt_ref[0] = zt[n_v + V_ROWS:, :] * ((IDX_HEADS * IDX_DIM) ** -0.5)

    gmat = gmat_ref[...]
    scale = HEAD_DIM ** -0.5 * LOG2E
    fq = _group_norm(za[:, 0:FOX_WIDTH], gmat, gains_ref[0:1, :], HEAD_DIM) * scale
    fk = _group_norm(za[:, FOX_WIDTH:2 * FOX_WIDTH], gmat, gains_ref[1:2, :], HEAD_DIM)
    fq_ref[0] = fq.astype(BF16)
    fk_ref[0] = fk.astype(BF16)

    pos = pos_ref[0]
    ang_a = pos * freq_ref[0:1, :]
    ang_i = pos * freq_ref[1:2, :]
    cos_a, sin_a = jnp.cos(ang_a), jnp.sin(ang_a)
    cos_i, sin_i = jnp.cos(ang_i), jnp.sin(ang_i)
    rot_a = HEAD_DIM // ROPE_FRACTION // 2
    rot_i = IDX_DIM // ROPE_FRACTION // 2

    dq = _group_norm(za[:, 2 * FOX_WIDTH:2 * FOX_WIDTH + DSA_WIDTH], gmat, gains_ref[2:3, :], HEAD_DIM)
    dq = _rope(dq, jnp.concatenate([cos_a] * 4, axis=1), jnp.concatenate([sin_a] * 4, axis=1), rot_a, HEAD_DIM)
    dq_ref[0] = (dq * scale).astype(BF16)

    dk4 = _group_norm(zb[:, 0:DK_TILED], gmat[0:DK_TILED, 0:DK_TILED], gains_ref[3:4, 0:DK_TILED], HEAD_DIM)
    dk4 = _rope(dk4, jnp.concatenate([cos_a] * 2, axis=1), jnp.concatenate([sin_a] * 2, axis=1), rot_a, HEAD_DIM)
    dk4_ref[0] = dk4.astype(BF16)

    cos_i2 = jnp.concatenate([cos_i] * 2, axis=1)
    sin_i2 = jnp.concatenate([sin_i] * 2, axis=1)
    o_ik, o_iq, o_sm = DK_TILED, DK_TILED + IDX_WIDTH, DK_TILED + 2 * IDX_WIDTH
    ik8_ref[0] = _rope(zb[:, o_ik:o_iq], cos_i2, sin_i2, rot_i, IDX_DIM).astype(BF16)
    iq_ref[0] = _rope(zb[:, o_iq:o_sm], cos_i2, sin_i2, rot_i, IDX_DIM).astype(BF16)

    sm = zb[:, o_sm:o_sm + LANES]
    v = sm + fbias_ref[...]
    logf = jnp.minimum(v, 0.0) - jnp.log1p(jnp.exp(-jnp.abs(v)))
    row = lax.broadcasted_iota(jnp.int32, (tm, tm), 0)
    col = lax.broadcasted_iota(jnp.int32, (tm, tm), 1)
    tri = jnp.where(row >= col, 1.0, 0.0).astype(BF16)
    hi, mid, lo = _split3(logf)
    csum = (jnp.dot(tri, hi, preferred_element_type=F32) + jnp.dot(tri, mid, preferred_element_type=F32)
            + jnp.dot(tri, lo, preferred_element_type=F32)) + carry_ref[...]
    carry_ref[...] = csum[tm - 1:tm, :]
    lane = lax.broadcasted_iota(jnp.int32, sm.shape, 1)

    nf = -csum * LOG2E
    t0 = nf.astype(BF16).astype(F32)
    r1 = nf - t0
    t1 = r1.astype(BF16).astype(F32)
    t2 = r1 - t1
    aug = jnp.where(lane < FOX_HEADS, t0,
                    jnp.where(lane < 2 * FOX_HEADS, pltpu.roll(t1, FOX_HEADS, axis=1),
                              jnp.where(lane < 3 * FOX_HEADS, pltpu.roll(t2, 2 * FOX_HEADS, axis=1),
                                        jnp.where(lane < 6 * FOX_HEADS, 1.0, 0.0))))
    faug_ref[0] = aug.astype(BF16)


def _mix_in_call(x, mod, g, pos, wa, wb, wt, ones, gmat, gains, freq, fbias):
    bsz, s, d = x.shape
    tm = min(TOKEN_TILE, s)

    def tok(w):
        return pl.BlockSpec((1, tm, w), lambda b, i: (b, i, 0))

    def tokt(r):
        return pl.BlockSpec((1, r, tm), lambda b, i: (b, 0, i))

    widths = (FOX_WIDTH, FOX_WIDTH, DSA_WIDTH, DK_TILED, IDX_WIDTH, IDX_WIDTH, LANES)
    rows = ((FOX_HEADS * V_ROWS, BF16), (V_ROWS, BF16), (2 * IDX_HEADS, F32))
    out_shapes = [jax.ShapeDtypeStruct((bsz, s, w), BF16) for w in widths]
    out_shapes += [jax.ShapeDtypeStruct((bsz, r, s), dt) for r, dt in rows]
    return pl.pallas_call(
        _mix_in_kernel,
        grid=(bsz, s // tm),
        in_specs=[tok(d),
                  pl.BlockSpec((1, 3 * N_SUBLAYERS, d), lambda b, i: (b, 0, 0)),
                  _const_spec((N_SUBLAYERS, d)),
                  tok(LANES),
                  _const_spec(wa.shape), _const_spec(wb.shape), _const_spec(wt.shape), _const_spec(ones.shape),
                  _const_spec(gmat.shape), _const_spec(gains.shape), _const_spec(freq.shape), _const_spec(fbias.shape)],
        out_specs=[tok(w) for w in widths] + [tokt(r) for r, _ in rows],
        out_shape=out_shapes,
        scratch_shapes=[pltpu.VMEM((1, LANES), F32)],
        compiler_params=_cparams(("arbitrary", "arbitrary")),
        name="mix_in",
    )(x, mod, g, pos, wa, wb, wt, ones, gmat, gains, freq, fbias)


def _fox_kernel(q_ref, qaug_ref, k_ref, faug_ref, vt_ref, hm_ref, fsel_ref, bound_ref, o_ref, qf_ref, m_ref, acc_ref):
    i = pl.program_id(2)
    tq = q_ref.shape[1]
    tk = tq
    q = q_ref[0]
    own = qaug_ref[0].astype(F32)
    for a in range(2):
        sel_row = fsel_ref[0, a:a + 1, :].astype(F32)
        plus_f = pltpu.roll(-(own * sel_row), 3 * FOX_HEADS, axis=1)
        qf_ref[a * tq:(a + 1) * tq, 0:LANES] = q * hm_ref[a:a + 1, :]
        qf_ref[a * tq:(a + 1) * tq, LANES:2 * LANES] = (sel_row + plus_f).astype(BF16)
    acc_ref[...] = jnp.zeros_like(acc_ref)
    bound = jnp.max(bound_ref[...])
    bounded = bound <= MAX_FIXED_SHIFT

    def block_bounded(j, masked):
        start = pl.multiple_of(j * tk, tk)
        kf = jnp.concatenate([k_ref[0, pl.ds(start, tk), :], faug_ref[0, pl.ds(start, tk), :]], axis=1)
        grp = min(FOX_GROUP, tq)
        for g in range(2 * tq // grp):
            rows = min(tk, (g * grp) % tq + grp) if masked else tk
            sl = slice(g * grp, (g + 1) * grp)
            s = lax.dot_general(kf[:rows], qf_ref[sl, :], _NT, preferred_element_type=F32) - bound
            if masked:
                krow = lax.broadcasted_iota(jnp.int32, s.shape, 0)
                qcol = lax.broadcasted_iota(jnp.int32, s.shape, 1) + (g * grp) % tq
                s = jnp.where(krow <= qcol, s, -jnp.inf)
            a = (g * grp) // tq
            vt = vt_ref[0, a * V_ROWS:(a + 1) * V_ROWS, pl.ds(start, rows)]
            acc_ref[:, sl] += jnp.dot(vt, jnp.exp2(s).astype(BF16), preferred_element_type=F32)

    def block(j, masked):
        start = pl.multiple_of(j * tk, tk)
        kf = jnp.concatenate([k_ref[0, pl.ds(start, tk), :], faug_ref[0, pl.ds(start, tk), :]], axis=1)
        grp = min(FOX_GROUP, tq)
        n_groups = 2 * tq // grp

        def rows(g):
            return min(tk, (g * grp) % tq + grp) if masked else tk

        def scores(g):
            s = lax.dot_general(kf[:rows(g)], qf_ref[g * grp:(g + 1) * grp, :], _NT,
                                preferred_element_type=F32)
            if masked:
                krow = lax.broadcasted_iota(jnp.int32, s.shape, 0)
                qcol = lax.broadcasted_iota(jnp.int32, s.shape, 1) + (g * grp) % tq
                s = jnp.where(krow <= qcol, s, -jnp.inf)
            return s

        ahead = [scores(g) for g in range(min(QK_AHEAD, n_groups))]
        for g in range(n_groups):
            s = ahead.pop(0)
            if g + QK_AHEAD < n_groups:
                ahead.append(scores(g + QK_AHEAD))
            sl = slice(g * grp, (g + 1) * grp)
            m_old = m_ref[:, sl]
            m_new = jnp.maximum(m_old, jnp.max(s, axis=0, keepdims=True))
            p = jnp.exp2((s - m_new).astype(BF16))
            alpha = jnp.exp2(m_old - m_new)
            m_ref[:, sl] = m_new
            a = (g * grp) // tq
            vt = vt_ref[0, a * V_ROWS:(a + 1) * V_ROWS, pl.ds(start, rows(g))]
            acc_ref[:, sl] = alpha * acc_ref[:, sl] + jnp.dot(vt, p, preferred_element_type=F32)

    def nf_of(rows16, a):
        return jnp.sum(rows16 * fsel_ref[0, a:a + 1, :].astype(F32), axis=1, keepdims=True)

    def sweep(block_fn, slack):
        block_fn(i, True)

        def older(jj, c):
            j = i - 1 - jj
            last = pl.multiple_of(j * tk + tk - 16, 16)
            tail_rows = faug_ref[0, pl.ds(last, 16), :].astype(F32)
            worst = -jnp.inf
            for a in range(2):
                worst = jnp.maximum(worst, jnp.max(nf_of(tail_rows, a)) - jnp.min(nf_of(own[0:16], a)))

            @pl.when(worst + slack > -FOX_ZERO_EXP)
            def _():
                block_fn(j, False)
            return c

        lax.fori_loop(0, i, older, 0)

    @pl.when(bounded)
    def _():
        sweep(block_bounded, 0.0)

    @pl.when(jnp.logical_not(bounded))
    def _():
        m_ref[...] = jnp.full_like(m_ref, -jnp.inf)
        sweep(block, 2.0 * bound)

    out = acc_ref[0:HEAD_DIM, :] / acc_ref[HEAD_DIM:HEAD_DIM + 1, :]
    for a in range(2):
        o_ref[0, a * HEAD_DIM:(a + 1) * HEAD_DIM, :] = out[:, a * tq:(a + 1) * tq].astype(BF16)


def _fox_call(fq, fk, faug, fvt, hm2, fsel, bound):
    bsz, s, _ = fq.shape
    tq = min(FOX_TILE, s)
    pairs = FOX_HEADS // 2
    return pl.pallas_call(
        _fox_kernel,
        grid=(bsz, pairs, s // tq),
        in_specs=[pl.BlockSpec((1, tq, LANES), lambda b, h, i: (b, i, h)),
                  pl.BlockSpec((1, tq, LANES), lambda b, h, i: (b, i, 0)),
                  pl.BlockSpec((1, s, LANES), lambda b, h, i: (b, 0, h)),
                  pl.BlockSpec((1, s, LANES), lambda b, h, i: (b, 0, 0)),
                  pl.BlockSpec((1, 2 * V_ROWS, s), lambda b, h, i: (b, h, 0)),
                  pl.BlockSpec((2, LANES), lambda b, h, i: (0, 0)),
                  pl.BlockSpec((1, 2, LANES), lambda b, h, i: (h, 0, 0)),
                  pl.BlockSpec((1, LANES), lambda b, h, i: (0, 0))],
        out_specs=pl.BlockSpec((1, LANES, tq), lambda b, h, i: (b, h, i)),
        out_shape=jax.ShapeDtypeStruct((bsz, FOX_WIDTH, s), BF16),
        scratch_shapes=[pltpu.VMEM((2 * tq, 2 * LANES), BF16),
                        pltpu.VMEM((1, 2 * tq), F32),
                        pltpu.VMEM((V_ROWS, 2 * tq), F32)],
        compiler_params=_cparams(("parallel", "parallel", "arbitrary")),
        name="fox_attn",
    )(fq, faug, fk, faug, fvt, hm2, fsel, bound)


def _dsa_kernel(dq_ref, iq_ref, wt_ref, dk4_ref, dvt_ref, ik8_ref, hm8_ref, hm4_ref, bound_ref, o_ref,
                iq8_ref, q8_ref, keys_ref, hi_ref, lo_ref, m_ref, acc_ref, *, top_k):
    i = pl.program_id(1)
    tq = dq_ref.shape[1]
    s_len = dk4_ref.shape[1]
    tk = min(DSA_K_TILE, s_len)
    half_tk = tk // 2
    q_start = i * tq
    rem = (q_start + tq) % tk
    n_full = (q_start + tq) // tk + jnp.where(rem > half_tk, 1, 0)
    has_tail = (rem > 0) & (rem <= half_tk)

    def sweep(fn, init):
        carry = lax.fori_loop(0, n_full, lambda j, c: fn(pl.multiple_of(j * tk, tk), tk, c), init)
        return lax.cond(has_tail, lambda c: fn(pl.multiple_of(n_full * tk, tk), half_tk, c), lambda c: c, carry)

    iq = iq_ref[0]
    dq = dq_ref[0]
    half = dq.shape[1] // 2
    for h in range(IDX_HEADS):
        iq8_ref[h * tq:(h + 1) * tq, :] = iq * hm8_ref[h:h + 1, :]
        q8_ref[h * tq:(h + 1) * tq, :] = dq[:, (h // 4) * half:(h // 4 + 1) * half] * hm4_ref[h % 4:h % 4 + 1, :]

    adm_end = q_start + (lax.broadcasted_iota(jnp.int32, (1, tq), 1) // CHUNK + 1) * CHUNK
    wt = wt_ref[0, :, pl.ds(pl.multiple_of(q_start, tq), tq)]

    def score_block(start, size, c):
        d = lax.dot_general(ik8_ref[0, pl.ds(start, size), :], iq8_ref[...], _NT, preferred_element_type=F32)
        sc = jnp.zeros((size, tq), F32)
        for h in range(IDX_HEADS):
            sc = sc + jnp.maximum(d[:, h * tq:(h + 1) * tq], 0.0) * wt[IDX_HEADS + h:IDX_HEADS + h + 1, :]
        sc = jnp.where(jnp.abs(sc) < F32_MIN_NORMAL, 0.0, sc)
        bits = lax.bitcast_convert_type(sc, jnp.int32)
        key = bits ^ ((bits >> 31) & F32_MAGNITUDE_BITS)
        top = lax.bitcast_convert_type(bits & TOP16_MASK, F32)
        keys_ref[pl.ds(start, size), :] = key
        hi_ref[pl.ds(start, size), :] = top.astype(BF16)
        return c

    sweep(score_block, 0)

    own = pl.ds(pl.multiple_of(q_start, tq), tq)
    adm = lax.broadcasted_iota(jnp.int32, (tq, tq), 0) + q_start < adm_end
    keys_ref[own, :] = jnp.where(adm, keys_ref[own, :], INT_MIN)
    hi_ref[own, :] = jnp.where(adm, hi_ref[own, :].astype(F32), jnp.nan).astype(BF16)
    swept_end = n_full * tk + jnp.where(has_tail, half_tk, 0)
    for r in range(1, max(half_tk // tq, 1)):
        @pl.when(q_start + r * tq < swept_end)
        def _():
            beyond = pl.ds(pl.multiple_of(q_start + r * tq, tq), tq)
            keys_ref[beyond, :] = jnp.full((tq, tq), INT_MIN, jnp.int32)
            hi_ref[beyond, :] = jnp.full((tq, tq), jnp.nan, BF16)

    def count_hi(cand):
        def body(start, size, acc):
            hit = jnp.where(hi_ref[pl.ds(start, size), :] >= cand, jnp.ones((), BF16), jnp.zeros((), BF16))
            part = hit[0:COUNT_ROWS]
            for r in range(1, size // COUNT_ROWS):
                part = part + hit[r * COUNT_ROWS:(r + 1) * COUNT_ROWS]
            return acc + part.astype(F32)
        return jnp.sum(sweep(body, jnp.zeros((COUNT_ROWS, tq), F32)), axis=0, keepdims=True)

    def count_lo(cand, n_above_bucket):
        c16 = ((cand & LOW16_MASK) - LOW16_OFFSET).astype(jnp.int16)

        def body(start, size, acc):
            hit = jnp.where(lo_ref[pl.ds(start, size), :] >= c16, jnp.ones((), jnp.int16), jnp.zeros((), jnp.int16))
            part = hit[0:COUNT_ROWS]
            for r in range(1, size // COUNT_ROWS):
                part = part + hit[r * COUNT_ROWS:(r + 1) * COUNT_ROWS]
            return acc + part.astype(jnp.int32)
        hits = jnp.sum(sweep(body, jnp.zeros((COUNT_ROWS, tq), jnp.int32)), axis=0, keepdims=True)
        return n_above_bucket + hits

    def step(cand, cnt, state):
        thr, done, n_ge, n_gt = state
        live = done == 0
        ok = (cnt >= top_k) & live
        return (jnp.where(ok, cand, thr), jnp.where(ok & (cnt == top_k), 1, done),
                jnp.where(ok, cnt, n_ge), jnp.where((cnt < top_k) & live, cnt, n_gt))

    def bisect_hi(it, state):
        cand = state[0] + jnp.left_shift(jnp.int32(1), 15 - it)
        pattern = jnp.where(cand >= 0, cand, cand ^ BF16_MAGNITUDE_BITS)
        pattern = jnp.where((pattern & BF16_EXPONENT_BITS) == 0, jnp.where(cand > 0, BF16_MIN_NORMAL_BITS, 0), pattern)
        cand_f = lax.bitcast_convert_type(jnp.left_shift(pattern, 16), F32).astype(BF16)
        return step(cand, count_hi(cand_f).astype(jnp.int32), state)

    def bisect_lo(carry):
        g, state = carry
        for u in range(LOW_PASSES_PER_CHECK):
            cand = state[0] + jnp.left_shift(jnp.int32(1), 15 - (g * LOW_PASSES_PER_CHECK + u))
            state = step(cand, count_lo(cand, n_bucket), state)
        return g + 1, state

    state = (jnp.full((1, tq), -2 ** 15, jnp.int32), jnp.where(adm_end <= top_k, 1, 0), adm_end,
             jnp.zeros((1, tq), jnp.int32))
    t16, done, n_ge, n_gt = lax.fori_loop(0, 16, bisect_hi, state)
    base = jnp.left_shift(t16, 16)
    n_bucket = n_gt

    def mark_bucket(start, size, c):
        off = keys_ref[pl.ds(start, size), :] - base
        inside = lax.bitcast_convert_type(off, jnp.uint32) < jnp.uint32(LOW16_MASK + 1)
        lo_ref[pl.ds(start, size), :] = jnp.where(inside, off - LOW16_OFFSET, -LOW16_OFFSET).astype(jnp.int16)
        return c

    sweep(mark_bucket, 0)
    above = count_lo(base + 1, n_bucket)
    exact = (done == 0) & (above == top_k)
    short = (done == 0) & (above < top_k)
    state = (jnp.where(exact, base + 1, base), jnp.where(above <= top_k, 1, done),
             jnp.where(exact, above, n_ge), jnp.where(short, above, n_gt))
    _, (thr, _, n_ge, n_gt) = lax.while_loop(
        lambda c: (c[0] < 16 // LOW_PASSES_PER_CHECK) & (jnp.min(c[1][1]) == 0), bisect_lo, (jnp.int32(0), state))
    thr = jnp.maximum(thr, INT_MIN + 1)

    @pl.when(jnp.max(n_ge) > top_k)
    def _():
        need = (top_k - n_gt).astype(F32)

        r2 = lax.broadcasted_iota(jnp.int32, (DEMOTE_ROWS, DEMOTE_ROWS), 0)
        c2 = lax.broadcasted_iota(jnp.int32, (DEMOTE_ROWS, DEMOTE_ROWS), 1)
        lower = jnp.where(c2 < r2, 1.0, 0.0).astype(BF16)

        def demote(start, size, seen):
            chunks = []
            for c0 in range(0, size, DEMOTE_ROWS):
                kblk = keys_ref[pl.ds(start + c0, DEMOTE_ROWS), :]
                eqf = jnp.where(kblk == thr, 1.0, 0.0)
                chunks.append((c0, kblk, eqf, jnp.dot(lower, eqf.astype(BF16), preferred_element_type=F32)))
            for c0, kblk, eqf, within in chunks:
                drop = (kblk == thr) & (within >= need - seen)
                keys_ref[pl.ds(start + c0, DEMOTE_ROWS), :] = jnp.where(drop, thr - 1, kblk)
                seen = seen + within[DEMOTE_ROWS - 1:DEMOTE_ROWS] + eqf[DEMOTE_ROWS - 1:DEMOTE_ROWS]
            return seen

        sweep(demote, jnp.zeros((1, tq), F32))

    acc_ref[...] = jnp.zeros_like(acc_ref)
    grp = min(DSA_GROUP, DSA_HEADS * tq)
    n_groups = DSA_HEADS * tq // grp
    bound = jnp.max(bound_ref[...])

    def attend_bounded(start, size, c):
        bias = jnp.where(keys_ref[pl.ds(start, size), :] >= thr, -bound, -jnp.inf)
        bias = jnp.concatenate([bias] * (grp // tq), axis=1)
        kb = dk4_ref[0, pl.ds(start, size), :]
        vt = dvt_ref[0, :, pl.ds(start, size)]
        for g in range(n_groups):
            sl = slice(g * grp, (g + 1) * grp)
            s = lax.dot_general(kb, q8_ref[sl, :], _NT, preferred_element_type=F32) + bias
            acc_ref[:, sl] += jnp.dot(vt, jnp.exp2(s).astype(BF16), preferred_element_type=F32)
        return c

    def attend(start, size, c):
        bias = jnp.where(keys_ref[pl.ds(start, size), :] >= thr, 0.0, -jnp.inf)
        bias = jnp.concatenate([bias] * (grp // tq), axis=1)
        kb = dk4_ref[0, pl.ds(start, size), :]
        vt = dvt_ref[0, :, pl.ds(start, size)]

        def scores(g):
            return lax.dot_general(kb, q8_ref[g * grp:(g + 1) * grp, :], _NT,
                                   preferred_element_type=F32) + bias

        ahead = [scores(g) for g in range(min(QK_AHEAD, n_groups))]
        for g in range(n_groups):
            s = ahead.pop(0)
            if g + QK_AHEAD < n_groups:
                ahead.append(scores(g + QK_AHEAD))
            sl = slice(g * grp, (g + 1) * grp)
            m_old = m_ref[:, sl]
            m_new = jnp.maximum(m_old, jnp.max(s, axis=0, keepdims=True))
            m_safe = jnp.where(m_new == -jnp.inf, 0.0, m_new)
            p = jnp.exp2((s - m_safe).astype(BF16))
            alpha = jnp.exp2(m_old - m_safe)
            m_ref[:, sl] = m_new
            acc_ref[:, sl] = alpha * acc_ref[:, sl] + jnp.dot(vt, p, preferred_element_type=F32)
        return c

    @pl.when(bound <= MAX_FIXED_SHIFT)
    def _():
        sweep(attend_bounded, 0)

    @pl.when(bound > MAX_FIXED_SHIFT)
    def _():
        m_ref[...] = jnp.full_like(m_ref, -jnp.inf)
        sweep(attend, 0)

    out = acc_ref[0:HEAD_DIM, :] / acc_ref[HEAD_DIM:HEAD_DIM + 1, :]
    for h in range(DSA_HEADS):
        o_ref[0, h * HEAD_DIM:(h + 1) * HEAD_DIM, :] = out[:, h * tq:(h + 1) * tq].astype(BF16)


def _dsa_call(dq, iq, wt, dk4, dvt, ik8, hm8, hm4, bound, top_k):
    bsz, s, _ = dq.shape
    tq = min(DSA_Q_TILE, s)

    def tok(w):
        return pl.BlockSpec((1, tq, w), lambda b, i: (b, i, 0))

    def seq(w):
        return pl.BlockSpec((1, s, w), lambda b, i: (b, 0, 0))

    def seqt(r):
        return pl.BlockSpec((1, r, s), lambda b, i: (b, 0, 0))

    return pl.pallas_call(
        functools.partial(_dsa_kernel, top_k=top_k),
        grid=(bsz, s // tq),
        in_specs=[tok(DSA_WIDTH), tok(IDX_WIDTH), seqt(2 * IDX_HEADS), seq(DK_TILED), seqt(V_ROWS), seq(IDX_WIDTH),
                  pl.BlockSpec(hm8.shape, lambda b, i: (0, 0)), pl.BlockSpec(hm4.shape, lambda b, i: (0, 0)),
                  pl.BlockSpec((1, LANES), lambda b, i: (0, 0))],
        out_specs=pl.BlockSpec((1, DSA_WIDTH, tq), lambda b, i: (b, 0, i)),
        out_shape=jax.ShapeDtypeStruct((bsz, DSA_WIDTH, s), BF16),
        scratch_shapes=[pltpu.VMEM((IDX_HEADS * tq, IDX_WIDTH), BF16), pltpu.VMEM((DSA_HEADS * tq, DK_TILED), BF16),
                        pltpu.VMEM((s, tq), jnp.int32), pltpu.VMEM((s, tq), BF16),
                        pltpu.VMEM((s, tq), jnp.int16),
                        pltpu.VMEM((1, DSA_HEADS * tq), F32),
                        pltpu.VMEM((V_ROWS, DSA_HEADS * tq), F32)],
        compiler_params=_cparams(("parallel", "arbitrary")),
        name="dsa_attn",
    )(dq, iq, wt, dk4, dvt, ik8, hm8, hm4, bound)


def _merge_kernel(x_ref, mod_ref, g_ref, ya_ref, yb_ref, wg_ref, wfa_ref, wfb_ref, wo_ref, o_ref):
    x = x_ref[0]
    d = x.shape[1]
    h = _modulated(x, mod_ref, g_ref, 1).astype(BF16)
    zg = jnp.dot(h, wg_ref[...], preferred_element_type=F32)
    pa = lax.dot_general(ya_ref[0], wfa_ref[...], _TN, preferred_element_type=F32)
    pb = lax.dot_general(yb_ref[0], wfb_ref[...], _TN, preferred_element_type=F32)
    merged = jax.nn.sigmoid(zg[:, :d]) * pa + jax.nn.sigmoid(zg[:, d:]) * pb
    y = jnp.dot(merged.astype(BF16), wo_ref[...], preferred_element_type=F32)
    o_ref[0] = x + mod_ref[0, 5:6, :] * y


def _merge_call(x, mod, g, ya, yb, wg, wfa, wfb, wo):
    bsz, s, d = x.shape
    tm = min(TOKEN_TILE, s)

    def tok(w):
        return pl.BlockSpec((1, tm, w), lambda b, i: (b, i, 0))

    def tokt(w):
        return pl.BlockSpec((1, w, tm), lambda b, i: (b, 0, i))

    return pl.pallas_call(
        _merge_kernel,
        grid=(bsz, s // tm),
        in_specs=[tok(d),
                  pl.BlockSpec((1, 3 * N_SUBLAYERS, d), lambda b, i: (b, 0, 0)),
                  _const_spec((N_SUBLAYERS, d)),
                  tokt(FOX_WIDTH), tokt(DSA_WIDTH),
                  _const_spec(wg.shape), _const_spec(wfa.shape), _const_spec(wfb.shape), _const_spec(wo.shape)],
        out_specs=tok(d),
        out_shape=jax.ShapeDtypeStruct(x.shape, F32),
        compiler_params=_cparams(("parallel", "parallel")),
        name="merge_out",
    )(x, mod, g, ya, yb, wg, wfa, wfb, wo)


def _head_mask(n_heads, width):
    return jnp.asarray(np.kron(np.eye(n_heads), np.ones((1, width))), BF16)


def _rope_freq_row(rot_dim, period):
    inv_freq = ROPE_THETA ** (-jnp.arange(0, rot_dim, 2, dtype=F32) / rot_dim)
    half = rot_dim // 2
    head = jnp.concatenate([-inv_freq, inv_freq, jnp.zeros((period - 2 * half,), F32)])
    return jnp.tile(head, LANES // period)


def kernel(x, c, positions, ada_w, ada_b, norm_g, ffn1_w1, ffn1_w3, ffn1_w2, w_in, fox_f_bias, fox_qk_g, dsa_qk_g,
           w_br_fox, w_br_dsa, w_out, ffn2_w1, ffn2_w3, ffn2_w2):
    bsz, s, d = x.shape
    top_k = min(TOPK_MAX, s // 4)
    depth = ada_w.shape[0]
    pos = jnp.broadcast_to(positions.astype(F32)[:, :, None], (bsz, s, LANES))
    freq = jnp.stack([_rope_freq_row(HEAD_DIM // ROPE_FRACTION, HEAD_DIM),
                      _rope_freq_row(IDX_DIM // ROPE_FRACTION, IDX_DIM)])
    gmat = jnp.asarray(np.kron(np.eye(FOX_HEADS), np.ones((HEAD_DIM, HEAD_DIM))), BF16)
    hm2 = _head_mask(2, HEAD_DIM)
    hm4 = _head_mask(4, HEAD_DIM)
    hm8 = _head_mask(IDX_HEADS, IDX_DIM)
    fsel = jnp.asarray(np.tile(np.eye(FOX_HEADS), (1, LANES // FOX_HEADS))
                       * (np.arange(LANES) < 3 * FOX_HEADS), BF16).reshape(FOX_HEADS // 2, 2, LANES)
    o_fq, o_fk, o_fv = 0, FOX_WIDTH, 2 * FOX_WIDTH
    o_ff = 3 * FOX_WIDTH
    o_dq = o_ff + FOX_HEADS
    o_dk = o_dq + DSA_WIDTH
    o_dv = o_dk + HEAD_DIM
    o_iq = o_dv + HEAD_DIM
    o_ik = o_iq + IDX_WIDTH
    o_iw = o_ik + IDX_DIM
    o_ga = o_iw + IDX_HEADS

    for l in range(depth):
        mod = _ada_call(c, ada_w[l], ada_b[l]).reshape(bsz, 3 * N_SUBLAYERS, d)
        g = norm_g[l]
        x = _ffn_call(x, mod, g, ffn1_w1[l].astype(BF16), ffn1_w3[l].astype(BF16), ffn1_w2[l].astype(BF16), 0)

        w = w_in[l]
        wa = jnp.concatenate([w[:, o_fq:o_fq + 2 * FOX_WIDTH], w[:, o_dq:o_dq + DSA_WIDTH]], axis=1).astype(BF16)
        pad = jnp.zeros((d, LANES - FOX_HEADS), F32)
        wb = jnp.concatenate([jnp.tile(w[:, o_dk:o_dk + HEAD_DIM], (1, 4)),
                              jnp.tile(w[:, o_ik:o_ik + IDX_DIM], (1, IDX_HEADS)),
                              w[:, o_iq:o_iq + IDX_WIDTH],
                              w[:, o_ff:o_ff + FOX_HEADS], pad], axis=1).astype(BF16)
        zrows = jnp.zeros((FOX_HEADS, V_ROWS - HEAD_DIM, d), F32)
        wfv = jnp.transpose(w[:, o_fv:o_fv + FOX_WIDTH]).reshape(FOX_HEADS, HEAD_DIM, d)
        wt = jnp.concatenate([jnp.concatenate([wfv, zrows], axis=1).reshape(FOX_HEADS * V_ROWS, d),
                              jnp.transpose(w[:, o_dv:o_dv + HEAD_DIM]), zrows[0],
                              jnp.zeros((IDX_HEADS, d), F32), jnp.transpose(w[:, o_iw:o_iw + IDX_HEADS])],
                             axis=0).astype(BF16)
        ones = jnp.asarray(np.tile((np.arange(FOX_HEADS * V_ROWS) % V_ROWS >= HEAD_DIM)[:, None], (1, LANES)), F32)
        wg = w[:, o_ga:o_ga + 2 * d].astype(BF16)
        gains = jnp.stack([jnp.tile(fox_qk_g[l, 0], FOX_HEADS), jnp.tile(fox_qk_g[l, 1], FOX_HEADS),
                           jnp.tile(dsa_qk_g[l, 0], DSA_HEADS), jnp.tile(dsa_qk_g[l, 1], DSA_HEADS)]).astype(F32)
        fbias = jnp.concatenate([fox_f_bias[l].astype(F32), jnp.zeros((LANES - FOX_HEADS,), F32)]).reshape(1, LANES)

        fq, fk, dq, dk4, ik8, iq, faug, fvt, dvt, iwt = _mix_in_call(x, mod, g, pos, wa, wb, wt, ones, gmat, gains,
                                                                       freq, fbias)
        unit = jnp.full((1, LANES), HEAD_DIM ** 0.5 * LOG2E * NORM_SLACK, F32)
        fox_bound = unit * (jnp.max(jnp.abs(fox_qk_g[l, 0])) * jnp.max(jnp.abs(fox_qk_g[l, 1])))
        dsa_bound = unit * (jnp.max(jnp.abs(dsa_qk_g[l, 0])) * jnp.max(jnp.abs(dsa_qk_g[l, 1])))
        yat = _fox_call(fq, fk, faug, fvt, hm2, fsel, fox_bound)
        ybt = _dsa_call(dq, iq, iwt, dk4, dvt, ik8, hm8, hm4, dsa_bound, top_k)
        x = _merge_call(x, mod, g, yat, ybt, wg, w_br_fox[l].astype(BF16), w_br_dsa[l].astype(BF16),
                        w_out[l].astype(BF16))
        x = _ffn_call(x, mod, g, ffn2_w1[l].astype(BF16), ffn2_w3[l].astype(BF16), ffn2_w2[l].astype(BF16), 2)
    return x
```

```python
import functools

import numpy as np
import jax
import jax.numpy as jnp
from jax import lax
from jax.experimental import pallas as pl
from jax.experimental.pallas import tpu as pltpu

CHUNK = 64
HEAD_DIM = 64
FOX_HEADS = 8
DSA_HEADS = 8
IDX_HEADS = 8
IDX_DIM = 32
TOPK_MAX = 256
ROPE_THETA = 500000.0
ROPE_FRACTION = 4
N_SUBLAYERS = 3
NORM_EPS = 1e-6
FOX_WIDTH = FOX_HEADS * HEAD_DIM
DSA_WIDTH = DSA_HEADS * HEAD_DIM
IDX_WIDTH = IDX_HEADS * IDX_DIM
DK_TILED = 4 * HEAD_DIM

LANES = 128
VMEM_LIMIT_BYTES = 56 * 1024 * 1024

ADA_COLS = 1152
TOKEN_TILE = 512
FOX_TILE = 1024
DSA_Q_TILE = 256
DSA_K_TILE = 1024
FOX_GROUP = 512
DSA_GROUP = 1024
QK_AHEAD = 3
LOW_PASSES_PER_CHECK = 2
DEMOTE_ROWS = 128
COUNT_ROWS = 32

MAX_FIXED_SHIFT = 60.0
FOX_ZERO_EXP = 160.0
V_ROWS = HEAD_DIM + 16
LOG2E = 1.4426950408889634
INT_MIN = -2 ** 31
F32_MAGNITUDE_BITS = 0x7FFFFFFF
LOW16_MASK = 0xFFFF
LOW16_OFFSET = 32768
TOP16_MASK = -65536
BF16_MAGNITUDE_BITS = 0x7FFF
BF16_EXPONENT_BITS = 0x7F80
BF16_MIN_NORMAL_BITS = 0x0080
NORM_SLACK = 1.02
F32_MIN_NORMAL = 1.1754943508222875e-38
F32 = jnp.float32
BF16 = jnp.bfloat16
_NT = (((1,), (1,)), ((), ()))
_TN = (((0,), (0,)), ((), ()))


def _cparams(sem):
    return pltpu.CompilerParams(dimension_semantics=sem, vmem_limit_bytes=VMEM_LIMIT_BYTES)


def _const_spec(shape):
    nd = len(shape)
    return pl.BlockSpec(shape, lambda *_: (0,) * nd, pipeline_mode=pl.Buffered(1))


def _modulated(x, mod_ref, g_ref, sub):
    ms = jnp.mean(x * x, axis=-1, keepdims=True)
    y = x * lax.rsqrt(ms + NORM_EPS) * g_ref[sub:sub + 1, :]
    shift = mod_ref[0, 3 * sub:3 * sub + 1, :]
    scale = mod_ref[0, 3 * sub + 1:3 * sub + 2, :]
    return y * (1.0 + scale) + shift


def _ada_kernel(c_ref, w_ref, b_ref, o_ref):
    c = c_ref[...]
    cond = c * jax.nn.sigmoid(c)
    o_ref[...] = jnp.dot(cond, w_ref[...], preferred_element_type=F32,
                         precision=lax.Precision.HIGHEST) + b_ref[...]


def _ada_call(c, w, b):
    bsz, d = c.shape
    n = w.shape[1]
    tn = ADA_COLS
    return pl.pallas_call(
        _ada_kernel,
        grid=(n // tn,),
        in_specs=[pl.BlockSpec((bsz, d), lambda j: (0, 0)),
                  pl.BlockSpec((d, tn), lambda j: (0, j)),
                  pl.BlockSpec((1, tn), lambda j: (0, j))],
        out_specs=pl.BlockSpec((bsz, tn), lambda j: (0, j)),
        out_shape=jax.ShapeDtypeStruct((bsz, n), F32),
        compiler_params=_cparams(("arbitrary",)),
        name="ada_ln",
    )(c, w, b.reshape(1, n))


def _ffn_kernel(x_ref, mod_ref, g_ref, w1_ref, w3_ref, w2_ref, o_ref, *, sub):
    x = x_ref[0]
    h = _modulated(x, mod_ref, g_ref, sub).astype(BF16)
    a = jnp.dot(h, w1_ref[...], preferred_element_type=F32)
    b = jnp.dot(h, w3_ref[...], preferred_element_type=F32)
    act = (a * jax.nn.sigmoid(a) * b).astype(BF16)
    y = jnp.dot(act, w2_ref[...], preferred_element_type=F32)
    gate = mod_ref[0, 3 * sub + 2:3 * sub + 3, :]
    o_ref[0] = x + (0.5 * gate) * y


def _ffn_call(x, mod, g, w1, w3, w2, sub):
    bsz, s, d = x.shape
    tm = min(TOKEN_TILE, s)
    f = w1.shape[1]
    tok = pl.BlockSpec((1, tm, d), lambda b, i: (b, i, 0))
    return pl.pallas_call(
        functools.partial(_ffn_kernel, sub=sub),
        grid=(bsz, s // tm),
        in_specs=[tok,
                  pl.BlockSpec((1, 3 * N_SUBLAYERS, d), lambda b, i: (b, 0, 0)),
                  _const_spec((N_SUBLAYERS, d)),
                  _const_spec((d, f)), _const_spec((d, f)), _const_spec((f, d))],
        out_specs=tok,
        out_shape=jax.ShapeDtypeStruct(x.shape, F32),
        compiler_params=_cparams(("parallel", "parallel")),
        name=f"ffn{sub}",
    )(x, mod, g, w1, w3, w2)


def _split3(v):
    hi = v.astype(BF16)
    r = v - hi.astype(F32)
    mid = r.astype(BF16)
    lo = (r - mid.astype(F32)).astype(BF16)
    return hi, mid, lo


def _group_norm(z, gmat, gain, group):
    ssq = jnp.dot((z * z).astype(BF16), gmat, preferred_element_type=F32)
    return z * lax.rsqrt(ssq * (1.0 / group) + NORM_EPS) * gain


def _rope(z, cos, sin, half, period):
    width = z.shape[1]
    lane = lax.broadcasted_iota(jnp.int32, z.shape, 1) % period
    up = pltpu.roll(z, width - half, axis=1)
    dn = pltpu.roll(z, half, axis=1)
    return z * cos + jnp.where(lane < half, up, dn) * sin


def _mix_in_kernel(x_ref, mod_ref, g_ref, pos_ref, wa_ref, wb_ref, wt_ref, ones_ref, gmat_ref, gains_ref, freq_ref,
                   fbias_ref, fq_ref, fk_ref, dq_ref, dk4_ref, ik8_ref, iq_ref, faug_ref, fvt_ref, dvt_ref, iwt_ref,
                   carry_ref):
    @pl.when(pl.program_id(1) == 0)
    def _():
        carry_ref[...] = jnp.zeros_like(carry_ref)

    x = x_ref[0]
    tm = x.shape[0]
    h = _modulated(x, mod_ref, g_ref, 1).astype(BF16)
    za = jnp.dot(h, wa_ref[...], preferred_element_type=F32)
    zb = jnp.dot(h, wb_ref[...], preferred_element_type=F32)
    zt = lax.dot_general(wt_ref[...], h, _NT, preferred_element_type=F32)
    n_v = FOX_HEADS * V_ROWS
    fvt_ref[0] = (zt[0:n_v] + jnp.concatenate([ones_ref[...]] * (tm // LANES), axis=1)).astype(BF16)
    dvt_ref[0] = (zt[n_v:n_v + V_ROWS]
                  + jnp.concatenate([ones_ref[0:V_ROWS, :]] * (tm // LANES), axis=1)).astype(BF16)
    iwt_ref[0] = zt[n_v + V_ROWS:, :] * ((IDX_HEADS * IDX_DIM) ** -0.5)

    gmat = gmat_ref[...]
    scale = HEAD_DIM ** -0.5 * LOG2E
    fq = _group_norm(za[:, 0:FOX_WIDTH], gmat, gains_ref[0:1, :], HEAD_DIM) * scale
    fk = _group_norm(za[:, FOX_WIDTH:2 * FOX_WIDTH], gmat, gains_ref[1:2, :], HEAD_DIM)
    fq_ref[0] = fq.astype(BF16)
    fk_ref[0] = fk.astype(BF16)

    pos = pos_ref[0]
    ang_a = pos * freq_ref[0:1, :]
    ang_i = pos * freq_ref[1:2, :]
    cos_a, sin_a = jnp.cos(ang_a), jnp.sin(ang_a)
    cos_i, sin_i = jnp.cos(ang_i), jnp.sin(ang_i)
    rot_a = HEAD_DIM // ROPE_FRACTION // 2
    rot_i = IDX_DIM // ROPE_FRACTION // 2

    dq = _group_norm(za[:, 2 * FOX_WIDTH:2 * FOX_WIDTH + DSA_WIDTH], gmat, gains_ref[2:3, :], HEAD_DIM)
    dq = _rope(dq, jnp.concatenate([cos_a] * 4, axis=1), jnp.concatenate([sin_a] * 4, axis=1), rot_a, HEAD_DIM)
    dq_ref[0] = (dq * scale).astype(BF16)

    dk4 = _group_norm(zb[:, 0:DK_TILED], gmat[0:DK_TILED, 0:DK_TILED], gains_ref[3:4, 0:DK_TILED], HEAD_DIM)
    dk4 = _rope(dk4, jnp.concatenate([cos_a] * 2, axis=1), jnp.concatenate([sin_a] * 2, axis=1), rot_a, HEAD_DIM)
    dk4_ref[0] = dk4.astype(BF16)

    cos_i2 = jnp.concatenate([cos_i] * 2, axis=1)
    sin_i2 = jnp.concatenate([sin_i] * 2, axis=1)
    o_ik, o_iq, o_sm = DK_TILED, DK_TILED + IDX_WIDTH, DK_TILED + 2 * IDX_WIDTH
    ik8_ref[0] = _rope(zb[:, o_ik:o_iq], cos_i2, sin_i2, rot_i, IDX_DIM).astype(BF16)
    iq_ref[0] = _rope(zb[:, o_iq:o_sm], cos_i2, sin_i2, rot_i, IDX_DIM).astype(BF16)

    sm = zb[:, o_sm:o_sm + LANES]
    v = sm + fbias_ref[...]
    logf = jnp.minimum(v, 0.0) - jnp.log1p(jnp.exp(-jnp.abs(v)))
    row = lax.broadcasted_iota(jnp.int32, (tm, tm), 0)
    col = lax.broadcasted_iota(jnp.int32, (tm, tm), 1)
    tri = jnp.where(row >= col, 1.0, 0.0).astype(BF16)
    hi, mid, lo = _split3(logf)
    csum = (jnp.dot(tri, hi, preferred_element_type=F32) + jnp.dot(tri, mid, preferred_element_type=F32)
            + jnp.dot(tri, lo, preferred_element_type=F32)) + carry_ref[...]
    carry_ref[...] = csum[tm - 1:tm, :]
    lane = lax.broadcasted_iota(jnp.int32, sm.shape, 1)

    nf = -csum * LOG2E
    t0 = nf.astype(BF16).astype(F32)
    r1 = nf - t0
    t1 = r1.astype(BF16).astype(F32)
    t2 = r1 - t1
    aug = jnp.where(lane < FOX_HEADS, t0,
                    jnp.where(lane < 2 * FOX_HEADS, pltpu.roll(t1, FOX_HEADS, axis=1),
                              jnp.where(lane < 3 * FOX_HEADS, pltpu.roll(t2, 2 * FOX_HEADS, axis=1),
                                        jnp.where(lane < 6 * FOX_HEADS, 1.0, 0.0))))
    faug_ref[0] = aug.astype(BF16)


def _mix_in_call(x, mod, g, pos, wa, wb, wt, ones, gmat, gains, freq, fbias):
    bsz, s, d = x.shape
    tm = min(TOKEN_TILE, s)

    def tok(w):
        return pl.BlockSpec((1, tm, w), lambda b, i: (b, i, 0))

    def tokt(r):
        return pl.BlockSpec((1, r, tm), lambda b, i: (b, 0, i))

    widths = (FOX_WIDTH, FOX_WIDTH, DSA_WIDTH, DK_TILED, IDX_WIDTH, IDX_WIDTH, LANES)
    rows = ((FOX_HEADS * V_ROWS, BF16), (V_ROWS, BF16), (2 * IDX_HEADS, F32))
    out_shapes = [jax.ShapeDtypeStruct((bsz, s, w), BF16) for w in widths]
    out_shapes += [jax.ShapeDtypeStruct((bsz, r, s), dt) for r, dt in rows]
    return pl.pallas_call(
        _mix_in_kernel,
        grid=(bsz, s // tm),
        in_specs=[tok(d),
                  pl.BlockSpec((1, 3 * N_SUBLAYERS, d), lambda b, i: (b, 0, 0)),
                  _const_spec((N_SUBLAYERS, d)),
                  tok(LANES),
                  _const_spec(wa.shape), _const_spec(wb.shape), _const_spec(wt.shape), _const_spec(ones.shape),
                  _const_spec(gmat.shape), _const_spec(gains.shape), _const_spec(freq.shape), _const_spec(fbias.shape)],
        out_specs=[tok(w) for w in widths] + [tokt(r) for r, _ in rows],
        out_shape=out_shapes,
        scratch_shapes=[pltpu.VMEM((1, LANES), F32)],
        compiler_params=_cparams(("arbitrary", "arbitrary")),
        name="mix_in",
    )(x, mod, g, pos, wa, wb, wt, ones, gmat, gains, freq, fbias)


def _fox_kernel(q_ref, qaug_ref, k_ref, faug_ref, vt_ref, hm_ref, fsel_ref, bound_ref, o_ref, qf_ref, m_ref, acc_ref):
    i = pl.program_id(2)
    tq = q_ref.shape[1]
    tk = tq
    q = q_ref[0]
    own = qaug_ref[0].astype(F32)
    for a in range(2):
        sel_row = fsel_ref[0, a:a + 1, :].astype(F32)
        plus_f = pltpu.roll(-(own * sel_row), 3 * FOX_HEADS, axis=1)
        qf_ref[a * tq:(a + 1) * tq, 0:LANES] = q * hm_ref[a:a + 1, :]
        qf_ref[a * tq:(a + 1) * tq, LANES:2 * LANES] = (sel_row + plus_f).astype(BF16)
    acc_ref[...] = jnp.zeros_like(acc_ref)
    bound = jnp.max(bound_ref[...])
    bounded = bound <= MAX_FIXED_SHIFT

    def block_bounded(j, masked):
        start = pl.multiple_of(j * tk, tk)
        kf = jnp.concatenate([k_ref[0, pl.ds(start, tk), :], faug_ref[0, pl.ds(start, tk), :]], axis=1)
        grp = min(FOX_GROUP, tq)
        for g in range(2 * tq // grp):
            rows = min(tk, (g * grp) % tq + grp) if masked else tk
            sl = slice(g * grp, (g + 1) * grp)
            s = lax.dot_general(kf[:rows], qf_ref[sl, :], _NT, preferred_element_type=F32) - bound
            if masked:
                krow = lax.broadcasted_iota(jnp.int32, s.shape, 0)
                qcol = lax.broadcasted_iota(jnp.int32, s.shape, 1) + (g * grp) % tq
                s = jnp.where(krow <= qcol, s, -jnp.inf)
            a = (g * grp) // tq
            vt = vt_ref[0, a * V_ROWS:(a + 1) * V_ROWS, pl.ds(start, rows)]
            acc_ref[:, sl] += jnp.dot(vt, jnp.exp2(s).astype(BF16), preferred_element_type=F32)

    def block(j, masked):
        start = pl.multiple_of(j * tk, tk)
        kf = jnp.concatenate([k_ref[0, pl.ds(start, tk), :], faug_ref[0, pl.ds(start, tk), :]], axis=1)
        grp = min(FOX_GROUP, tq)
        n_groups = 2 * tq // grp

        def rows(g):
            return min(tk, (g * grp) % tq + grp) if masked else tk

        def scores(g):
            s = lax.dot_general(kf[:rows(g)], qf_ref[g * grp:(g + 1) * grp, :], _NT,
                                preferred_element_type=F32)
            if masked:
                krow = lax.broadcasted_iota(jnp.int32, s.shape, 0)
                qcol = lax.broadcasted_iota(jnp.int32, s.shape, 1) + (g * grp) % tq
                s = jnp.where(krow <= qcol, s, -jnp.inf)
            return s

        ahead = [scores(g) for g in range(min(QK_AHEAD, n_groups))]
        for g in range(n_groups):
            s = ahead.pop(0)
            if g + QK_AHEAD < n_groups:
                ahead.append(scores(g + QK_AHEAD))
            sl = slice(g * grp, (g + 1) * grp)
            m_old = m_ref[:, sl]
            m_new = jnp.maximum(m_old, jnp.max(s, axis=0, keepdims=True))
            p = jnp.exp2((s - m_new).astype(BF16))
            alpha = jnp.exp2(m_old - m_new)
            m_ref[:, sl] = m_new
            a = (g * grp) // tq
            vt = vt_ref[0, a * V_ROWS:(a + 1) * V_ROWS, pl.ds(start, rows(g))]
            acc_ref[:, sl] = alpha * acc_ref[:, sl] + jnp.dot(vt, p, preferred_element_type=F32)

    def nf_of(rows16, a):
        return jnp.sum(rows16 * fsel_ref[0, a:a + 1, :].astype(F32), axis=1, keepdims=True)

    def sweep(block_fn, slack):
        block_fn(i, True)

        def older(jj, c):
            j = i - 1 - jj
            last = pl.multiple_of(j * tk + tk - 16, 16)
            tail_rows = faug_ref[0, pl.ds(last, 16), :].astype(F32)
            worst = -jnp.inf
            for a in range(2):
                worst = jnp.maximum(worst, jnp.max(nf_of(tail_rows, a)) - jnp.min(nf_of(own[0:16], a)))

            @pl.when(worst + slack > -FOX_ZERO_EXP)
            def _():
                block_fn(j, False)
            return c

        lax.fori_loop(0, i, older, 0)

    @pl.when(bounded)
    def _():
        sweep(block_bounded, 0.0)

    @pl.when(jnp.logical_not(bounded))
    def _():
        m_ref[...] = jnp.full_like(m_ref, -jnp.inf)
        sweep(block, 2.0 * bound)

    out = acc_ref[0:HEAD_DIM, :] / acc_ref[HEAD_DIM:HEAD_DIM + 1, :]
    for a in range(2):
        o_ref[0, a * HEAD_DIM:(a + 1) * HEAD_DIM, :] = out[:, a * tq:(a + 1) * tq].astype(BF16)


def _fox_call(fq, fk, faug, fvt, hm2, fsel, bound):
    bsz, s, _ = fq.shape
    tq = min(FOX_TILE, s)
    pairs = FOX_HEADS // 2
    return pl.pallas_call(
        _fox_kernel,
        grid=(bsz, pairs, s // tq),
        in_specs=[pl.BlockSpec((1, tq, LANES), lambda b, h, i: (b, i, h)),
                  pl.BlockSpec((1, tq, LANES), lambda b, h, i: (b, i, 0)),
                  pl.BlockSpec((1, s, LANES), lambda b, h, i: (b, 0, h)),
                  pl.BlockSpec((1, s, LANES), lambda b, h, i: (b, 0, 0)),
                  pl.BlockSpec((1, 2 * V_ROWS, s), lambda b, h, i: (b, h, 0)),
                  pl.BlockSpec((2, LANES), lambda b, h, i: (0, 0)),
                  pl.BlockSpec((1, 2, LANES), lambda b, h, i: (h, 0, 0)),
                  pl.BlockSpec((1, LANES), lambda b, h, i: (0, 0))],
        out_specs=pl.BlockSpec((1, LANES, tq), lambda b, h, i: (b, h, i)),
        out_shape=jax.ShapeDtypeStruct((bsz, FOX_WIDTH, s), BF16),
        scratch_shapes=[pltpu.VMEM((2 * tq, 2 * LANES), BF16),
                        pltpu.VMEM((1, 2 * tq), F32),
                        pltpu.VMEM((V_ROWS, 2 * tq), F32)],
        compiler_params=_cparams(("parallel", "parallel", "arbitrary")),
        name="fox_attn",
    )(fq, faug, fk, faug, fvt, hm2, fsel, bound)


def _dsa_kernel(dq_ref, iq_ref, wt_ref, dk4_ref, dvt_ref, ik8_ref, hm8_ref, hm4_ref, bound_ref, o_ref,
                iq8_ref, q8_ref, keys_ref, hi_ref, lo_ref, m_ref, acc_ref, *, top_k):
    i = pl.program_id(1)
    tq = dq_ref.shape[1]
    s_len = dk4_ref.shape[1]
    tk = min(DSA_K_TILE, s_len)
    half_tk = tk // 2
    q_start = i * tq
    rem = (q_start + tq) % tk
    n_full = (q_start + tq) // tk + jnp.where(rem > half_tk, 1, 0)
    has_tail = (rem > 0) & (rem <= half_tk)

    def sweep(fn, init):
        carry = lax.fori_loop(0, n_full, lambda j, c: fn(pl.multiple_of(j * tk, tk), tk, c), init)
        return lax.cond(has_tail, lambda c: fn(pl.multiple_of(n_full * tk, tk), half_tk, c), lambda c: c, carry)

    iq = iq_ref[0]
    dq = dq_ref[0]
    half = dq.shape[1] // 2
    for h in range(IDX_HEADS):
        iq8_ref[h * tq:(h + 1) * tq, :] = iq * hm8_ref[h:h + 1, :]
        q8_ref[h * tq:(h + 1) * tq, :] = dq[:, (h // 4) * half:(h // 4 + 1) * half] * hm4_ref[h % 4:h % 4 + 1, :]

    adm_end = q_start + (lax.broadcasted_iota(jnp.int32, (1, tq), 1) // CHUNK + 1) * CHUNK
    wt = wt_ref[0, :, pl.ds(pl.multiple_of(q_start, tq), tq)]

    def score_block(start, size, c):
        d = lax.dot_general(ik8_ref[0, pl.ds(start, size), :], iq8_ref[...], _NT, preferred_element_type=F32)
        sc = jnp.zeros((size, tq), F32)
        for h in range(IDX_HEADS):
            sc = sc + jnp.maximum(d[:, h * tq:(h + 1) * tq], 0.0) * wt[IDX_HEADS + h:IDX_HEADS + h + 1, :]
        sc = jnp.where(jnp.abs(sc) < F32_MIN_NORMAL, 0.0, sc)
        bits = lax.bitcast_convert_type(sc, jnp.int32)
        key = bits ^ ((bits >> 31) & F32_MAGNITUDE_BITS)
        top = lax.bitcast_convert_type(bits & TOP16_MASK, F32)
        keys_ref[pl.ds(start, size), :] = key
        hi_ref[pl.ds(start, size), :] = top.astype(BF16)
        return c

    sweep(score_block, 0)

    own = pl.ds(pl.multiple_of(q_start, tq), tq)
    adm = lax.broadcasted_iota(jnp.int32, (tq, tq), 0) + q_start < adm_end
    keys_ref[own, :] = jnp.where(adm, keys_ref[own, :], INT_MIN)
    hi_ref[own, :] = jnp.where(adm, hi_ref[own, :].astype(F32), jnp.nan).astype(BF16)
    swept_end = n_full * tk + jnp.where(has_tail, half_tk, 0)
    for r in range(1, max(half_tk // tq, 1)):
        @pl.when(q_start + r * tq < swept_end)
        def _():
            beyond = pl.ds(pl.multiple_of(q_start + r * tq, tq), tq)
            keys_ref[beyond, :] = jnp.full((tq, tq), INT_MIN, jnp.int32)
            hi_ref[beyond, :] = jnp.full((tq, tq), jnp.nan, BF16)

    def count_hi(cand):
        def body(start, size, acc):
            hit = jnp.where(hi_ref[pl.ds(start, size), :] >= cand, jnp.ones((), BF16), jnp.zeros((), BF16))
            part = hit[0:COUNT_ROWS]
            for r in range(1, size // COUNT_ROWS):
                part = part + hit[r * COUNT_ROWS:(r + 1) * COUNT_ROWS]
            return acc + part.astype(F32)
        return jnp.sum(sweep(body, jnp.zeros((COUNT_ROWS, tq), F32)), axis=0, keepdims=True)

    def count_lo(cand, n_above_bucket):
        c16 = ((cand & LOW16_MASK) - LOW16_OFFSET).astype(jnp.int16)

        def body(start, size, acc):
            hit = jnp.where(lo_ref[pl.ds(start, size), :] >= c16, jnp.ones((), jnp.int16), jnp.zeros((), jnp.int16))
            part = hit[0:COUNT_ROWS]
            for r in range(1, size // COUNT_ROWS):
                part = part + hit[r * COUNT_ROWS:(r + 1) * COUNT_ROWS]
            return acc + part.astype(jnp.int32)
        hits = jnp.sum(sweep(body, jnp.zeros((COUNT_ROWS, tq), jnp.int32)), axis=0, keepdims=True)
        return n_above_bucket + hits

    def step(cand, cnt, state):
        thr, done, n_ge, n_gt = state
        live = done == 0
        ok = (cnt >= top_k) & live
        return (jnp.where(ok, cand, thr), jnp.where(ok & (cnt == top_k), 1, done),
                jnp.where(ok, cnt, n_ge), jnp.where((cnt < top_k) & live, cnt, n_gt))

    def bisect_hi(it, state):
        cand = state[0] + jnp.left_shift(jnp.int32(1), 15 - it)
        pattern = jnp.where(cand >= 0, cand, cand ^ BF16_MAGNITUDE_BITS)
        pattern = jnp.where((pattern & BF16_EXPONENT_BITS) == 0, jnp.where(cand > 0, BF16_MIN_NORMAL_BITS, 0), pattern)
        cand_f = lax.bitcast_convert_type(jnp.left_shift(pattern, 16), F32).astype(BF16)
        return step(cand, count_hi(cand_f).astype(jnp.int32), state)

    def bisect_lo(carry):
        g, state = carry
        for u in range(LOW_PASSES_PER_CHECK):
            cand = state[0] + jnp.left_shift(jnp.int32(1), 15 - (g * LOW_PASSES_PER_CHECK + u))
            state = step(cand, count_lo(cand, n_bucket), state)
        return g + 1, state

    state = (jnp.full((1, tq), -2 ** 15, jnp.int32), jnp.where(adm_end <= top_k, 1, 0), adm_end,
             jnp.zeros((1, tq), jnp.int32))
    t16, done, n_ge, n_gt = lax.fori_loop(0, 16, bisect_hi, state)
    base = jnp.left_shift(t16, 16)
    n_bucket = n_gt

    def mark_bucket(start, size, acc):
        off = keys_ref[pl.ds(start, size), :] - base
        inside = lax.bitcast_convert_type(off, jnp.uint32) < jnp.uint32(LOW16_MASK + 1)
        lo = jnp.where(inside, off - LOW16_OFFSET, -LOW16_OFFSET).astype(jnp.int16)
        lo_ref[pl.ds(start, size), :] = lo
        hit = jnp.where(lo > jnp.int16(-LOW16_OFFSET), jnp.ones((), jnp.int16), jnp.zeros((), jnp.int16))
        part = hit[0:COUNT_ROWS]
        for r in range(1, size // COUNT_ROWS):
            part = part + hit[r * COUNT_ROWS:(r + 1) * COUNT_ROWS]
        return acc + part.astype(jnp.int32)

    above = n_bucket + jnp.sum(sweep(mark_bucket, jnp.zeros((COUNT_ROWS, tq), jnp.int32)), axis=0, keepdims=True)
    exact = (done == 0) & (above == top_k)
    short = (done == 0) & (above < top_k)
    state = (jnp.where(exact, base + 1, base), jnp.where(above <= top_k, 1, done),
             jnp.where(exact, above, n_ge), jnp.where(short, above, n_gt))
    _, (thr, _, n_ge, n_gt) = lax.while_loop(
        lambda c: (c[0] < 16 // LOW_PASSES_PER_CHECK) & (jnp.min(c[1][1]) == 0), bisect_lo, (jnp.int32(0), state))
    thr = jnp.maximum(thr, INT_MIN + 1)

    @pl.when(jnp.max(n_ge) > top_k)
    def _():
        need = (top_k - n_gt).astype(F32)

        r2 = lax.broadcasted_iota(jnp.int32, (DEMOTE_ROWS, DEMOTE_ROWS), 0)
        c2 = lax.broadcasted_iota(jnp.int32, (DEMOTE_ROWS, DEMOTE_ROWS), 1)
        lower = jnp.where(c2 < r2, 1.0, 0.0).astype(BF16)

        def demote(start, size, seen):
            chunks = []
            for c0 in range(0, size, DEMOTE_ROWS):
                kblk = keys_ref[pl.ds(start + c0, DEMOTE_ROWS), :]
                eqf = jnp.where(kblk == thr, 1.0, 0.0)
                chunks.append((c0, kblk, eqf, jnp.dot(lower, eqf.astype(BF16), preferred_element_type=F32)))
            for c0, kblk, eqf, within in chunks:
                drop = (kblk == thr) & (within >= need - seen)
                keys_ref[pl.ds(start + c0, DEMOTE_ROWS), :] = jnp.where(drop, thr - 1, kblk)
                seen = seen + within[DEMOTE_ROWS - 1:DEMOTE_ROWS] + eqf[DEMOTE_ROWS - 1:DEMOTE_ROWS]
            return seen

        sweep(demote, jnp.zeros((1, tq), F32))

    acc_ref[...] = jnp.zeros_like(acc_ref)
    grp = min(DSA_GROUP, DSA_HEADS * tq)
    n_groups = DSA_HEADS * tq // grp
    bound = jnp.max(bound_ref[...])

    def attend_bounded(start, size, c):
        bias = jnp.where(keys_ref[pl.ds(start, size), :] >= thr, -bound, -jnp.inf)
        bias = jnp.concatenate([bias] * (grp // tq), axis=1)
        kb = dk4_ref[0, pl.ds(start, size), :]
        vt = dvt_ref[0, :, pl.ds(start, size)]
        for g in range(n_groups):
            sl = slice(g * grp, (g + 1) * grp)
            s = lax.dot_general(kb, q8_ref[sl, :], _NT, preferred_element_type=F32) + bias
            acc_ref[:, sl] += jnp.dot(vt, jnp.exp2(s).astype(BF16), preferred_element_type=F32)
        return c

    def attend(start, size, c):
        bias = jnp.where(keys_ref[pl.ds(start, size), :] >= thr, 0.0, -jnp.inf)
        bias = jnp.concatenate([bias] * (grp // tq), axis=1)
        kb = dk4_ref[0, pl.ds(start, size), :]
        vt = dvt_ref[0, :, pl.ds(start, size)]

        def scores(g):
            return lax.dot_general(kb, q8_ref[g * grp:(g + 1) * grp, :], _NT,
                                   preferred_element_type=F32) + bias

        ahead = [scores(g) for g in range(min(QK_AHEAD, n_groups))]
        for g in range(n_groups):
            s = ahead.pop(0)
            if g + QK_AHEAD < n_groups:
                ahead.append(scores(g + QK_AHEAD))
            sl = slice(g * grp, (g + 1) * grp)
            m_old = m_ref[:, sl]
            m_new = jnp.maximum(m_old, jnp.max(s, axis=0, keepdims=True))
            m_safe = jnp.where(m_new == -jnp.inf, 0.0, m_new)
            p = jnp.exp2((s - m_safe).astype(BF16))
            alpha = jnp.exp2(m_old - m_safe)
            m_ref[:, sl] = m_new
            acc_ref[:, sl] = alpha * acc_ref[:, sl] + jnp.dot(vt, p, preferred_element_type=F32)
        return c

    @pl.when(bound <= MAX_FIXED_SHIFT)
    def _():
        sweep(attend_bounded, 0)

    @pl.when(bound > MAX_FIXED_SHIFT)
    def _():
        m_ref[...] = jnp.full_like(m_ref, -jnp.inf)
        sweep(attend, 0)

    out = acc_ref[0:HEAD_DIM, :] / acc_ref[HEAD_DIM:HEAD_DIM + 1, :]
    for h in range(DSA_HEADS):
        o_ref[0, h * HEAD_DIM:(h + 1) * HEAD_DIM, :] = out[:, h * tq:(h + 1) * tq].astype(BF16)


def _dsa_call(dq, iq, wt, dk4, dvt, ik8, hm8, hm4, bound, top_k):
    bsz, s, _ = dq.shape
    tq = min(DSA_Q_TILE, s)

    def tok(w):
        return pl.BlockSpec((1, tq, w), lambda b, i: (b, i, 0))

    def seq(w):
        return pl.BlockSpec((1, s, w), lambda b, i: (b, 0, 0))

    def seqt(r):
        return pl.BlockSpec((1, r, s), lambda b, i: (b, 0, 0))

    return pl.pallas_call(
        functools.partial(_dsa_kernel, top_k=top_k),
        grid=(bsz, s // tq),
        in_specs=[tok(DSA_WIDTH), tok(IDX_WIDTH), seqt(2 * IDX_HEADS), seq(DK_TILED), seqt(V_ROWS), seq(IDX_WIDTH),
                  pl.BlockSpec(hm8.shape, lambda b, i: (0, 0)), pl.BlockSpec(hm4.shape, lambda b, i: (0, 0)),
                  pl.BlockSpec((1, LANES), lambda b, i: (0, 0))],
        out_specs=pl.BlockSpec((1, DSA_WIDTH, tq), lambda b, i: (b, 0, i)),
        out_shape=jax.ShapeDtypeStruct((bsz, DSA_WIDTH, s), BF16),
        scratch_shapes=[pltpu.VMEM((IDX_HEADS * tq, IDX_WIDTH), BF16), pltpu.VMEM((DSA_HEADS * tq, DK_TILED), BF16),
                        pltpu.VMEM((s, tq), jnp.int32), pltpu.VMEM((s, tq), BF16),
                        pltpu.VMEM((s, tq), jnp.int16),
                        pltpu.VMEM((1, DSA_HEADS * tq), F32),
                        pltpu.VMEM((V_ROWS, DSA_HEADS * tq), F32)],
        compiler_params=_cparams(("parallel", "arbitrary")),
        name="dsa_attn",
    )(dq, iq, wt, dk4, dvt, ik8, hm8, hm4, bound)


def _merge_kernel(x_ref, mod_ref, g_ref, ya_ref, yb_ref, wg_ref, wfa_ref, wfb_ref, wo_ref, o_ref):
    x = x_ref[0]
    d = x.shape[1]
    h = _modulated(x, mod_ref, g_ref, 1).astype(BF16)
    zg = jnp.dot(h, wg_ref[...], preferred_element_type=F32)
    pa = lax.dot_general(ya_ref[0], wfa_ref[...], _TN, preferred_element_type=F32)
    pb = lax.dot_general(yb_ref[0], wfb_ref[...], _TN, preferred_element_type=F32)
    merged = jax.nn.sigmoid(zg[:, :d]) * pa + jax.nn.sigmoid(zg[:, d:]) * pb
    y = jnp.dot(merged.astype(BF16), wo_ref[...], preferred_element_type=F32)
    o_ref[0] = x + mod_ref[0, 5:6, :] * y


def _merge_call(x, mod, g, ya, yb, wg, wfa, wfb, wo):
    bsz, s, d = x.shape
    tm = min(TOKEN_TILE, s)

    def tok(w):
        return pl.BlockSpec((1, tm, w), lambda b, i: (b, i, 0))

    def tokt(w):
        return pl.BlockSpec((1, w, tm), lambda b, i: (b, 0, i))

    return pl.pallas_call(
        _merge_kernel,
        grid=(bsz, s // tm),
        in_specs=[tok(d),
                  pl.BlockSpec((1, 3 * N_SUBLAYERS, d), lambda b, i: (b, 0, 0)),
                  _const_spec((N_SUBLAYERS, d)),
                  tokt(FOX_WIDTH), tokt(DSA_WIDTH),
                  _const_spec(wg.shape), _const_spec(wfa.shape), _const_spec(wfb.shape), _const_spec(wo.shape)],
        out_specs=tok(d),
        out_shape=jax.ShapeDtypeStruct(x.shape, F32),
        compiler_params=_cparams(("parallel", "parallel")),
        name="merge_out",
    )(x, mod, g, ya, yb, wg, wfa, wfb, wo)


def _head_mask(n_heads, width):
    return jnp.asarray(np.kron(np.eye(n_heads), np.ones((1, width))), BF16)


def _rope_freq_row(rot_dim, period):
    inv_freq = ROPE_THETA ** (-jnp.arange(0, rot_dim, 2, dtype=F32) / rot_dim)
    half = rot_dim // 2
    head = jnp.concatenate([-inv_freq, inv_freq, jnp.zeros((period - 2 * half,), F32)])
    return jnp.tile(head, LANES // period)


def kernel(x, c, positions, ada_w, ada_b, norm_g, ffn1_w1, ffn1_w3, ffn1_w2, w_in, fox_f_bias, fox_qk_g, dsa_qk_g,
           w_br_fox, w_br_dsa, w_out, ffn2_w1, ffn2_w3, ffn2_w2):
    bsz, s, d = x.shape
    top_k = min(TOPK_MAX, s // 4)
    depth = ada_w.shape[0]
    pos = jnp.broadcast_to(positions.astype(F32)[:, :, None], (bsz, s, LANES))
    freq = jnp.stack([_rope_freq_row(HEAD_DIM // ROPE_FRACTION, HEAD_DIM),
                      _rope_freq_row(IDX_DIM // ROPE_FRACTION, IDX_DIM)])
    gmat = jnp.asarray(np.kron(np.eye(FOX_HEADS), np.ones((HEAD_DIM, HEAD_DIM))), BF16)
    hm2 = _head_mask(2, HEAD_DIM)
    hm4 = _head_mask(4, HEAD_DIM)
    hm8 = _head_mask(IDX_HEADS, IDX_DIM)
    fsel = jnp.asarray(np.tile(np.eye(FOX_HEADS), (1, LANES // FOX_HEADS))
                       * (np.arange(LANES) < 3 * FOX_HEADS), BF16).reshape(FOX_HEADS // 2, 2, LANES)
    o_fq, o_fk, o_fv = 0, FOX_WIDTH, 2 * FOX_WIDTH
    o_ff = 3 * FOX_WIDTH
    o_dq = o_ff + FOX_HEADS
    o_dk = o_dq + DSA_WIDTH
    o_dv = o_dk + HEAD_DIM
    o_iq = o_dv + HEAD_DIM
    o_ik = o_iq + IDX_WIDTH
    o_iw = o_ik + IDX_DIM
    o_ga = o_iw + IDX_HEADS

    for l in range(depth):
        mod = _ada_call(c, ada_w[l], ada_b[l]).reshape(bsz, 3 * N_SUBLAYERS, d)
        g = norm_g[l]
        x = _ffn_call(x, mod, g, ffn1_w1[l].astype(BF16), ffn1_w3[l].astype(BF16), ffn1_w2[l].astype(BF16), 0)

        w = w_in[l]
        wa = jnp.concatenate([w[:, o_fq:o_fq + 2 * FOX_WIDTH], w[:, o_dq:o_dq + DSA_WIDTH]], axis=1).astype(BF16)
        pad = jnp.zeros((d, LANES - FOX_HEADS), F32)
        wb = jnp.concatenate([jnp.tile(w[:, o_dk:o_dk + HEAD_DIM], (1, 4)),
                              jnp.tile(w[:, o_ik:o_ik + IDX_DIM], (1, IDX_HEADS)),
                              w[:, o_iq:o_iq + IDX_WIDTH],
                              w[:, o_ff:o_ff + FOX_HEADS], pad], axis=1).astype(BF16)
        zrows = jnp.zeros((FOX_HEADS, V_ROWS - HEAD_DIM, d), F32)
        wfv = jnp.transpose(w[:, o_fv:o_fv + FOX_WIDTH]).reshape(FOX_HEADS, HEAD_DIM, d)
        wt = jnp.concatenate([jnp.concatenate([wfv, zrows], axis=1).reshape(FOX_HEADS * V_ROWS, d),
                              jnp.transpose(w[:, o_dv:o_dv + HEAD_DIM]), zrows[0],
                              jnp.zeros((IDX_HEADS, d), F32), jnp.transpose(w[:, o_iw:o_iw + IDX_HEADS])],
                             axis=0).astype(BF16)
        ones = jnp.asarray(np.tile((np.arange(FOX_HEADS * V_ROWS) % V_ROWS >= HEAD_DIM)[:, None], (1, LANES)), F32)
        wg = w[:, o_ga:o_ga + 2 * d].astype(BF16)
        gains = jnp.stack([jnp.tile(fox_qk_g[l, 0], FOX_HEADS), jnp.tile(fox_qk_g[l, 1], FOX_HEADS),
                           jnp.tile(dsa_qk_g[l, 0], DSA_HEADS), jnp.tile(dsa_qk_g[l, 1], DSA_HEADS)]).astype(F32)
        fbias = jnp.concatenate([fox_f_bias[l].astype(F32), jnp.zeros((LANES - FOX_HEADS,), F32)]).reshape(1, LANES)

        fq, fk, dq, dk4, ik8, iq, faug, fvt, dvt, iwt = _mix_in_call(x, mod, g, pos, wa, wb, wt, ones, gmat, gains,
                                                                       freq, fbias)
        unit = jnp.full((1, LANES), HEAD_DIM ** 0.5 * LOG2E * NORM_SLACK, F32)
        fox_bound = unit * (jnp.max(jnp.abs(fox_qk_g[l, 0])) * jnp.max(jnp.abs(fox_qk_g[l, 1])))
        dsa_bound = unit * (jnp.max(jnp.abs(dsa_qk_g[l, 0])) * jnp.max(jnp.abs(dsa_qk_g[l, 1])))
        yat = _fox_call(fq, fk, faug, fvt, hm2, fsel, fox_bound)
        ybt = _dsa_call(dq, iq, iwt, dk4, dvt, ik8, hm8, hm4, dsa_bound, top_k)
        x = _merge_call(x, mod, g, yat, ybt, wg, w_br_fox[l].astype(BF16), w_br_dsa[l].astype(BF16),
                        w_out[l].astype(BF16))
        x = _ffn_call(x, mod, g, ffn2_w1[l].astype(BF16), ffn2_w3[l].astype(BF16), ffn2_w2[l].astype(BF16), 2)
    return x
```
